```python
import math
import jax
import jax.numpy as jnp
from jax import lax
import numpy as np

D_MODEL = 1024
BATCH = 16
SEQ = 256
DEPTH = 4
DEC_BATCH = 4
DEC_SEQ = 1024
PAST_LEN = 512

GRID_W = 64
N_EVEN = (DEPTH + 1) // 2
N_ODD = DEPTH // 2
H_A = 4
DK_A = 128
DV_A = D_MODEL // 2 // H_A
F_A = H_A * DK_A
CHUNK = 64
H_B = 8
KV_B = 2
G_B = H_B // KV_B
HD_B = D_MODEL // 2 // H_B
WINDOW = 128
QBLK = 128
H_C = 8
HD_C = D_MODEL // (2 * H_C)
D_FF = 4 * D_MODEL
ROPE_BASE = 10000.0
EPS = 1e-6
D_MIX = D_MODEL
EVEN_SIZES = (F_A, F_A, F_A, H_A * DV_A, H_A * DV_A, H_B * HD_B, KV_B * HD_B, KV_B * HD_B)
EVEN_COLS = 3 * F_A + 2 * H_A * DV_A + (H_B + 2 * KV_B) * HD_B
ODD_W = H_C * 2 * HD_C
ODD_COLS = 3 * ODD_W

kernel_name = "hybrid_dit_hgrn2_swa_diffattn_step"

F32 = jnp.float32


def rms_norm(x, w):
    xf = x.astype(F32)
    y = xf * lax.rsqrt(jnp.mean(xf * xf, axis=-1, keepdims=True) + EPS)
    return (y * w.astype(F32)).astype(x.dtype)


def split_cols(a, sizes):
    out, start = [], 0
    for s in sizes:
        out.append(a[..., start:start + s])
        start += s
    return out


def ada_mods(cvec, w_ada, b_ada):
    m = jax.nn.silu(cvec) @ w_ada + b_ada
    return jnp.split(m[:, None, :], 6, axis=-1)


def rope_2d(x):
    L, d = x.shape[1], x.shape[-1]
    t = jnp.arange(L)
    row = (t // GRID_W).astype(F32)
    col = (t % GRID_W).astype(F32)
    half = d // 2
    inv = ROPE_BASE ** (-jnp.arange(0, half, 2, dtype=F32) / half)
    ang = jnp.concatenate([row[:, None] * inv, col[:, None] * inv], -1)[None, :, None, :]
    cos, sin = jnp.cos(ang), jnp.sin(ang)
    xf = x.astype(F32)
    x1, x2 = xf[..., 0::2], xf[..., 1::2]
    return jnp.stack([x1 * cos - x2 * sin, x1 * sin + x2 * cos], -1).reshape(x.shape).astype(x.dtype)


def map_query_blocks(fn, q):
    b, l = q.shape[:2]
    n = l // QBLK
    qb = jnp.moveaxis(q.reshape((b, n, QBLK) + q.shape[2:]), 1, 0)
    out = lax.map(lambda xs: fn(xs[0], xs[1]), (qb, jnp.arange(n)))
    return jnp.moveaxis(out, 0, 1).reshape((b, l) + out.shape[3:])


def hgrn_scan(q, k, v, g, s0):
    b, h, l, _ = q.shape
    n = l // CHUNK
    def chunks(a):
        return jnp.moveaxis(a.reshape(b, h, n, CHUNK, a.shape[-1]), 2, 0)
    tril = jnp.tril(jnp.ones((CHUNK, CHUNK), bool))[:, :, None]
    def step(s, xs):
        qc, kc, vc, gc = xs
        cum = jnp.cumsum(gc, axis=2)
        diff = cum[:, :, :, None, :] - cum[:, :, None, :, :]
        decay = jnp.exp(jnp.where(tril, diff, -jnp.inf))
        att = jnp.einsum('bhtk,bhsk,bhtsk->bhts', qc, kc, decay)
        o = jnp.einsum('bhts,bhsv->bhtv', att, vc) + jnp.einsum('bhtk,bhkv->bhtv', qc * jnp.exp(cum), s)
        last = cum[:, :, -1:, :]
        s_new = jnp.exp(last[:, :, 0, :])[..., None] * s + jnp.einsum('bhsk,bhsv->bhkv', kc * jnp.exp(last - cum), vc)
        return s_new, o
    s_fin, o = lax.scan(step, s0, (chunks(q), chunks(k), chunks(v), chunks(g)))
    return jnp.moveaxis(o, 0, 2).reshape(b, h, l, -1), s_fin


def hgrn_bidir(q, g_f, g_b, v, s_f0, s_b0):
    o_f, s_f = hgrn_scan(q, -jnp.expm1(g_f), v, g_f, s_f0)
    rev = lambda a: jnp.flip(a, axis=2)
    o_b, s_b = hgrn_scan(rev(q), rev(-jnp.expm1(g_b)), rev(v), rev(g_b), s_b0)
    return o_f + rev(o_b), jnp.stack([s_f, s_b], axis=1)


def hgrn_readout(o, gate, norm_w):
    o = rms_norm(o.transpose(0, 2, 1, 3), norm_w) * jax.nn.silu(gate.astype(F32))
    return o.reshape(o.shape[0], o.shape[1], -1).astype(gate.dtype)


def even_inputs(h, w_in, lb, qn_w, kn_w):
    b, l = h.shape[:2]
    qa, zf, zb, ia, ga, qb, kb, vb = split_cols(h @ w_in, EVEN_SIZES)
    def rec(a, d):
        return a.reshape(b, l, H_A, d).transpose(0, 2, 1, 3).astype(F32)
    q_r = rec(jax.nn.silu(qa), DK_A) * (DK_A ** -0.5)
    lbf = lb.astype(F32).reshape(2, H_A, 1, DK_A)
    g_f = jnp.log(lbf[0] + (1.0 - lbf[0]) * jax.nn.sigmoid(rec(zf, DK_A)))
    g_b = jnp.log(lbf[1] + (1.0 - lbf[1]) * jax.nn.sigmoid(rec(zb, DK_A)))
    v_r = rec(ia, DV_A)
    gate = ga.reshape(b, l, H_A, DV_A)
    q_a = rms_norm(qb.reshape(b, l, H_B, HD_B), qn_w)
    k_a = rms_norm(kb.reshape(b, l, KV_B, HD_B), kn_w)
    v_a = vb.reshape(b, l, KV_B, HD_B)
    return (q_r, g_f, g_b, v_r, gate), (q_a, k_a, v_a)


def swa_context_attention(q, k, v, sink):
    b, l = q.shape[:2]
    qg = q.reshape(b, l, KV_B, G_B, HD_B)
    sink_l = sink.astype(F32).reshape(KV_B, G_B, 1, 1)
    def blk(qb, _i):
        s = jnp.einsum('bqngd,bknd->bngqk', qb, k).astype(F32) * HD_B ** -0.5
        s = jnp.concatenate([s, jnp.broadcast_to(sink_l, s.shape[:-1] + (1,))], -1)
        pr = jax.nn.softmax(s, axis=-1)[..., :-1].astype(v.dtype)
        return jnp.einsum('bngqk,bknd->bqngd', pr, v)
    return map_query_blocks(blk, qg).reshape(b, l, H_B * HD_B)


def swa_latent_attention(q, k, v, k_ctx, v_ctx, sink):
    b, l = q.shape[:2]
    qg = q.reshape(b, l, KV_B, G_B, HD_B)
    pad = ((0, 0), (WINDOW, WINDOW), (0, 0), (0, 0))
    k_pad, v_pad = jnp.pad(k, pad), jnp.pad(v, pad)
    span = QBLK + 2 * WINDOW
    sink_l = sink.astype(F32).reshape(KV_B, G_B, 1, 1)
    def blk(qb, i):
        start = i * QBLK
        kw = lax.dynamic_slice_in_dim(k_pad, start, span, axis=1)
        vw = lax.dynamic_slice_in_dim(v_pad, start, span, axis=1)
        tq = start + jnp.arange(QBLK)
        tk = start - WINDOW + jnp.arange(span)
        valid = (jnp.abs(tq[:, None] - tk[None, :]) <= WINDOW) & (tk >= 0)[None, :] & (tk < l)[None, :]
        s_w = jnp.einsum('bqngd,bknd->bngqk', qb, kw).astype(F32) * HD_B ** -0.5
        s_w = jnp.where(valid, s_w, -jnp.inf)
        s_c = jnp.einsum('bqngd,bknd->bngqk', qb, k_ctx).astype(F32) * HD_B ** -0.5
        s = jnp.concatenate([s_w, s_c, jnp.broadcast_to(sink_l, s_w.shape[:-1] + (1,))], -1)
        pr = jax.nn.softmax(s, axis=-1).astype(v.dtype)
        return (jnp.einsum('bngqk,bknd->bqngd', pr[..., :span], vw)
                + jnp.einsum('bngqk,bknd->bqngd', pr[..., span:-1], v_ctx))
    return map_query_blocks(blk, qg).reshape(b, l, H_B * HD_B)


def odd_inputs(h, w_in, qn_w, kn_w):
    b, l = h.shape[:2]
    q, k, v = split_cols(h @ w_in, (ODD_W, ODD_W, ODD_W))
    q = rms_norm(q.reshape(b, l, H_C, 2, HD_C), qn_w)
    k = rms_norm(k.reshape(b, l, H_C, 2, HD_C), kn_w)
    return q, k, v.reshape(b, l, H_C, 2 * HD_C)


def rope_2d_pairs(x):
    b, l = x.shape[:2]
    return rope_2d(x.reshape(b, l, H_C * 2, HD_C)).reshape(x.shape)


def diff_lambda(lam_p, lam_init):
    lp = lam_p.astype(F32)
    return jnp.exp(jnp.sum(lp[0] * lp[1])) - jnp.exp(jnp.sum(lp[2] * lp[3])) + lam_init


def diff_attention(q, k_parts, v_parts, lam):
    sizes = [kp.shape[1] for kp in k_parts]
    def blk(qb, _i):
        s = jnp.concatenate([jnp.einsum('bqhcd,bkhcd->bhcqk', qb, kp) for kp in k_parts], -1)
        pr = jax.nn.softmax(s.astype(F32) * HD_C ** -0.5, axis=-1)
        w = (pr[:, :, 0] - lam * pr[:, :, 1]).astype(v_parts[0].dtype)
        terms, start = [], 0
        for vp, n in zip(v_parts, sizes):
            terms.append(jnp.einsum('bhqk,bkhe->bqhe', w[..., start:start + n], vp))
            start += n
        return sum(terms[1:], terms[0])
    return map_query_blocks(blk, q)


def diff_out(o, subln_w, lam_init, w_out):
    o = rms_norm(o, subln_w) * (1.0 - lam_init)
    return o.reshape(o.shape[0], o.shape[1], -1) @ w_out


def lambda_init_of(li):
    return 0.8 - 0.6 * math.exp(-0.3 * li)


def lower_bounds(lb_logits):
    sm = jax.nn.softmax(lb_logits.astype(F32), axis=0)
    return jnp.cumsum(sm, axis=0) - sm[0]


def even_layer_ctx(h, p, j, lbs):
    (q_r, g_f, g_b, v_r, gate), (q_a, k_a, v_a) = even_inputs(
        h, p['w_in_even'][j], lbs[j], p['swa_qnorm_w'][j], p['swa_knorm_w'][j])
    zeros = jnp.zeros((h.shape[0], H_A, DK_A, DV_A), F32)
    o_r, s_ctx = hgrn_bidir(q_r, g_f, g_b, v_r, zeros, zeros)
    out_a = hgrn_readout(o_r, gate, p['hgrn_norm_w'][j])
    out_b = swa_context_attention(q_a, k_a, v_a, p['swa_sink'][j])
    y = jnp.concatenate([out_a, out_b.astype(out_a.dtype)], -1) @ p['w_out_even'][j]
    return y, k_a, v_a, s_ctx.astype(h.dtype)


def even_layer_lat(h, p, j, lbs, k_ctx, v_ctx, s_ctx):
    (q_r, g_f, g_b, v_r, gate), (q_a, k_a, v_a) = even_inputs(
        h, p['w_in_even'][j], lbs[j], p['swa_qnorm_w'][j], p['swa_knorm_w'][j])
    s0 = s_ctx.astype(F32)
    o_r, _ = hgrn_bidir(q_r, g_f, g_b, v_r, s0[:, 0], s0[:, 1])
    out_a = hgrn_readout(o_r, gate, p['hgrn_norm_w'][j])
    out_b = swa_latent_attention(rope_2d(q_a), rope_2d(k_a), v_a, k_ctx, v_ctx, p['swa_sink'][j])
    return jnp.concatenate([out_a, out_b.astype(out_a.dtype)], -1) @ p['w_out_even'][j]


def odd_layer_ctx(h, p, j, li):
    q, k, v = odd_inputs(h, p['w_in_odd'][j], p['diff_qnorm_w'][j], p['diff_knorm_w'][j])
    lam_init = lambda_init_of(li)
    o = diff_attention(q, [k], [v], diff_lambda(p['diff_lambda_p'][j], lam_init))
    return diff_out(o, p['diff_subln_w'][j], lam_init, p['w_out_odd'][j]), k, v


def odd_layer_lat(h, p, j, li, k_ctx, v_ctx):
    q, k, v = odd_inputs(h, p['w_in_odd'][j], p['diff_qnorm_w'][j], p['diff_knorm_w'][j])
    lam_init = lambda_init_of(li)
    o = diff_attention(rope_2d_pairs(q), [rope_2d_pairs(k), k_ctx], [v, v_ctx],
                       diff_lambda(p['diff_lambda_p'][j], lam_init))
    return diff_out(o, p['diff_subln_w'][j], lam_init, p['w_out_odd'][j])


def sq_relu_mlp(h, w1, w2):
    return jnp.square(jax.nn.relu(h @ w1)) @ w2


def context_trunk(x, c_ctx, p):
    lbs = lower_bounds(p['hgrn_lb_logits'])
    cvec = c_ctx[None, :]
    ks, vs, ss, kd, vd = [], [], [], [], []
    for li in range(DEPTH):
        sh1, sc1, g1, sh2, sc2, g2 = ada_mods(cvec, p['w_ada'][li], p['b_ada'][li])
        h = rms_norm(x, p['norm_w'][li, 0]) * (1 + sc1) + sh1
        j = li // 2
        if li % 2 == 0:
            y, k, v, s = even_layer_ctx(h, p, j, lbs)
            ks.append(k); vs.append(v); ss.append(s)
        else:
            y, k, v = odd_layer_ctx(h, p, j, li)
            kd.append(k); vd.append(v)
        x = x + g1 * y
        h = rms_norm(x, p['norm_w'][li, 1]) * (1 + sc2) + sh2
        x = x + g2 * sq_relu_mlp(h, p['w_mlp1'][li], p['w_mlp2'][li])
    return (x, jnp.stack(ks, 1), jnp.stack(vs, 1), jnp.stack(ss, 1),
            jnp.stack(kd, 1), jnp.stack(vd, 1))


def latent_trunk(x, c, cache_k_swa, cache_v_swa, state_hgrn, cache_k_diff, cache_v_diff, p):
    lbs = lower_bounds(p['hgrn_lb_logits'])
    for li in range(DEPTH):
        sh1, sc1, g1, sh2, sc2, g2 = ada_mods(c, p['w_ada'][li], p['b_ada'][li])
        h = rms_norm(x, p['norm_w'][li, 0]) * (1 + sc1) + sh1
        j = li // 2
        if li % 2 == 0:
            y = even_layer_lat(h, p, j, lbs, cache_k_swa[:, j], cache_v_swa[:, j], state_hgrn[:, j])
        else:
            y = odd_layer_lat(h, p, j, li, cache_k_diff[:, j], cache_v_diff[:, j])
        x = x + g1 * y
        h = rms_norm(x, p['norm_w'][li, 1]) * (1 + sc2) + sh2
        x = x + g2 * sq_relu_mlp(h, p['w_mlp1'][li], p['w_mlp2'][li])
    return x


def setup_inputs(seed: int = 0) -> dict:
    key = jax.random.key(seed)
    ks = jax.random.split(key, 27)
    D = D_MODEL
    def nrm(k, shape, scale=1.0):
        return scale * jax.random.normal(k, shape, F32)
    return {
        'x_prompt': nrm(ks[0], (BATCH, SEQ, D)),
        'x_sample': nrm(ks[1], (DEC_BATCH, DEC_SEQ, D)),
        'cache_k_swa': nrm(ks[2], (DEC_BATCH, N_EVEN, PAST_LEN, KV_B, HD_B)),
        'cache_v_swa': nrm(ks[3], (DEC_BATCH, N_EVEN, PAST_LEN, KV_B, HD_B)),
        'state_hgrn': nrm(ks[4], (DEC_BATCH, N_EVEN, 2, H_A, DK_A, DV_A), 0.5),
        'cache_k_diff': nrm(ks[5], (DEC_BATCH, N_ODD, PAST_LEN, H_C, 2, HD_C)),
        'cache_v_diff': nrm(ks[6], (DEC_BATCH, N_ODD, PAST_LEN, H_C, 2 * HD_C)),
        'c': nrm(ks[7], (DEC_BATCH, D)),
        'c_ctx': nrm(ks[8], (D,)),
        'norm_w': 1.0 + nrm(ks[9], (DEPTH, 2, D), 0.02),
        'w_ada': nrm(ks[10], (DEPTH, D, 6 * D), 0.5 * D ** -0.5),
        'b_ada': nrm(ks[11], (DEPTH, 6 * D), 0.01),
        'w_in_even': nrm(ks[12], (N_EVEN, D, EVEN_COLS), D ** -0.5),
        'w_out_even': nrm(ks[13], (N_EVEN, D_MIX, D), D_MIX ** -0.5),
        'hgrn_lb_logits': nrm(ks[14], (N_EVEN, 2, F_A), 0.5),
        'hgrn_norm_w': 1.0 + nrm(ks[15], (N_EVEN, DV_A), 0.02),
        'swa_qnorm_w': 1.0 + nrm(ks[16], (N_EVEN, HD_B), 0.02),
        'swa_knorm_w': 1.0 + nrm(ks[17], (N_EVEN, HD_B), 0.02),
        'swa_sink': nrm(ks[18], (N_EVEN, H_B), 0.5),
        'w_in_odd': nrm(ks[19], (N_ODD, D, ODD_COLS), D ** -0.5),
        'w_out_odd': nrm(ks[20], (N_ODD, ODD_W, D), ODD_W ** -0.5),
        'diff_qnorm_w': 1.0 + nrm(ks[21], (N_ODD, HD_C), 0.02),
        'diff_knorm_w': 1.0 + nrm(ks[22], (N_ODD, HD_C), 0.02),
        'diff_lambda_p': nrm(ks[23], (N_ODD, 4, HD_C), 0.1),
        'diff_subln_w': 1.0 + nrm(ks[24], (N_ODD, 2 * HD_C), 0.02),
        'w_mlp1': nrm(ks[25], (DEPTH, D, D_FF), D ** -0.5),
        'w_mlp2': nrm(ks[26], (DEPTH, D_FF, D), D_FF ** -0.5),
    }


def reference(x_prompt, x_sample, cache_k_swa, cache_v_swa, state_hgrn, cache_k_diff, cache_v_diff,
              c, c_ctx, norm_w, w_ada, b_ada, w_in_even, w_out_even, hgrn_lb_logits, hgrn_norm_w,
              swa_qnorm_w, swa_knorm_w, swa_sink, w_in_odd, w_out_odd, diff_qnorm_w, diff_knorm_w,
              diff_lambda_p, diff_subln_w, w_mlp1, w_mlp2):
    p = dict(norm_w=norm_w, w_ada=w_ada, b_ada=b_ada, w_in_even=w_in_even, w_out_even=w_out_even,
             hgrn_lb_logits=hgrn_lb_logits, hgrn_norm_w=hgrn_norm_w, swa_qnorm_w=swa_qnorm_w,
             swa_knorm_w=swa_knorm_w, swa_sink=swa_sink, w_in_odd=w_in_odd, w_out_odd=w_out_odd,
             diff_qnorm_w=diff_qnorm_w, diff_knorm_w=diff_knorm_w, diff_lambda_p=diff_lambda_p,
             diff_subln_w=diff_subln_w, w_mlp1=w_mlp1, w_mlp2=w_mlp2)
    y_prompt, new_k_swa, new_v_swa, new_state_hgrn, new_k_diff, new_v_diff = context_trunk(x_prompt, c_ctx, p)
    y_sample = latent_trunk(x_sample, c, cache_k_swa, cache_v_swa, state_hgrn, cache_k_diff, cache_v_diff, p)
    return (y_prompt, y_sample, new_k_swa, new_v_swa, new_state_hgrn, new_k_diff, new_v_diff)
```

```python
import functools
import math

import numpy as np
import jax
import jax.numpy as jnp
from jax import lax
from jax.experimental import pallas as pl
from jax.experimental.pallas import tpu as pltpu

F32 = jnp.float32
BF16 = jnp.bfloat16

D_MODEL = 1024
BATCH = 16
SEQ = 256
DEPTH = 4
DEC_BATCH = 4
DEC_SEQ = 1024
PAST_LEN = 512
GRID_W = 64
N_EVEN = (DEPTH + 1) // 2
N_ODD = DEPTH // 2
H_A = 4
DK_A = 128
DV_A = D_MODEL // 2 // H_A
F_A = H_A * DK_A
H_B = 8
KV_B = 2
G_B = H_B // KV_B
HD_B = D_MODEL // 2 // H_B
WINDOW = 128
H_C = 8
HD_C = D_MODEL // (2 * H_C)
D_FF = 4 * D_MODEL
ROPE_BASE = 10000.0
EPS = 1e-6
EVEN_COLS = 3 * F_A + 2 * H_A * DV_A + (H_B + 2 * KV_B) * HD_B
ODD_W = H_C * 2 * HD_C

N_CTX_TOK = BATCH * SEQ
N_LAT_TOK = DEC_BATCH * DEC_SEQ
N_TOK = N_CTX_TOK + N_LAT_TOK
MOD_ROWS = 8
N_MOD = 6 * D_MODEL

HEAD_GROUP = 64
LANES = 128
HGRN_CHUNK = 128
HGRN_LEVELS = 7
VMEM_LIMIT = 48 * 1024 * 1024

PROJ_TM = 512
PROJ_TN = 512
EVEN_COLS_PAD = 7 * PROJ_TN
MLP_TM = 512
MLP_TK = 512
ADA_TN = 1536
ATT_TQ = 256


def _silu(x):
    return x * jax.nn.sigmoid(x)


def _nt_dot(a, b):
    return lax.dot_general(a, b, (((1,), (1,)), ((), ())), preferred_element_type=F32)


def _params(sem):
    return pltpu.CompilerParams(dimension_semantics=sem, vmem_limit_bytes=VMEM_LIMIT)


def _mods_kernel(c_ref, w_ref, b_ref, o_ref):
    s = _silu(c_ref[...]).astype(BF16)
    o_ref[...] = jnp.dot(s, w_ref[...], preferred_element_type=F32) + b_ref[...]


def _mods_call(c_all, w_ada, b_ada):
    return pl.pallas_call(
        _mods_kernel,
        grid=(DEPTH, N_MOD // ADA_TN),
        in_specs=[
            pl.BlockSpec((MOD_ROWS, D_MODEL), lambda l, j: (0, 0)),
            pl.BlockSpec((None, D_MODEL, ADA_TN), lambda l, j: (l, 0, j)),
            pl.BlockSpec((None, 1, ADA_TN), lambda l, j: (l, 0, j)),
        ],
        out_specs=pl.BlockSpec((None, MOD_ROWS, ADA_TN), lambda l, j: (l, 0, j)),
        out_shape=jax.ShapeDtypeStruct((DEPTH, MOD_ROWS, N_MOD), F32),
        compiler_params=_params(("parallel", "parallel")),
        name="ada_mods",
    )(c_all, w_ada, b_ada)


def _mod_row(i, tm):
    n_ctx_tiles = N_CTX_TOK // tm
    tiles_per_lat = DEC_SEQ // tm
    return jnp.where(i >= n_ctx_tiles, 1 + (i - n_ctx_tiles) // tiles_per_lat, 0)


def _norm_mod(x, nw, sc, sh):
    ms = jnp.mean(x * x, axis=-1, keepdims=True)
    return (x * lax.rsqrt(ms + EPS) * nw) * (1.0 + sc) + sh


def _group_rms(y, w_t, bd_ref):
    yy = (y * y).astype(BF16)
    bw = bd_ref.shape[0]
    parts = [jnp.dot(yy[:, s:s + bw], bd_ref[...], preferred_element_type=F32)
             for s in range(0, y.shape[1], bw)]
    ss = jnp.concatenate(parts, axis=1)
    return y * lax.rsqrt(ss * (1.0 / HEAD_GROUP) + EPS) * w_t


def _rope(y, cos_ref, sin_ref):
    n = y.shape[1]
    lane = lax.broadcasted_iota(jnp.int32, y.shape, 1)
    nxt = pltpu.roll(y, n - 1, axis=1)
    prv = pltpu.roll(y, 1, axis=1)
    swapped = jnp.where((lane & 1) == 0, nxt, prv)
    return y * cos_ref[...] + swapped * sin_ref[...]


def _lower_bounds(lbl_ref, jl):
    rows = [lbl_ref[pl.ds(2 * m, 2), :] for m in range(N_EVEN)]
    mx = functools.reduce(jnp.maximum, rows)
    es = [jnp.exp(r - mx) for r in rows]
    den = functools.reduce(lambda a, b: a + b, es)
    sm = [e / den for e in es]
    cs = sm[0]
    for m in range(1, jl + 1):
        cs = cs + sm[m]
    return cs - sm[0]


def _proj_kernel(x_ref, mods_ref, nw_ref, w_ref, lbl_ref, qn_ref, kn_ref, cos_ref, sin_ref,
                 bd_ref, o_ref, h_scr, *, kinds, jl):
    i = pl.program_id(0)
    j = pl.program_id(1)
    tm = x_ref.shape[0]
    is_lat = i >= N_CTX_TOK // tm

    @pl.when(j == 0)
    def _():
        row = _mod_row(i, tm)
        sh = mods_ref[pl.ds(row, 1), 0:D_MODEL]
        sc = mods_ref[pl.ds(row, 1), D_MODEL:2 * D_MODEL]
        h_scr[...] = _norm_mod(x_ref[...], nw_ref[...], sc, sh).astype(BF16)

    y = jnp.dot(h_scr[...], w_ref[...], preferred_element_type=F32)

    def store_maybe_rope(val, scale):
        @pl.when(is_lat)
        def _():
            r = _rope(val, cos_ref, sin_ref)
            o_ref[...] = r * scale if scale != 1.0 else r

        @pl.when(jnp.logical_not(is_lat))
        def _():
            o_ref[...] = val * scale if scale != 1.0 else val

    def epilogue(kind):
        if kind == "silu_scale":
            o_ref[...] = _silu(y) * (DK_A ** -0.5)
        elif kind in ("loggate0", "loggate1"):
            d = int(kind[-1])
            lb = _lower_bounds(lbl_ref, jl)[d:d + 1, :]
            o_ref[...] = jnp.log(lb + (1.0 - lb) * jax.nn.sigmoid(y))
        elif kind == "ident":
            o_ref[...] = y
        elif kind == "silu":
            o_ref[...] = _silu(y)
        elif kind == "qnorm":
            store_maybe_rope(_group_rms(y, qn_ref[...], bd_ref), HEAD_GROUP ** -0.5)
        elif kind == "knorm":
            store_maybe_rope(_group_rms(y, kn_ref[...], bd_ref), 1.0)
        elif kind == "kv":
            kn = _group_rms(y, kn_ref[...], bd_ref)
            lane = lax.broadcasted_iota(jnp.int32, y.shape, 1)
            is_k = lane < KV_B * HD_B

            @pl.when(is_lat)
            def _():
                o_ref[...] = jnp.where(is_k, _rope(kn, cos_ref, sin_ref), y)

            @pl.when(jnp.logical_not(is_lat))
            def _():
                o_ref[...] = jnp.where(is_k, kn, y)
        else:
            raise ValueError(kind)

    for jj, kind in enumerate(kinds):
        pl.when(j == jj)(functools.partial(epilogue, kind))


def _proj_call(x, mods_l, nw, w, lbl, qn_t, kn_t, cos_t, sin_t, bd, kinds, jl, name):
    tm, tn = PROJ_TM, PROJ_TN
    n_ctx_tiles = N_CTX_TOK // tm
    tiles_per_lat = DEC_SEQ // tm
    n_cols = w.shape[1]
    assert n_cols == tn * len(kinds)

    def rope_idx(i, j):
        return (jnp.where(i >= n_ctx_tiles, (i - n_ctx_tiles) % tiles_per_lat, 0), 0)

    const = lambda i, j: (0, 0)
    return pl.pallas_call(
        functools.partial(_proj_kernel, kinds=kinds, jl=jl),
        grid=(N_TOK // tm, n_cols // tn),
        in_specs=[
            pl.BlockSpec((tm, D_MODEL), lambda i, j: (i, 0)),
            pl.BlockSpec((MOD_ROWS, N_MOD), const),
            pl.BlockSpec((1, D_MODEL), const),
            pl.BlockSpec((D_MODEL, tn), lambda i, j: (0, j)),
            pl.BlockSpec(lbl.shape, const),
            pl.BlockSpec((1, tn), const),
            pl.BlockSpec((1, tn), const),
            pl.BlockSpec((tm, tn), rope_idx),
            pl.BlockSpec((tm, tn), rope_idx),
            pl.BlockSpec(bd.shape, const),
        ],
        out_specs=pl.BlockSpec((tm, tn), lambda i, j: (i, j)),
        out_shape=jax.ShapeDtypeStruct((N_TOK, n_cols), F32),
        scratch_shapes=[pltpu.VMEM((tm, D_MODEL), BF16)],
        compiler_params=_params(("parallel", "arbitrary")),
        name=name,
    )(x, mods_l, nw, w, lbl, qn_t, kn_t, cos_t, sin_t, bd)


def _hgrn_tables(rev):
    c = HGRN_CHUNK
    t = np.arange(c)
    w = np.zeros((2 + HGRN_LEVELS, c, c), np.float32)
    if not rev:
        w[0] = t[None, :] <= t[:, None]
        w[1] = t[None, :] > t[:, None]
    else:
        w[0] = t[None, :] >= t[:, None]
        w[1] = t[None, :] < t[:, None]
    for l in range(HGRN_LEVELS):
        hb = 1 << l
        for ti in range(c):
            mid = ti - ti % (2 * hb) + hb
            if not rev:
                if ti >= mid:
                    w[2 + l, ti, mid:ti + 1] = 1.0
                else:
                    w[2 + l, ti, ti + 1:mid] = 1.0
            else:
                if ti < mid:
                    w[2 + l, ti, ti:mid] = 1.0
                else:
                    w[2 + l, ti, mid:ti] = 1.0
    w = w.reshape((2 + HGRN_LEVELS) * c, c)
    w3 = np.concatenate([w, w, w], axis=1)
    x = t[:, None] ^ t[None, :]
    lv = np.where(x > 0, np.floor(np.log2(np.maximum(x, 1))).astype(np.int32), HGRN_LEVELS)
    causal = (t[None, :] < t[:, None]) if not rev else (t[None, :] > t[:, None])
    lv = np.where(causal | (x == 0), lv, -1).astype(np.int32)
    return jnp.asarray(w3, BF16), jnp.asarray(lv)


def _hgrn_chunk(q, g, v, st, w_ref, lv_ref, rev):
    c = HGRN_CHUNK
    g1 = g.astype(BF16)
    r1 = g - g1.astype(F32)
    g2 = r1.astype(BF16)
    g3 = (r1 - g2.astype(F32)).astype(BF16)
    gs = jnp.concatenate([g1, g2, g3], axis=0)
    z = jnp.exp(jnp.dot(w_ref[...], gs, preferred_element_type=F32))
    k = 1.0 - jnp.exp(g)
    zq = z[0:c]
    zk = z[c:2 * c]
    last = 0 if rev else c - 1
    total = zq[last:last + 1, :]
    qd = (q * zq).astype(BF16)
    kd = (k * zk).astype(BF16)
    vb = v.astype(BF16)
    o = _nt_dot(qd, st.astype(BF16))
    st_new = st * total + jnp.dot(v.T.astype(BF16), kd, preferred_element_type=F32)
    lv = lv_ref[...]
    row = lax.broadcasted_iota(jnp.int32, (c, DK_A), 0)
    a = jnp.where(lv == HGRN_LEVELS, jnp.sum(q * k, axis=-1, keepdims=True), 0.0)
    for l in range(HGRN_LEVELS):
        zl = z[(2 + l) * c:(3 + l) * c]
        bit = ((row >> l) & 1) == 1
        q_role = jnp.logical_not(bit) if rev else bit
        ql = jnp.where(q_role, q * zl, 0.0).astype(BF16)
        kl = jnp.where(q_role, 0.0, k * zl).astype(BF16)
        a = jnp.where(lv == l, _nt_dot(ql, kl), a)
    o = o + jnp.dot(a.astype(BF16), vb, preferred_element_type=F32)
    return o, st_new


def _hgrn_kernel(*refs, n_chunks, has_init, emit_state):
    refs = list(refs)
    q_ref, gf_ref, gb_ref, v_ref, sg_ref, nw_ref, wf_ref, wb_ref, lvf_ref, lvb_ref = refs[:10]
    pos = 10
    s0_ref = None
    if has_init:
        s0_ref = refs[pos]
        pos += 1
    o_ref = refs[pos]
    pos += 1
    so_ref = None
    if emit_state:
        so_ref = refs[pos]
        pos += 1
    of_scr, ob_scr, st_scr = refs[pos:pos + 3]

    for d in range(2):
        if has_init:
            st_scr[d] = s0_ref[d].T
        else:
            st_scr[d] = jnp.zeros((DV_A, DK_A), F32)

    def body(c, carry):
        for d, (g_ref, w_ref, lv_ref, scr) in enumerate(
                ((gf_ref, wf_ref, lvf_ref, of_scr), (gb_ref, wb_ref, lvb_ref, ob_scr))):
            cc = c if d == 0 else n_chunks - 1 - c
            r0 = pl.multiple_of(cc * HGRN_CHUNK, HGRN_CHUNK)
            rows = pl.ds(r0, HGRN_CHUNK)
            o, st = _hgrn_chunk(q_ref[rows, :], g_ref[rows, :], v_ref[rows, :], st_scr[d],
                                w_ref, lv_ref, rev=(d == 1))
            st_scr[d] = st
            scr[rows, :] = o
        return carry

    lax.fori_loop(0, n_chunks, body, 0)
    o = of_scr[...] + ob_scr[...]
    y = o * lax.rsqrt(jnp.mean(o * o, axis=-1, keepdims=True) + EPS) * nw_ref[...]
    o_ref[...] = (y * sg_ref[...]).astype(o_ref.dtype)
    if emit_state:
        for d in range(2):
            so_ref[d] = st_scr[d].T


def _hgrn_call(p, nw, tabs, seq_len, n_seq, row_blk0, s0, jl, name):
    wf, lvf, wb, lvb = tabs
    has_init = s0 is not None
    emit_state = s0 is None
    n_hcols = F_A // DK_A

    def col(part):
        return lambda b, h: (row_blk0 + b, part * n_hcols + h)

    const = lambda b, h: (0, 0)
    blk = (seq_len, DK_A)
    in_specs = [pl.BlockSpec(blk, col(part)) for part in range(5)]
    in_specs += [
        pl.BlockSpec((1, DV_A), const),
        pl.BlockSpec(wf.shape, const), pl.BlockSpec(wb.shape, const),
        pl.BlockSpec(lvf.shape, const), pl.BlockSpec(lvb.shape, const),
    ]
    args = [p, p, p, p, p, nw, wf, wb, lvf, lvb]
    if has_init:
        in_specs.append(pl.BlockSpec((None, None, 2, None, DK_A, DV_A),
                                     lambda b, h: (b, jl, 0, h, 0, 0)))
        args.append(s0)
    out_shape = [jax.ShapeDtypeStruct((n_seq * seq_len, H_A * DV_A), BF16)]
    out_specs = [pl.BlockSpec((seq_len, DV_A), lambda b, h: (b, h))]
    if emit_state:
        out_shape.append(jax.ShapeDtypeStruct((n_seq, 2, H_A, DK_A, DV_A), F32))
        out_specs.append(pl.BlockSpec((None, 2, None, DK_A, DV_A), lambda b, h: (b, 0, h, 0, 0)))
    return pl.pallas_call(
        functools.partial(_hgrn_kernel, n_chunks=seq_len // HGRN_CHUNK, has_init=has_init,
                          emit_state=emit_state),
        grid=(n_seq, H_A),
        in_specs=in_specs,
        out_specs=out_specs,
        out_shape=out_shape,
        scratch_shapes=[pltpu.VMEM((seq_len, DV_A), F32), pltpu.VMEM((seq_len, DV_A), F32),
                        pltpu.VMEM((2, DV_A, DK_A), F32)],
        compiler_params=_params(("parallel", "parallel")),
        name=name,
    )(*args)


def _swa_ctx_kernel(q_ref, kv_ref, sink_ref, o_ref, kc_ref, vc_ref):
    kv = kv_ref[...]
    k32 = kv[:, 0:KV_B * HD_B]
    v32 = kv[:, KV_B * HD_B:2 * KV_B * HD_B]
    kc_ref[...] = k32
    vc_ref[...] = v32
    q = q_ref[...].astype(BF16)
    k = k32.astype(BF16)
    v = v32.astype(BF16)
    outs = []
    for h in range(H_B):
        n = h // G_B
        s = _nt_dot(q[:, h * HD_B:(h + 1) * HD_B], k[:, n * HD_B:(n + 1) * HD_B])
        sink = sink_ref[0:1, h:h + 1]
        m = jnp.maximum(jnp.max(s, axis=-1, keepdims=True), sink)
        p = jnp.exp(s - m)
        den = jnp.sum(p, axis=-1, keepdims=True) + jnp.exp(sink - m)
        pv = jnp.dot(p.astype(BF16), v[:, n * HD_B:(n + 1) * HD_B], preferred_element_type=F32)
        outs.append(pv / den)
    o_ref[...] = jnp.concatenate(outs, axis=1).astype(o_ref.dtype)


def _swa_ctx_call(p, sink, name):
    qcol = (3 * F_A + 2 * H_A * DV_A) // PROJ_TN
    return pl.pallas_call(
        _swa_ctx_kernel,
        grid=(BATCH,),
        in_specs=[
            pl.BlockSpec((SEQ, H_B * HD_B), lambda b: (b, qcol)),
            pl.BlockSpec((SEQ, PROJ_TN), lambda b: (b, qcol + 1)),
            pl.BlockSpec((1, H_B), lambda b: (0, 0)),
        ],
        out_specs=[
            pl.BlockSpec((SEQ, H_B * HD_B), lambda b: (b, 0)),
            pl.BlockSpec((None, SEQ, KV_B * HD_B), lambda b: (b, 0, 0)),
            pl.BlockSpec((None, SEQ, KV_B * HD_B), lambda b: (b, 0, 0)),
        ],
        out_shape=[
            jax.ShapeDtypeStruct((N_CTX_TOK, H_B * HD_B), BF16),
            jax.ShapeDtypeStruct((BATCH, SEQ, KV_B * HD_B), F32),
            jax.ShapeDtypeStruct((BATCH, SEQ, KV_B * HD_B), F32),
        ],
        compiler_params=_params(("parallel",)),
        name=name,
    )(p, p, sink)


def _swa_lat_kernel(q_ref, kv_ref, ck_ref, cv_ref, sink_ref, o_ref):
    qi = pl.program_id(1)
    tq = q_ref.shape[0]
    span = tq + 2 * WINDOW
    ws = pl.multiple_of(jnp.clip(qi * tq - WINDOW, 0, DEC_SEQ - span), WINDOW)
    kvw = kv_ref[pl.ds(ws, span), :]
    kw = kvw[:, 0:KV_B * HD_B].astype(BF16)
    vw = kvw[:, KV_B * HD_B:2 * KV_B * HD_B].astype(BF16)
    kc = ck_ref[...].astype(BF16)
    vc = cv_ref[...].astype(BF16)
    q = q_ref[...].astype(BF16)
    t_q = qi * tq + lax.broadcasted_iota(jnp.int32, (tq, span), 0)
    t_k = ws + lax.broadcasted_iota(jnp.int32, (tq, span), 1)
    valid = jnp.abs(t_q - t_k) <= WINDOW
    outs = []
    for h in range(H_B):
        n = h // G_B
        qh = q[:, h * HD_B:(h + 1) * HD_B]
        ksl = slice(n * HD_B, (n + 1) * HD_B)
        s_w = jnp.where(valid, _nt_dot(qh, kw[:, ksl]), -jnp.inf)
        s_c = _nt_dot(qh, kc[:, ksl])
        sink = sink_ref[0:1, h:h + 1]
        m = jnp.maximum(jnp.maximum(jnp.max(s_w, axis=-1, keepdims=True),
                                    jnp.max(s_c, axis=-1, keepdims=True)), sink)
        p_w = jnp.exp(s_w - m)
        p_c = jnp.exp(s_c - m)
        den = (jnp.sum(p_w, axis=-1, keepdims=True) + jnp.sum(p_c, axis=-1, keepdims=True)
               + jnp.exp(sink - m))
        pv = (jnp.dot(p_w.astype(BF16), vw[:, ksl], preferred_element_type=F32)
              + jnp.dot(p_c.astype(BF16), vc[:, ksl], preferred_element_type=F32))
        outs.append(pv / den)
    o_ref[...] = jnp.concatenate(outs, axis=1).astype(o_ref.dtype)


def _swa_lat_call(p, ck, cv, sink, jl, name):
    tq = ATT_TQ
    qcol = (3 * F_A + 2 * H_A * DV_A) // PROJ_TN
    nq = DEC_SEQ // tq
    q_blk0 = N_CTX_TOK // tq
    s_blk0 = N_CTX_TOK // DEC_SEQ
    return pl.pallas_call(
        _swa_lat_kernel,
        grid=(DEC_BATCH, nq),
        in_specs=[
            pl.BlockSpec((tq, H_B * HD_B), lambda b, i: (q_blk0 + b * nq + i, qcol)),
            pl.BlockSpec((DEC_SEQ, PROJ_TN), lambda b, i: (s_blk0 + b, qcol + 1)),
            pl.BlockSpec((None, None, PAST_LEN, KV_B * HD_B), lambda b, i: (b, jl, 0, 0)),
            pl.BlockSpec((None, None, PAST_LEN, KV_B * HD_B), lambda b, i: (b, jl, 0, 0)),
            pl.BlockSpec((1, H_B), lambda b, i: (0, 0)),
        ],
        out_specs=pl.BlockSpec((tq, H_B * HD_B), lambda b, i: (b * nq + i, 0)),
        out_shape=jax.ShapeDtypeStruct((N_LAT_TOK, H_B * HD_B), BF16),
        compiler_params=_params(("parallel", "parallel")),
        name=name,
    )(p, p, ck, cv, sink)


def _diff_lambda(lp_ref, lam_init):
    lp = lp_ref[...]
    a = jnp.sum(lp[0:1] * lp[1:2], axis=-1, keepdims=True)
    b = jnp.sum(lp[2:3] * lp[3:4], axis=-1, keepdims=True)
    return jnp.exp(a) - jnp.exp(b) + lam_init


def _softmax_parts(parts):
    m = functools.reduce(jnp.maximum, [jnp.max(s, axis=-1, keepdims=True) for s in parts])
    es = [jnp.exp(s - m) for s in parts]
    den = functools.reduce(lambda a, b: a + b, [jnp.sum(e, axis=-1, keepdims=True) for e in es])
    inv = 1.0 / den
    return [e * inv for e in es]


def _diff_core(q, k_parts, v_parts, lam, lam_init, sw):
    pr = []
    for c in range(2):
        sl = slice(c * HD_C, (c + 1) * HD_C)
        pr.append(_softmax_parts([_nt_dot(q[:, sl], kp[:, sl]) for kp in k_parts]))
    o = None
    for i, vp in enumerate(v_parts):
        w = (pr[0][i] - lam * pr[1][i]).astype(BF16)
        t = jnp.dot(w, vp, preferred_element_type=F32)
        o = t if o is None else o + t
    y = o * lax.rsqrt(jnp.mean(o * o, axis=-1, keepdims=True) + EPS) * sw
    return y * (1.0 - lam_init)


def _diff_ctx_kernel(q_ref, k_ref, v_ref, lp_ref, sw_ref, o_ref, kc_ref, vc_ref, *, lam_init):
    k32 = k_ref[...]
    v32 = v_ref[...]
    kc_ref[...] = k32
    vc_ref[...] = v32
    lam = _diff_lambda(lp_ref, lam_init)
    y = _diff_core(q_ref[...].astype(BF16), [k32.astype(BF16)], [v32.astype(BF16)], lam,
                   lam_init, sw_ref[...])
    o_ref[...] = y.astype(o_ref.dtype)


def _diff_ctx_call(p, lp, sw, lam_init, name):
    hw = 2 * HD_C
    return pl.pallas_call(
        functools.partial(_diff_ctx_kernel, lam_init=lam_init),
        grid=(BATCH, H_C),
        in_specs=[
            pl.BlockSpec((SEQ, hw), lambda b, h: (b, h)),
            pl.BlockSpec((SEQ, hw), lambda b, h: (b, H_C + h)),
            pl.BlockSpec((SEQ, hw), lambda b, h: (b, 2 * H_C + h)),
            pl.BlockSpec((4, HD_C), lambda b, h: (0, 0)),
            pl.BlockSpec((1, hw), lambda b, h: (0, 0)),
        ],
        out_specs=[
            pl.BlockSpec((SEQ, hw), lambda b, h: (b, h)),
            pl.BlockSpec((None, SEQ, hw), lambda b, h: (b, 0, h)),
            pl.BlockSpec((None, SEQ, hw), lambda b, h: (b, 0, h)),
        ],
        out_shape=[
            jax.ShapeDtypeStruct((N_CTX_TOK, ODD_W), BF16),
            jax.ShapeDtypeStruct((BATCH, SEQ, ODD_W), F32),
            jax.ShapeDtypeStruct((BATCH, SEQ, ODD_W), F32),
        ],
        compiler_params=_params(("parallel", "parallel")),
        name=name,
    )(p, p, p, lp, sw)


def _diff_lat_kernel(q_ref, k_ref, v_ref, ck_ref, cv_ref, lp_ref, sw_ref, o_ref, *, lam_init):
    lam = _diff_lambda(lp_ref, lam_init)
    y = _diff_core(q_ref[...].astype(BF16),
                   [k_ref[...].astype(BF16), ck_ref[...].astype(BF16)],
                   [v_ref[...].astype(BF16), cv_ref[...].astype(BF16)],
                   lam, lam_init, sw_ref[...])
    o_ref[...] = y.astype(o_ref.dtype)


def _diff_lat_call(p, ck, cv, lp, sw, lam_init, jl, name):
    hw = 2 * HD_C
    tq = ATT_TQ
    nq = DEC_SEQ // tq
    q_blk0 = N_CTX_TOK // tq
    s_blk0 = N_CTX_TOK // DEC_SEQ
    return pl.pallas_call(
        functools.partial(_diff_lat_kernel, lam_init=lam_init),
        grid=(DEC_BATCH, H_C, nq),
        in_specs=[
            pl.BlockSpec((tq, hw), lambda b, h, i: (q_blk0 + b * nq + i, h)),
            pl.BlockSpec((DEC_SEQ, hw), lambda b, h, i: (s_blk0 + b, H_C + h)),
            pl.BlockSpec((DEC_SEQ, hw), lambda b, h, i: (s_blk0 + b, 2 * H_C + h)),
            pl.BlockSpec((None, None, PAST_LEN, hw), lambda b, h, i: (b, jl, 0, h)),
            pl.BlockSpec((None, None, PAST_LEN, hw), lambda b, h, i: (b, jl, 0, h)),
            pl.BlockSpec((4, HD_C), lambda b, h, i: (0, 0)),
            pl.BlockSpec((1, hw), lambda b, h, i: (0, 0)),
        ],
        out_specs=pl.BlockSpec((tq, hw), lambda b, h, i: (b * nq + i, h)),
        out_shape=jax.ShapeDtypeStruct((N_LAT_TOK, ODD_W), BF16),
        compiler_params=_params(("parallel", "parallel", "parallel")),
        name=name,
    )(p, p, p, ck, cv, lp, sw)


def _mlp_kernel(x_ref, mix_ref, wo_ref, mods_ref, nw_ref, w1_ref, w2_ref, o_ref,
                x1_scr, h_scr, acc_scr):
    i = pl.program_id(0)
    k = pl.program_id(1)
    row = _mod_row(i, x_ref.shape[0])

    def mod(a):
        return mods_ref[pl.ds(row, 1), a * D_MODEL:(a + 1) * D_MODEL]

    @pl.when(k == 0)
    def _():
        y = jnp.dot(mix_ref[...], wo_ref[...], preferred_element_type=F32)
        x1 = x_ref[...] + mod(2) * y
        x1_scr[...] = x1
        h_scr[...] = _norm_mod(x1, nw_ref[...], mod(4), mod(3)).astype(BF16)
        acc_scr[...] = jnp.zeros_like(acc_scr)

    u = jnp.dot(h_scr[...], w1_ref[...], preferred_element_type=F32)
    u = jnp.square(jnp.maximum(u, 0.0)).astype(BF16)
    acc_scr[...] += jnp.dot(u, w2_ref[...], preferred_element_type=F32)

    @pl.when(k == pl.num_programs(1) - 1)
    def _():
        o_ref[...] = x1_scr[...] + mod(5) * acc_scr[...]


def _mlp_call(x, mix, wo, mods_l, nw2, w1, w2, name):
    tm, tk = MLP_TM, MLP_TK
    const = lambda i, k: (0, 0)
    return pl.pallas_call(
        _mlp_kernel,
        grid=(N_TOK // tm, D_FF // tk),
        in_specs=[
            pl.BlockSpec((tm, D_MODEL), lambda i, k: (i, 0)),
            pl.BlockSpec((tm, D_MODEL), lambda i, k: (i, 0)),
            pl.BlockSpec((D_MODEL, D_MODEL), const),
            pl.BlockSpec((MOD_ROWS, N_MOD), const),
            pl.BlockSpec((1, D_MODEL), const),
            pl.BlockSpec((D_MODEL, tk), lambda i, k: (0, k)),
            pl.BlockSpec((tk, D_MODEL), lambda i, k: (k, 0)),
        ],
        out_specs=pl.BlockSpec((tm, D_MODEL), lambda i, k: (i, 0)),
        out_shape=jax.ShapeDtypeStruct((N_TOK, D_MODEL), F32),
        scratch_shapes=[pltpu.VMEM((tm, D_MODEL), F32), pltpu.VMEM((tm, D_MODEL), BF16),
                        pltpu.VMEM((tm, D_MODEL), F32)],
        compiler_params=_params(("parallel", "arbitrary")),
        name=name,
    )(x, mix, wo, mods_l, nw2, w1, w2)


def _rope_tables(width):
    t = np.arange(DEC_SEQ)
    half = HEAD_GROUP // 2
    inv = ROPE_BASE ** (-np.arange(0, half, 2, dtype=np.float64) / half)
    ang = np.concatenate([(t // GRID_W)[:, None] * inv, (t % GRID_W)[:, None] * inv], axis=-1)
    cos = np.repeat(np.cos(ang), 2, axis=-1)
    sin = np.repeat(np.sin(ang), 2, axis=-1)
    sign = np.tile(np.array([-1.0, 1.0]), HEAD_GROUP // 2)
    reps = width // HEAD_GROUP
    return (jnp.asarray(np.tile(cos, (1, reps)), F32),
            jnp.asarray(np.tile(sin * sign, (1, reps)), F32))


def _block_diag_ones(n):
    g = np.arange(n) // HEAD_GROUP
    return jnp.asarray(g[:, None] == g[None, :], BF16)


def _tile_row(w, width):
    return jnp.tile(w.astype(F32), width // w.shape[0])[None, :]


def _lambda_init(li):
    return 0.8 - 0.6 * math.exp(-0.3 * li)


def kernel(x_prompt, x_sample, cache_k_swa, cache_v_swa, state_hgrn, cache_k_diff, cache_v_diff, c, c_ctx, norm_w, w_ada, b_ada, w_in_even, w_out_even, hgrn_lb_logits, hgrn_norm_w, swa_qnorm_w, swa_knorm_w, swa_sink, w_in_odd, w_out_odd, diff_qnorm_w, diff_knorm_w, diff_lambda_p, diff_subln_w, w_mlp1, w_mlp2):
    x = jnp.concatenate([x_prompt.reshape(N_CTX_TOK, D_MODEL),
                         x_sample.reshape(N_LAT_TOK, D_MODEL)], axis=0)
    c_all = jnp.concatenate(
        [c_ctx[None, :], c, jnp.zeros((MOD_ROWS - 1 - DEC_BATCH, D_MODEL), F32)], axis=0)
    mods = _mods_call(c_all, w_ada.astype(BF16), b_ada.reshape(DEPTH, 1, N_MOD))

    cos_t, sin_t = _rope_tables(PROJ_TN)
    bd = _block_diag_ones(256)
    tabs_f = _hgrn_tables(False)
    tabs_b = _hgrn_tables(True)
    hgrn_tabs = (tabs_f[0], tabs_f[1], tabs_b[0], tabs_b[1])
    lbl = hgrn_lb_logits.astype(F32).reshape(N_EVEN * 2, F_A)

    w_in_even_b = jnp.pad(w_in_even.astype(BF16), ((0, 0), (0, 0), (0, EVEN_COLS_PAD - EVEN_COLS)))
    w_in_odd_b = w_in_odd.astype(BF16)
    w_out_even_b = w_out_even.astype(BF16)
    w_out_odd_b = w_out_odd.astype(BF16)
    w1_b = w_mlp1.astype(BF16)
    w2_b = w_mlp2.astype(BF16)

    ck_swa = cache_k_swa.reshape(DEC_BATCH, N_EVEN, PAST_LEN, KV_B * HD_B)
    cv_swa = cache_v_swa.reshape(DEC_BATCH, N_EVEN, PAST_LEN, KV_B * HD_B)
    ck_diff = cache_k_diff.reshape(DEC_BATCH, N_ODD, PAST_LEN, ODD_W)
    cv_diff = cache_v_diff.reshape(DEC_BATCH, N_ODD, PAST_LEN, ODD_W)

    even_kinds = ("silu_scale", "loggate0", "loggate1", "ident", "silu", "qnorm", "kv")
    odd_kinds = ("qnorm", "qnorm", "knorm", "knorm", "ident", "ident")

    ks, vs, ss, kd, vd = [], [], [], [], []
    for li in range(DEPTH):
        j = li // 2
        nw1 = norm_w[li, 0][None, :]
        nw2 = norm_w[li, 1][None, :]
        if li % 2 == 0:
            p = _proj_call(x, mods[li], nw1, w_in_even_b[j], lbl,
                           _tile_row(swa_qnorm_w[j], PROJ_TN), _tile_row(swa_knorm_w[j], PROJ_TN),
                           cos_t, sin_t, bd, even_kinds, j, f"proj_even{j}")
            hn = hgrn_norm_w[j].astype(F32)[None, :]
            a_ctx, s_new = _hgrn_call(p, hn, hgrn_tabs, SEQ, BATCH, 0, None, j, f"hgrn_ctx{j}")
            (a_lat,) = _hgrn_call(p, hn, hgrn_tabs, DEC_SEQ, DEC_BATCH, N_CTX_TOK // DEC_SEQ,
                                  state_hgrn, j, f"hgrn_lat{j}")
            sink = swa_sink[j].astype(F32)[None, :]
            b_ctx, k_new, v_new = _swa_ctx_call(p, sink, f"swa_ctx{j}")
            b_lat = _swa_lat_call(p, ck_swa, cv_swa, sink, j, f"swa_lat{j}")
            mix = jnp.concatenate([jnp.concatenate([a_ctx, b_ctx], axis=1),
                                   jnp.concatenate([a_lat, b_lat], axis=1)], axis=0)
            wo = w_out_even_b[j]
            ks.append(k_new)
            vs.append(v_new)
            ss.append(s_new)
        else:
            p = _proj_call(x, mods[li], nw1, w_in_odd_b[j], lbl,
                           _tile_row(diff_qnorm_w[j], PROJ_TN), _tile_row(diff_knorm_w[j], PROJ_TN),
                           cos_t, sin_t, bd, odd_kinds, 0, f"proj_odd{j}")
            lam_init = _lambda_init(li)
            lp = diff_lambda_p[j].astype(F32)
            sw = diff_subln_w[j].astype(F32)[None, :]
            o_ctx, k_new, v_new = _diff_ctx_call(p, lp, sw, lam_init, f"diff_ctx{j}")
            o_lat = _diff_lat_call(p, ck_diff, cv_diff, lp, sw, lam_init, j, f"diff_lat{j}")
            mix = jnp.concatenate([o_ctx, o_lat], axis=0)
            wo = w_out_odd_b[j]
            kd.append(k_new)
            vd.append(v_new)
        x = _mlp_call(x, mix, wo, mods[li], nw2, w1_b[li], w2_b[li], f"mlp{li}")

    y_prompt = x[:N_CTX_TOK].reshape(BATCH, SEQ, D_MODEL)
    y_sample = x[N_CTX_TOK:].reshape(DEC_BATCH, DEC_SEQ, D_MODEL)
    new_k_swa = jnp.stack(ks, axis=1).reshape(BATCH, N_EVEN, SEQ, KV_B, HD_B)
    new_v_swa = jnp.stack(vs, axis=1).reshape(BATCH, N_EVEN, SEQ, KV_B, HD_B)
    new_state = jnp.stack(ss, axis=1)
    new_k_diff = jnp.stack(kd, axis=1).reshape(BATCH, N_ODD, SEQ, H_C, 2, HD_C)
    new_v_diff = jnp.stack(vd, axis=1).reshape(BATCH, N_ODD, SEQ, H_C, 2 * HD_C)
    return (y_prompt, y_sample, new_k_swa, new_v_swa, new_state, new_k_diff, new_v_diff)
```

```python
import functools
import math

import numpy as np
import jax
import jax.numpy as jnp
from jax import lax
from jax.experimental import pallas as pl
from jax.experimental.pallas import tpu as pltpu

F32 = jnp.float32
BF16 = jnp.bfloat16

D_MODEL = 1024
BATCH = 16
SEQ = 256
DEPTH = 4
DEC_BATCH = 4
DEC_SEQ = 1024
PAST_LEN = 512
GRID_W = 64
N_EVEN = (DEPTH + 1) // 2
N_ODD = DEPTH // 2
H_A = 4
DK_A = 128
DV_A = D_MODEL // 2 // H_A
F_A = H_A * DK_A
H_B = 8
KV_B = 2
G_B = H_B // KV_B
HD_B = D_MODEL // 2 // H_B
WINDOW = 128
H_C = 8
HD_C = D_MODEL // (2 * H_C)
D_FF = 4 * D_MODEL
ROPE_BASE = 10000.0
EPS = 1e-6
EVEN_COLS = 3 * F_A + 2 * H_A * DV_A + (H_B + 2 * KV_B) * HD_B
ODD_W = H_C * 2 * HD_C
D_MIX = D_MODEL

N_CTX_TOK = BATCH * SEQ
N_LAT_TOK = DEC_BATCH * DEC_SEQ
N_TOK = N_CTX_TOK + N_LAT_TOK
MOD_ROWS = 8
N_MOD = 6 * D_MODEL

HEAD_GROUP = 64
HGRN_CHUNK = 128
HGRN_LEVELS = 7
HGRN_SPLIT = 2
VMEM_LIMIT = 48 * 1024 * 1024

PROJ_TM = 1024
PROJ_TN = 512
EVEN_COLS_PAD = 7 * PROJ_TN
MLP_TM = 1024
MLP_TK = 512
ADA_TN = 1536
ATT_TQ = 256


def _silu(x):
    return x * jax.nn.sigmoid(x)


def _nt_dot(a, b):
    return lax.dot_general(a, b, (((1,), (1,)), ((), ())), preferred_element_type=F32)


def _params(sem):
    return pltpu.CompilerParams(dimension_semantics=sem, vmem_limit_bytes=VMEM_LIMIT)


def _aliased_call(kernel, *, grid, in_specs, args, out_specs, out_shape, carried, sem, name,
                  scratch_shapes=()):
    n_in = len(args)
    extra = [buf for buf in carried if buf is not None]
    aliases = {}
    for k, buf in enumerate(carried):
        if buf is not None:
            aliases[n_in + len(aliases)] = k
    n_extra = len(extra)

    def body(*refs):
        kernel(*refs[:n_in], *refs[n_in + n_extra:])

    return pl.pallas_call(
        body,
        grid=grid,
        in_specs=list(in_specs) + [pl.BlockSpec(memory_space=pl.ANY)] * n_extra,
        out_specs=out_specs,
        out_shape=out_shape,
        input_output_aliases=aliases,
        scratch_shapes=list(scratch_shapes),
        compiler_params=_params(sem),
        name=name,
    )(*args, *extra)


def _mods_kernel(c_ref, w_ref, b_ref, o_ref):
    s = _silu(c_ref[...]).astype(BF16)
    o_ref[...] = jnp.dot(s, w_ref[...].astype(BF16), preferred_element_type=F32) + b_ref[...]


def _mods_call(c_all, w_ada, b_ada):
    return pl.pallas_call(
        _mods_kernel,
        grid=(DEPTH, N_MOD // ADA_TN),
        in_specs=[
            pl.BlockSpec((MOD_ROWS, D_MODEL), lambda l, j: (0, 0)),
            pl.BlockSpec((None, D_MODEL, ADA_TN), lambda l, j: (l, 0, j)),
            pl.BlockSpec((None, 1, ADA_TN), lambda l, j: (l, 0, j)),
        ],
        out_specs=pl.BlockSpec((None, MOD_ROWS, ADA_TN), lambda l, j: (l, 0, j)),
        out_shape=jax.ShapeDtypeStruct((DEPTH, MOD_ROWS, N_MOD), F32),
        compiler_params=_params(("parallel", "parallel")),
        name="ada_mods",
    )(c_all, w_ada, b_ada)


def _mod_row(i, tm):
    n_ctx_tiles = N_CTX_TOK // tm
    tiles_per_lat = DEC_SEQ // tm
    return jnp.where(i >= n_ctx_tiles, 1 + (i - n_ctx_tiles) // tiles_per_lat, 0)


def _norm_mod(x, nw, sc, sh):
    ms = jnp.mean(x * x, axis=-1, keepdims=True)
    return (x * lax.rsqrt(ms + EPS) * nw) * (1.0 + sc) + sh


def _group_rms(y, w_t, bd_ref):
    yy = (y * y).astype(BF16)
    bw = bd_ref.shape[0]
    parts = [jnp.dot(yy[:, s:s + bw], bd_ref[...], preferred_element_type=F32)
             for s in range(0, y.shape[1], bw)]
    ss = jnp.concatenate(parts, axis=1)
    return y * lax.rsqrt(ss * (1.0 / HEAD_GROUP) + EPS) * w_t


def _rope(y, cos_ref, sin_ref):
    n = y.shape[1]
    lane = lax.broadcasted_iota(jnp.int32, y.shape, 1)
    nxt = pltpu.roll(y, n - 1, axis=1)
    prv = pltpu.roll(y, 1, axis=1)
    swapped = jnp.where((lane & 1) == 0, nxt, prv)
    return y * cos_ref[...] + swapped * sin_ref[...]


def _lower_bounds(lbl_ref, jl):
    rows = [lbl_ref[pl.ds(2 * m, 2), :] for m in range(N_EVEN)]
    mx = functools.reduce(jnp.maximum, rows)
    es = [jnp.exp(r - mx) for r in rows]
    den = functools.reduce(lambda a, b: a + b, es)
    sm = [e / den for e in es]
    cs = sm[0]
    for m in range(1, jl + 1):
        cs = cs + sm[m]
    return cs - sm[0]


def _proj_kernel(x_ref, mods_ref, nw_ref, w_ref, lbl_ref, qn_ref, kn_ref, cos_ref, sin_ref,
                 bd_ref, o_ref, h_scr, *, kinds, jl):
    i = pl.program_id(0)
    j = pl.program_id(1)
    tm = x_ref.shape[0]
    is_lat = i >= N_CTX_TOK // tm

    @pl.when(j == 0)
    def _():
        row = _mod_row(i, tm)
        sh = mods_ref[pl.ds(row, 1), 0:D_MODEL]
        sc = mods_ref[pl.ds(row, 1), D_MODEL:2 * D_MODEL]
        h_scr[...] = _norm_mod(x_ref[...], nw_ref[...], sc, sh).astype(BF16)

    y = jnp.dot(h_scr[...], w_ref[...], preferred_element_type=F32)

    def store_maybe_rope(val, scale):
        @pl.when(is_lat)
        def _():
            r = _rope(val, cos_ref, sin_ref)
            o_ref[...] = r * scale if scale != 1.0 else r

        @pl.when(jnp.logical_not(is_lat))
        def _():
            o_ref[...] = val * scale if scale != 1.0 else val

    def epilogue(kind):
        if kind == "silu_scale":
            o_ref[...] = _silu(y) * (DK_A ** -0.5)
        elif kind in ("loggate0", "loggate1"):
            d = int(kind[-1])
            lb = _lower_bounds(lbl_ref, jl)[d:d + 1, :]
            o_ref[...] = jnp.log(lb + (1.0 - lb) * jax.nn.sigmoid(y))
        elif kind == "ident":
            o_ref[...] = y
        elif kind == "silu":
            o_ref[...] = _silu(y)
        elif kind == "qnorm":
            store_maybe_rope(_group_rms(y, qn_ref[...], bd_ref), HEAD_GROUP ** -0.5)
        elif kind == "knorm":
            store_maybe_rope(_group_rms(y, kn_ref[...], bd_ref), 1.0)
        elif kind == "kv":
            kn = _group_rms(y, kn_ref[...], bd_ref)
            lane = lax.broadcasted_iota(jnp.int32, y.shape, 1)
            is_k = lane < KV_B * HD_B

            @pl.when(is_lat)
            def _():
                o_ref[...] = jnp.where(is_k, _rope(kn, cos_ref, sin_ref), y)

            @pl.when(jnp.logical_not(is_lat))
            def _():
                o_ref[...] = jnp.where(is_k, kn, y)
        else:
            raise ValueError(kind)

    for jj, kind in enumerate(kinds):
        pl.when(j == jj)(functools.partial(epilogue, kind))


def _proj_call(x, mods, li, norm_w, w, jl, lbl, qn_t, kn_t, cos_t, sin_t, bd, kinds, name):
    tm, tn = PROJ_TM, PROJ_TN
    assert tm == DEC_SEQ
    n_cols = w.shape[2]
    assert n_cols == tn * len(kinds)
    const = lambda i, j: (0, 0)
    return pl.pallas_call(
        functools.partial(_proj_kernel, kinds=kinds, jl=jl),
        grid=(N_TOK // tm, n_cols // tn),
        in_specs=[
            pl.BlockSpec((tm, D_MODEL), lambda i, j: (i, 0)),
            pl.BlockSpec((None, MOD_ROWS, N_MOD), lambda i, j: (li, 0, 0)),
            pl.BlockSpec((None, None, 1, D_MODEL), lambda i, j: (li, 0, 0, 0)),
            pl.BlockSpec((None, D_MODEL, tn), lambda i, j: (jl, 0, j)),
            pl.BlockSpec(lbl.shape, const),
            pl.BlockSpec((1, tn), const),
            pl.BlockSpec((1, tn), const),
            pl.BlockSpec((tm, tn), const),
            pl.BlockSpec((tm, tn), const),
            pl.BlockSpec(bd.shape, const),
        ],
        out_specs=pl.BlockSpec((tm, tn), lambda i, j: (i, j)),
        out_shape=jax.ShapeDtypeStruct((N_TOK, n_cols), F32),
        scratch_shapes=[pltpu.VMEM((tm, D_MODEL), BF16)],
        compiler_params=_params(("parallel", "arbitrary")),
        name=name,
    )(x, mods, norm_w, w, lbl, qn_t, kn_t, cos_t, sin_t, bd)


def _hgrn_tables(rev):
    c = HGRN_CHUNK
    t = np.arange(c)
    w = np.zeros((2 + HGRN_LEVELS, c, c), np.float32)
    if not rev:
        w[0] = t[None, :] <= t[:, None]
        w[1] = t[None, :] > t[:, None]
    else:
        w[0] = t[None, :] >= t[:, None]
        w[1] = t[None, :] < t[:, None]
    for l in range(HGRN_LEVELS):
        hb = 1 << l
        for ti in range(c):
            mid = ti - ti % (2 * hb) + hb
            if not rev:
                if ti >= mid:
                    w[2 + l, ti, mid:ti + 1] = 1.0
                else:
                    w[2 + l, ti, ti + 1:mid] = 1.0
            else:
                if ti < mid:
                    w[2 + l, ti, ti:mid] = 1.0
                else:
                    w[2 + l, ti, mid:ti] = 1.0
    w = w.reshape((2 + HGRN_LEVELS) * c, c)
    ws = np.concatenate([w] * HGRN_SPLIT, axis=1)
    x = t[:, None] ^ t[None, :]
    lv = np.where(x > 0, np.floor(np.log2(np.maximum(x, 1))).astype(np.int32), HGRN_LEVELS)
    causal = (t[None, :] < t[:, None]) if not rev else (t[None, :] > t[:, None])
    lv = np.where(causal | (x == 0), lv, -1).astype(np.int32)
    return jnp.asarray(ws, BF16), jnp.asarray(lv)


def _hgrn_chunk(q, g, v, st, w_ref, lv_ref, rev):
    c = HGRN_CHUNK
    terms = []
    rem = g
    for _ in range(HGRN_SPLIT):
        term = rem.astype(BF16)
        terms.append(term)
        rem = rem - term.astype(F32)
    gs = jnp.concatenate(terms, axis=0)
    z = jnp.exp(jnp.dot(w_ref[...], gs, preferred_element_type=F32))
    k = 1.0 - jnp.exp(g)
    zq = z[0:c]
    zk = z[c:2 * c]
    last = 0 if rev else c - 1
    total = zq[last:last + 1, :]
    qd = (q * zq).astype(BF16)
    kd = (k * zk).astype(BF16)
    vb = v.astype(BF16)
    o = _nt_dot(qd, st.astype(BF16))
    st_new = st * total + jnp.dot(v.T.astype(BF16), kd, preferred_element_type=F32)
    lv = lv_ref[...]
    row = lax.broadcasted_iota(jnp.int32, (c, DK_A), 0)
    a = jnp.where(lv == HGRN_LEVELS, jnp.sum(q * k, axis=-1, keepdims=True), 0.0)
    for l in range(HGRN_LEVELS):
        zl = z[(2 + l) * c:(3 + l) * c]
        bit = ((row >> l) & 1) == 1
        q_role = jnp.logical_not(bit) if rev else bit
        ql = jnp.where(q_role, q * zl, 0.0).astype(BF16)
        kl = jnp.where(q_role, 0.0, k * zl).astype(BF16)
        a = jnp.where(lv == l, _nt_dot(ql, kl), a)
    o = o + jnp.dot(a.astype(BF16), vb, preferred_element_type=F32)
    return o, st_new


def _hgrn_kernel(*refs, n_chunks, has_init, emit_state):
    refs = list(refs)
    q_ref, gf_ref, gb_ref, v_ref, sg_ref, nw_ref, wf_ref, wb_ref, lvf_ref, lvb_ref = refs[:10]
    pos = 10
    s0_ref = None
    if has_init:
        s0_ref = refs[pos]
        pos += 1
    o_ref = refs[pos]
    pos += 1
    so_ref = None
    if emit_state:
        so_ref = refs[pos]
        pos += 1
    of_scr, ob_scr, st_scr = refs[pos:pos + 3]

    for d in range(2):
        if has_init:
            st_scr[d] = s0_ref[d].T
        else:
            st_scr[d] = jnp.zeros((DV_A, DK_A), F32)

    def body(c, carry):
        for d, (g_ref, w_ref, lv_ref, scr) in enumerate(
                ((gf_ref, wf_ref, lvf_ref, of_scr), (gb_ref, wb_ref, lvb_ref, ob_scr))):
            cc = c if d == 0 else n_chunks - 1 - c
            r0 = pl.multiple_of(cc * HGRN_CHUNK, HGRN_CHUNK)
            rows = pl.ds(r0, HGRN_CHUNK)
            o, st = _hgrn_chunk(q_ref[rows, :], g_ref[rows, :], v_ref[rows, :], st_scr[d],
                                w_ref, lv_ref, rev=(d == 1))
            st_scr[d] = st
            scr[rows, :] = o
        return carry

    lax.fori_loop(0, n_chunks, body, 0)
    o = of_scr[...] + ob_scr[...]
    y = o * lax.rsqrt(jnp.mean(o * o, axis=-1, keepdims=True) + EPS) * nw_ref[...]
    o_ref[...] = (y * sg_ref[...]).astype(o_ref.dtype)
    if emit_state:
        for d in range(2):
            so_ref[d] = st_scr[d].T


def _hgrn_call(p, nw, tabs, seq_len, n_seq, row_blk0, s0, jl, mix, state_out, name):
    wf, lvf, wb, lvb = tabs
    has_init = s0 is not None
    emit_state = s0 is None
    n_hcols = F_A // DK_A

    def col(part):
        return lambda b, h: (row_blk0 + b, part * n_hcols + h)

    const = lambda b, h: (0, 0)
    blk = (seq_len, DK_A)
    state_spec = pl.BlockSpec((None, None, 2, None, DK_A, DV_A), lambda b, h: (b, jl, 0, h, 0, 0))
    in_specs = [pl.BlockSpec(blk, col(part)) for part in range(5)]
    in_specs += [
        pl.BlockSpec((None, 1, DV_A), lambda b, h: (jl, 0, 0)),
        pl.BlockSpec(wf.shape, const), pl.BlockSpec(wb.shape, const),
        pl.BlockSpec(lvf.shape, const), pl.BlockSpec(lvb.shape, const),
    ]
    args = [p, p, p, p, p, nw, wf, wb, lvf, lvb]
    if has_init:
        in_specs.append(state_spec)
        args.append(s0)
    out_shape = [jax.ShapeDtypeStruct((N_TOK, D_MIX), BF16)]
    out_specs = [pl.BlockSpec((seq_len, DV_A), lambda b, h: (row_blk0 + b, h))]
    carried = [mix]
    if emit_state:
        out_shape.append(jax.ShapeDtypeStruct((BATCH, N_EVEN, 2, H_A, DK_A, DV_A), F32))
        out_specs.append(state_spec)
        carried.append(state_out)
    return _aliased_call(
        functools.partial(_hgrn_kernel, n_chunks=seq_len // HGRN_CHUNK, has_init=has_init,
                          emit_state=emit_state),
        grid=(n_seq, H_A), in_specs=in_specs, args=args, out_specs=out_specs,
        out_shape=out_shape, carried=carried, sem=("parallel", "parallel"), name=name,
        scratch_shapes=[pltpu.VMEM((seq_len, DV_A), F32), pltpu.VMEM((seq_len, DV_A), F32),
                        pltpu.VMEM((2, DV_A, DK_A), F32)])


def _swa_ctx_kernel(q_ref, kv_ref, sink_ref, o_ref, kc_ref, vc_ref):
    kv = kv_ref[...]
    k32 = kv[:, 0:KV_B * HD_B]
    v32 = kv[:, KV_B * HD_B:2 * KV_B * HD_B]
    kc_ref[...] = k32
    vc_ref[...] = v32
    q = q_ref[...].astype(BF16)
    k = k32.astype(BF16)
    v = v32.astype(BF16)
    outs = []
    for h in range(H_B):
        n = h // G_B
        s = _nt_dot(q[:, h * HD_B:(h + 1) * HD_B], k[:, n * HD_B:(n + 1) * HD_B])
        sink = sink_ref[0:1, h:h + 1]
        m = jnp.maximum(jnp.max(s, axis=-1, keepdims=True), sink)
        p = jnp.exp(s - m)
        den = jnp.sum(p, axis=-1, keepdims=True) + jnp.exp(sink - m)
        pv = jnp.dot(p.astype(BF16), v[:, n * HD_B:(n + 1) * HD_B], preferred_element_type=F32)
        outs.append(pv / den)
    o_ref[...] = jnp.concatenate(outs, axis=1).astype(o_ref.dtype)


def _swa_ctx_call(p, sink, jl, mix, k_out, v_out, name):
    qcol = (3 * F_A + 2 * H_A * DV_A) // PROJ_TN
    cache_spec = pl.BlockSpec((None, None, SEQ, KV_B * HD_B), lambda b: (b, jl, 0, 0))
    cache_shape = jax.ShapeDtypeStruct((BATCH, N_EVEN, SEQ, KV_B * HD_B), F32)
    return _aliased_call(
        _swa_ctx_kernel,
        grid=(BATCH,),
        in_specs=[
            pl.BlockSpec((SEQ, H_B * HD_B), lambda b: (b, qcol)),
            pl.BlockSpec((SEQ, PROJ_TN), lambda b: (b, qcol + 1)),
            pl.BlockSpec((None, 1, H_B), lambda b: (jl, 0, 0)),
        ],
        args=[p, p, sink],
        out_specs=[pl.BlockSpec((SEQ, H_B * HD_B), lambda b: (b, 1)), cache_spec, cache_spec],
        out_shape=[jax.ShapeDtypeStruct((N_TOK, D_MIX), BF16), cache_shape, cache_shape],
        carried=[mix, k_out, v_out], sem=("parallel",), name=name)


def _swa_lat_kernel(q_ref, kv_ref, ck_ref, cv_ref, sink_ref, o_ref):
    qi = pl.program_id(1)
    tq = q_ref.shape[0]
    span = tq + 2 * WINDOW
    ws = pl.multiple_of(jnp.clip(qi * tq - WINDOW, 0, DEC_SEQ - span), WINDOW)
    kvw = kv_ref[pl.ds(ws, span), :]
    kw = kvw[:, 0:KV_B * HD_B].astype(BF16)
    vw = kvw[:, KV_B * HD_B:2 * KV_B * HD_B].astype(BF16)
    kc = ck_ref[...].astype(BF16)
    vc = cv_ref[...].astype(BF16)
    q = q_ref[...].astype(BF16)
    t_q = qi * tq + lax.broadcasted_iota(jnp.int32, (tq, span), 0)
    t_k = ws + lax.broadcasted_iota(jnp.int32, (tq, span), 1)
    valid = jnp.abs(t_q - t_k) <= WINDOW
    outs = []
    for h in range(H_B):
        n = h // G_B
        qh = q[:, h * HD_B:(h + 1) * HD_B]
        ksl = slice(n * HD_B, (n + 1) * HD_B)
        s_w = jnp.where(valid, _nt_dot(qh, kw[:, ksl]), -jnp.inf)
        s_c = _nt_dot(qh, kc[:, ksl])
        sink = sink_ref[0:1, h:h + 1]
        m = jnp.maximum(jnp.maximum(jnp.max(s_w, axis=-1, keepdims=True),
                                    jnp.max(s_c, axis=-1, keepdims=True)), sink)
        p_w = jnp.exp(s_w - m)
        p_c = jnp.exp(s_c - m)
        den = (jnp.sum(p_w, axis=-1, keepdims=True) + jnp.sum(p_c, axis=-1, keepdims=True)
               + jnp.exp(sink - m))
        pv = (jnp.dot(p_w.astype(BF16), vw[:, ksl], preferred_element_type=F32)
              + jnp.dot(p_c.astype(BF16), vc[:, ksl], preferred_element_type=F32))
        outs.append(pv / den)
    o_ref[...] = jnp.concatenate(outs, axis=1).astype(o_ref.dtype)


def _swa_lat_call(p, ck, cv, sink, jl, mix, name):
    tq = ATT_TQ
    qcol = (3 * F_A + 2 * H_A * DV_A) // PROJ_TN
    nq = DEC_SEQ // tq
    q_blk0 = N_CTX_TOK // tq
    s_blk0 = N_CTX_TOK // DEC_SEQ
    cache_spec = pl.BlockSpec((None, None, PAST_LEN, KV_B * HD_B), lambda b, i: (b, jl, 0, 0))
    (out,) = _aliased_call(
        _swa_lat_kernel,
        grid=(DEC_BATCH, nq),
        in_specs=[
            pl.BlockSpec((tq, H_B * HD_B), lambda b, i: (q_blk0 + b * nq + i, qcol)),
            pl.BlockSpec((DEC_SEQ, PROJ_TN), lambda b, i: (s_blk0 + b, qcol + 1)),
            cache_spec, cache_spec,
            pl.BlockSpec((None, 1, H_B), lambda b, i: (jl, 0, 0)),
        ],
        args=[p, p, ck, cv, sink],
        out_specs=[pl.BlockSpec((tq, H_B * HD_B), lambda b, i: (q_blk0 + b * nq + i, 1))],
        out_shape=[jax.ShapeDtypeStruct((N_TOK, D_MIX), BF16)],
        carried=[mix], sem=("parallel", "parallel"), name=name)
    return out


def _diff_lambda(lp_ref, lam_init):
    lp = lp_ref[...]
    a = jnp.sum(lp[0:1] * lp[1:2], axis=-1, keepdims=True)
    b = jnp.sum(lp[2:3] * lp[3:4], axis=-1, keepdims=True)
    return jnp.exp(a) - jnp.exp(b) + lam_init


def _softmax_parts(parts):
    m = functools.reduce(jnp.maximum, [jnp.max(s, axis=-1, keepdims=True) for s in parts])
    es = [jnp.exp(s - m) for s in parts]
    den = functools.reduce(lambda a, b: a + b, [jnp.sum(e, axis=-1, keepdims=True) for e in es])
    inv = 1.0 / den
    return [e * inv for e in es]


def _diff_core(q, k_parts, v_parts, lam, lam_init, sw):
    pr = []
    for c in range(2):
        sl = slice(c * HD_C, (c + 1) * HD_C)
        pr.append(_softmax_parts([_nt_dot(q[:, sl], kp[:, sl]) for kp in k_parts]))
    o = None
    for i, vp in enumerate(v_parts):
        w = (pr[0][i] - lam * pr[1][i]).astype(BF16)
        t = jnp.dot(w, vp, preferred_element_type=F32)
        o = t if o is None else o + t
    y = o * lax.rsqrt(jnp.mean(o * o, axis=-1, keepdims=True) + EPS) * sw
    return y * (1.0 - lam_init)


def _diff_ctx_kernel(q_ref, k_ref, v_ref, lp_ref, sw_ref, o_ref, kc_ref, vc_ref, *, lam_init):
    lam = _diff_lambda(lp_ref, lam_init)
    hw = 2 * HD_C
    for h in range(H_C):
        sl = slice(h * hw, (h + 1) * hw)
        k32 = k_ref[:, sl]
        v32 = v_ref[:, sl]
        kc_ref[:, sl] = k32
        vc_ref[:, sl] = v32
        y = _diff_core(q_ref[:, sl].astype(BF16), [k32.astype(BF16)], [v32.astype(BF16)], lam,
                       lam_init, sw_ref[...])
        o_ref[:, sl] = y.astype(o_ref.dtype)


def _diff_ctx_call(p, lp, sw, lam_init, jl, k_out, v_out, name):
    cache_spec = pl.BlockSpec((None, None, SEQ, ODD_W), lambda b: (b, jl, 0, 0))
    cache_shape = jax.ShapeDtypeStruct((BATCH, N_ODD, SEQ, ODD_W), F32)
    return _aliased_call(
        functools.partial(_diff_ctx_kernel, lam_init=lam_init),
        grid=(BATCH,),
        in_specs=[
            pl.BlockSpec((SEQ, ODD_W), lambda b: (b, 0)),
            pl.BlockSpec((SEQ, ODD_W), lambda b: (b, 1)),
            pl.BlockSpec((SEQ, ODD_W), lambda b: (b, 2)),
            pl.BlockSpec((None, 4, HD_C), lambda b: (jl, 0, 0)),
            pl.BlockSpec((None, 1, 2 * HD_C), lambda b: (jl, 0, 0)),
        ],
        args=[p, p, p, lp, sw],
        out_specs=[pl.BlockSpec((SEQ, ODD_W), lambda b: (b, 0)), cache_spec, cache_spec],
        out_shape=[jax.ShapeDtypeStruct((N_TOK, D_MIX), BF16), cache_shape, cache_shape],
        carried=[None, k_out, v_out], sem=("parallel",), name=name)


def _diff_lat_kernel(q_ref, k_ref, v_ref, ck_ref, cv_ref, lp_ref, sw_ref, o_ref, *, lam_init):
    lam = _diff_lambda(lp_ref, lam_init)
    k_parts = [k_ref[...].astype(BF16), ck_ref[...].astype(BF16)]
    v_parts = [v_ref[...].astype(BF16), cv_ref[...].astype(BF16)]
    tq = ATT_TQ
    for i in range(q_ref.shape[0] // tq):
        rows = slice(i * tq, (i + 1) * tq)
        y = _diff_core(q_ref[rows, :].astype(BF16), k_parts, v_parts, lam, lam_init, sw_ref[...])
        o_ref[rows, :] = y.astype(o_ref.dtype)


def _diff_lat_call(p, ck, cv, lp, sw, lam_init, jl, mix, name):
    hw = 2 * HD_C
    s_blk0 = N_CTX_TOK // DEC_SEQ
    cache_spec = pl.BlockSpec((None, None, PAST_LEN, hw), lambda b, h: (b, jl, 0, h))
    (out,) = _aliased_call(
        functools.partial(_diff_lat_kernel, lam_init=lam_init),
        grid=(DEC_BATCH, H_C),
        in_specs=[
            pl.BlockSpec((DEC_SEQ, hw), lambda b, h: (s_blk0 + b, h)),
            pl.BlockSpec((DEC_SEQ, hw), lambda b, h: (s_blk0 + b, H_C + h)),
            pl.BlockSpec((DEC_SEQ, hw), lambda b, h: (s_blk0 + b, 2 * H_C + h)),
            cache_spec, cache_spec,
            pl.BlockSpec((None, 4, HD_C), lambda b, h: (jl, 0, 0)),
            pl.BlockSpec((None, 1, hw), lambda b, h: (jl, 0, 0)),
        ],
        args=[p, p, p, ck, cv, lp, sw],
        out_specs=[pl.BlockSpec((DEC_SEQ, hw), lambda b, h: (s_blk0 + b, h))],
        out_shape=[jax.ShapeDtypeStruct((N_TOK, D_MIX), BF16)],
        carried=[mix], sem=("parallel", "parallel"), name=name)
    return out


def _mlp_kernel(x_ref, mix_ref, wo_ref, mods_ref, nw_ref, w1_ref, w2_ref, o_ref,
                x1_scr, h_scr, acc_scr):
    i = pl.program_id(0)
    k = pl.program_id(1)
    row = _mod_row(i, x_ref.shape[0])

    def mod(a):
        return mods_ref[pl.ds(row, 1), a * D_MODEL:(a + 1) * D_MODEL]

    @pl.when(k == 0)
    def _():
        y = jnp.dot(mix_ref[...], wo_ref[...], preferred_element_type=F32)
        x1 = x_ref[...] + mod(2) * y
        x1_scr[...] = x1
        h_scr[...] = _norm_mod(x1, nw_ref[...], mod(4), mod(3)).astype(BF16)
        acc_scr[...] = jnp.zeros_like(acc_scr)

    u = jnp.dot(h_scr[...], w1_ref[...], preferred_element_type=F32)
    u = jnp.square(jnp.maximum(u, 0.0)).astype(BF16)
    acc_scr[...] += jnp.dot(u, w2_ref[...], preferred_element_type=F32)

    @pl.when(k == pl.num_programs(1) - 1)
    def _():
        o_ref[...] = x1_scr[...] + mod(5) * acc_scr[...]


def _mlp_call(x, mix, wo, jl, mods, li, norm_w, w1, w2, name):
    tm, tk = MLP_TM, MLP_TK
    return pl.pallas_call(
        _mlp_kernel,
        grid=(N_TOK // tm, D_FF // tk),
        in_specs=[
            pl.BlockSpec((tm, D_MODEL), lambda i, k: (i, 0)),
            pl.BlockSpec((tm, D_MIX), lambda i, k: (i, 0)),
            pl.BlockSpec((None, D_MIX, D_MODEL), lambda i, k: (jl, 0, 0)),
            pl.BlockSpec((None, MOD_ROWS, N_MOD), lambda i, k: (li, 0, 0)),
            pl.BlockSpec((None, None, 1, D_MODEL), lambda i, k: (li, 1, 0, 0)),
            pl.BlockSpec((None, D_MODEL, tk), lambda i, k: (li, 0, k)),
            pl.BlockSpec((None, tk, D_MODEL), lambda i, k: (li, k, 0)),
        ],
        out_specs=pl.BlockSpec((tm, D_MODEL), lambda i, k: (i, 0)),
        out_shape=jax.ShapeDtypeStruct((N_TOK, D_MODEL), F32),
        scratch_shapes=[pltpu.VMEM((tm, D_MODEL), F32), pltpu.VMEM((tm, D_MODEL), BF16),
                        pltpu.VMEM((tm, D_MODEL), F32)],
        compiler_params=_params(("parallel", "arbitrary")),
        name=name,
    )(x, mix, wo, mods, norm_w, w1, w2)


def _rope_tables(width):
    t = np.arange(DEC_SEQ)
    half = HEAD_GROUP // 2
    inv = ROPE_BASE ** (-np.arange(0, half, 2, dtype=np.float64) / half)
    ang = np.concatenate([(t // GRID_W)[:, None] * inv, (t % GRID_W)[:, None] * inv], axis=-1)
    cos = np.repeat(np.cos(ang), 2, axis=-1)
    sin = np.repeat(np.sin(ang), 2, axis=-1)
    sign = np.tile(np.array([-1.0, 1.0]), HEAD_GROUP // 2)
    reps = width // HEAD_GROUP
    return (jnp.asarray(np.tile(cos, (1, reps)), F32),
            jnp.asarray(np.tile(sin * sign, (1, reps)), F32))


def _block_diag_ones(n):
    g = np.arange(n) // HEAD_GROUP
    return jnp.asarray(g[:, None] == g[None, :], BF16)


def _tile_row(w, width):
    return jnp.tile(w.astype(F32), width // w.shape[0])[None, :]


def _lambda_init(li):
    return 0.8 - 0.6 * math.exp(-0.3 * li)


def kernel(x_prompt, x_sample, cache_k_swa, cache_v_swa, state_hgrn, cache_k_diff, cache_v_diff, c, c_ctx, norm_w, w_ada, b_ada, w_in_even, w_out_even, hgrn_lb_logits, hgrn_norm_w, swa_qnorm_w, swa_knorm_w, swa_sink, w_in_odd, w_out_odd, diff_qnorm_w, diff_knorm_w, diff_lambda_p, diff_subln_w, w_mlp1, w_mlp2):
    x = jnp.concatenate([x_prompt.reshape(N_CTX_TOK, D_MODEL),
                         x_sample.reshape(N_LAT_TOK, D_MODEL)], axis=0)
    c_all = jnp.concatenate(
        [c_ctx[None, :], c, jnp.zeros((MOD_ROWS - 1 - DEC_BATCH, D_MODEL), F32)], axis=0)
    mods = _mods_call(c_all, w_ada, b_ada.reshape(DEPTH, 1, N_MOD))

    cos_t, sin_t = _rope_tables(PROJ_TN)
    bd = _block_diag_ones(256)
    tabs_f = _hgrn_tables(False)
    tabs_b = _hgrn_tables(True)
    hgrn_tabs = (tabs_f[0], tabs_f[1], tabs_b[0], tabs_b[1])
    lbl = hgrn_lb_logits.astype(F32).reshape(N_EVEN * 2, F_A)
    norm_w4 = norm_w.astype(F32).reshape(DEPTH, 2, 1, D_MODEL)

    w_in_even_b = jnp.pad(w_in_even.astype(BF16), ((0, 0), (0, 0), (0, EVEN_COLS_PAD - EVEN_COLS)))
    w_in_odd_b = w_in_odd.astype(BF16)
    w_out_even_b = w_out_even.astype(BF16)
    w_out_odd_b = w_out_odd.astype(BF16)
    w1_b = w_mlp1.astype(BF16)
    w2_b = w_mlp2.astype(BF16)

    ck_swa = cache_k_swa.reshape(DEC_BATCH, N_EVEN, PAST_LEN, KV_B * HD_B)
    cv_swa = cache_v_swa.reshape(DEC_BATCH, N_EVEN, PAST_LEN, KV_B * HD_B)
    ck_diff = cache_k_diff.reshape(DEC_BATCH, N_ODD, PAST_LEN, ODD_W)
    cv_diff = cache_v_diff.reshape(DEC_BATCH, N_ODD, PAST_LEN, ODD_W)
    hgrn_nw = hgrn_norm_w.astype(F32).reshape(N_EVEN, 1, DV_A)
    sink = swa_sink.astype(F32).reshape(N_EVEN, 1, H_B)
    lam_p = diff_lambda_p.astype(F32)
    subln = diff_subln_w.astype(F32).reshape(N_ODD, 1, 2 * HD_C)

    even_kinds = ("silu_scale", "loggate0", "loggate1", "ident", "silu", "qnorm", "kv")
    odd_kinds = ("qnorm", "qnorm", "knorm", "knorm", "ident", "ident")

    k_swa = v_swa = states = k_diff = v_diff = None
    for li in range(DEPTH):
        j = li // 2
        if li % 2 == 0:
            p = _proj_call(x, mods, li, norm_w4, w_in_even_b, j, lbl,
                           _tile_row(swa_qnorm_w[j], PROJ_TN), _tile_row(swa_knorm_w[j], PROJ_TN),
                           cos_t, sin_t, bd, even_kinds, f"proj_even{j}")
            mix, states = _hgrn_call(p, hgrn_nw, hgrn_tabs, SEQ, BATCH, 0, None, j, None, states,
                                     f"hgrn_ctx{j}")
            (mix,) = _hgrn_call(p, hgrn_nw, hgrn_tabs, DEC_SEQ, DEC_BATCH, N_CTX_TOK // DEC_SEQ,
                                state_hgrn, j, mix, None, f"hgrn_lat{j}")
            mix, k_swa, v_swa = _swa_ctx_call(p, sink, j, mix, k_swa, v_swa, f"swa_ctx{j}")
            mix = _swa_lat_call(p, ck_swa, cv_swa, sink, j, mix, f"swa_lat{j}")
            wo = w_out_even_b
        else:
            p = _proj_call(x, mods, li, norm_w4, w_in_odd_b, j, lbl,
                           _tile_row(diff_qnorm_w[j], PROJ_TN), _tile_row(diff_knorm_w[j], PROJ_TN),
                           cos_t, sin_t, bd, odd_kinds, f"proj_odd{j}")
            lam_init = _lambda_init(li)
            mix, k_diff, v_diff = _diff_ctx_call(p, lam_p, subln, lam_init, j, k_diff, v_diff,
                                                 f"diff_ctx{j}")
            mix = _diff_lat_call(p, ck_diff, cv_diff, lam_p, subln, lam_init, j, mix,
                                 f"diff_lat{j}")
            wo = w_out_odd_b
        x = _mlp_call(x, mix, wo, j, mods, li, norm_w4, w1_b, w2_b, f"mlp{li}")

    y_prompt = x[:N_CTX_TOK].reshape(BATCH, SEQ, D_MODEL)
    y_sample = x[N_CTX_TOK:].reshape(DEC_BATCH, DEC_SEQ, D_MODEL)
    return (y_prompt, y_sample,
            k_swa.reshape(BATCH, N_EVEN, SEQ, KV_B, HD_B),
            v_swa.reshape(BATCH, N_EVEN, SEQ, KV_B, HD_B),
            states,
            k_diff.reshape(BATCH, N_ODD, SEQ, H_C, 2, HD_C),
            v_diff.reshape(BATCH, N_ODD, SEQ, H_C, 2 * HD_C))
```

```python
import functools
import math

import numpy as np
import jax
import jax.numpy as jnp
from jax import lax
from jax.experimental import pallas as pl
from jax.experimental.pallas import tpu as pltpu

F32 = jnp.float32
BF16 = jnp.bfloat16

D_MODEL = 1024
BATCH = 16
SEQ = 256
DEPTH = 4
DEC_BATCH = 4
DEC_SEQ = 1024
PAST_LEN = 512
GRID_W = 64
N_EVEN = (DEPTH + 1) // 2
N_ODD = DEPTH // 2
H_A = 4
DK_A = 128
DV_A = D_MODEL // 2 // H_A
F_A = H_A * DK_A
H_B = 8
KV_B = 2
G_B = H_B // KV_B
HD_B = D_MODEL // 2 // H_B
WINDOW = 128
H_C = 8
HD_C = D_MODEL // (2 * H_C)
D_FF = 4 * D_MODEL
ROPE_BASE = 10000.0
EPS = 1e-6
EVEN_COLS = 3 * F_A + 2 * H_A * DV_A + (H_B + 2 * KV_B) * HD_B
ODD_W = H_C * 2 * HD_C
D_MIX = D_MODEL

N_CTX_TOK = BATCH * SEQ
N_LAT_TOK = DEC_BATCH * DEC_SEQ
N_TOK = N_CTX_TOK + N_LAT_TOK
MOD_ROWS = 8
N_MOD = 6 * D_MODEL

HEAD_GROUP = 64
HGRN_CHUNK = 128
HGRN_LEVELS = 7
HGRN_SPLIT = 2
VMEM_LIMIT = 48 * 1024 * 1024

PROJ_TM = 1024
PROJ_TN = 512
PROJ_RC = 256
MLP_TM = 1024
MLP_TK = 512
ADA_TN = 1536
ATT_TQ = 256


def _silu(x):
    return x * jax.nn.sigmoid(x)


def _nt_dot(a, b):
    return lax.dot_general(a, b, (((1,), (1,)), ((), ())), preferred_element_type=F32)


def _params(sem):
    return pltpu.CompilerParams(dimension_semantics=sem, vmem_limit_bytes=VMEM_LIMIT)


def _aliased_call(kernel, *, grid, in_specs, args, out_specs, out_shape, carried, sem, name,
                  scratch_shapes=()):
    n_in = len(args)
    extra = [buf for buf in carried if buf is not None]
    aliases = {}
    for k, buf in enumerate(carried):
        if buf is not None:
            aliases[n_in + len(aliases)] = k
    n_extra = len(extra)

    def body(*refs):
        kernel(*refs[:n_in], *refs[n_in + n_extra:])

    return pl.pallas_call(
        body,
        grid=grid,
        in_specs=list(in_specs) + [pl.BlockSpec(memory_space=pl.ANY)] * n_extra,
        out_specs=out_specs,
        out_shape=out_shape,
        input_output_aliases=aliases,
        scratch_shapes=list(scratch_shapes),
        compiler_params=_params(sem),
        name=name,
    )(*args, *extra)


def _mods_kernel(c_ref, w_ref, b_ref, o_ref):
    s = _silu(c_ref[...]).astype(BF16)
    o_ref[...] = jnp.dot(s, w_ref[...].astype(BF16), preferred_element_type=F32) + b_ref[...]


def _mods_call(c_all, w_ada, b_ada):
    return pl.pallas_call(
        _mods_kernel,
        grid=(DEPTH, N_MOD // ADA_TN),
        in_specs=[
            pl.BlockSpec((MOD_ROWS, D_MODEL), lambda l, j: (0, 0)),
            pl.BlockSpec((None, D_MODEL, ADA_TN), lambda l, j: (l, 0, j)),
            pl.BlockSpec((None, 1, ADA_TN), lambda l, j: (l, 0, j)),
        ],
        out_specs=pl.BlockSpec((None, MOD_ROWS, ADA_TN), lambda l, j: (l, 0, j)),
        out_shape=jax.ShapeDtypeStruct((DEPTH, MOD_ROWS, N_MOD), F32),
        compiler_params=_params(("parallel", "parallel")),
        name="ada_mods",
    )(c_all, w_ada, b_ada)


def _norm_mod(x, nw, sc, sh):
    ms = jnp.mean(x * x, axis=-1, keepdims=True)
    return (x * lax.rsqrt(ms + EPS) * nw) * (1.0 + sc) + sh


def _group_rms(y, w_t, bd_ref):
    yy = (y * y).astype(BF16)
    bw = bd_ref.shape[0]
    parts = [jnp.dot(yy[:, s:s + bw], bd_ref[...], preferred_element_type=F32)
             for s in range(0, y.shape[1], bw)]
    ss = parts[0] if len(parts) == 1 else jnp.concatenate(parts, axis=1)
    return y * lax.rsqrt(ss * (1.0 / HEAD_GROUP) + EPS) * w_t


def _rope(y, cos, sin):
    n = y.shape[1]
    lane = lax.broadcasted_iota(jnp.int32, y.shape, 1)
    nxt = pltpu.roll(y, n - 1, axis=1)
    prv = pltpu.roll(y, 1, axis=1)
    swapped = jnp.where((lane & 1) == 0, nxt, prv)
    return y * cos + swapped * sin


def _lower_bounds(lbl_ref, jl):
    rows = [lbl_ref[pl.ds(2 * m, 2), :] for m in range(N_EVEN)]
    mx = functools.reduce(jnp.maximum, rows)
    es = [jnp.exp(r - mx) for r in rows]
    den = functools.reduce(lambda a, b: a + b, es)
    sm = [e / den for e in es]
    cs = sm[0]
    for m in range(1, jl + 1):
        cs = cs + sm[m]
    return cs - sm[0]


def _proj_kernel(x_ref, mods_ref, nw_ref, w_ref, wkv_ref, lbl_ref, qn_ref, kn_ref, cos_ref,
                 sin_ref, bd_ref, o_ref, h_scr, w_scr, *, kinds, jl, lat):
    i = pl.program_id(0)
    j = pl.program_id(1)
    tm, tn = o_ref.shape
    kvw = 2 * KV_B * HD_B

    @pl.when(j == 0)
    def _():
        row = 1 + i if lat else 0
        sh = mods_ref[pl.ds(row, 1), 0:D_MODEL]
        sc = mods_ref[pl.ds(row, 1), D_MODEL:2 * D_MODEL]
        h_scr[...] = _norm_mod(x_ref[...], nw_ref[...], sc, sh).astype(BF16)

    def finish(kind, y, rows):
        if kind == "silu_scale":
            return _silu(y) * (DK_A ** -0.5)
        if kind in ("loggate0", "loggate1"):
            d = int(kind[-1])
            lb = _lower_bounds(lbl_ref, jl)[d:d + 1, :]
            return jnp.log(lb + (1.0 - lb) * jax.nn.sigmoid(y))
        if kind == "ident":
            return y
        if kind == "silu":
            return _silu(y)
        if kind in ("qnorm", "knorm"):
            w_t = qn_ref[...] if kind == "qnorm" else kn_ref[...]
            r = _group_rms(y, w_t, bd_ref)
            if lat:
                r = _rope(r, cos_ref[rows, :], sin_ref[rows, :])
            return r * (HEAD_GROUP ** -0.5) if kind == "qnorm" else r
        if kind == "kv":
            kn = _group_rms(y, kn_ref[:, 0:kvw], bd_ref)
            if lat:
                kn = _rope(kn, cos_ref[rows, 0:kvw], sin_ref[rows, 0:kvw])
            lane = lax.broadcasted_iota(jnp.int32, y.shape, 1)
            return jnp.where(lane < KV_B * HD_B, kn, y)
        raise ValueError(kind)

    def run(kind):
        if kind == "kv":
            w_scr[:, 0:kvw] = wkv_ref[...].astype(BF16)
        else:
            w_scr[...] = w_ref[...].astype(BF16)
        for r in range(tm // PROJ_RC):
            rows = slice(r * PROJ_RC, (r + 1) * PROJ_RC)
            w = w_scr[:, 0:kvw] if kind == "kv" else w_scr[...]
            y = jnp.dot(h_scr[rows, :], w, preferred_element_type=F32)
            if kind == "kv":
                o_ref[rows, 0:kvw] = finish(kind, y, rows)
                o_ref[rows, kvw:tn] = jnp.zeros((PROJ_RC, tn - kvw), F32)
            else:
                o_ref[rows, :] = finish(kind, y, rows)

    for jj, kind in enumerate(kinds):
        pl.when(j == jj)(functools.partial(run, kind))


def _proj_call(x_src, x_tile0, lat, p_prev, mods, li, norm_w, w, jl, lbl, qn_t, kn_t, cos_t, sin_t,
               bd, kinds, name):
    tm, tn = PROJ_TM, PROJ_TN
    assert tm == DEC_SEQ
    n_ctx_tiles = N_CTX_TOK // tm
    n_tiles = (N_LAT_TOK if lat else N_CTX_TOK) // tm
    tile0 = n_ctx_tiles if lat else 0
    n_main = sum(1 for k in kinds if k != "kv")
    n_cols = tn * len(kinds)
    kvw = 2 * KV_B * HD_B
    kv_blk = (n_main * tn) // kvw if "kv" in kinds else 0
    const = lambda i, j: (0, 0)
    (out,) = _aliased_call(
        functools.partial(_proj_kernel, kinds=kinds, jl=jl, lat=lat),
        grid=(n_tiles, len(kinds)),
        in_specs=[
            pl.BlockSpec((tm, D_MODEL), lambda i, j: (x_tile0 + i, 0)),
            pl.BlockSpec((None, MOD_ROWS, N_MOD), lambda i, j: (li, 0, 0)),
            pl.BlockSpec((None, None, 1, D_MODEL), lambda i, j: (li, 0, 0, 0)),
            pl.BlockSpec((None, D_MODEL, tn), lambda i, j: (jl, 0, jnp.minimum(j, n_main - 1))),
            pl.BlockSpec((None, D_MODEL, kvw), lambda i, j: (jl, 0, kv_blk)),
            pl.BlockSpec(lbl.shape, const),
            pl.BlockSpec((1, tn), const),
            pl.BlockSpec((1, tn), const),
            pl.BlockSpec((tm, tn), const),
            pl.BlockSpec((tm, tn), const),
            pl.BlockSpec(bd.shape, const),
        ],
        args=[x_src, mods, norm_w, w, w, lbl, qn_t, kn_t, cos_t, sin_t, bd],
        out_specs=[pl.BlockSpec((tm, tn), lambda i, j: (tile0 + i, j))],
        out_shape=[jax.ShapeDtypeStruct((N_TOK, n_cols), F32)],
        carried=[p_prev], sem=("parallel", "arbitrary"), name=name,
        scratch_shapes=[pltpu.VMEM((tm, D_MODEL), BF16), pltpu.VMEM((D_MODEL, tn), BF16)])
    return out


def _hgrn_tables(rev):
    c = HGRN_CHUNK
    t = np.arange(c)
    w = np.zeros((2 + HGRN_LEVELS, c, c), np.float32)
    if not rev:
        w[0] = t[None, :] <= t[:, None]
        w[1] = t[None, :] > t[:, None]
    else:
        w[0] = t[None, :] >= t[:, None]
        w[1] = t[None, :] < t[:, None]
    for l in range(HGRN_LEVELS):
        hb = 1 << l
        for ti in range(c):
            mid = ti - ti % (2 * hb) + hb
            if not rev:
                if ti >= mid:
                    w[2 + l, ti, mid:ti + 1] = 1.0
                else:
                    w[2 + l, ti, ti + 1:mid] = 1.0
            else:
                if ti < mid:
                    w[2 + l, ti, ti:mid] = 1.0
                else:
                    w[2 + l, ti, mid:ti] = 1.0
    w = w.reshape((2 + HGRN_LEVELS) * c, c)
    ws = np.concatenate([w] * HGRN_SPLIT, axis=1)
    x = t[:, None] ^ t[None, :]
    lv = np.where(x > 0, np.floor(np.log2(np.maximum(x, 1))).astype(np.int32), HGRN_LEVELS)
    causal = (t[None, :] < t[:, None]) if not rev else (t[None, :] > t[:, None])
    lv = np.where(causal | (x == 0), lv, -1).astype(np.int32)
    return jnp.asarray(ws, BF16), jnp.asarray(lv)


def _hgrn_chunk(q, g, v, st, w_ref, lv_ref, rev):
    c = HGRN_CHUNK
    terms = []
    rem = g
    for _ in range(HGRN_SPLIT):
        term = rem.astype(BF16)
        terms.append(term)
        rem = rem - term.astype(F32)
    gs = jnp.concatenate(terms, axis=0)
    z = jnp.exp(jnp.dot(w_ref[...], gs, preferred_element_type=F32))
    k = 1.0 - jnp.exp(g)
    zq = z[0:c]
    zk = z[c:2 * c]
    last = 0 if rev else c - 1
    total = zq[last:last + 1, :]
    qd = (q * zq).astype(BF16)
    kd = (k * zk).astype(BF16)
    vb = v.astype(BF16)
    o = _nt_dot(qd, st.astype(BF16))
    st_new = st * total + jnp.dot(v.T.astype(BF16), kd, preferred_element_type=F32)
    lv = lv_ref[...]
    row = lax.broadcasted_iota(jnp.int32, (c, DK_A), 0)
    a = jnp.where(lv == HGRN_LEVELS, jnp.sum(q * k, axis=-1, keepdims=True), 0.0)
    for l in range(HGRN_LEVELS):
        zl = z[(2 + l) * c:(3 + l) * c]
        bit = ((row >> l) & 1) == 1
        q_role = jnp.logical_not(bit) if rev else bit
        ql = jnp.where(q_role, q * zl, 0.0).astype(BF16)
        kl = jnp.where(q_role, 0.0, k * zl).astype(BF16)
        a = jnp.where(lv == l, _nt_dot(ql, kl), a)
    o = o + jnp.dot(a.astype(BF16), vb, preferred_element_type=F32)
    return o, st_new


def _hgrn_kernel(*refs, n_chunks, has_init, emit_state):
    refs = list(refs)
    q_ref, gf_ref, gb_ref, v_ref, sg_ref, nw_ref, wf_ref, wb_ref, lvf_ref, lvb_ref = refs[:10]
    pos = 10
    s0_ref = None
    if has_init:
        s0_ref = refs[pos]
        pos += 1
    o_ref = refs[pos]
    pos += 1
    so_ref = None
    if emit_state:
        so_ref = refs[pos]
        pos += 1
    of_scr, ob_scr, st_scr = refs[pos:pos + 3]

    for d in range(2):
        for h in range(H_A):
            if has_init:
                st_scr[d, h] = s0_ref[d, h].T
            else:
                st_scr[d, h] = jnp.zeros((DV_A, DK_A), F32)

    def body(c, carry):
        for h in range(H_A):
            cols = slice(h * DK_A, (h + 1) * DK_A)
            for d, (g_ref, w_ref, lv_ref, scr) in enumerate(
                    ((gf_ref, wf_ref, lvf_ref, of_scr), (gb_ref, wb_ref, lvb_ref, ob_scr))):
                cc = c if d == 0 else n_chunks - 1 - c
                r0 = pl.multiple_of(cc * HGRN_CHUNK, HGRN_CHUNK)
                rows = pl.ds(r0, HGRN_CHUNK)
                o, st = _hgrn_chunk(q_ref[rows, cols], g_ref[rows, cols], v_ref[rows, cols],
                                    st_scr[d, h], w_ref, lv_ref, rev=(d == 1))
                st_scr[d, h] = st
                scr[rows, cols] = o
        return carry

    lax.fori_loop(0, n_chunks, body, 0)
    for h in range(H_A):
        cols = slice(h * DV_A, (h + 1) * DV_A)
        o = of_scr[:, cols] + ob_scr[:, cols]
        y = o * lax.rsqrt(jnp.mean(o * o, axis=-1, keepdims=True) + EPS) * nw_ref[...]
        o_ref[:, cols] = (y * sg_ref[:, cols]).astype(o_ref.dtype)
    if emit_state:
        for d in range(2):
            for h in range(H_A):
                so_ref[d, h] = st_scr[d, h].T


def _hgrn_call(p, nw, tabs, seq_len, n_seq, row_blk0, s0, jl, mix, state_out, name):
    wf, lvf, wb, lvb = tabs
    has_init = s0 is not None
    emit_state = s0 is None
    const = lambda b: (0, 0)
    blk = (seq_len, F_A)
    state_spec = pl.BlockSpec((None, None, 2, H_A, DK_A, DV_A), lambda b: (b, jl, 0, 0, 0, 0))
    in_specs = [pl.BlockSpec(blk, (lambda b, part=part: (row_blk0 + b, part))) for part in range(5)]
    in_specs += [
        pl.BlockSpec((None, 1, DV_A), lambda b: (jl, 0, 0)),
        pl.BlockSpec(wf.shape, const), pl.BlockSpec(wb.shape, const),
        pl.BlockSpec(lvf.shape, const), pl.BlockSpec(lvb.shape, const),
    ]
    args = [p, p, p, p, p, nw, wf, wb, lvf, lvb]
    if has_init:
        in_specs.append(state_spec)
        args.append(s0)
    out_shape = [jax.ShapeDtypeStruct((N_TOK, D_MIX), BF16)]
    out_specs = [pl.BlockSpec((seq_len, H_A * DV_A), lambda b: (row_blk0 + b, 0))]
    carried = [mix]
    if emit_state:
        out_shape.append(jax.ShapeDtypeStruct((BATCH, N_EVEN, 2, H_A, DK_A, DV_A), F32))
        out_specs.append(state_spec)
        carried.append(state_out)
    return _aliased_call(
        functools.partial(_hgrn_kernel, n_chunks=seq_len // HGRN_CHUNK, has_init=has_init,
                          emit_state=emit_state),
        grid=(n_seq,), in_specs=in_specs, args=args, out_specs=out_specs,
        out_shape=out_shape, carried=carried, sem=("parallel",), name=name,
        scratch_shapes=[pltpu.VMEM((seq_len, H_A * DV_A), F32),
                        pltpu.VMEM((seq_len, H_A * DV_A), F32),
                        pltpu.VMEM((2, H_A, DV_A, DK_A), F32)])


def _swa_ctx_kernel(q_ref, kv_ref, sink_ref, o_ref, kc_ref, vc_ref):
    kv = kv_ref[...]
    k32 = kv[:, 0:KV_B * HD_B]
    v32 = kv[:, KV_B * HD_B:2 * KV_B * HD_B]
    kc_ref[...] = k32
    vc_ref[...] = v32
    q = q_ref[...].astype(BF16)
    k = k32.astype(BF16)
    v = v32.astype(BF16)
    outs = []
    for h in range(H_B):
        n = h // G_B
        s = _nt_dot(q[:, h * HD_B:(h + 1) * HD_B], k[:, n * HD_B:(n + 1) * HD_B])
        sink = sink_ref[0:1, h:h + 1]
        m = jnp.maximum(jnp.max(s, axis=-1, keepdims=True), sink)
        p = jnp.exp(s - m)
        den = jnp.sum(p, axis=-1, keepdims=True) + jnp.exp(sink - m)
        pv = jnp.dot(p.astype(BF16), v[:, n * HD_B:(n + 1) * HD_B], preferred_element_type=F32)
        outs.append(pv / den)
    o_ref[...] = jnp.concatenate(outs, axis=1).astype(o_ref.dtype)


def _swa_ctx_call(p, sink, jl, mix, k_out, v_out, name):
    qcol = (3 * F_A + 2 * H_A * DV_A) // PROJ_TN
    cache_spec = pl.BlockSpec((None, None, SEQ, KV_B * HD_B), lambda b: (b, jl, 0, 0))
    cache_shape = jax.ShapeDtypeStruct((BATCH, N_EVEN, SEQ, KV_B * HD_B), F32)
    return _aliased_call(
        _swa_ctx_kernel,
        grid=(BATCH,),
        in_specs=[
            pl.BlockSpec((SEQ, H_B * HD_B), lambda b: (b, qcol)),
            pl.BlockSpec((SEQ, PROJ_TN), lambda b: (b, qcol + 1)),
            pl.BlockSpec((None, 1, H_B), lambda b: (jl, 0, 0)),
        ],
        args=[p, p, sink],
        out_specs=[pl.BlockSpec((SEQ, H_B * HD_B), lambda b: (b, 1)), cache_spec, cache_spec],
        out_shape=[jax.ShapeDtypeStruct((N_TOK, D_MIX), BF16), cache_shape, cache_shape],
        carried=[mix, k_out, v_out], sem=("parallel",), name=name)


def _swa_lat_kernel(q_ref, kv_ref, ck_ref, cv_ref, sink_ref, o_ref):
    qi = pl.program_id(1)
    tq = q_ref.shape[0]
    span = tq + 2 * WINDOW
    ws = pl.multiple_of(jnp.clip(qi * tq - WINDOW, 0, DEC_SEQ - span), WINDOW)
    kvw = kv_ref[pl.ds(ws, span), :]
    kw = kvw[:, 0:KV_B * HD_B].astype(BF16)
    vw = kvw[:, KV_B * HD_B:2 * KV_B * HD_B].astype(BF16)
    kc = ck_ref[...].astype(BF16)
    vc = cv_ref[...].astype(BF16)
    q = q_ref[...].astype(BF16)
    t_q = qi * tq + lax.broadcasted_iota(jnp.int32, (tq, span), 0)
    t_k = ws + lax.broadcasted_iota(jnp.int32, (tq, span), 1)
    valid = jnp.abs(t_q - t_k) <= WINDOW
    outs = []
    for h in range(H_B):
        n = h // G_B
        qh = q[:, h * HD_B:(h + 1) * HD_B]
        ksl = slice(n * HD_B, (n + 1) * HD_B)
        s_w = jnp.where(valid, _nt_dot(qh, kw[:, ksl]), -jnp.inf)
        s_c = _nt_dot(qh, kc[:, ksl])
        sink = sink_ref[0:1, h:h + 1]
        m = jnp.maximum(jnp.maximum(jnp.max(s_w, axis=-1, keepdims=True),
                                    jnp.max(s_c, axis=-1, keepdims=True)), sink)
        p_w = jnp.exp(s_w - m)
        p_c = jnp.exp(s_c - m)
        den = (jnp.sum(p_w, axis=-1, keepdims=True) + jnp.sum(p_c, axis=-1, keepdims=True)
               + jnp.exp(sink - m))
        pv = (jnp.dot(p_w.astype(BF16), vw[:, ksl], preferred_element_type=F32)
              + jnp.dot(p_c.astype(BF16), vc[:, ksl], preferred_element_type=F32))
        outs.append(pv / den)
    o_ref[...] = jnp.concatenate(outs, axis=1).astype(o_ref.dtype)


def _swa_lat_call(p, ck, cv, sink, jl, mix, name):
    tq = ATT_TQ
    qcol = (3 * F_A + 2 * H_A * DV_A) // PROJ_TN
    nq = DEC_SEQ // tq
    q_blk0 = N_CTX_TOK // tq
    s_blk0 = N_CTX_TOK // DEC_SEQ
    cache_spec = pl.BlockSpec((None, None, PAST_LEN, KV_B * HD_B), lambda b, i: (b, jl, 0, 0))
    (out,) = _aliased_call(
        _swa_lat_kernel,
        grid=(DEC_BATCH, nq),
        in_specs=[
            pl.BlockSpec((tq, H_B * HD_B), lambda b, i: (q_blk0 + b * nq + i, qcol)),
            pl.BlockSpec((DEC_SEQ, PROJ_TN), lambda b, i: (s_blk0 + b, qcol + 1)),
            cache_spec, cache_spec,
            pl.BlockSpec((None, 1, H_B), lambda b, i: (jl, 0, 0)),
        ],
        args=[p, p, ck, cv, sink],
        out_specs=[pl.BlockSpec((tq, H_B * HD_B), lambda b, i: (q_blk0 + b * nq + i, 1))],
        out_shape=[jax.ShapeDtypeStruct((N_TOK, D_MIX), BF16)],
        carried=[mix], sem=("parallel", "parallel"), name=name)
    return out


def _diff_lambda(lp_ref, lam_init):
    lp = lp_ref[...]
    a = jnp.sum(lp[0:1] * lp[1:2], axis=-1, keepdims=True)
    b = jnp.sum(lp[2:3] * lp[3:4], axis=-1, keepdims=True)
    return jnp.exp(a) - jnp.exp(b) + lam_init


def _softmax_parts(parts):
    m = functools.reduce(jnp.maximum, [jnp.max(s, axis=-1, keepdims=True) for s in parts])
    es = [jnp.exp(s - m) for s in parts]
    den = functools.reduce(lambda a, b: a + b, [jnp.sum(e, axis=-1, keepdims=True) for e in es])
    inv = 1.0 / den
    return [e * inv for e in es]


def _diff_core(q, k_parts, v_parts, lam, lam_init, sw):
    pr = []
    for c in range(2):
        sl = slice(c * HD_C, (c + 1) * HD_C)
        pr.append(_softmax_parts([_nt_dot(q[:, sl], kp[:, sl]) for kp in k_parts]))
    o = None
    for i, vp in enumerate(v_parts):
        w = (pr[0][i] - lam * pr[1][i]).astype(BF16)
        t = jnp.dot(w, vp, preferred_element_type=F32)
        o = t if o is None else o + t
    y = o * lax.rsqrt(jnp.mean(o * o, axis=-1, keepdims=True) + EPS) * sw
    return y * (1.0 - lam_init)


def _diff_ctx_kernel(q_ref, k_ref, v_ref, lp_ref, sw_ref, o_ref, kc_ref, vc_ref, *, lam_init):
    lam = _diff_lambda(lp_ref, lam_init)
    hw = 2 * HD_C
    for h in range(H_C):
        sl = slice(h * hw, (h + 1) * hw)
        k32 = k_ref[:, sl]
        v32 = v_ref[:, sl]
        kc_ref[:, sl] = k32
        vc_ref[:, sl] = v32
        y = _diff_core(q_ref[:, sl].astype(BF16), [k32.astype(BF16)], [v32.astype(BF16)], lam,
                       lam_init, sw_ref[...])
        o_ref[:, sl] = y.astype(o_ref.dtype)


def _diff_ctx_call(p, lp, sw, lam_init, jl, k_out, v_out, name):
    cache_spec = pl.BlockSpec((None, None, SEQ, ODD_W), lambda b: (b, jl, 0, 0))
    cache_shape = jax.ShapeDtypeStruct((BATCH, N_ODD, SEQ, ODD_W), F32)
    return _aliased_call(
        functools.partial(_diff_ctx_kernel, lam_init=lam_init),
        grid=(BATCH,),
        in_specs=[
            pl.BlockSpec((SEQ, ODD_W), lambda b: (b, 0)),
            pl.BlockSpec((SEQ, ODD_W), lambda b: (b, 1)),
            pl.BlockSpec((SEQ, ODD_W), lambda b: (b, 2)),
            pl.BlockSpec((None, 4, HD_C), lambda b: (jl, 0, 0)),
            pl.BlockSpec((None, 1, 2 * HD_C), lambda b: (jl, 0, 0)),
        ],
        args=[p, p, p, lp, sw],
        out_specs=[pl.BlockSpec((SEQ, ODD_W), lambda b: (b, 0)), cache_spec, cache_spec],
        out_shape=[jax.ShapeDtypeStruct((N_TOK, D_MIX), BF16), cache_shape, cache_shape],
        carried=[None, k_out, v_out], sem=("parallel",), name=name)


def _diff_lat_kernel(q_ref, k_ref, v_ref, ck_ref, cv_ref, lp_ref, sw_ref, o_ref, *, lam_init):
    lam = _diff_lambda(lp_ref, lam_init)
    k_parts = [k_ref[...].astype(BF16), ck_ref[...].astype(BF16)]
    v_parts = [v_ref[...].astype(BF16), cv_ref[...].astype(BF16)]
    tq = ATT_TQ
    for i in range(q_ref.shape[0] // tq):
        rows = slice(i * tq, (i + 1) * tq)
        y = _diff_core(q_ref[rows, :].astype(BF16), k_parts, v_parts, lam, lam_init, sw_ref[...])
        o_ref[rows, :] = y.astype(o_ref.dtype)


def _diff_lat_call(p, ck, cv, lp, sw, lam_init, jl, mix, name):
    hw = 2 * HD_C
    s_blk0 = N_CTX_TOK // DEC_SEQ
    cache_spec = pl.BlockSpec((None, None, PAST_LEN, hw), lambda b, h: (b, jl, 0, h))
    (out,) = _aliased_call(
        functools.partial(_diff_lat_kernel, lam_init=lam_init),
        grid=(DEC_BATCH, H_C),
        in_specs=[
            pl.BlockSpec((DEC_SEQ, hw), lambda b, h: (s_blk0 + b, h)),
            pl.BlockSpec((DEC_SEQ, hw), lambda b, h: (s_blk0 + b, H_C + h)),
            pl.BlockSpec((DEC_SEQ, hw), lambda b, h: (s_blk0 + b, 2 * H_C + h)),
            cache_spec, cache_spec,
            pl.BlockSpec((None, 4, HD_C), lambda b, h: (jl, 0, 0)),
            pl.BlockSpec((None, 1, hw), lambda b, h: (jl, 0, 0)),
        ],
        args=[p, p, p, ck, cv, lp, sw],
        out_specs=[pl.BlockSpec((DEC_SEQ, hw), lambda b, h: (s_blk0 + b, h))],
        out_shape=[jax.ShapeDtypeStruct((N_TOK, D_MIX), BF16)],
        carried=[mix], sem=("parallel", "parallel"), name=name)
    return out


def _mlp_kernel(x_ref, mix_ref, wo_ref, mods_ref, nw_ref, w1_ref, w2_ref, o_ref,
                h_scr, acc_scr, *, lat):
    i = pl.program_id(0)
    k = pl.program_id(1)
    row = 1 + i if lat else 0

    def mod(a):
        return mods_ref[pl.ds(row, 1), a * D_MODEL:(a + 1) * D_MODEL]

    @pl.when(k == 0)
    def _():
        y = jnp.dot(mix_ref[...], wo_ref[...], preferred_element_type=F32)
        x1 = x_ref[...] + mod(2) * y
        o_ref[...] = x1
        h_scr[...] = _norm_mod(x1, nw_ref[...], mod(4), mod(3)).astype(BF16)
        acc_scr[...] = jnp.zeros_like(acc_scr)

    u = jnp.dot(h_scr[...], w1_ref[...].astype(BF16), preferred_element_type=F32)
    u = jnp.square(jnp.maximum(u, 0.0)).astype(BF16)
    acc_scr[...] += jnp.dot(u, w2_ref[...].astype(BF16), preferred_element_type=F32)

    @pl.when(k == pl.num_programs(1) - 1)
    def _():
        o_ref[...] = o_ref[...] + mod(5) * acc_scr[...]


def _mlp_call(x_src, x_tile0, lat, mix, wo, jl, mods, li, norm_w, w1, w2, out_prev, out_rows,
              out_tile0, name):
    tm, tk = MLP_TM, MLP_TK
    n_tiles = (N_LAT_TOK if lat else N_CTX_TOK) // tm
    mix_tile0 = N_CTX_TOK // tm if lat else 0
    (out,) = _aliased_call(
        functools.partial(_mlp_kernel, lat=lat),
        grid=(n_tiles, D_FF // tk),
        in_specs=[
            pl.BlockSpec((tm, D_MODEL), lambda i, k: (x_tile0 + i, 0)),
            pl.BlockSpec((tm, D_MIX), lambda i, k: (mix_tile0 + i, 0)),
            pl.BlockSpec((None, D_MIX, D_MODEL), lambda i, k: (jl, 0, 0)),
            pl.BlockSpec((None, MOD_ROWS, N_MOD), lambda i, k: (li, 0, 0)),
            pl.BlockSpec((None, None, 1, D_MODEL), lambda i, k: (li, 1, 0, 0)),
            pl.BlockSpec((None, D_MODEL, tk), lambda i, k: (li, 0, k)),
            pl.BlockSpec((None, tk, D_MODEL), lambda i, k: (li, k, 0)),
        ],
        args=[x_src, mix, wo, mods, norm_w, w1, w2],
        out_specs=[pl.BlockSpec((tm, D_MODEL), lambda i, k: (out_tile0 + i, 0))],
        out_shape=[jax.ShapeDtypeStruct((out_rows, D_MODEL), F32)],
        carried=[out_prev], sem=("parallel", "arbitrary"), name=name,
        scratch_shapes=[pltpu.VMEM((tm, D_MODEL), BF16), pltpu.VMEM((tm, D_MODEL), F32)])
    return out


def _rope_tables(width):
    t = np.arange(DEC_SEQ)
    half = HEAD_GROUP // 2
    inv = ROPE_BASE ** (-np.arange(0, half, 2, dtype=np.float64) / half)
    ang = np.concatenate([(t // GRID_W)[:, None] * inv, (t % GRID_W)[:, None] * inv], axis=-1)
    cos = np.repeat(np.cos(ang), 2, axis=-1)
    sin = np.repeat(np.sin(ang), 2, axis=-1)
    sign = np.tile(np.array([-1.0, 1.0]), HEAD_GROUP // 2)
    reps = width // HEAD_GROUP
    return (jnp.asarray(np.tile(cos, (1, reps)), F32),
            jnp.asarray(np.tile(sin * sign, (1, reps)), F32))


def _block_diag_ones(n):
    g = np.arange(n) // HEAD_GROUP
    return jnp.asarray(g[:, None] == g[None, :], BF16)


def _tile_row(w, width):
    return jnp.tile(w.astype(F32), width // w.shape[0])[None, :]


def _lambda_init(li):
    return 0.8 - 0.6 * math.exp(-0.3 * li)


def kernel(x_prompt, x_sample, cache_k_swa, cache_v_swa, state_hgrn, cache_k_diff, cache_v_diff, c, c_ctx, norm_w, w_ada, b_ada, w_in_even, w_out_even, hgrn_lb_logits, hgrn_norm_w, swa_qnorm_w, swa_knorm_w, swa_sink, w_in_odd, w_out_odd, diff_qnorm_w, diff_knorm_w, diff_lambda_p, diff_subln_w, w_mlp1, w_mlp2):
    assert PROJ_TM == MLP_TM
    n_ctx_tiles = N_CTX_TOK // PROJ_TM
    x = None
    x_ctx0 = x_prompt.reshape(N_CTX_TOK, D_MODEL)
    x_lat0 = x_sample.reshape(N_LAT_TOK, D_MODEL)
    c_all = jnp.concatenate(
        [c_ctx[None, :], c, jnp.zeros((MOD_ROWS - 1 - DEC_BATCH, D_MODEL), F32)], axis=0)
    mods = _mods_call(c_all, w_ada, b_ada.reshape(DEPTH, 1, N_MOD))

    cos_t, sin_t = _rope_tables(PROJ_TN)
    bd = _block_diag_ones(256)
    tabs_f = _hgrn_tables(False)
    tabs_b = _hgrn_tables(True)
    hgrn_tabs = (tabs_f[0], tabs_f[1], tabs_b[0], tabs_b[1])
    lbl = hgrn_lb_logits.astype(F32).reshape(N_EVEN * 2, F_A)
    norm_w4 = norm_w.astype(F32).reshape(DEPTH, 2, 1, D_MODEL)

    w_out_even_b = w_out_even.astype(BF16)
    w_out_odd_b = w_out_odd.astype(BF16)

    ck_swa = cache_k_swa.reshape(DEC_BATCH, N_EVEN, PAST_LEN, KV_B * HD_B)
    cv_swa = cache_v_swa.reshape(DEC_BATCH, N_EVEN, PAST_LEN, KV_B * HD_B)
    ck_diff = cache_k_diff.reshape(DEC_BATCH, N_ODD, PAST_LEN, ODD_W)
    cv_diff = cache_v_diff.reshape(DEC_BATCH, N_ODD, PAST_LEN, ODD_W)
    hgrn_nw = hgrn_norm_w.astype(F32).reshape(N_EVEN, 1, DV_A)
    sink = swa_sink.astype(F32).reshape(N_EVEN, 1, H_B)
    lam_p = diff_lambda_p.astype(F32)
    subln = diff_subln_w.astype(F32).reshape(N_ODD, 1, 2 * HD_C)

    even_kinds = ("silu_scale", "loggate0", "loggate1", "ident", "silu", "qnorm", "kv")
    odd_kinds = ("qnorm", "qnorm", "knorm", "knorm", "ident", "ident")

    k_swa = v_swa = states = k_diff = v_diff = None
    for li in range(DEPTH):
        j = li // 2
        srcs = ((x_ctx0, 0), (x_lat0, 0)) if li == 0 else ((x, 0), (x, n_ctx_tiles))
        if li % 2 == 0:
            p = None
            for lat, (src, t0) in enumerate(srcs):
                p = _proj_call(src, t0, bool(lat), p, mods, li, norm_w4, w_in_even, j, lbl,
                               _tile_row(swa_qnorm_w[j], PROJ_TN),
                               _tile_row(swa_knorm_w[j], PROJ_TN),
                               cos_t, sin_t, bd, even_kinds, f"proj_even{j}_{lat}")
            mix, states = _hgrn_call(p, hgrn_nw, hgrn_tabs, SEQ, BATCH, 0, None, j, None, states,
                                     f"hgrn_ctx{j}")
            (mix,) = _hgrn_call(p, hgrn_nw, hgrn_tabs, DEC_SEQ, DEC_BATCH, N_CTX_TOK // DEC_SEQ,
                                state_hgrn, j, mix, None, f"hgrn_lat{j}")
            mix, k_swa, v_swa = _swa_ctx_call(p, sink, j, mix, k_swa, v_swa, f"swa_ctx{j}")
            mix = _swa_lat_call(p, ck_swa, cv_swa, sink, j, mix, f"swa_lat{j}")
            wo = w_out_even_b
        else:
            p = None
            for lat, (src, t0) in enumerate(srcs):
                p = _proj_call(src, t0, bool(lat), p, mods, li, norm_w4, w_in_odd, j, lbl,
                               _tile_row(diff_qnorm_w[j], PROJ_TN),
                               _tile_row(diff_knorm_w[j], PROJ_TN),
                               cos_t, sin_t, bd, odd_kinds, f"proj_odd{j}_{lat}")
            lam_init = _lambda_init(li)
            mix, k_diff, v_diff = _diff_ctx_call(p, lam_p, subln, lam_init, j, k_diff, v_diff,
                                                 f"diff_ctx{j}")
            mix = _diff_lat_call(p, ck_diff, cv_diff, lam_p, subln, lam_init, j, mix,
                                 f"diff_lat{j}")
            wo = w_out_odd_b
        last = li == DEPTH - 1
        outs = []
        x_next = None
        for lat, (src, t0) in enumerate(srcs):
            rows = (N_LAT_TOK if lat else N_CTX_TOK) if last else N_TOK
            out_t0 = 0 if last else lat * n_ctx_tiles
            x_next = _mlp_call(src, t0, bool(lat), mix, wo, j, mods, li, norm_w4, w_mlp1, w_mlp2,
                               None if last else x_next, rows, out_t0, f"mlp{li}_{lat}")
            outs.append(x_next)
        x = x_next

    y_prompt = outs[0].reshape(BATCH, SEQ, D_MODEL)
    y_sample = outs[1].reshape(DEC_BATCH, DEC_SEQ, D_MODEL)
    return (y_prompt, y_sample,
            k_swa.reshape(BATCH, N_EVEN, SEQ, KV_B, HD_B),
            v_swa.reshape(BATCH, N_EVEN, SEQ, KV_B, HD_B),
            states,
            k_diff.reshape(BATCH, N_ODD, SEQ, H_C, 2, HD_C),
            v_diff.reshape(BATCH, N_ODD, SEQ, H_C, 2 * HD_C))
```

```python
import functools
import math

import numpy as np
import jax
import jax.numpy as jnp
from jax import lax
from jax.experimental import pallas as pl
from jax.experimental.pallas import tpu as pltpu

F32 = jnp.float32
BF16 = jnp.bfloat16

D_MODEL = 1024
BATCH = 16
SEQ = 256
DEPTH = 4
DEC_BATCH = 4
DEC_SEQ = 1024
PAST_LEN = 512
GRID_W = 64
N_EVEN = (DEPTH + 1) // 2
N_ODD = DEPTH // 2
H_A = 4
DK_A = 128
DV_A = D_MODEL // 2 // H_A
F_A = H_A * DK_A
H_B = 8
KV_B = 2
G_B = H_B // KV_B
HD_B = D_MODEL // 2 // H_B
WINDOW = 128
H_C = 8
HD_C = D_MODEL // (2 * H_C)
D_FF = 4 * D_MODEL
ROPE_BASE = 10000.0
EPS = 1e-6
EVEN_COLS = 3 * F_A + 2 * H_A * DV_A + (H_B + 2 * KV_B) * HD_B
ODD_W = H_C * 2 * HD_C
D_MIX = D_MODEL

N_CTX_TOK = BATCH * SEQ
N_LAT_TOK = DEC_BATCH * DEC_SEQ
N_TOK = N_CTX_TOK + N_LAT_TOK
MOD_ROWS = 8
N_MOD = 6 * D_MODEL

HEAD_GROUP = 64
HGRN_CHUNK = 128
HGRN_LEVELS = 7
HGRN_SPLIT = 3
LOG2_E = math.log2(math.e)
VMEM_LIMIT = 48 * 1024 * 1024

PROJ_TM = 1024
PROJ_TN = 512
PROJ_RC = 256
MLP_TM = 1024
MLP_TK = 1024
ADA_TN = 1536
ATT_TQ = 256
ATT_RB = 256


def _silu(x):
    return x * jax.nn.sigmoid(x)


def _nt_dot(a, b):
    return lax.dot_general(a, b, (((1,), (1,)), ((), ())), preferred_element_type=F32)


def _params(sem):
    return pltpu.CompilerParams(dimension_semantics=sem, vmem_limit_bytes=VMEM_LIMIT)


def _aliased_call(kernel, *, grid, in_specs, args, out_specs, out_shape, carried, sem, name,
                  scratch_shapes=()):
    n_in = len(args)
    extra = [buf for buf in carried if buf is not None]
    aliases = {}
    for k, buf in enumerate(carried):
        if buf is not None:
            aliases[n_in + len(aliases)] = k
    n_extra = len(extra)

    def body(*refs):
        kernel(*refs[:n_in], *refs[n_in + n_extra:])

    return pl.pallas_call(
        body,
        grid=grid,
        in_specs=list(in_specs) + [pl.BlockSpec(memory_space=pl.ANY)] * n_extra,
        out_specs=out_specs,
        out_shape=out_shape,
        input_output_aliases=aliases,
        scratch_shapes=list(scratch_shapes),
        compiler_params=_params(sem),
        name=name,
    )(*args, *extra)


def _mods_kernel(c_ref, w_ref, b_ref, o_ref):
    s = _silu(c_ref[...]).astype(BF16)
    o_ref[...] = jnp.dot(s, w_ref[...].astype(BF16), preferred_element_type=F32) + b_ref[...]


def _mods_call(c_all, w_ada, b_ada):
    return pl.pallas_call(
        _mods_kernel,
        grid=(DEPTH, N_MOD // ADA_TN),
        in_specs=[
            pl.BlockSpec((MOD_ROWS, D_MODEL), lambda l, j: (0, 0)),
            pl.BlockSpec((None, D_MODEL, ADA_TN), lambda l, j: (l, 0, j)),
            pl.BlockSpec((None, 1, ADA_TN), lambda l, j: (l, 0, j)),
        ],
        out_specs=pl.BlockSpec((None, MOD_ROWS, ADA_TN), lambda l, j: (l, 0, j)),
        out_shape=jax.ShapeDtypeStruct((DEPTH, MOD_ROWS, N_MOD), F32),
        compiler_params=_params(("parallel", "parallel")),
        name="ada_mods",
    )(c_all, w_ada, b_ada)


def _norm_mod(x, nw, sc, sh):
    ms = jnp.mean(x * x, axis=-1, keepdims=True)
    return (x * lax.rsqrt(ms + EPS) * nw) * (1.0 + sc) + sh


def _group_rms(y, w_t, bd_ref):
    yy = (y * y).astype(BF16)
    bw = bd_ref.shape[0]
    parts = [jnp.dot(yy[:, s:s + bw], bd_ref[...], preferred_element_type=F32)
             for s in range(0, y.shape[1], bw)]
    ss = parts[0] if len(parts) == 1 else jnp.concatenate(parts, axis=1)
    return y * lax.rsqrt(ss * (1.0 / HEAD_GROUP) + EPS) * w_t


def _rope(y, cos, sin):
    n = y.shape[1]
    lane = lax.broadcasted_iota(jnp.int32, y.shape, 1)
    nxt = pltpu.roll(y, n - 1, axis=1)
    prv = pltpu.roll(y, 1, axis=1)
    swapped = jnp.where((lane & 1) == 0, nxt, prv)
    return y * cos + swapped * sin


def _lower_bounds(lbl_ref, jl):
    rows = [lbl_ref[pl.ds(2 * m, 2), :] for m in range(N_EVEN)]
    mx = functools.reduce(jnp.maximum, rows)
    es = [jnp.exp(r - mx) for r in rows]
    den = functools.reduce(lambda a, b: a + b, es)
    sm = [e / den for e in es]
    cs = sm[0]
    for m in range(1, jl + 1):
        cs = cs + sm[m]
    return cs - sm[0]


def _proj_kernel(x_ref, mods_ref, nw_ref, w_ref, wkv_ref, lbl_ref, qn_ref, kn_ref, cos_ref,
                 sin_ref, bd_ref, o_ref, h_scr, w_scr, *, kinds, jl, lat):
    j = pl.program_id(0)
    i = pl.program_id(1)
    tm, tn = o_ref.shape
    kvw = 2 * KV_B * HD_B

    @pl.when(j == 0)
    def _():
        row = 1 + i if lat else 0
        sh = mods_ref[pl.ds(row, 1), 0:D_MODEL]
        sc = mods_ref[pl.ds(row, 1), D_MODEL:2 * D_MODEL]
        h_scr[i] = _norm_mod(x_ref[...], nw_ref[...], sc, sh).astype(BF16)

    def finish(kind, y, rows):
        if kind == "silu_scale":
            return _silu(y) * (DK_A ** -0.5)
        if kind in ("loggate0", "loggate1"):
            d = int(kind[-1])
            lb = _lower_bounds(lbl_ref, jl)[d:d + 1, :]
            return jnp.log2(lb + (1.0 - lb) * jax.nn.sigmoid(y))
        if kind == "ident":
            return y
        if kind == "silu":
            return _silu(y)
        if kind in ("qnorm", "knorm"):
            w_t = qn_ref[...] if kind == "qnorm" else kn_ref[...]
            r = _group_rms(y, w_t, bd_ref)
            if lat:
                r = _rope(r, cos_ref[rows, :], sin_ref[rows, :])
            return r * (HEAD_GROUP ** -0.5 * LOG2_E) if kind == "qnorm" else r
        if kind == "kv":
            kn = _group_rms(y, kn_ref[:, 0:kvw], bd_ref)
            if lat:
                kn = _rope(kn, cos_ref[rows, 0:kvw], sin_ref[rows, 0:kvw])
            lane = lax.broadcasted_iota(jnp.int32, y.shape, 1)
            return jnp.where(lane < KV_B * HD_B, kn, y)
        raise ValueError(kind)

    def run(kind):
        @pl.when(i == 0)
        def _():
            if kind == "kv":
                w_scr[:, 0:kvw] = wkv_ref[...].astype(BF16)
            else:
                w_scr[...] = w_ref[...].astype(BF16)

        for r in range(tm // PROJ_RC):
            rows = slice(r * PROJ_RC, (r + 1) * PROJ_RC)
            w = w_scr[:, 0:kvw] if kind == "kv" else w_scr[...]
            y = jnp.dot(h_scr[i, rows, :], w, preferred_element_type=F32)
            if kind == "kv":
                o_ref[rows, 0:kvw] = finish(kind, y, rows)
                o_ref[rows, kvw:tn] = jnp.zeros((PROJ_RC, tn - kvw), F32)
            else:
                o_ref[rows, :] = finish(kind, y, rows)

    for jj, kind in enumerate(kinds):
        pl.when(j == jj)(functools.partial(run, kind))


def _proj_call(x_src, x_tile0, lat, p_prev, mods, li, norm_w, w, jl, lbl, qn_t, kn_t, cos_t, sin_t,
               bd, kinds, name):
    tm, tn = PROJ_TM, PROJ_TN
    assert tm == DEC_SEQ
    n_ctx_tiles = N_CTX_TOK // tm
    n_tiles = (N_LAT_TOK if lat else N_CTX_TOK) // tm
    tile0 = n_ctx_tiles if lat else 0
    n_main = sum(1 for k in kinds if k != "kv")
    n_cols = tn * len(kinds)
    kvw = 2 * KV_B * HD_B
    kv_blk = (n_main * tn) // kvw if "kv" in kinds else 0
    const = lambda j, i: (0, 0)
    (out,) = _aliased_call(
        functools.partial(_proj_kernel, kinds=kinds, jl=jl, lat=lat),
        grid=(len(kinds), n_tiles),
        in_specs=[
            pl.BlockSpec((tm, D_MODEL),
                         lambda j, i: (x_tile0 + jnp.where(j == 0, i, n_tiles - 1), 0)),
            pl.BlockSpec((None, MOD_ROWS, N_MOD), lambda j, i: (li, 0, 0)),
            pl.BlockSpec((None, None, 1, D_MODEL), lambda j, i: (li, 0, 0, 0)),
            pl.BlockSpec((None, D_MODEL, tn), lambda j, i: (jl, 0, jnp.minimum(j, n_main - 1))),
            pl.BlockSpec((None, D_MODEL, kvw), lambda j, i: (jl, 0, kv_blk)),
            pl.BlockSpec(lbl.shape, const),
            pl.BlockSpec((1, tn), const),
            pl.BlockSpec((1, tn), const),
            pl.BlockSpec((tm, tn), const),
            pl.BlockSpec((tm, tn), const),
            pl.BlockSpec(bd.shape, const),
        ],
        args=[x_src, mods, norm_w, w, w, lbl, qn_t, kn_t, cos_t, sin_t, bd],
        out_specs=[pl.BlockSpec((tm, tn), lambda j, i: (tile0 + i, j))],
        out_shape=[jax.ShapeDtypeStruct((N_TOK, n_cols), F32)],
        carried=[p_prev], sem=("arbitrary", "arbitrary"), name=name,
        scratch_shapes=[pltpu.VMEM((n_tiles, tm, D_MODEL), BF16),
                        pltpu.VMEM((D_MODEL, tn), BF16)])
    return out


def _hgrn_tables(rev):
    c = HGRN_CHUNK
    t = np.arange(c)
    w = (t[None, :] <= t[:, None]) if not rev else (t[None, :] >= t[:, None])
    ws = np.concatenate([w.astype(np.float32)] * HGRN_SPLIT, axis=1)
    x = t[:, None] ^ t[None, :]
    lv = np.where(x > 0, np.floor(np.log2(np.maximum(x, 1))).astype(np.int32), HGRN_LEVELS)
    causal = (t[None, :] < t[:, None]) if not rev else (t[None, :] > t[:, None])
    lv = np.where(causal | (x == 0), lv, -1).astype(np.int32)
    return jnp.asarray(ws, BF16), jnp.asarray(lv)


def _hgrn_level_exponents(cum, rev):
    c = HGRN_CHUNK
    sub_rows = 8
    c3 = cum.reshape(c // sub_rows, sub_rows, DK_A)
    sub = lax.broadcasted_iota(jnp.int32, c3.shape, 1)
    out = []
    for l in range(1, HGRN_LEVELS):
        hb = 1 << l
        if 2 * hb <= sub_rows:
            r = None
            for b0 in range(0, sub_rows, 2 * hb):
                idx = b0 + (hb if rev else hb - 1)
                rk = c3[:, idx:idx + 1, :]
                r = rk if r is None else jnp.where(sub < b0, r, rk)
            d = c3 - r
            bit = (sub & hb) != 0
            q_role = jnp.logical_not(bit) if rev else bit
            out.append(jnp.where(q_role, d, -d).reshape(c, DK_A))
        else:
            pieces = []
            for b0 in range(0, c, 2 * hb):
                mid = b0 + hb
                ridx = mid if rev else mid - 1
                r = cum[ridx:ridx + 1, :]
                lo = cum[b0:mid]
                hi = cum[mid:b0 + 2 * hb]
                pieces += [lo - r, r - hi] if rev else [r - lo, hi - r]
            out.append(jnp.concatenate(pieces, axis=0))
    return out


def _hgrn_chunk(q, g, v, st, w_ref, lv_ref, rev):
    c = HGRN_CHUNK
    terms = []
    rem = g
    for _ in range(HGRN_SPLIT):
        term = rem.astype(BF16)
        terms.append(term)
        rem = rem - term.astype(F32)
    gs = jnp.concatenate(terms, axis=0)
    cum = jnp.dot(w_ref[...], gs, preferred_element_type=F32)
    f = jnp.exp2(g)
    k = 1.0 - f
    last = 0 if rev else c - 1
    total_e = cum[last:last + 1, :]
    qd = (q * jnp.exp2(cum)).astype(BF16)
    kd = (k * jnp.exp2(total_e - cum)).astype(BF16)
    vb = v.astype(BF16)
    o = _nt_dot(qd, st.astype(BF16))
    st_new = st * jnp.exp2(total_e) + jnp.dot(v.T.astype(BF16), kd, preferred_element_type=F32)
    lv = lv_ref[...]
    row = lax.broadcasted_iota(jnp.int32, (c, DK_A), 0)
    zs = [None] + [jnp.exp2(e) for e in _hgrn_level_exponents(cum, rev)]
    a = None
    for l in reversed(range(HGRN_LEVELS)):
        hb = 1 << l
        if hb >= 8:
            z = zs[l]
            zero = jnp.zeros((hb, DK_A), F32)
            qparts, kparts = [], []
            for b0 in range(0, c, 2 * hb):
                lo, hi = slice(b0, b0 + hb), slice(b0 + hb, b0 + 2 * hb)
                if rev:
                    qparts += [q[lo] * z[lo], zero]
                    kparts += [zero, k[hi] * z[hi]]
                else:
                    qparts += [zero, q[hi] * z[hi]]
                    kparts += [k[lo] * z[lo], zero]
            ql = jnp.concatenate(qparts, axis=0).astype(BF16)
            kl = jnp.concatenate(kparts, axis=0).astype(BF16)
        else:
            bit = ((row >> l) & 1) == 1
            q_role = jnp.logical_not(bit) if rev else bit
            ql = jnp.where(q_role, q * (f if l == 0 else zs[l]), 0.0).astype(BF16)
            kl = jnp.where(q_role, 0.0, k if l == 0 else k * zs[l]).astype(BF16)
        a_l = _nt_dot(ql, kl)
        a = a_l if a is None else jnp.where(lv == l, a_l, a)
    a = jnp.where(lv == HGRN_LEVELS, jnp.sum(q * k, axis=-1, keepdims=True), a)
    o = o + jnp.dot(a.astype(BF16), vb, preferred_element_type=F32)
    return o, st_new


def _hgrn_kernel(*refs, n_chunks, has_init, emit_state):
    refs = list(refs)
    q_ref, gf_ref, gb_ref, v_ref, sg_ref, nw_ref, wf_ref, wb_ref, lvf_ref, lvb_ref = refs[:10]
    pos = 10
    s0_ref = None
    if has_init:
        s0_ref = refs[pos]
        pos += 1
    o_ref = refs[pos]
    pos += 1
    so_ref = None
    if emit_state:
        so_ref = refs[pos]
        pos += 1
    of_scr, ob_scr, st_scr = refs[pos:pos + 3]

    for d in range(2):
        for h in range(H_A):
            if has_init:
                st_scr[d, h] = s0_ref[d, h].T
            else:
                st_scr[d, h] = jnp.zeros((DV_A, DK_A), F32)

    def body(c, carry):
        for h in range(H_A):
            cols = slice(h * DK_A, (h + 1) * DK_A)
            for d, (g_ref, w_ref, lv_ref, scr) in enumerate(
                    ((gf_ref, wf_ref, lvf_ref, of_scr), (gb_ref, wb_ref, lvb_ref, ob_scr))):
                cc = c if d == 0 else n_chunks - 1 - c
                r0 = pl.multiple_of(cc * HGRN_CHUNK, HGRN_CHUNK)
                rows = pl.ds(r0, HGRN_CHUNK)
                o, st = _hgrn_chunk(q_ref[rows, cols], g_ref[rows, cols], v_ref[rows, cols],
                                    st_scr[d, h], w_ref, lv_ref, rev=(d == 1))
                st_scr[d, h] = st
                scr[rows, cols] = o
        return carry

    lax.fori_loop(0, n_chunks, body, 0)
    for h in range(H_A):
        cols = slice(h * DV_A, (h + 1) * DV_A)
        o = of_scr[:, cols] + ob_scr[:, cols]
        y = o * lax.rsqrt(jnp.mean(o * o, axis=-1, keepdims=True) + EPS) * nw_ref[...]
        o_ref[:, cols] = (y * sg_ref[:, cols]).astype(o_ref.dtype)
    if emit_state:
        for d in range(2):
            for h in range(H_A):
                so_ref[d, h] = st_scr[d, h].T


def _hgrn_call(p, nw, tabs, seq_len, n_seq, row_blk0, s0, jl, mix, state_out, name):
    wf, lvf, wb, lvb = tabs
    has_init = s0 is not None
    emit_state = s0 is None
    const = lambda b: (0, 0)
    blk = (seq_len, F_A)
    state_spec = pl.BlockSpec((None, None, 2, H_A, DK_A, DV_A), lambda b: (b, jl, 0, 0, 0, 0))
    in_specs = [pl.BlockSpec(blk, (lambda b, part=part: (row_blk0 + b, part))) for part in range(5)]
    in_specs += [
        pl.BlockSpec((None, 1, DV_A), lambda b: (jl, 0, 0)),
        pl.BlockSpec(wf.shape, const), pl.BlockSpec(wb.shape, const),
        pl.BlockSpec(lvf.shape, const), pl.BlockSpec(lvb.shape, const),
    ]
    args = [p, p, p, p, p, nw, wf, wb, lvf, lvb]
    if has_init:
        in_specs.append(state_spec)
        args.append(s0)
    out_shape = [jax.ShapeDtypeStruct((N_TOK, D_MIX), BF16)]
    out_specs = [pl.BlockSpec((seq_len, H_A * DV_A), lambda b: (row_blk0 + b, 0))]
    carried = [mix]
    if emit_state:
        out_shape.append(jax.ShapeDtypeStruct((BATCH, N_EVEN, 2, H_A, DK_A, DV_A), F32))
        out_specs.append(state_spec)
        carried.append(state_out)
    return _aliased_call(
        functools.partial(_hgrn_kernel, n_chunks=seq_len // HGRN_CHUNK, has_init=has_init,
                          emit_state=emit_state),
        grid=(n_seq,), in_specs=in_specs, args=args, out_specs=out_specs,
        out_shape=out_shape, carried=carried, sem=("parallel",), name=name,
        scratch_shapes=[pltpu.VMEM((seq_len, H_A * DV_A), F32),
                        pltpu.VMEM((seq_len, H_A * DV_A), F32),
                        pltpu.VMEM((2, H_A, DV_A, DK_A), F32)])


def _swa_ctx_kernel(q_ref, kv_ref, sink_ref, o_ref, kc_ref, vc_ref):
    kv = kv_ref[...]
    k32 = kv[:, 0:KV_B * HD_B]
    v32 = kv[:, KV_B * HD_B:2 * KV_B * HD_B]
    kc_ref[...] = k32
    vc_ref[...] = v32
    k = k32.astype(BF16)
    v = v32.astype(BF16)
    for r in range(q_ref.shape[0] // ATT_RB):
        rows = slice(r * ATT_RB, (r + 1) * ATT_RB)
        q = q_ref[rows, :].astype(BF16)
        outs = []
        for h in range(H_B):
            n = h // G_B
            s = _nt_dot(q[:, h * HD_B:(h + 1) * HD_B], k[:, n * HD_B:(n + 1) * HD_B])
            sink = sink_ref[0:1, h:h + 1] * LOG2_E
            m = jnp.maximum(jnp.max(s, axis=-1, keepdims=True), sink)
            p = jnp.exp2(s - m)
            den = jnp.sum(p, axis=-1, keepdims=True) + jnp.exp2(sink - m)
            pv = jnp.dot(p.astype(BF16), v[:, n * HD_B:(n + 1) * HD_B],
                         preferred_element_type=F32)
            outs.append(pv / den)
        o_ref[rows, :] = jnp.concatenate(outs, axis=1).astype(o_ref.dtype)


def _swa_ctx_call(p, sink, jl, mix, k_out, v_out, name):
    qcol = (3 * F_A + 2 * H_A * DV_A) // PROJ_TN
    cache_spec = pl.BlockSpec((None, None, SEQ, KV_B * HD_B), lambda b: (b, jl, 0, 0))
    cache_shape = jax.ShapeDtypeStruct((BATCH, N_EVEN, SEQ, KV_B * HD_B), F32)
    return _aliased_call(
        _swa_ctx_kernel,
        grid=(BATCH,),
        in_specs=[
            pl.BlockSpec((SEQ, H_B * HD_B), lambda b: (b, qcol)),
            pl.BlockSpec((SEQ, PROJ_TN), lambda b: (b, qcol + 1)),
            pl.BlockSpec((None, 1, H_B), lambda b: (jl, 0, 0)),
        ],
        args=[p, p, sink],
        out_specs=[pl.BlockSpec((SEQ, H_B * HD_B), lambda b: (b, 1)), cache_spec, cache_spec],
        out_shape=[jax.ShapeDtypeStruct((N_TOK, D_MIX), BF16), cache_shape, cache_shape],
        carried=[mix, k_out, v_out], sem=("parallel",), name=name)


def _swa_lat_kernel(q_ref, kv_ref, ck_ref, cv_ref, sink_ref, o_ref):
    qi = pl.program_id(1)
    tq = q_ref.shape[0]
    span = tq + 2 * WINDOW
    ws = pl.multiple_of(jnp.clip(qi * tq - WINDOW, 0, DEC_SEQ - span), WINDOW)
    kvw = kv_ref[pl.ds(ws, span), :]
    kw = kvw[:, 0:KV_B * HD_B].astype(BF16)
    vw = kvw[:, KV_B * HD_B:2 * KV_B * HD_B].astype(BF16)
    kc = ck_ref[...].astype(BF16)
    vc = cv_ref[...].astype(BF16)
    q = q_ref[...].astype(BF16)
    t_q = qi * tq + lax.broadcasted_iota(jnp.int32, (tq, span), 0)
    t_k = ws + lax.broadcasted_iota(jnp.int32, (tq, span), 1)
    valid = jnp.abs(t_q - t_k) <= WINDOW
    outs = []
    for h in range(H_B):
        n = h // G_B
        qh = q[:, h * HD_B:(h + 1) * HD_B]
        ksl = slice(n * HD_B, (n + 1) * HD_B)
        s_w = jnp.where(valid, _nt_dot(qh, kw[:, ksl]), -jnp.inf)
        s_c = _nt_dot(qh, kc[:, ksl])
        sink = sink_ref[0:1, h:h + 1] * LOG2_E
        m = jnp.maximum(jnp.maximum(jnp.max(s_w, axis=-1, keepdims=True),
                                    jnp.max(s_c, axis=-1, keepdims=True)), sink)
        p_w = jnp.exp2(s_w - m)
        p_c = jnp.exp2(s_c - m)
        den = (jnp.sum(p_w, axis=-1, keepdims=True) + jnp.sum(p_c, axis=-1, keepdims=True)
               + jnp.exp2(sink - m))
        pv = (jnp.dot(p_w.astype(BF16), vw[:, ksl], preferred_element_type=F32)
              + jnp.dot(p_c.astype(BF16), vc[:, ksl], preferred_element_type=F32))
        outs.append(pv / den)
    o_ref[...] = jnp.concatenate(outs, axis=1).astype(o_ref.dtype)


def _swa_lat_call(p, ck, cv, sink, jl, mix, name):
    tq = ATT_TQ
    qcol = (3 * F_A + 2 * H_A * DV_A) // PROJ_TN
    nq = DEC_SEQ // tq
    q_blk0 = N_CTX_TOK // tq
    s_blk0 = N_CTX_TOK // DEC_SEQ
    cache_spec = pl.BlockSpec((None, None, PAST_LEN, KV_B * HD_B), lambda b, i: (b, jl, 0, 0))
    (out,) = _aliased_call(
        _swa_lat_kernel,
        grid=(DEC_BATCH, nq),
        in_specs=[
            pl.BlockSpec((tq, H_B * HD_B), lambda b, i: (q_blk0 + b * nq + i, qcol)),
            pl.BlockSpec((DEC_SEQ, PROJ_TN), lambda b, i: (s_blk0 + b, qcol + 1)),
            cache_spec, cache_spec,
            pl.BlockSpec((None, 1, H_B), lambda b, i: (jl, 0, 0)),
        ],
        args=[p, p, ck, cv, sink],
        out_specs=[pl.BlockSpec((tq, H_B * HD_B), lambda b, i: (q_blk0 + b * nq + i, 1))],
        out_shape=[jax.ShapeDtypeStruct((N_TOK, D_MIX), BF16)],
        carried=[mix], sem=("parallel", "parallel"), name=name)
    return out


def _diff_lambda(lp_ref, lam_init):
    lp = lp_ref[...]
    a = jnp.sum(lp[0:1] * lp[1:2], axis=-1, keepdims=True)
    b = jnp.sum(lp[2:3] * lp[3:4], axis=-1, keepdims=True)
    return jnp.exp(a) - jnp.exp(b) + lam_init


def _diff_core(q, k_parts, v_parts, lam, lam_init, sw):
    pr = []
    for c in range(2):
        sl = slice(c * HD_C, (c + 1) * HD_C)
        ss = [_nt_dot(q[:, sl], kp[:, sl]) for kp in k_parts]
        m = functools.reduce(jnp.maximum, [jnp.max(s, axis=-1, keepdims=True) for s in ss])
        es = [jnp.exp2(s - m) for s in ss]
        den = functools.reduce(lambda a, b: a + b,
                               [jnp.sum(e, axis=-1, keepdims=True) for e in es])
        inv = (1.0 if c == 0 else lam) / den
        pr.append([e * inv for e in es])
    o = functools.reduce(
        lambda a, b: a + b,
        [jnp.dot((p0 - p1).astype(BF16), vp, preferred_element_type=F32)
         for p0, p1, vp in zip(pr[0], pr[1], v_parts)])
    y = o * lax.rsqrt(jnp.mean(o * o, axis=-1, keepdims=True) + EPS) * sw
    return y * (1.0 - lam_init)


def _diff_ctx_kernel(q_ref, k_ref, v_ref, lp_ref, sw_ref, o_ref, kc_ref, vc_ref, *, lam_init):
    lam = _diff_lambda(lp_ref, lam_init)
    hw = 2 * HD_C
    for h in range(H_C):
        sl = slice(h * hw, (h + 1) * hw)
        k32 = k_ref[:, sl]
        v32 = v_ref[:, sl]
        kc_ref[:, sl] = k32
        vc_ref[:, sl] = v32
        k_parts = [k32.astype(BF16)]
        v_parts = [v32.astype(BF16)]
        for r in range(q_ref.shape[0] // ATT_RB):
            rows = slice(r * ATT_RB, (r + 1) * ATT_RB)
            y = _diff_core(q_ref[rows, sl].astype(BF16), k_parts, v_parts, lam, lam_init,
                           sw_ref[...])
            o_ref[rows, sl] = y.astype(o_ref.dtype)


def _diff_ctx_call(p, lp, sw, lam_init, jl, k_out, v_out, name):
    cache_spec = pl.BlockSpec((None, None, SEQ, ODD_W), lambda b: (b, jl, 0, 0))
    cache_shape = jax.ShapeDtypeStruct((BATCH, N_ODD, SEQ, ODD_W), F32)
    return _aliased_call(
        functools.partial(_diff_ctx_kernel, lam_init=lam_init),
        grid=(BATCH,),
        in_specs=[
            pl.BlockSpec((SEQ, ODD_W), lambda b: (b, 0)),
            pl.BlockSpec((SEQ, ODD_W), lambda b: (b, 1)),
            pl.BlockSpec((SEQ, ODD_W), lambda b: (b, 2)),
            pl.BlockSpec((None, 4, HD_C), lambda b: (jl, 0, 0)),
            pl.BlockSpec((None, 1, 2 * HD_C), lambda b: (jl, 0, 0)),
        ],
        args=[p, p, p, lp, sw],
        out_specs=[pl.BlockSpec((SEQ, ODD_W), lambda b: (b, 0)), cache_spec, cache_spec],
        out_shape=[jax.ShapeDtypeStruct((N_TOK, D_MIX), BF16), cache_shape, cache_shape],
        carried=[None, k_out, v_out], sem=("parallel",), name=name)


def _diff_lat_kernel(q_ref, k_ref, v_ref, ck_ref, cv_ref, lp_ref, sw_ref, o_ref, *, lam_init):
    lam = _diff_lambda(lp_ref, lam_init)
    k_parts = [k_ref[...].astype(BF16), ck_ref[...].astype(BF16)]
    v_parts = [v_ref[...].astype(BF16), cv_ref[...].astype(BF16)]
    tq = ATT_TQ
    for i in range(q_ref.shape[0] // tq):
        rows = slice(i * tq, (i + 1) * tq)
        y = _diff_core(q_ref[rows, :].astype(BF16), k_parts, v_parts, lam, lam_init, sw_ref[...])
        o_ref[rows, :] = y.astype(o_ref.dtype)


def _diff_lat_call(p, ck, cv, lp, sw, lam_init, jl, mix, name):
    hw = 2 * HD_C
    s_blk0 = N_CTX_TOK // DEC_SEQ
    cache_spec = pl.BlockSpec((None, None, PAST_LEN, hw), lambda b, h: (b, jl, 0, h))
    (out,) = _aliased_call(
        functools.partial(_diff_lat_kernel, lam_init=lam_init),
        grid=(DEC_BATCH, H_C),
        in_specs=[
            pl.BlockSpec((DEC_SEQ, hw), lambda b, h: (s_blk0 + b, h)),
            pl.BlockSpec((DEC_SEQ, hw), lambda b, h: (s_blk0 + b, H_C + h)),
            pl.BlockSpec((DEC_SEQ, hw), lambda b, h: (s_blk0 + b, 2 * H_C + h)),
            cache_spec, cache_spec,
            pl.BlockSpec((None, 4, HD_C), lambda b, h: (jl, 0, 0)),
            pl.BlockSpec((None, 1, hw), lambda b, h: (jl, 0, 0)),
        ],
        args=[p, p, p, ck, cv, lp, sw],
        out_specs=[pl.BlockSpec((DEC_SEQ, hw), lambda b, h: (s_blk0 + b, h))],
        out_shape=[jax.ShapeDtypeStruct((N_TOK, D_MIX), BF16)],
        carried=[mix], sem=("parallel", "parallel"), name=name)
    return out


def _mlp_kernel(x_ref, mix_ref, wo_ref, mods_ref, nw_ref, w1_ref, w2_ref, o_ref,
                h_scr, *, lat):
    i = pl.program_id(0)
    k = pl.program_id(1)
    row = 1 + i if lat else 0

    def mod(a):
        return mods_ref[pl.ds(row, 1), a * D_MODEL:(a + 1) * D_MODEL]

    @pl.when(k == 0)
    def _():
        for r in range(x_ref.shape[0] // PROJ_RC):
            rows = slice(r * PROJ_RC, (r + 1) * PROJ_RC)
            y = jnp.dot(mix_ref[rows, :], wo_ref[...], preferred_element_type=F32)
            x1 = x_ref[rows, :] + mod(2) * y
            o_ref[rows, :] = x1
            h_scr[rows, :] = _norm_mod(x1, nw_ref[...], mod(4), mod(3)).astype(BF16)

    u = jnp.dot(h_scr[...], w1_ref[...].astype(BF16), preferred_element_type=F32)
    u = jnp.square(jnp.maximum(u, 0.0)).astype(BF16)
    o_ref[...] += mod(5) * jnp.dot(u, w2_ref[...].astype(BF16), preferred_element_type=F32)


def _mlp_call(x_src, x_tile0, lat, mix, wo, jl, mods, li, norm_w, w1, w2, out_prev, out_rows,
              out_tile0, name):
    tm, tk = MLP_TM, MLP_TK
    n_tiles = (N_LAT_TOK if lat else N_CTX_TOK) // tm
    mix_tile0 = N_CTX_TOK // tm if lat else 0
    (out,) = _aliased_call(
        functools.partial(_mlp_kernel, lat=lat),
        grid=(n_tiles, D_FF // tk),
        in_specs=[
            pl.BlockSpec((tm, D_MODEL), lambda i, k: (x_tile0 + i, 0)),
            pl.BlockSpec((tm, D_MIX), lambda i, k: (mix_tile0 + i, 0)),
            pl.BlockSpec((None, D_MIX, D_MODEL), lambda i, k: (jl, 0, 0)),
            pl.BlockSpec((None, MOD_ROWS, N_MOD), lambda i, k: (li, 0, 0)),
            pl.BlockSpec((None, None, 1, D_MODEL), lambda i, k: (li, 1, 0, 0)),
            pl.BlockSpec((None, D_MODEL, tk), lambda i, k: (li, 0, k)),
            pl.BlockSpec((None, tk, D_MODEL), lambda i, k: (li, k, 0)),
        ],
        args=[x_src, mix, wo, mods, norm_w, w1, w2],
        out_specs=[pl.BlockSpec((tm, D_MODEL), lambda i, k: (out_tile0 + i, 0))],
        out_shape=[jax.ShapeDtypeStruct((out_rows, D_MODEL), F32)],
        carried=[out_prev], sem=("parallel", "arbitrary"), name=name,
        scratch_shapes=[pltpu.VMEM((tm, D_MODEL), BF16)])
    return out


def _rope_tables(width):
    t = np.arange(DEC_SEQ)
    half = HEAD_GROUP // 2
    inv = ROPE_BASE ** (-np.arange(0, half, 2, dtype=np.float64) / half)
    ang = np.concatenate([(t // GRID_W)[:, None] * inv, (t % GRID_W)[:, None] * inv], axis=-1)
    cos = np.repeat(np.cos(ang), 2, axis=-1)
    sin = np.repeat(np.sin(ang), 2, axis=-1)
    sign = np.tile(np.array([-1.0, 1.0]), HEAD_GROUP // 2)
    reps = width // HEAD_GROUP
    return (jnp.asarray(np.tile(cos, (1, reps)), F32),
            jnp.asarray(np.tile(sin * sign, (1, reps)), F32))


def _block_diag_ones(n):
    g = np.arange(n) // HEAD_GROUP
    return jnp.asarray(g[:, None] == g[None, :], BF16)


def _tile_row(w, width):
    return jnp.tile(w.astype(F32), width // w.shape[0])[None, :]


def _lambda_init(li):
    return 0.8 - 0.6 * math.exp(-0.3 * li)


def kernel(x_prompt, x_sample, cache_k_swa, cache_v_swa, state_hgrn, cache_k_diff, cache_v_diff, c, c_ctx, norm_w, w_ada, b_ada, w_in_even, w_out_even, hgrn_lb_logits, hgrn_norm_w, swa_qnorm_w, swa_knorm_w, swa_sink, w_in_odd, w_out_odd, diff_qnorm_w, diff_knorm_w, diff_lambda_p, diff_subln_w, w_mlp1, w_mlp2):
    assert PROJ_TM == MLP_TM
    n_ctx_tiles = N_CTX_TOK // PROJ_TM
    x = None
    x_ctx0 = x_prompt.reshape(N_CTX_TOK, D_MODEL)
    x_lat0 = x_sample.reshape(N_LAT_TOK, D_MODEL)
    c_all = jnp.concatenate(
        [c_ctx[None, :], c, jnp.zeros((MOD_ROWS - 1 - DEC_BATCH, D_MODEL), F32)], axis=0)
    mods = _mods_call(c_all, w_ada, b_ada.reshape(DEPTH, 1, N_MOD))

    cos_t, sin_t = _rope_tables(PROJ_TN)
    bd = _block_diag_ones(256)
    tabs_f = _hgrn_tables(False)
    tabs_b = _hgrn_tables(True)
    hgrn_tabs = (tabs_f[0], tabs_f[1], tabs_b[0], tabs_b[1])
    lbl = hgrn_lb_logits.astype(F32).reshape(N_EVEN * 2, F_A)
    norm_w4 = norm_w.astype(F32).reshape(DEPTH, 2, 1, D_MODEL)

    w_out_even_b = w_out_even.astype(BF16)
    w_out_odd_b = w_out_odd.astype(BF16)

    ck_swa = cache_k_swa.reshape(DEC_BATCH, N_EVEN, PAST_LEN, KV_B * HD_B)
    cv_swa = cache_v_swa.reshape(DEC_BATCH, N_EVEN, PAST_LEN, KV_B * HD_B)
    ck_diff = cache_k_diff.reshape(DEC_BATCH, N_ODD, PAST_LEN, ODD_W)
    cv_diff = cache_v_diff.reshape(DEC_BATCH, N_ODD, PAST_LEN, ODD_W)
    hgrn_nw = hgrn_norm_w.astype(F32).reshape(N_EVEN, 1, DV_A)
    sink = swa_sink.astype(F32).reshape(N_EVEN, 1, H_B)
    lam_p = diff_lambda_p.astype(F32)
    subln = diff_subln_w.astype(F32).reshape(N_ODD, 1, 2 * HD_C)

    even_kinds = ("silu_scale", "loggate0", "loggate1", "ident", "silu", "qnorm", "kv")
    odd_kinds = ("qnorm", "qnorm", "knorm", "knorm", "ident", "ident")

    k_swa = v_swa = states = k_diff = v_diff = None
    for li in range(DEPTH):
        j = li // 2
        srcs = ((x_ctx0, 0), (x_lat0, 0)) if li == 0 else ((x, 0), (x, n_ctx_tiles))
        if li % 2 == 0:
            p = None
            for lat, (src, t0) in enumerate(srcs):
                p = _proj_call(src, t0, bool(lat), p, mods, li, norm_w4, w_in_even, j, lbl,
                               _tile_row(swa_qnorm_w[j], PROJ_TN),
                               _tile_row(swa_knorm_w[j], PROJ_TN),
                               cos_t, sin_t, bd, even_kinds, f"proj_even{j}_{lat}")
            mix, states = _hgrn_call(p, hgrn_nw, hgrn_tabs, SEQ, BATCH, 0, None, j, None, states,
                                     f"hgrn_ctx{j}")
            (mix,) = _hgrn_call(p, hgrn_nw, hgrn_tabs, DEC_SEQ, DEC_BATCH, N_CTX_TOK // DEC_SEQ,
                                state_hgrn, j, mix, None, f"hgrn_lat{j}")
            mix, k_swa, v_swa = _swa_ctx_call(p, sink, j, mix, k_swa, v_swa, f"swa_ctx{j}")
            mix = _swa_lat_call(p, ck_swa, cv_swa, sink, j, mix, f"swa_lat{j}")
            wo = w_out_even_b
        else:
            p = None
            for lat, (src, t0) in enumerate(srcs):
                p = _proj_call(src, t0, bool(lat), p, mods, li, norm_w4, w_in_odd, j, lbl,
                               _tile_row(diff_qnorm_w[j], PROJ_TN),
                               _tile_row(diff_knorm_w[j], PROJ_TN),
                               cos_t, sin_t, bd, odd_kinds, f"proj_odd{j}_{lat}")
            lam_init = _lambda_init(li)
            mix, k_diff, v_diff = _diff_ctx_call(p, lam_p, subln, lam_init, j, k_diff, v_diff,
                                                 f"diff_ctx{j}")
            mix = _diff_lat_call(p, ck_diff, cv_diff, lam_p, subln, lam_init, j, mix,
                                 f"diff_lat{j}")
            wo = w_out_odd_b
        last = li == DEPTH - 1
        outs = []
        x_next = None
        for lat, (src, t0) in enumerate(srcs):
            rows = (N_LAT_TOK if lat else N_CTX_TOK) if last else N_TOK
            out_t0 = 0 if last else lat * n_ctx_tiles
            x_next = _mlp_call(src, t0, bool(lat), mix, wo, j, mods, li, norm_w4, w_mlp1, w_mlp2,
                               None if last else x_next, rows, out_t0, f"mlp{li}_{lat}")
            outs.append(x_next)
        x = x_next

    y_prompt = outs[0].reshape(BATCH, SEQ, D_MODEL)
    y_sample = outs[1].reshape(DEC_BATCH, DEC_SEQ, D_MODEL)
    return (y_prompt, y_sample,
            k_swa.reshape(BATCH, N_EVEN, SEQ, KV_B, HD_B),
            v_swa.reshape(BATCH, N_EVEN, SEQ, KV_B, HD_B),
            states,
            k_diff.reshape(BATCH, N_ODD, SEQ, H_C, 2, HD_C),
            v_diff.reshape(BATCH, N_ODD, SEQ, H_C, 2 * HD_C))
```

```python
import functools
import math

import numpy as np
import jax
import jax.numpy as jnp
from jax import lax
from jax.experimental import pallas as pl
from jax.experimental.pallas import tpu as pltpu

F32 = jnp.float32
BF16 = jnp.bfloat16

D_MODEL = 1024
BATCH = 16
SEQ = 256
DEPTH = 4
DEC_BATCH = 4
DEC_SEQ = 1024
PAST_LEN = 512
GRID_W = 64
N_EVEN = (DEPTH + 1) // 2
N_ODD = DEPTH // 2
H_A = 4
DK_A = 128
DV_A = D_MODEL // 2 // H_A
F_A = H_A * DK_A
H_B = 8
KV_B = 2
G_B = H_B // KV_B
HD_B = D_MODEL // 2 // H_B
WINDOW = 128
H_C = 8
HD_C = D_MODEL // (2 * H_C)
D_FF = 4 * D_MODEL
ROPE_BASE = 10000.0
EPS = 1e-6
EVEN_COLS = 3 * F_A + 2 * H_A * DV_A + (H_B + 2 * KV_B) * HD_B
ODD_W = H_C * 2 * HD_C
D_MIX = D_MODEL

N_CTX_TOK = BATCH * SEQ
N_LAT_TOK = DEC_BATCH * DEC_SEQ
N_TOK = N_CTX_TOK + N_LAT_TOK
MOD_ROWS = 8
N_MOD = 6 * D_MODEL

HEAD_GROUP = 64
HGRN_CHUNK = 128
HGRN_LEVELS = 7
HGRN_SPLIT = 3
LOG2_E = math.log2(math.e)
VMEM_LIMIT = 48 * 1024 * 1024

PROJ_TM = 1024
PROJ_TN = 512
PROJ_RC = 256
MLP_TM = 1024
MLP_TK = 1024
ADA_TN = 1536
ATT_TQ = 256
ATT_RB = 256


def _silu(x):
    return x * jax.nn.sigmoid(x)


def _nt_dot(a, b):
    return lax.dot_general(a, b, (((1,), (1,)), ((), ())), preferred_element_type=F32)


def _pipelined(n, first, second):
    cur = first(0)
    for i in range(n):
        nxt = first(i + 1) if i + 1 < n else None
        second(i, cur)
        cur = nxt


def _params(sem):
    return pltpu.CompilerParams(dimension_semantics=sem, vmem_limit_bytes=VMEM_LIMIT)


def _aliased_call(kernel, *, grid, in_specs, args, out_specs, out_shape, carried, sem, name,
                  scratch_shapes=()):
    n_in = len(args)
    extra = [buf for buf in carried if buf is not None]
    aliases = {}
    for k, buf in enumerate(carried):
        if buf is not None:
            aliases[n_in + len(aliases)] = k
    n_extra = len(extra)

    def body(*refs):
        kernel(*refs[:n_in], *refs[n_in + n_extra:])

    return pl.pallas_call(
        body,
        grid=grid,
        in_specs=list(in_specs) + [pl.BlockSpec(memory_space=pl.ANY)] * n_extra,
        out_specs=out_specs,
        out_shape=out_shape,
        input_output_aliases=aliases,
        scratch_shapes=list(scratch_shapes),
        compiler_params=_params(sem),
        name=name,
    )(*args, *extra)


def _mods_kernel(c_ref, w_ref, b_ref, o_ref):
    s = _silu(c_ref[...]).astype(BF16)
    o_ref[...] = jnp.dot(s, w_ref[...].astype(BF16), preferred_element_type=F32) + b_ref[...]


def _mods_call(c_all, w_ada, b_ada):
    return pl.pallas_call(
        _mods_kernel,
        grid=(DEPTH, N_MOD // ADA_TN),
        in_specs=[
            pl.BlockSpec((MOD_ROWS, D_MODEL), lambda l, j: (0, 0)),
            pl.BlockSpec((None, D_MODEL, ADA_TN), lambda l, j: (l, 0, j)),
            pl.BlockSpec((None, 1, ADA_TN), lambda l, j: (l, 0, j)),
        ],
        out_specs=pl.BlockSpec((None, MOD_ROWS, ADA_TN), lambda l, j: (l, 0, j)),
        out_shape=jax.ShapeDtypeStruct((DEPTH, MOD_ROWS, N_MOD), F32),
        compiler_params=_params(("parallel", "parallel")),
        name="ada_mods",
    )(c_all, w_ada, b_ada)


def _norm_mod(x, nw, sc, sh):
    ms = jnp.mean(x * x, axis=-1, keepdims=True)
    return (x * lax.rsqrt(ms + EPS) * nw) * (1.0 + sc) + sh


def _group_rms(y, w_t, bd_ref):
    yy = (y * y).astype(BF16)
    bw = bd_ref.shape[0]
    parts = [jnp.dot(yy[:, s:s + bw], bd_ref[...], preferred_element_type=F32)
             for s in range(0, y.shape[1], bw)]
    ss = parts[0] if len(parts) == 1 else jnp.concatenate(parts, axis=1)
    return y * lax.rsqrt(ss * (1.0 / HEAD_GROUP) + EPS) * w_t


def _rope(y, cos, sin):
    n = y.shape[1]
    lane = lax.broadcasted_iota(jnp.int32, y.shape, 1)
    nxt = pltpu.roll(y, n - 1, axis=1)
    prv = pltpu.roll(y, 1, axis=1)
    swapped = jnp.where((lane & 1) == 0, nxt, prv)
    return y * cos + swapped * sin


def _lower_bounds(lbl_ref, jl):
    rows = [lbl_ref[pl.ds(2 * m, 2), :] for m in range(N_EVEN)]
    mx = functools.reduce(jnp.maximum, rows)
    es = [jnp.exp(r - mx) for r in rows]
    den = functools.reduce(lambda a, b: a + b, es)
    sm = [e / den for e in es]
    cs = sm[0]
    for m in range(1, jl + 1):
        cs = cs + sm[m]
    return cs - sm[0]


def _proj_kernel(x_ref, mods_ref, nw_ref, w_ref, wkv_ref, lbl_ref, qn_ref, kn_ref, cos_ref,
                 sin_ref, bd_ref, o_ref, h_scr, w_scr, *, kinds, jl, lat):
    j = pl.program_id(0)
    i = pl.program_id(1)
    tm, tn = o_ref.shape
    kvw = 2 * KV_B * HD_B

    @pl.when(j == 0)
    def _():
        row = 1 + i if lat else 0
        sh = mods_ref[pl.ds(row, 1), 0:D_MODEL]
        sc = mods_ref[pl.ds(row, 1), D_MODEL:2 * D_MODEL]
        h_scr[i] = _norm_mod(x_ref[...], nw_ref[...], sc, sh).astype(BF16)

    def finish(kind, y, rows):
        if kind == "silu_scale":
            return _silu(y) * (DK_A ** -0.5)
        if kind in ("loggate0", "loggate1"):
            d = int(kind[-1])
            lb = _lower_bounds(lbl_ref, jl)[d:d + 1, :]
            return jnp.log2(lb + (1.0 - lb) * jax.nn.sigmoid(y))
        if kind == "ident":
            return y
        if kind == "silu":
            return _silu(y)
        if kind in ("qnorm", "knorm"):
            w_t = qn_ref[...] if kind == "qnorm" else kn_ref[...]
            r = _group_rms(y, w_t, bd_ref)
            if lat:
                r = _rope(r, cos_ref[rows, :], sin_ref[rows, :])
            return r * (HEAD_GROUP ** -0.5 * LOG2_E) if kind == "qnorm" else r
        if kind == "kv":
            kn = _group_rms(y, kn_ref[:, 0:kvw], bd_ref)
            if lat:
                kn = _rope(kn, cos_ref[rows, 0:kvw], sin_ref[rows, 0:kvw])
            lane = lax.broadcasted_iota(jnp.int32, y.shape, 1)
            return jnp.where(lane < KV_B * HD_B, kn, y)
        raise ValueError(kind)

    def run(kind):
        @pl.when(i == 0)
        def _():
            if kind == "kv":
                w_scr[:, 0:kvw] = wkv_ref[...].astype(BF16)
            else:
                w_scr[...] = w_ref[...].astype(BF16)

        def first(r):
            w = w_scr[:, 0:kvw] if kind == "kv" else w_scr[...]
            return jnp.dot(h_scr[i, r * PROJ_RC:(r + 1) * PROJ_RC, :], w,
                           preferred_element_type=F32)

        def second(r, y):
            rows = slice(r * PROJ_RC, (r + 1) * PROJ_RC)
            if kind == "kv":
                o_ref[rows, 0:kvw] = finish(kind, y, rows)
                o_ref[rows, kvw:tn] = jnp.zeros((PROJ_RC, tn - kvw), F32)
            else:
                o_ref[rows, :] = finish(kind, y, rows)

        _pipelined(tm // PROJ_RC, first, second)

    for jj, kind in enumerate(kinds):
        pl.when(j == jj)(functools.partial(run, kind))


def _proj_call(x_src, x_tile0, lat, p_prev, mods, li, norm_w, w, jl, lbl, qn_t, kn_t, cos_t, sin_t,
               bd, kinds, name):
    tm, tn = PROJ_TM, PROJ_TN
    assert tm == DEC_SEQ
    n_ctx_tiles = N_CTX_TOK // tm
    n_tiles = (N_LAT_TOK if lat else N_CTX_TOK) // tm
    tile0 = n_ctx_tiles if lat else 0
    n_main = sum(1 for k in kinds if k != "kv")
    n_cols = tn * len(kinds)
    kvw = 2 * KV_B * HD_B
    kv_blk = (n_main * tn) // kvw if "kv" in kinds else 0
    const = lambda j, i: (0, 0)
    (out,) = _aliased_call(
        functools.partial(_proj_kernel, kinds=kinds, jl=jl, lat=lat),
        grid=(len(kinds), n_tiles),
        in_specs=[
            pl.BlockSpec((tm, D_MODEL),
                         lambda j, i: (x_tile0 + jnp.where(j == 0, i, n_tiles - 1), 0)),
            pl.BlockSpec((None, MOD_ROWS, N_MOD), lambda j, i: (li, 0, 0)),
            pl.BlockSpec((None, None, 1, D_MODEL), lambda j, i: (li, 0, 0, 0)),
            pl.BlockSpec((None, D_MODEL, tn), lambda j, i: (jl, 0, jnp.minimum(j, n_main - 1))),
            pl.BlockSpec((None, D_MODEL, kvw), lambda j, i: (jl, 0, kv_blk)),
            pl.BlockSpec(lbl.shape, const),
            pl.BlockSpec((1, tn), const),
            pl.BlockSpec((1, tn), const),
            pl.BlockSpec((tm, tn), const),
            pl.BlockSpec((tm, tn), const),
            pl.BlockSpec(bd.shape, const),
        ],
        args=[x_src, mods, norm_w, w, w, lbl, qn_t, kn_t, cos_t, sin_t, bd],
        out_specs=[pl.BlockSpec((tm, tn), lambda j, i: (tile0 + i, j))],
        out_shape=[jax.ShapeDtypeStruct((N_TOK, n_cols), F32)],
        carried=[p_prev], sem=("arbitrary", "arbitrary"), name=name,
        scratch_shapes=[pltpu.VMEM((n_tiles, tm, D_MODEL), BF16),
                        pltpu.VMEM((D_MODEL, tn), BF16)])
    return out


def _hgrn_tables(rev):
    c = HGRN_CHUNK
    t = np.arange(c)
    w = (t[None, :] <= t[:, None]) if not rev else (t[None, :] >= t[:, None])
    ws = np.concatenate([w.astype(np.float32)] * HGRN_SPLIT, axis=1)
    x = t[:, None] ^ t[None, :]
    lv = np.where(x > 0, np.floor(np.log2(np.maximum(x, 1))).astype(np.int32), HGRN_LEVELS)
    causal = (t[None, :] < t[:, None]) if not rev else (t[None, :] > t[:, None])
    lv = np.where(causal | (x == 0), lv, -1).astype(np.int32)
    return jnp.asarray(ws, BF16), jnp.asarray(lv)


def _hgrn_level_exponents(cum, rev):
    c = HGRN_CHUNK
    sub_rows = 8
    c3 = cum.reshape(c // sub_rows, sub_rows, DK_A)
    sub = lax.broadcasted_iota(jnp.int32, c3.shape, 1)
    out = []
    for l in range(1, HGRN_LEVELS):
        hb = 1 << l
        if 2 * hb <= sub_rows:
            r = None
            for b0 in range(0, sub_rows, 2 * hb):
                idx = b0 + (hb if rev else hb - 1)
                rk = c3[:, idx:idx + 1, :]
                r = rk if r is None else jnp.where(sub < b0, r, rk)
            d = c3 - r
            bit = (sub & hb) != 0
            q_role = jnp.logical_not(bit) if rev else bit
            out.append(jnp.where(q_role, d, -d).reshape(c, DK_A))
        else:
            pieces = []
            for b0 in range(0, c, 2 * hb):
                mid = b0 + hb
                ridx = mid if rev else mid - 1
                r = cum[ridx:ridx + 1, :]
                lo = cum[b0:mid]
                hi = cum[mid:b0 + 2 * hb]
                pieces += [lo - r, r - hi] if rev else [r - lo, hi - r]
            out.append(jnp.concatenate(pieces, axis=0))
    return out


def _hgrn_level_operands(l, q, k, f, z, rev, row):
    c = HGRN_CHUNK
    hb = 1 << l
    if hb >= 8:
        zero = jnp.zeros((hb, DK_A), F32)
        qparts, kparts = [], []
        for b0 in range(0, c, 2 * hb):
            lo, hi = slice(b0, b0 + hb), slice(b0 + hb, b0 + 2 * hb)
            if rev:
                qparts += [q[lo] * z[lo], zero]
                kparts += [zero, k[hi] * z[hi]]
            else:
                qparts += [zero, q[hi] * z[hi]]
                kparts += [k[lo] * z[lo], zero]
        return (jnp.concatenate(qparts, axis=0).astype(BF16),
                jnp.concatenate(kparts, axis=0).astype(BF16))
    bit = ((row >> l) & 1) == 1
    q_role = jnp.logical_not(bit) if rev else bit
    ql = jnp.where(q_role, q * (f if l == 0 else z), 0.0).astype(BF16)
    kl = jnp.where(q_role, 0.0, k if l == 0 else k * z).astype(BF16)
    return ql, kl


def _hgrn_chunks(chains):
    c = HGRN_CHUNK
    row = lax.broadcasted_iota(jnp.int32, (c, DK_A), 0)
    cums = []
    for q, g, v, st, w_ref, lv_ref, rev in chains:
        terms = []
        rem = g
        for _ in range(HGRN_SPLIT):
            term = rem.astype(BF16)
            terms.append(term)
            rem = rem - term.astype(F32)
        cums.append(jnp.dot(w_ref[...], jnp.concatenate(terms, axis=0),
                            preferred_element_type=F32))
    work = []
    for (q, g, v, st, w_ref, lv_ref, rev), cum in zip(chains, cums):
        f = jnp.exp2(g)
        k = 1.0 - f
        last = 0 if rev else c - 1
        total_e = cum[last:last + 1, :]
        qd = (q * jnp.exp2(cum)).astype(BF16)
        kd = (k * jnp.exp2(total_e - cum)).astype(BF16)
        o = _nt_dot(qd, st.astype(BF16))
        st_new = (st * jnp.exp2(total_e)
                  + jnp.dot(v.T.astype(BF16), kd, preferred_element_type=F32))
        zs = [None] + [jnp.exp2(e) for e in _hgrn_level_exponents(cum, rev)]
        work.append((k, f, zs, o, st_new))
    accs = [None] * len(chains)
    for l in reversed(range(HGRN_LEVELS)):
        for i, ((q, g, v, st, w_ref, lv_ref, rev), (k, f, zs, o, st_new)) in enumerate(
                zip(chains, work)):
            ql, kl = _hgrn_level_operands(l, q, k, f, zs[l], rev, row)
            a_l = _nt_dot(ql, kl)
            accs[i] = a_l if accs[i] is None else jnp.where(lv_ref[...] == l, a_l, accs[i])
    outs = []
    for (q, g, v, st, w_ref, lv_ref, rev), (k, f, zs, o, st_new), a in zip(chains, work, accs):
        a = jnp.where(lv_ref[...] == HGRN_LEVELS, jnp.sum(q * k, axis=-1, keepdims=True), a)
        outs.append((o + jnp.dot(a.astype(BF16), v.astype(BF16), preferred_element_type=F32),
                     st_new))
    return outs


def _hgrn_kernel(*refs, n_chunks, has_init, emit_state):
    refs = list(refs)
    q_ref, gf_ref, gb_ref, v_ref, sg_ref, nw_ref, wf_ref, wb_ref, lvf_ref, lvb_ref = refs[:10]
    pos = 10
    s0_ref = None
    if has_init:
        s0_ref = refs[pos]
        pos += 1
    o_ref = refs[pos]
    pos += 1
    so_ref = None
    if emit_state:
        so_ref = refs[pos]
        pos += 1
    of_scr, ob_scr, st_scr = refs[pos:pos + 3]

    for d in range(2):
        for h in range(H_A):
            if has_init:
                st_scr[d, h] = s0_ref[d, h].T
            else:
                st_scr[d, h] = jnp.zeros((DV_A, DK_A), F32)

    def body(c, carry):
        chains, dests = [], []
        for h in range(H_A):
            cols = slice(h * DK_A, (h + 1) * DK_A)
            for d, (g_ref, w_ref, lv_ref, scr) in enumerate(
                    ((gf_ref, wf_ref, lvf_ref, of_scr), (gb_ref, wb_ref, lvb_ref, ob_scr))):
                cc = c if d == 0 else n_chunks - 1 - c
                r0 = pl.multiple_of(cc * HGRN_CHUNK, HGRN_CHUNK)
                rows = pl.ds(r0, HGRN_CHUNK)
                chains.append((q_ref[rows, cols], g_ref[rows, cols], v_ref[rows, cols],
                               st_scr[d, h], w_ref, lv_ref, d == 1))
                dests.append((scr, rows, cols, d, h))
        for (o, st), (scr, rows, cols, d, h) in zip(_hgrn_chunks(chains), dests):
            st_scr[d, h] = st
            scr[rows, cols] = o
        return carry

    lax.fori_loop(0, n_chunks, body, 0)
    for h in range(H_A):
        cols = slice(h * DV_A, (h + 1) * DV_A)
        o = of_scr[:, cols] + ob_scr[:, cols]
        y = o * lax.rsqrt(jnp.mean(o * o, axis=-1, keepdims=True) + EPS) * nw_ref[...]
        o_ref[:, cols] = (y * sg_ref[:, cols]).astype(o_ref.dtype)
    if emit_state:
        for d in range(2):
            for h in range(H_A):
                so_ref[d, h] = st_scr[d, h].T


def _hgrn_call(p, nw, tabs, seq_len, n_seq, row_blk0, s0, jl, mix, state_out, name):
    wf, lvf, wb, lvb = tabs
    has_init = s0 is not None
    emit_state = s0 is None
    const = lambda b: (0, 0)
    blk = (seq_len, F_A)
    state_spec = pl.BlockSpec((None, None, 2, H_A, DK_A, DV_A), lambda b: (b, jl, 0, 0, 0, 0))
    in_specs = [pl.BlockSpec(blk, (lambda b, part=part: (row_blk0 + b, part))) for part in range(5)]
    in_specs += [
        pl.BlockSpec((None, 1, DV_A), lambda b: (jl, 0, 0)),
        pl.BlockSpec(wf.shape, const), pl.BlockSpec(wb.shape, const),
        pl.BlockSpec(lvf.shape, const), pl.BlockSpec(lvb.shape, const),
    ]
    args = [p, p, p, p, p, nw, wf, wb, lvf, lvb]
    if has_init:
        in_specs.append(state_spec)
        args.append(s0)
    out_shape = [jax.ShapeDtypeStruct((N_TOK, D_MIX), BF16)]
    out_specs = [pl.BlockSpec((seq_len, H_A * DV_A), lambda b: (row_blk0 + b, 0))]
    carried = [mix]
    if emit_state:
        out_shape.append(jax.ShapeDtypeStruct((BATCH, N_EVEN, 2, H_A, DK_A, DV_A), F32))
        out_specs.append(state_spec)
        carried.append(state_out)
    return _aliased_call(
        functools.partial(_hgrn_kernel, n_chunks=seq_len // HGRN_CHUNK, has_init=has_init,
                          emit_state=emit_state),
        grid=(n_seq,), in_specs=in_specs, args=args, out_specs=out_specs,
        out_shape=out_shape, carried=carried, sem=("parallel",), name=name,
        scratch_shapes=[pltpu.VMEM((seq_len, H_A * DV_A), F32),
                        pltpu.VMEM((seq_len, H_A * DV_A), F32),
                        pltpu.VMEM((2, H_A, DV_A, DK_A), F32)])


def _swa_ctx_kernel(q_ref, kv_ref, sink_ref, o_ref, kc_ref, vc_ref):
    kv = kv_ref[...]
    k32 = kv[:, 0:KV_B * HD_B]
    v32 = kv[:, KV_B * HD_B:2 * KV_B * HD_B]
    kc_ref[...] = k32
    vc_ref[...] = v32
    k = k32.astype(BF16)
    vt = v32.T.astype(BF16)
    q = q_ref[...].astype(BF16)
    ksl = [slice((h // G_B) * HD_B, (h // G_B + 1) * HD_B) for h in range(H_B)]
    sts = [_nt_dot(k[:, ksl[h]], q[:, h * HD_B:(h + 1) * HD_B]) for h in range(H_B)]
    ps, dens = [], []
    for h in range(H_B):
        sink = sink_ref[0:1, h:h + 1] * LOG2_E
        m = jnp.maximum(jnp.max(sts[h], axis=0, keepdims=True), sink)
        p = jnp.exp2(sts[h] - m)
        dens.append(jnp.sum(p, axis=0, keepdims=True) + jnp.exp2(sink - m))
        ps.append(p.astype(BF16))
    outs = [jnp.dot(vt[ksl[h], :], ps[h], preferred_element_type=F32) / dens[h]
            for h in range(H_B)]
    o_ref[...] = jnp.concatenate(outs, axis=0).T.astype(o_ref.dtype)


def _swa_ctx_call(p, sink, jl, mix, k_out, v_out, name):
    qcol = (3 * F_A + 2 * H_A * DV_A) // PROJ_TN
    cache_spec = pl.BlockSpec((None, None, SEQ, KV_B * HD_B), lambda b: (b, jl, 0, 0))
    cache_shape = jax.ShapeDtypeStruct((BATCH, N_EVEN, SEQ, KV_B * HD_B), F32)
    return _aliased_call(
        _swa_ctx_kernel,
        grid=(BATCH,),
        in_specs=[
            pl.BlockSpec((SEQ, H_B * HD_B), lambda b: (b, qcol)),
            pl.BlockSpec((SEQ, PROJ_TN), lambda b: (b, qcol + 1)),
            pl.BlockSpec((None, 1, H_B), lambda b: (jl, 0, 0)),
        ],
        args=[p, p, sink],
        out_specs=[pl.BlockSpec((SEQ, H_B * HD_B), lambda b: (b, 1)), cache_spec, cache_spec],
        out_shape=[jax.ShapeDtypeStruct((N_TOK, D_MIX), BF16), cache_shape, cache_shape],
        carried=[mix, k_out, v_out], sem=("parallel",), name=name)


def _swa_lat_kernel(q_ref, kv_ref, ck_ref, cv_ref, sink_ref, o_ref):
    qi = pl.program_id(1)
    tq = q_ref.shape[0]
    span = tq + 2 * WINDOW
    ws = pl.multiple_of(jnp.clip(qi * tq - WINDOW, 0, DEC_SEQ - span), WINDOW)
    kvw = kv_ref[pl.ds(ws, span), :]
    kw = kvw[:, 0:KV_B * HD_B].astype(BF16)
    vwt = kvw[:, KV_B * HD_B:2 * KV_B * HD_B].T.astype(BF16)
    kc = ck_ref[...].astype(BF16)
    vct = cv_ref[...].T.astype(BF16)
    q = q_ref[...].astype(BF16)
    t_k = ws + lax.broadcasted_iota(jnp.int32, (span, tq), 0)
    t_q = qi * tq + lax.broadcasted_iota(jnp.int32, (span, tq), 1)
    valid = jnp.abs(t_q - t_k) <= WINDOW
    outs = []

    def first(h):
        qh = q[:, h * HD_B:(h + 1) * HD_B]
        ksl = slice((h // G_B) * HD_B, (h // G_B + 1) * HD_B)
        return _nt_dot(kw[:, ksl], qh), _nt_dot(kc[:, ksl], qh)

    def second(h, scores):
        ksl = slice((h // G_B) * HD_B, (h // G_B + 1) * HD_B)
        s_w = jnp.where(valid, scores[0], -jnp.inf)
        s_c = scores[1]
        sink = sink_ref[0:1, h:h + 1] * LOG2_E
        m = jnp.maximum(jnp.maximum(jnp.max(s_w, axis=0, keepdims=True),
                                    jnp.max(s_c, axis=0, keepdims=True)), sink)
        p_w = jnp.exp2(s_w - m)
        p_c = jnp.exp2(s_c - m)
        den = (jnp.sum(p_w, axis=0, keepdims=True) + jnp.sum(p_c, axis=0, keepdims=True)
               + jnp.exp2(sink - m))
        ot = (jnp.dot(vwt[ksl, :], p_w.astype(BF16), preferred_element_type=F32)
              + jnp.dot(vct[ksl, :], p_c.astype(BF16), preferred_element_type=F32))
        outs.append(ot / den)

    _pipelined(H_B, first, second)
    o_ref[...] = jnp.concatenate(outs, axis=0).T.astype(o_ref.dtype)


def _swa_lat_call(p, ck, cv, sink, jl, mix, name):
    tq = ATT_TQ
    qcol = (3 * F_A + 2 * H_A * DV_A) // PROJ_TN
    nq = DEC_SEQ // tq
    q_blk0 = N_CTX_TOK // tq
    s_blk0 = N_CTX_TOK // DEC_SEQ
    cache_spec = pl.BlockSpec((None, None, PAST_LEN, KV_B * HD_B), lambda b, i: (b, jl, 0, 0))
    (out,) = _aliased_call(
        _swa_lat_kernel,
        grid=(DEC_BATCH, nq),
        in_specs=[
            pl.BlockSpec((tq, H_B * HD_B), lambda b, i: (q_blk0 + b * nq + i, qcol)),
            pl.BlockSpec((DEC_SEQ, PROJ_TN), lambda b, i: (s_blk0 + b, qcol + 1)),
            cache_spec, cache_spec,
            pl.BlockSpec((None, 1, H_B), lambda b, i: (jl, 0, 0)),
        ],
        args=[p, p, ck, cv, sink],
        out_specs=[pl.BlockSpec((tq, H_B * HD_B), lambda b, i: (q_blk0 + b * nq + i, 1))],
        out_shape=[jax.ShapeDtypeStruct((N_TOK, D_MIX), BF16)],
        carried=[mix], sem=("parallel", "parallel"), name=name)
    return out


def _diff_lambda(lp_ref, lam_init):
    lp = lp_ref[...]
    a = jnp.sum(lp[0:1] * lp[1:2], axis=-1, keepdims=True)
    b = jnp.sum(lp[2:3] * lp[3:4], axis=-1, keepdims=True)
    return jnp.exp(a) - jnp.exp(b) + lam_init


def _diff_scores(q, k_parts):
    return [[_nt_dot(kp[:, c * HD_C:(c + 1) * HD_C], q[:, c * HD_C:(c + 1) * HD_C])
             for kp in k_parts] for c in range(2)]


def _diff_finish(scores, vt_parts, lam, lam_init, sw_t):
    pr = []
    for c in range(2):
        ss = scores[c]
        m = functools.reduce(jnp.maximum, [jnp.max(s, axis=0, keepdims=True) for s in ss])
        es = [jnp.exp2(s - m) for s in ss]
        den = functools.reduce(lambda a, b: a + b,
                               [jnp.sum(e, axis=0, keepdims=True) for e in es])
        inv = (1.0 if c == 0 else lam) / den
        pr.append([e * inv for e in es])
    ot = functools.reduce(
        lambda a, b: a + b,
        [jnp.dot(vt, (p0 - p1).astype(BF16), preferred_element_type=F32)
         for p0, p1, vt in zip(pr[0], pr[1], vt_parts)])
    yt = ot * lax.rsqrt(jnp.mean(ot * ot, axis=0, keepdims=True) + EPS) * sw_t
    return (yt * (1.0 - lam_init)).T


def _diff_ctx_kernel(q_ref, k_ref, v_ref, lp_ref, sw_ref, o_ref, kc_ref, vc_ref, *, lam_init):
    lam = _diff_lambda(lp_ref, lam_init)
    hw = 2 * HD_C
    vts = {}

    def first(h):
        sl = slice(h * hw, (h + 1) * hw)
        k32 = k_ref[:, sl]
        v32 = v_ref[:, sl]
        kc_ref[:, sl] = k32
        vc_ref[:, sl] = v32
        vts[h] = v32.T.astype(BF16)
        return _diff_scores(q_ref[:, sl].astype(BF16), [k32.astype(BF16)])

    def second(h, scores):
        y = _diff_finish(scores, [vts.pop(h)], lam, lam_init, sw_ref[...])
        o_ref[:, h * hw:(h + 1) * hw] = y.astype(o_ref.dtype)

    _pipelined(H_C, first, second)


def _diff_ctx_call(p, lp, sw, lam_init, jl, k_out, v_out, name):
    cache_spec = pl.BlockSpec((None, None, SEQ, ODD_W), lambda b: (b, jl, 0, 0))
    cache_shape = jax.ShapeDtypeStruct((BATCH, N_ODD, SEQ, ODD_W), F32)
    return _aliased_call(
        functools.partial(_diff_ctx_kernel, lam_init=lam_init),
        grid=(BATCH,),
        in_specs=[
            pl.BlockSpec((SEQ, ODD_W), lambda b: (b, 0)),
            pl.BlockSpec((SEQ, ODD_W), lambda b: (b, 1)),
            pl.BlockSpec((SEQ, ODD_W), lambda b: (b, 2)),
            pl.BlockSpec((None, 4, HD_C), lambda b: (jl, 0, 0)),
            pl.BlockSpec((None, 2 * HD_C, ATT_TQ), lambda b: (jl, 0, 0)),
        ],
        args=[p, p, p, lp, sw],
        out_specs=[pl.BlockSpec((SEQ, ODD_W), lambda b: (b, 0)), cache_spec, cache_spec],
        out_shape=[jax.ShapeDtypeStruct((N_TOK, D_MIX), BF16), cache_shape, cache_shape],
        carried=[None, k_out, v_out], sem=("parallel",), name=name)


def _diff_lat_kernel(q_ref, k_ref, v_ref, ck_ref, cv_ref, lp_ref, sw_ref, o_ref, *, lam_init):
    lam = _diff_lambda(lp_ref, lam_init)
    k_parts = [k_ref[...].astype(BF16), ck_ref[...].astype(BF16)]
    vt_parts = [v_ref[...].T.astype(BF16), cv_ref[...].T.astype(BF16)]
    tq = ATT_TQ

    def first(i):
        return _diff_scores(q_ref[i * tq:(i + 1) * tq, :].astype(BF16), k_parts)

    def second(i, scores):
        y = _diff_finish(scores, vt_parts, lam, lam_init, sw_ref[...])
        o_ref[i * tq:(i + 1) * tq, :] = y.astype(o_ref.dtype)

    _pipelined(q_ref.shape[0] // tq, first, second)


def _diff_lat_call(p, ck, cv, lp, sw, lam_init, jl, mix, name):
    hw = 2 * HD_C
    s_blk0 = N_CTX_TOK // DEC_SEQ
    cache_spec = pl.BlockSpec((None, None, PAST_LEN, hw), lambda b, h: (b, jl, 0, h))
    (out,) = _aliased_call(
        functools.partial(_diff_lat_kernel, lam_init=lam_init),
        grid=(DEC_BATCH, H_C),
        in_specs=[
            pl.BlockSpec((DEC_SEQ, hw), lambda b, h: (s_blk0 + b, h)),
            pl.BlockSpec((DEC_SEQ, hw), lambda b, h: (s_blk0 + b, H_C + h)),
            pl.BlockSpec((DEC_SEQ, hw), lambda b, h: (s_blk0 + b, 2 * H_C + h)),
            cache_spec, cache_spec,
            pl.BlockSpec((None, 4, HD_C), lambda b, h: (jl, 0, 0)),
            pl.BlockSpec((None, hw, ATT_TQ), lambda b, h: (jl, 0, 0)),
        ],
        args=[p, p, p, ck, cv, lp, sw],
        out_specs=[pl.BlockSpec((DEC_SEQ, hw), lambda b, h: (s_blk0 + b, h))],
        out_shape=[jax.ShapeDtypeStruct((N_TOK, D_MIX), BF16)],
        carried=[mix], sem=("parallel", "parallel"), name=name)
    return out


def _mlp_kernel(x_ref, mix_ref, wo_ref, mods_ref, nw_ref, w1_ref, w2_ref, o_ref,
                h_scr, *, lat):
    i = pl.program_id(0)
    k = pl.program_id(1)
    row = 1 + i if lat else 0

    def mod(a):
        return mods_ref[pl.ds(row, 1), a * D_MODEL:(a + 1) * D_MODEL]

    @pl.when(k == 0)
    def _():
        for r in range(x_ref.shape[0] // PROJ_RC):
            rows = slice(r * PROJ_RC, (r + 1) * PROJ_RC)
            y = jnp.dot(mix_ref[rows, :], wo_ref[...], preferred_element_type=F32)
            x1 = x_ref[rows, :] + mod(2) * y
            o_ref[rows, :] = x1
            h_scr[rows, :] = _norm_mod(x1, nw_ref[...], mod(4), mod(3)).astype(BF16)

    u = jnp.dot(h_scr[...], w1_ref[...].astype(BF16), preferred_element_type=F32)
    u = jnp.square(jnp.maximum(u, 0.0)).astype(BF16)
    o_ref[...] += mod(5) * jnp.dot(u, w2_ref[...].astype(BF16), preferred_element_type=F32)


def _mlp_call(x_src, x_tile0, lat, mix, wo, jl, mods, li, norm_w, w1, w2, out_prev, out_rows,
              out_tile0, name):
    tm, tk = MLP_TM, MLP_TK
    n_tiles = (N_LAT_TOK if lat else N_CTX_TOK) // tm
    mix_tile0 = N_CTX_TOK // tm if lat else 0
    (out,) = _aliased_call(
        functools.partial(_mlp_kernel, lat=lat),
        grid=(n_tiles, D_FF // tk),
        in_specs=[
            pl.BlockSpec((tm, D_MODEL), lambda i, k: (x_tile0 + i, 0)),
            pl.BlockSpec((tm, D_MIX), lambda i, k: (mix_tile0 + i, 0)),
            pl.BlockSpec((None, D_MIX, D_MODEL), lambda i, k: (jl, 0, 0)),
            pl.BlockSpec((None, MOD_ROWS, N_MOD), lambda i, k: (li, 0, 0)),
            pl.BlockSpec((None, None, 1, D_MODEL), lambda i, k: (li, 1, 0, 0)),
            pl.BlockSpec((None, D_MODEL, tk), lambda i, k: (li, 0, k)),
            pl.BlockSpec((None, tk, D_MODEL), lambda i, k: (li, k, 0)),
        ],
        args=[x_src, mix, wo, mods, norm_w, w1, w2],
        out_specs=[pl.BlockSpec((tm, D_MODEL), lambda i, k: (out_tile0 + i, 0))],
        out_shape=[jax.ShapeDtypeStruct((out_rows, D_MODEL), F32)],
        carried=[out_prev], sem=("parallel", "arbitrary"), name=name,
        scratch_shapes=[pltpu.VMEM((tm, D_MODEL), BF16)])
    return out


def _rope_tables(width):
    t = np.arange(DEC_SEQ)
    half = HEAD_GROUP // 2
    inv = ROPE_BASE ** (-np.arange(0, half, 2, dtype=np.float64) / half)
    ang = np.concatenate([(t // GRID_W)[:, None] * inv, (t % GRID_W)[:, None] * inv], axis=-1)
    cos = np.repeat(np.cos(ang), 2, axis=-1)
    sin = np.repeat(np.sin(ang), 2, axis=-1)
    sign = np.tile(np.array([-1.0, 1.0]), HEAD_GROUP // 2)
    reps = width // HEAD_GROUP
    return (jnp.asarray(np.tile(cos, (1, reps)), F32),
            jnp.asarray(np.tile(sin * sign, (1, reps)), F32))


def _block_diag_ones(n):
    g = np.arange(n) // HEAD_GROUP
    return jnp.asarray(g[:, None] == g[None, :], BF16)


def _tile_row(w, width):
    return jnp.tile(w.astype(F32), width // w.shape[0])[None, :]


def _lambda_init(li):
    return 0.8 - 0.6 * math.exp(-0.3 * li)


def kernel(x_prompt, x_sample, cache_k_swa, cache_v_swa, state_hgrn, cache_k_diff, cache_v_diff, c, c_ctx, norm_w, w_ada, b_ada, w_in_even, w_out_even, hgrn_lb_logits, hgrn_norm_w, swa_qnorm_w, swa_knorm_w, swa_sink, w_in_odd, w_out_odd, diff_qnorm_w, diff_knorm_w, diff_lambda_p, diff_subln_w, w_mlp1, w_mlp2):
    assert PROJ_TM == MLP_TM
    n_ctx_tiles = N_CTX_TOK // PROJ_TM
    x = None
    x_ctx0 = x_prompt.reshape(N_CTX_TOK, D_MODEL)
    x_lat0 = x_sample.reshape(N_LAT_TOK, D_MODEL)
    c_all = jnp.concatenate(
        [c_ctx[None, :], c, jnp.zeros((MOD_ROWS - 1 - DEC_BATCH, D_MODEL), F32)], axis=0)
    mods = _mods_call(c_all, w_ada, b_ada.reshape(DEPTH, 1, N_MOD))

    cos_t, sin_t = _rope_tables(PROJ_TN)
    bd = _block_diag_ones(256)
    tabs_f = _hgrn_tables(False)
    tabs_b = _hgrn_tables(True)
    hgrn_tabs = (tabs_f[0], tabs_f[1], tabs_b[0], tabs_b[1])
    lbl = hgrn_lb_logits.astype(F32).reshape(N_EVEN * 2, F_A)
    norm_w4 = norm_w.astype(F32).reshape(DEPTH, 2, 1, D_MODEL)

    w_out_even_b = w_out_even.astype(BF16)
    w_out_odd_b = w_out_odd.astype(BF16)

    ck_swa = cache_k_swa.reshape(DEC_BATCH, N_EVEN, PAST_LEN, KV_B * HD_B)
    cv_swa = cache_v_swa.reshape(DEC_BATCH, N_EVEN, PAST_LEN, KV_B * HD_B)
    ck_diff = cache_k_diff.reshape(DEC_BATCH, N_ODD, PAST_LEN, ODD_W)
    cv_diff = cache_v_diff.reshape(DEC_BATCH, N_ODD, PAST_LEN, ODD_W)
    hgrn_nw = hgrn_norm_w.astype(F32).reshape(N_EVEN, 1, DV_A)
    sink = swa_sink.astype(F32).reshape(N_EVEN, 1, H_B)
    lam_p = diff_lambda_p.astype(F32)
    assert ATT_TQ == SEQ
    subln = jnp.broadcast_to(diff_subln_w.astype(F32)[:, :, None], (N_ODD, 2 * HD_C, ATT_TQ))

    even_kinds = ("silu_scale", "loggate0", "loggate1", "ident", "silu", "qnorm", "kv")
    odd_kinds = ("qnorm", "qnorm", "knorm", "knorm", "ident", "ident")

    k_swa = v_swa = states = k_diff = v_diff = None
    for li in range(DEPTH):
        j = li // 2
        srcs = ((x_ctx0, 0), (x_lat0, 0)) if li == 0 else ((x, 0), (x, n_ctx_tiles))
        if li % 2 == 0:
            p = None
            for lat, (src, t0) in enumerate(srcs):
                p = _proj_call(src, t0, bool(lat), p, mods, li, norm_w4, w_in_even, j, lbl,
                               _tile_row(swa_qnorm_w[j], PROJ_TN),
                               _tile_row(swa_knorm_w[j], PROJ_TN),
                               cos_t, sin_t, bd, even_kinds, f"proj_even{j}_{lat}")
            mix, states = _hgrn_call(p, hgrn_nw, hgrn_tabs, SEQ, BATCH, 0, None, j, None, states,
                                     f"hgrn_ctx{j}")
            (mix,) = _hgrn_call(p, hgrn_nw, hgrn_tabs, DEC_SEQ, DEC_BATCH, N_CTX_TOK // DEC_SEQ,
                                state_hgrn, j, mix, None, f"hgrn_lat{j}")
            mix, k_swa, v_swa = _swa_ctx_call(p, sink, j, mix, k_swa, v_swa, f"swa_ctx{j}")
            mix = _swa_lat_call(p, ck_swa, cv_swa, sink, j, mix, f"swa_lat{j}")
            wo = w_out_even_b
        else:
            p = None
            for lat, (src, t0) in enumerate(srcs):
                p = _proj_call(src, t0, bool(lat), p, mods, li, norm_w4, w_in_odd, j, lbl,
                               _tile_row(diff_qnorm_w[j], PROJ_TN),
                               _tile_row(diff_knorm_w[j], PROJ_TN),
                               cos_t, sin_t, bd, odd_kinds, f"proj_odd{j}_{lat}")
            lam_init = _lambda_init(li)
            mix, k_diff, v_diff = _diff_ctx_call(p, lam_p, subln, lam_init, j, k_diff, v_diff,
                                                 f"diff_ctx{j}")
            mix = _diff_lat_call(p, ck_diff, cv_diff, lam_p, subln, lam_init, j, mix,
                                 f"diff_lat{j}")
            wo = w_out_odd_b
        last = li == DEPTH - 1
        outs = []
        x_next = None
        for lat, (src, t0) in enumerate(srcs):
            rows = (N_LAT_TOK if lat else N_CTX_TOK) if last else N_TOK
            out_t0 = 0 if last else lat * n_ctx_tiles
            x_next = _mlp_call(src, t0, bool(lat), mix, wo, j, mods, li, norm_w4, w_mlp1, w_mlp2,
                               None if last else x_next, rows, out_t0, f"mlp{li}_{lat}")
            outs.append(x_next)
        x = x_next

    y_prompt = outs[0].reshape(BATCH, SEQ, D_MODEL)
    y_sample = outs[1].reshape(DEC_BATCH, DEC_SEQ, D_MODEL)
    return (y_prompt, y_sample,
            k_swa.reshape(BATCH, N_EVEN, SEQ, KV_B, HD_B),
            v_swa.reshape(BATCH, N_EVEN, SEQ, KV_B, HD_B),
            states,
            k_diff.reshape(BATCH, N_ODD, SEQ, H_C, 2, HD_C),
            v_diff.reshape(BATCH, N_ODD, SEQ, H_C, 2 * HD_C))
```

```python
import functools
import math

import numpy as np
import jax
import jax.numpy as jnp
from jax import lax
from jax.experimental import pallas as pl
from jax.experimental.pallas import tpu as pltpu

F32 = jnp.float32
BF16 = jnp.bfloat16

D_MODEL = 1024
BATCH = 16
SEQ = 256
DEPTH = 4
DEC_BATCH = 4
DEC_SEQ = 1024
PAST_LEN = 512
GRID_W = 64
N_EVEN = (DEPTH + 1) // 2
N_ODD = DEPTH // 2
H_A = 4
DK_A = 128
DV_A = D_MODEL // 2 // H_A
F_A = H_A * DK_A
H_B = 8
KV_B = 2
G_B = H_B // KV_B
HD_B = D_MODEL // 2 // H_B
WINDOW = 128
H_C = 8
HD_C = D_MODEL // (2 * H_C)
D_FF = 4 * D_MODEL
ROPE_BASE = 10000.0
EPS = 1e-6
EVEN_COLS = 3 * F_A + 2 * H_A * DV_A + (H_B + 2 * KV_B) * HD_B
ODD_W = H_C * 2 * HD_C
D_MIX = D_MODEL

N_CTX_TOK = BATCH * SEQ
N_LAT_TOK = DEC_BATCH * DEC_SEQ
N_TOK = N_CTX_TOK + N_LAT_TOK
MOD_ROWS = 8
N_MOD = 6 * D_MODEL

HEAD_GROUP = 64
HGRN_CHUNK = 128
HGRN_LEVELS = 7
HGRN_SPLIT = 3
LOG2_E = math.log2(math.e)
VMEM_LIMIT = 48 * 1024 * 1024

PROJ_TM = 1024
PROJ_TN = 512
PROJ_RC = 128
MLP_TM = 1024
MLP_TK = 1024
MLP_RC = 256
ADA_TN = 1536
ATT_TQ = 256
ONES_ROWS = 16


def _silu(x):
    return x * jax.nn.sigmoid(x)


def _nt_dot(a, b):
    return lax.dot_general(a, b, (((1,), (1,)), ((), ())), preferred_element_type=F32)


def _pipelined(n, first, second):
    cur = first(0)
    for i in range(n):
        nxt = first(i + 1) if i + 1 < n else None
        second(i, cur)
        cur = nxt


def _params(sem):
    return pltpu.CompilerParams(dimension_semantics=sem, vmem_limit_bytes=VMEM_LIMIT)


def _aliased_call(kernel, *, grid, in_specs, args, out_specs, out_shape, carried, sem, name,
                  scratch_shapes=()):
    n_in = len(args)
    extra = [buf for buf in carried if buf is not None]
    aliases = {}
    for k, buf in enumerate(carried):
        if buf is not None:
            aliases[n_in + len(aliases)] = k
    n_extra = len(extra)

    def body(*refs):
        kernel(*refs[:n_in], *refs[n_in + n_extra:])

    return pl.pallas_call(
        body,
        grid=grid,
        in_specs=list(in_specs) + [pl.BlockSpec(memory_space=pl.ANY)] * n_extra,
        out_specs=out_specs,
        out_shape=out_shape,
        input_output_aliases=aliases,
        scratch_shapes=list(scratch_shapes),
        compiler_params=_params(sem),
        name=name,
    )(*args, *extra)


def _mods_kernel(c_ref, w_ref, b_ref, o_ref):
    s = _silu(c_ref[...]).astype(BF16)
    o_ref[...] = jnp.dot(s, w_ref[...].astype(BF16), preferred_element_type=F32) + b_ref[...]


def _mods_call(c_all, w_ada, b_ada):
    return pl.pallas_call(
        _mods_kernel,
        grid=(DEPTH, N_MOD // ADA_TN),
        in_specs=[
            pl.BlockSpec((MOD_ROWS, D_MODEL), lambda l, j: (0, 0)),
            pl.BlockSpec((None, D_MODEL, ADA_TN), lambda l, j: (l, 0, j)),
            pl.BlockSpec((None, 1, ADA_TN), lambda l, j: (l, 0, j)),
        ],
        out_specs=pl.BlockSpec((None, MOD_ROWS, ADA_TN), lambda l, j: (l, 0, j)),
        out_shape=jax.ShapeDtypeStruct((DEPTH, MOD_ROWS, N_MOD), F32),
        compiler_params=_params(("parallel", "parallel")),
        name="ada_mods",
    )(c_all, w_ada, b_ada)


def _norm_mod(x, nw, sc, sh):
    ms = jnp.mean(x * x, axis=-1, keepdims=True)
    return (x * lax.rsqrt(ms + EPS) * nw) * (1.0 + sc) + sh


def _group_rms(y, w_t, bd_ref):
    yy = (y * y).astype(BF16)
    bw = bd_ref.shape[0]
    parts = [jnp.dot(yy[:, s:s + bw], bd_ref[...], preferred_element_type=F32)
             for s in range(0, y.shape[1], bw)]
    ss = parts[0] if len(parts) == 1 else jnp.concatenate(parts, axis=1)
    return y * lax.rsqrt(ss * (1.0 / HEAD_GROUP) + EPS) * w_t


def _rope(y, cos, sin):
    n = y.shape[1]
    lane = lax.broadcasted_iota(jnp.int32, y.shape, 1)
    nxt = pltpu.roll(y, n - 1, axis=1)
    prv = pltpu.roll(y, 1, axis=1)
    swapped = jnp.where((lane & 1) == 0, nxt, prv)
    return y * cos + swapped * sin


def _lower_bounds(lbl_ref, jl):
    rows = [lbl_ref[pl.ds(2 * m, 2), :] for m in range(N_EVEN)]
    mx = functools.reduce(jnp.maximum, rows)
    es = [jnp.exp(r - mx) for r in rows]
    den = functools.reduce(lambda a, b: a + b, es)
    sm = [e / den for e in es]
    cs = sm[0]
    for m in range(1, jl + 1):
        cs = cs + sm[m]
    return cs - sm[0]


def _proj_kernel(x_ref, mods_ref, nw_ref, w_ref, wkv_ref, lbl_ref, qn_ref, kn_ref, cos_ref,
                 sin_ref, bd_ref, o_ref, h_scr, w_scr, ya_scr, yb_scr, *, kinds, jl, lat,
                 n_tiles):
    s = pl.program_id(0)
    tm, tn = o_ref.shape
    kvw = 2 * KV_B * HD_B
    n_steps = len(kinds) * n_tiles
    j = s // n_tiles
    i = s % n_tiles
    j_prev = (s + n_tiles - 1) // n_tiles - 1
    rows = slice(0, tm)

    @pl.when(s < n_tiles)
    def _():
        row = 1 + i if lat else 0
        sh = mods_ref[pl.ds(row, 1), 0:D_MODEL]
        sc = mods_ref[pl.ds(row, 1), D_MODEL:2 * D_MODEL]
        h_scr[i] = _norm_mod(x_ref[...], nw_ref[...], sc, sh).astype(BF16)

    if "kv" in kinds:
        @pl.when(s == kinds.index("kv") * n_tiles)
        def _():
            w_scr[:, 0:kvw] = wkv_ref[...].astype(BF16)
            w_scr[:, kvw:tn] = jnp.zeros((D_MODEL, tn - kvw), BF16)

    n_main = sum(1 for kd in kinds if kd != "kv")

    @pl.when(jnp.logical_and(i == 0, j < n_main))
    def _():
        w_scr[...] = w_ref[...].astype(BF16)

    def finish(kind, y, rows):
        if kind == "silu_scale":
            return _silu(y) * (DK_A ** -0.5)
        if kind in ("loggate0", "loggate1"):
            d = int(kind[-1])
            lb = _lower_bounds(lbl_ref, jl)[d:d + 1, :]
            return jnp.log2(lb + (1.0 - lb) * jax.nn.sigmoid(y))
        if kind == "ident":
            return y
        if kind == "silu":
            return _silu(y)
        if kind in ("qnorm", "knorm"):
            w_t = qn_ref[...] if kind == "qnorm" else kn_ref[...]
            r = _group_rms(y, w_t, bd_ref)
            if lat:
                r = _rope(r, cos_ref[rows, :], sin_ref[rows, :])
            return r * (HEAD_GROUP ** -0.5 * LOG2_E) if kind == "qnorm" else r
        if kind == "kv":
            kn = _group_rms(y, kn_ref[:, 0:kvw], bd_ref)
            if lat:
                kn = _rope(kn, cos_ref[rows, 0:kvw], sin_ref[rows, 0:kvw])
            lane = lax.broadcasted_iota(jnp.int32, y.shape, 1)
            return jnp.where(lane < KV_B * HD_B, kn, y)
        raise ValueError(kind)

    def step(kind, y_new, y_old):
        if kind is not None:
            for r in range(tm // PROJ_RC):
                rows = slice(r * PROJ_RC, (r + 1) * PROJ_RC)
                if kind == "kv":
                    o_ref[rows, 0:kvw] = finish(kind, y_old[rows, 0:kvw], rows)
                    o_ref[rows, kvw:tn] = jnp.zeros((PROJ_RC, tn - kvw), F32)
                else:
                    o_ref[rows, :] = finish(kind, y_old[rows, :], rows)
        y_new[...] = jnp.dot(h_scr[i], w_scr[...], preferred_element_type=F32)

    even = s % 2 == 0
    pl.when(s == 0)(functools.partial(step, None, ya_scr, yb_scr))
    for y_new, y_old, par in ((ya_scr, yb_scr, even), (yb_scr, ya_scr, jnp.logical_not(even))):
        for jj, kind in enumerate(kinds):
            pl.when(jnp.logical_and(j_prev == jj, par))(
                functools.partial(step, kind, y_new, y_old))


def _proj_call(x_src, x_tile0, lat, p_prev, mods, li, norm_w, w, jl, lbl, qn_t, kn_t, cos_t, sin_t,
               bd, kinds, name):
    tm, tn = PROJ_TM, PROJ_TN
    assert tm == DEC_SEQ
    n_ctx_tiles = N_CTX_TOK // tm
    n_tiles = (N_LAT_TOK if lat else N_CTX_TOK) // tm
    tile0 = n_ctx_tiles if lat else 0
    if not lat:
        cos_t = sin_t = jnp.zeros((8, 128), F32)
    n_main = sum(1 for k in kinds if k != "kv")
    n_cols = tn * len(kinds)
    kvw = 2 * KV_B * HD_B
    kv_blk = (n_main * tn) // kvw if "kv" in kinds else 0
    const = lambda s: (0, 0)
    n_steps = len(kinds) * n_tiles

    def out_idx(s):
        prev = jnp.maximum(s - 1, 0)
        return (tile0 + prev % n_tiles, prev // n_tiles)

    (out,) = _aliased_call(
        functools.partial(_proj_kernel, kinds=kinds, jl=jl, lat=lat, n_tiles=n_tiles),
        grid=(n_steps + 1,),
        in_specs=[
            pl.BlockSpec((tm, D_MODEL), lambda s: (x_tile0 + jnp.minimum(s, n_tiles - 1), 0)),
            pl.BlockSpec((None, MOD_ROWS, N_MOD), lambda s: (li, 0, 0)),
            pl.BlockSpec((None, None, 1, D_MODEL), lambda s: (li, 0, 0, 0)),
            pl.BlockSpec((None, D_MODEL, tn),
                         lambda s: (jl, 0, jnp.minimum(s // n_tiles, n_main - 1))),
            pl.BlockSpec((None, D_MODEL, kvw), lambda s: (jl, 0, kv_blk)),
            pl.BlockSpec(lbl.shape, const),
            pl.BlockSpec((1, tn), const),
            pl.BlockSpec((1, tn), const),
            pl.BlockSpec(cos_t.shape, const),
            pl.BlockSpec(sin_t.shape, const),
            pl.BlockSpec(bd.shape, const),
        ],
        args=[x_src, mods, norm_w, w, w, lbl, qn_t, kn_t, cos_t, sin_t, bd],
        out_specs=[pl.BlockSpec((tm, tn), out_idx)],
        out_shape=[jax.ShapeDtypeStruct((N_TOK, n_cols), F32)],
        carried=[p_prev], sem=("arbitrary",), name=name,
        scratch_shapes=[pltpu.VMEM((n_tiles, tm, D_MODEL), BF16),
                        pltpu.VMEM((D_MODEL, tn), BF16),
                        pltpu.VMEM((tm, tn), F32), pltpu.VMEM((tm, tn), F32)])
    return out


def _hgrn_tables(rev):
    c = HGRN_CHUNK
    t = np.arange(c)
    w = (t[None, :] <= t[:, None]) if not rev else (t[None, :] >= t[:, None])
    ws = np.concatenate([w.astype(np.float32)] * HGRN_SPLIT, axis=1)
    x = t[:, None] ^ t[None, :]
    lv = np.where(x > 0, np.floor(np.log2(np.maximum(x, 1))).astype(np.int32), HGRN_LEVELS)
    causal = (t[None, :] < t[:, None]) if not rev else (t[None, :] > t[:, None])
    lv = np.where(causal | (x == 0), lv, -1).astype(np.int32)
    return jnp.asarray(ws, BF16), jnp.asarray(lv)


def _hgrn_level_exponents(cum, rev):
    c = HGRN_CHUNK
    sub_rows = 8
    c3 = cum.reshape(c // sub_rows, sub_rows, DK_A)
    sub = lax.broadcasted_iota(jnp.int32, c3.shape, 1)
    out = []
    for l in range(1, HGRN_LEVELS):
        hb = 1 << l
        if 2 * hb <= sub_rows:
            r = None
            for b0 in range(0, sub_rows, 2 * hb):
                idx = b0 + (hb if rev else hb - 1)
                rk = c3[:, idx:idx + 1, :]
                r = rk if r is None else jnp.where(sub < b0, r, rk)
            d = c3 - r
            bit = (sub & hb) != 0
            q_role = jnp.logical_not(bit) if rev else bit
            out.append(jnp.where(q_role, d, -d).reshape(c, DK_A))
        else:
            pieces = []
            for b0 in range(0, c, 2 * hb):
                mid = b0 + hb
                ridx = mid if rev else mid - 1
                r = cum[ridx:ridx + 1, :]
                lo = cum[b0:mid]
                hi = cum[mid:b0 + 2 * hb]
                pieces += [lo - r, r - hi] if rev else [r - lo, hi - r]
            out.append(jnp.concatenate(pieces, axis=0))
    return out


def _hgrn_level_operands(l, q, k, f, z, rev, row):
    c = HGRN_CHUNK
    hb = 1 << l
    if hb >= 8:
        zero = jnp.zeros((hb, DK_A), F32)
        qparts, kparts = [], []
        for b0 in range(0, c, 2 * hb):
            lo, hi = slice(b0, b0 + hb), slice(b0 + hb, b0 + 2 * hb)
            if rev:
                qparts += [q[lo] * z[lo], zero]
                kparts += [zero, k[hi] * z[hi]]
            else:
                qparts += [zero, q[hi] * z[hi]]
                kparts += [k[lo] * z[lo], zero]
        return (jnp.concatenate(qparts, axis=0).astype(BF16),
                jnp.concatenate(kparts, axis=0).astype(BF16))
    bit = ((row >> l) & 1) == 1
    q_role = jnp.logical_not(bit) if rev else bit
    ql = jnp.where(q_role, q * (f if l == 0 else z), 0.0).astype(BF16)
    kl = jnp.where(q_role, 0.0, k if l == 0 else k * z).astype(BF16)
    return ql, kl


def _hgrn_chunks(chains):
    c = HGRN_CHUNK
    row = lax.broadcasted_iota(jnp.int32, (c, DK_A), 0)
    cums = []
    for q, g, v, st, w_ref, lv_ref, rev in chains:
        terms = []
        rem = g
        for _ in range(HGRN_SPLIT):
            term = rem.astype(BF16)
            terms.append(term)
            rem = rem - term.astype(F32)
        cums.append(jnp.dot(w_ref[...], jnp.concatenate(terms, axis=0),
                            preferred_element_type=F32))
    work = []
    for (q, g, v, st, w_ref, lv_ref, rev), cum in zip(chains, cums):
        f = jnp.exp2(g)
        k = 1.0 - f
        last = 0 if rev else c - 1
        total_e = cum[last:last + 1, :]
        qd = (q * jnp.exp2(cum)).astype(BF16)
        kd = (k * jnp.exp2(total_e - cum)).astype(BF16)
        o = _nt_dot(qd, st.astype(BF16))
        st_new = (st * jnp.exp2(total_e)
                  + jnp.dot(v.T.astype(BF16), kd, preferred_element_type=F32))
        zs = [None] + [jnp.exp2(e) for e in _hgrn_level_exponents(cum, rev)]
        work.append((k, f, zs, o, st_new))
    accs = [None] * len(chains)
    for l in reversed(range(HGRN_LEVELS)):
        for i, ((q, g, v, st, w_ref, lv_ref, rev), (k, f, zs, o, st_new)) in enumerate(
                zip(chains, work)):
            ql, kl = _hgrn_level_operands(l, q, k, f, zs[l], rev, row)
            a_l = _nt_dot(ql, kl)
            accs[i] = a_l if accs[i] is None else jnp.where(lv_ref[...] == l, a_l, accs[i])
    outs = []
    for (q, g, v, st, w_ref, lv_ref, rev), (k, f, zs, o, st_new), a in zip(chains, work, accs):
        a = jnp.where(lv_ref[...] == HGRN_LEVELS, jnp.sum(q * k, axis=-1, keepdims=True), a)
        outs.append((o + jnp.dot(a.astype(BF16), v.astype(BF16), preferred_element_type=F32),
                     st_new))
    return outs


def _hgrn_kernel(*refs, n_chunks, has_init, emit_state):
    refs = list(refs)
    q_ref, gf_ref, gb_ref, v_ref, sg_ref, nw_ref, wf_ref, wb_ref, lvf_ref, lvb_ref = refs[:10]
    pos = 10
    s0_ref = None
    if has_init:
        s0_ref = refs[pos]
        pos += 1
    o_ref = refs[pos]
    pos += 1
    so_ref = None
    if emit_state:
        so_ref = refs[pos]
        pos += 1
    of_scr, ob_scr, st_scr = refs[pos:pos + 3]

    for d in range(2):
        for h in range(H_A):
            if has_init:
                st_scr[d, h] = s0_ref[d, h].T
            else:
                st_scr[d, h] = jnp.zeros((DV_A, DK_A), F32)

    def body(c, carry):
        chains, dests = [], []
        for h in range(H_A):
            cols = slice(h * DK_A, (h + 1) * DK_A)
            for d, (g_ref, w_ref, lv_ref, scr) in enumerate(
                    ((gf_ref, wf_ref, lvf_ref, of_scr), (gb_ref, wb_ref, lvb_ref, ob_scr))):
                cc = c if d == 0 else n_chunks - 1 - c
                r0 = pl.multiple_of(cc * HGRN_CHUNK, HGRN_CHUNK)
                rows = pl.ds(r0, HGRN_CHUNK)
                chains.append((q_ref[rows, cols], g_ref[rows, cols], v_ref[rows, cols],
                               st_scr[d, h], w_ref, lv_ref, d == 1))
                dests.append((scr, rows, cols, d, h))
        for (o, st), (scr, rows, cols, d, h) in zip(_hgrn_chunks(chains), dests):
            st_scr[d, h] = st
            scr[rows, cols] = o
        return carry

    lax.fori_loop(0, n_chunks, body, 0)
    for h in range(H_A):
        cols = slice(h * DV_A, (h + 1) * DV_A)
        o = of_scr[:, cols] + ob_scr[:, cols]
        y = o * lax.rsqrt(jnp.mean(o * o, axis=-1, keepdims=True) + EPS) * nw_ref[...]
        o_ref[:, cols] = (y * sg_ref[:, cols]).astype(o_ref.dtype)
    if emit_state:
        for d in range(2):
            for h in range(H_A):
                so_ref[d, h] = st_scr[d, h].T


def _hgrn_call(p, nw, tabs, seq_len, n_seq, row_blk0, s0, jl, mix, state_out, name):
    wf, lvf, wb, lvb = tabs
    has_init = s0 is not None
    emit_state = s0 is None
    const = lambda b: (0, 0)
    blk = (seq_len, F_A)
    state_spec = pl.BlockSpec((None, None, 2, H_A, DK_A, DV_A), lambda b: (b, jl, 0, 0, 0, 0))
    in_specs = [pl.BlockSpec(blk, (lambda b, part=part: (row_blk0 + b, part))) for part in range(5)]
    in_specs += [
        pl.BlockSpec((None, 1, DV_A), lambda b: (jl, 0, 0)),
        pl.BlockSpec(wf.shape, const), pl.BlockSpec(wb.shape, const),
        pl.BlockSpec(lvf.shape, const), pl.BlockSpec(lvb.shape, const),
    ]
    args = [p, p, p, p, p, nw, wf, wb, lvf, lvb]
    if has_init:
        in_specs.append(state_spec)
        args.append(s0)
    out_shape = [jax.ShapeDtypeStruct((N_TOK, D_MIX), BF16)]
    out_specs = [pl.BlockSpec((seq_len, H_A * DV_A), lambda b: (row_blk0 + b, 0))]
    carried = [mix]
    if emit_state:
        out_shape.append(jax.ShapeDtypeStruct((BATCH, N_EVEN, 2, H_A, DK_A, DV_A), F32))
        out_specs.append(state_spec)
        carried.append(state_out)
    return _aliased_call(
        functools.partial(_hgrn_kernel, n_chunks=seq_len // HGRN_CHUNK, has_init=has_init,
                          emit_state=emit_state),
        grid=(n_seq,), in_specs=in_specs, args=args, out_specs=out_specs,
        out_shape=out_shape, carried=carried, sem=("parallel",), name=name,
        scratch_shapes=[pltpu.VMEM((seq_len, H_A * DV_A), F32),
                        pltpu.VMEM((seq_len, H_A * DV_A), F32),
                        pltpu.VMEM((2, H_A, DV_A, DK_A), F32)])


def _swa_ctx_kernel(q_ref, kv_ref, sink_ref, o_ref, kc_ref, vc_ref):
    kv = kv_ref[...]
    k32 = kv[:, 0:KV_B * HD_B]
    v32 = kv[:, KV_B * HD_B:2 * KV_B * HD_B]
    kc_ref[...] = k32
    vc_ref[...] = v32
    k = k32.astype(BF16)
    vt32 = v32.T
    vts = [_with_ones_rows(vt32[n * HD_B:(n + 1) * HD_B]) for n in range(KV_B)]
    q = q_ref[...].astype(BF16)
    ksl = [slice((h // G_B) * HD_B, (h // G_B + 1) * HD_B) for h in range(H_B)]
    sts = [_nt_dot(k[:, ksl[h]], q[:, h * HD_B:(h + 1) * HD_B]) for h in range(H_B)]
    ps, sinks = [], []
    for h in range(H_B):
        sink = sink_ref[0:1, h:h + 1] * LOG2_E
        m = jnp.maximum(jnp.max(sts[h], axis=0, keepdims=True), sink)
        ps.append(jnp.exp2(sts[h] - m).astype(BF16))
        sinks.append(jnp.exp2(sink - m))
    outs = []
    for h in range(H_B):
        ota = jnp.dot(vts[h // G_B], ps[h], preferred_element_type=F32)
        outs.append(ota[0:HD_B] / (ota[HD_B:HD_B + 1] + sinks[h]))
    o_ref[...] = jnp.concatenate(outs, axis=0).T.astype(o_ref.dtype)


def _swa_ctx_call(p, sink, jl, mix, k_out, v_out, name):
    qcol = (3 * F_A + 2 * H_A * DV_A) // PROJ_TN
    cache_spec = pl.BlockSpec((None, None, SEQ, KV_B * HD_B), lambda b: (b, jl, 0, 0))
    cache_shape = jax.ShapeDtypeStruct((BATCH, N_EVEN, SEQ, KV_B * HD_B), F32)
    return _aliased_call(
        _swa_ctx_kernel,
        grid=(BATCH,),
        in_specs=[
            pl.BlockSpec((SEQ, H_B * HD_B), lambda b: (b, qcol)),
            pl.BlockSpec((SEQ, PROJ_TN), lambda b: (b, qcol + 1)),
            pl.BlockSpec((None, 1, H_B), lambda b: (jl, 0, 0)),
        ],
        args=[p, p, sink],
        out_specs=[pl.BlockSpec((SEQ, H_B * HD_B), lambda b: (b, 1)), cache_spec, cache_spec],
        out_shape=[jax.ShapeDtypeStruct((N_TOK, D_MIX), BF16), cache_shape, cache_shape],
        carried=[mix, k_out, v_out], sem=("parallel",), name=name)


def _swa_lat_kernel(q_ref, kv_ref, ck_ref, cv_ref, sink_ref, o_ref):
    qi = pl.program_id(1)
    tq = q_ref.shape[0]
    span = tq + 2 * WINDOW
    ws = pl.multiple_of(jnp.clip(qi * tq - WINDOW, 0, DEC_SEQ - span), WINDOW)
    kvw = kv_ref[pl.ds(ws, span), :]
    kw = kvw[:, 0:KV_B * HD_B].astype(BF16)
    vwt32 = kvw[:, KV_B * HD_B:2 * KV_B * HD_B].T
    kc = ck_ref[...].astype(BF16)
    vct32 = cv_ref[...].T
    vwts = [_with_ones_rows(vwt32[n * HD_B:(n + 1) * HD_B]) for n in range(KV_B)]
    vcts = [_with_ones_rows(vct32[n * HD_B:(n + 1) * HD_B]) for n in range(KV_B)]
    q = q_ref[...].astype(BF16)
    t_k = ws + lax.broadcasted_iota(jnp.int32, (span, tq), 0)
    t_q = qi * tq + lax.broadcasted_iota(jnp.int32, (span, tq), 1)
    valid = jnp.abs(t_q - t_k) <= WINDOW
    outs = []

    def first(h):
        qh = q[:, h * HD_B:(h + 1) * HD_B]
        ksl = slice((h // G_B) * HD_B, (h // G_B + 1) * HD_B)
        return _nt_dot(kw[:, ksl], qh), _nt_dot(kc[:, ksl], qh)

    def second(h, scores):
        s_w = jnp.where(valid, scores[0], -jnp.inf)
        s_c = scores[1]
        sink = sink_ref[0:1, h:h + 1] * LOG2_E
        m = jnp.maximum(jnp.maximum(jnp.max(s_w, axis=0, keepdims=True),
                                    jnp.max(s_c, axis=0, keepdims=True)), sink)
        ota = (jnp.dot(vwts[h // G_B], jnp.exp2(s_w - m).astype(BF16),
                       preferred_element_type=F32)
               + jnp.dot(vcts[h // G_B], jnp.exp2(s_c - m).astype(BF16),
                         preferred_element_type=F32))
        outs.append(ota[0:HD_B] / (ota[HD_B:HD_B + 1] + jnp.exp2(sink - m)))

    _pipelined(H_B, first, second)
    o_ref[...] = jnp.concatenate(outs, axis=0).T.astype(o_ref.dtype)


def _swa_lat_call(p, ck, cv, sink, jl, mix, name):
    tq = ATT_TQ
    qcol = (3 * F_A + 2 * H_A * DV_A) // PROJ_TN
    nq = DEC_SEQ // tq
    q_blk0 = N_CTX_TOK // tq
    s_blk0 = N_CTX_TOK // DEC_SEQ
    cache_spec = pl.BlockSpec((None, None, PAST_LEN, KV_B * HD_B), lambda b, i: (b, jl, 0, 0))
    (out,) = _aliased_call(
        _swa_lat_kernel,
        grid=(DEC_BATCH, nq),
        in_specs=[
            pl.BlockSpec((tq, H_B * HD_B), lambda b, i: (q_blk0 + b * nq + i, qcol)),
            pl.BlockSpec((DEC_SEQ, PROJ_TN), lambda b, i: (s_blk0 + b, qcol + 1)),
            cache_spec, cache_spec,
            pl.BlockSpec((None, 1, H_B), lambda b, i: (jl, 0, 0)),
        ],
        args=[p, p, ck, cv, sink],
        out_specs=[pl.BlockSpec((tq, H_B * HD_B), lambda b, i: (q_blk0 + b * nq + i, 1))],
        out_shape=[jax.ShapeDtypeStruct((N_TOK, D_MIX), BF16)],
        carried=[mix], sem=("parallel", "parallel"), name=name)
    return out


def _diff_lambda(lp_ref, lam_init):
    lp = lp_ref[...]
    a = jnp.sum(lp[0:1] * lp[1:2], axis=-1, keepdims=True)
    b = jnp.sum(lp[2:3] * lp[3:4], axis=-1, keepdims=True)
    return jnp.exp(a) - jnp.exp(b) + lam_init


def _diff_scores(q, k_parts):
    return [[_nt_dot(kp[:, c * HD_C:(c + 1) * HD_C], q[:, c * HD_C:(c + 1) * HD_C])
             for kp in k_parts] for c in range(2)]


def _diff_finish(scores, vt_parts, lam, lam_init, sw_t):
    hw = 2 * HD_C
    comps = []
    for c in range(2):
        ss = scores[c]
        m = functools.reduce(jnp.maximum, [jnp.max(s, axis=0, keepdims=True) for s in ss])
        ota = functools.reduce(
            lambda a, b: a + b,
            [jnp.dot(vt, jnp.exp2(s - m).astype(BF16), preferred_element_type=F32)
             for s, vt in zip(ss, vt_parts)])
        comps.append(ota[0:hw] / ota[hw:hw + 1])
    ot = comps[0] - lam * comps[1]
    yt = ot * lax.rsqrt(jnp.mean(ot * ot, axis=0, keepdims=True) + EPS) * sw_t
    return (yt * (1.0 - lam_init)).T


def _with_ones_rows(vt):
    return jnp.concatenate([vt, jnp.ones((ONES_ROWS, vt.shape[1]), F32)], axis=0).astype(BF16)


def _diff_ctx_kernel(q_ref, k_ref, v_ref, lp_ref, sw_ref, o_ref, kc_ref, vc_ref, *, lam_init):
    lam = _diff_lambda(lp_ref, lam_init)
    hw = 2 * HD_C
    vts = {}

    def first(h):
        sl = slice(h * hw, (h + 1) * hw)
        k32 = k_ref[:, sl]
        v32 = v_ref[:, sl]
        kc_ref[:, sl] = k32
        vc_ref[:, sl] = v32
        vts[h] = _with_ones_rows(v32.T)
        return _diff_scores(q_ref[:, sl].astype(BF16), [k32.astype(BF16)])

    def second(h, scores):
        y = _diff_finish(scores, [vts.pop(h)], lam, lam_init, sw_ref[...])
        o_ref[:, h * hw:(h + 1) * hw] = y.astype(o_ref.dtype)

    _pipelined(H_C, first, second)


def _diff_ctx_call(p, lp, sw, lam_init, jl, k_out, v_out, name):
    cache_spec = pl.BlockSpec((None, None, SEQ, ODD_W), lambda b: (b, jl, 0, 0))
    cache_shape = jax.ShapeDtypeStruct((BATCH, N_ODD, SEQ, ODD_W), F32)
    return _aliased_call(
        functools.partial(_diff_ctx_kernel, lam_init=lam_init),
        grid=(BATCH,),
        in_specs=[
            pl.BlockSpec((SEQ, ODD_W), lambda b: (b, 0)),
            pl.BlockSpec((SEQ, ODD_W), lambda b: (b, 1)),
            pl.BlockSpec((SEQ, ODD_W), lambda b: (b, 2)),
            pl.BlockSpec((None, 4, HD_C), lambda b: (jl, 0, 0)),
            pl.BlockSpec((None, 2 * HD_C, ATT_TQ), lambda b: (jl, 0, 0)),
        ],
        args=[p, p, p, lp, sw],
        out_specs=[pl.BlockSpec((SEQ, ODD_W), lambda b: (b, 0)), cache_spec, cache_spec],
        out_shape=[jax.ShapeDtypeStruct((N_TOK, D_MIX), BF16), cache_shape, cache_shape],
        carried=[None, k_out, v_out], sem=("parallel",), name=name)


def _diff_lat_kernel(q_ref, k_ref, v_ref, ck_ref, cv_ref, lp_ref, sw_ref, o_ref, *, lam_init):
    lam = _diff_lambda(lp_ref, lam_init)
    k_parts = [k_ref[...].astype(BF16), ck_ref[...].astype(BF16)]
    vt_parts = [_with_ones_rows(v_ref[...].T), _with_ones_rows(cv_ref[...].T)]
    tq = ATT_TQ

    def first(i):
        return _diff_scores(q_ref[i * tq:(i + 1) * tq, :].astype(BF16), k_parts)

    def second(i, scores):
        y = _diff_finish(scores, vt_parts, lam, lam_init, sw_ref[...])
        o_ref[i * tq:(i + 1) * tq, :] = y.astype(o_ref.dtype)

    _pipelined(q_ref.shape[0] // tq, first, second)


def _diff_lat_call(p, ck, cv, lp, sw, lam_init, jl, mix, name):
    hw = 2 * HD_C
    s_blk0 = N_CTX_TOK // DEC_SEQ
    cache_spec = pl.BlockSpec((None, None, PAST_LEN, hw), lambda b, h: (b, jl, 0, h))
    (out,) = _aliased_call(
        functools.partial(_diff_lat_kernel, lam_init=lam_init),
        grid=(DEC_BATCH, H_C),
        in_specs=[
            pl.BlockSpec((DEC_SEQ, hw), lambda b, h: (s_blk0 + b, h)),
            pl.BlockSpec((DEC_SEQ, hw), lambda b, h: (s_blk0 + b, H_C + h)),
            pl.BlockSpec((DEC_SEQ, hw), lambda b, h: (s_blk0 + b, 2 * H_C + h)),
            cache_spec, cache_spec,
            pl.BlockSpec((None, 4, HD_C), lambda b, h: (jl, 0, 0)),
            pl.BlockSpec((None, hw, ATT_TQ), lambda b, h: (jl, 0, 0)),
        ],
        args=[p, p, p, ck, cv, lp, sw],
        out_specs=[pl.BlockSpec((DEC_SEQ, hw), lambda b, h: (s_blk0 + b, h))],
        out_shape=[jax.ShapeDtypeStruct((N_TOK, D_MIX), BF16)],
        carried=[mix], sem=("parallel", "parallel"), name=name)
    return out


def _mlp_kernel(x_ref, mix_ref, wo_ref, mods_ref, nw_ref, w1_ref, w2_ref, o_ref,
                h_scr, *, lat):
    i = pl.program_id(0)
    k = pl.program_id(1)
    row = 1 + i if lat else 0

    def mod(a):
        return mods_ref[pl.ds(row, 1), a * D_MODEL:(a + 1) * D_MODEL]

    @pl.when(k == 0)
    def _():
        def first(r):
            return jnp.dot(mix_ref[r * MLP_RC:(r + 1) * MLP_RC, :], wo_ref[...],
                           preferred_element_type=F32)

        def second(r, y):
            rows = slice(r * MLP_RC, (r + 1) * MLP_RC)
            x1 = x_ref[rows, :] + mod(2) * y
            o_ref[rows, :] = x1
            h_scr[rows, :] = _norm_mod(x1, nw_ref[...], mod(4), mod(3)).astype(BF16)

        _pipelined(x_ref.shape[0] // MLP_RC, first, second)

    u = jnp.dot(h_scr[...], w1_ref[...].astype(BF16), preferred_element_type=F32)
    u = jnp.square(jnp.maximum(u, 0.0)).astype(BF16)
    o_ref[...] += mod(5) * jnp.dot(u, w2_ref[...].astype(BF16), preferred_element_type=F32)


def _mlp_call(x_src, x_tile0, lat, mix, wo, jl, mods, li, norm_w, w1, w2, out_prev, out_rows,
              out_tile0, name):
    tm, tk = MLP_TM, MLP_TK
    n_tiles = (N_LAT_TOK if lat else N_CTX_TOK) // tm
    mix_tile0 = N_CTX_TOK // tm if lat else 0
    (out,) = _aliased_call(
        functools.partial(_mlp_kernel, lat=lat),
        grid=(n_tiles, D_FF // tk),
        in_specs=[
            pl.BlockSpec((tm, D_MODEL), lambda i, k: (x_tile0 + i, 0)),
            pl.BlockSpec((tm, D_MIX), lambda i, k: (mix_tile0 + i, 0)),
            pl.BlockSpec((None, D_MIX, D_MODEL), lambda i, k: (jl, 0, 0)),
            pl.BlockSpec((None, MOD_ROWS, N_MOD), lambda i, k: (li, 0, 0)),
            pl.BlockSpec((None, None, 1, D_MODEL), lambda i, k: (li, 1, 0, 0)),
            pl.BlockSpec((None, D_MODEL, tk), lambda i, k: (li, 0, k)),
            pl.BlockSpec((None, tk, D_MODEL), lambda i, k: (li, k, 0)),
        ],
        args=[x_src, mix, wo, mods, norm_w, w1, w2],
        out_specs=[pl.BlockSpec((tm, D_MODEL), lambda i, k: (out_tile0 + i, 0))],
        out_shape=[jax.ShapeDtypeStruct((out_rows, D_MODEL), F32)],
        carried=[out_prev], sem=("parallel", "arbitrary"), name=name,
        scratch_shapes=[pltpu.VMEM((tm, D_MODEL), BF16)])
    return out


def _rope_tables(width):
    t = np.arange(DEC_SEQ)
    half = HEAD_GROUP // 2
    inv = ROPE_BASE ** (-np.arange(0, half, 2, dtype=np.float64) / half)
    ang = np.concatenate([(t // GRID_W)[:, None] * inv, (t % GRID_W)[:, None] * inv], axis=-1)
    cos = np.repeat(np.cos(ang), 2, axis=-1)
    sin = np.repeat(np.sin(ang), 2, axis=-1)
    sign = np.tile(np.array([-1.0, 1.0]), HEAD_GROUP // 2)
    reps = width // HEAD_GROUP
    return (jnp.asarray(np.tile(cos, (1, reps)), F32),
            jnp.asarray(np.tile(sin * sign, (1, reps)), F32))


def _block_diag_ones(n):
    g = np.arange(n) // HEAD_GROUP
    return jnp.asarray(g[:, None] == g[None, :], BF16)


def _tile_row(w, width):
    return jnp.tile(w.astype(F32), width // w.shape[0])[None, :]


def _lambda_init(li):
    return 0.8 - 0.6 * math.exp(-0.3 * li)


def kernel(x_prompt, x_sample, cache_k_swa, cache_v_swa, state_hgrn, cache_k_diff, cache_v_diff, c, c_ctx, norm_w, w_ada, b_ada, w_in_even, w_out_even, hgrn_lb_logits, hgrn_norm_w, swa_qnorm_w, swa_knorm_w, swa_sink, w_in_odd, w_out_odd, diff_qnorm_w, diff_knorm_w, diff_lambda_p, diff_subln_w, w_mlp1, w_mlp2):
    assert PROJ_TM == MLP_TM
    n_ctx_tiles = N_CTX_TOK // PROJ_TM
    x = None
    x_ctx0 = x_prompt.reshape(N_CTX_TOK, D_MODEL)
    x_lat0 = x_sample.reshape(N_LAT_TOK, D_MODEL)
    c_all = jnp.concatenate(
        [c_ctx[None, :], c, jnp.zeros((MOD_ROWS - 1 - DEC_BATCH, D_MODEL), F32)], axis=0)
    mods = _mods_call(c_all, w_ada, b_ada.reshape(DEPTH, 1, N_MOD))

    cos_t, sin_t = _rope_tables(PROJ_TN)
    bd = _block_diag_ones(256)
    tabs_f = _hgrn_tables(False)
    tabs_b = _hgrn_tables(True)
    hgrn_tabs = (tabs_f[0], tabs_f[1], tabs_b[0], tabs_b[1])
    lbl = hgrn_lb_logits.astype(F32).reshape(N_EVEN * 2, F_A)
    norm_w4 = norm_w.astype(F32).reshape(DEPTH, 2, 1, D_MODEL)

    w_out_even_b = w_out_even.astype(BF16)
    w_out_odd_b = w_out_odd.astype(BF16)

    ck_swa = cache_k_swa.reshape(DEC_BATCH, N_EVEN, PAST_LEN, KV_B * HD_B)
    cv_swa = cache_v_swa.reshape(DEC_BATCH, N_EVEN, PAST_LEN, KV_B * HD_B)
    ck_diff = cache_k_diff.reshape(DEC_BATCH, N_ODD, PAST_LEN, ODD_W)
    cv_diff = cache_v_diff.reshape(DEC_BATCH, N_ODD, PAST_LEN, ODD_W)
    hgrn_nw = hgrn_norm_w.astype(F32).reshape(N_EVEN, 1, DV_A)
    sink = swa_sink.astype(F32).reshape(N_EVEN, 1, H_B)
    lam_p = diff_lambda_p.astype(F32)
    assert ATT_TQ == SEQ
    subln = jnp.broadcast_to(diff_subln_w.astype(F32)[:, :, None], (N_ODD, 2 * HD_C, ATT_TQ))

    even_kinds = ("silu_scale", "loggate0", "loggate1", "ident", "silu", "qnorm", "kv")
    odd_kinds = ("qnorm", "qnorm", "knorm", "knorm", "ident", "ident")

    k_swa = v_swa = states = k_diff = v_diff = None
    for li in range(DEPTH):
        j = li // 2
        srcs = ((x_ctx0, 0), (x_lat0, 0)) if li == 0 else ((x, 0), (x, n_ctx_tiles))
        if li % 2 == 0:
            p = None
            for lat, (src, t0) in enumerate(srcs):
                p = _proj_call(src, t0, bool(lat), p, mods, li, norm_w4, w_in_even, j, lbl,
                               _tile_row(swa_qnorm_w[j], PROJ_TN),
                               _tile_row(swa_knorm_w[j], PROJ_TN),
                               cos_t, sin_t, bd, even_kinds, f"proj_even{j}_{lat}")
            mix, states = _hgrn_call(p, hgrn_nw, hgrn_tabs, SEQ, BATCH, 0, None, j, None, states,
                                     f"hgrn_ctx{j}")
            (mix,) = _hgrn_call(p, hgrn_nw, hgrn_tabs, DEC_SEQ, DEC_BATCH, N_CTX_TOK // DEC_SEQ,
                                state_hgrn, j, mix, None, f"hgrn_lat{j}")
            mix, k_swa, v_swa = _swa_ctx_call(p, sink, j, mix, k_swa, v_swa, f"swa_ctx{j}")
            mix = _swa_lat_call(p, ck_swa, cv_swa, sink, j, mix, f"swa_lat{j}")
            wo = w_out_even_b
        else:
            p = None
            for lat, (src, t0) in enumerate(srcs):
                p = _proj_call(src, t0, bool(lat), p, mods, li, norm_w4, w_in_odd, j, lbl,
                               _tile_row(diff_qnorm_w[j], PROJ_TN),
                               _tile_row(diff_knorm_w[j], PROJ_TN),
                               cos_t, sin_t, bd, odd_kinds, f"proj_odd{j}_{lat}")
            lam_init = _lambda_init(li)
            mix, k_diff, v_diff = _diff_ctx_call(p, lam_p, subln, lam_init, j, k_diff, v_diff,
                                                 f"diff_ctx{j}")
            mix = _diff_lat_call(p, ck_diff, cv_diff, lam_p, subln, lam_init, j, mix,
                                 f"diff_lat{j}")
            wo = w_out_odd_b
        last = li == DEPTH - 1
        outs = []
        x_next = None
        for lat, (src, t0) in enumerate(srcs):
            rows = (N_LAT_TOK if lat else N_CTX_TOK) if last else N_TOK
            out_t0 = 0 if last else lat * n_ctx_tiles
            x_next = _mlp_call(src, t0, bool(lat), mix, wo, j, mods, li, norm_w4, w_mlp1, w_mlp2,
                               None if last else x_next, rows, out_t0, f"mlp{li}_{lat}")
            outs.append(x_next)
        x = x_next

    y_prompt = outs[0].reshape(BATCH, SEQ, D_MODEL)
    y_sample = outs[1].reshape(DEC_BATCH, DEC_SEQ, D_MODEL)
    return (y_prompt, y_sample,
            k_swa.reshape(BATCH, N_EVEN, SEQ, KV_B, HD_B),
            v_swa.reshape(BATCH, N_EVEN, SEQ, KV_B, HD_B),
            states,
            k_diff.reshape(BATCH, N_ODD, SEQ, H_C, 2, HD_C),
            v_diff.reshape(BATCH, N_ODD, SEQ, H_C, 2 * HD_C))
```

```python
import functools
import math

import numpy as np
import jax
import jax.numpy as jnp
from jax import lax
from jax.experimental import pallas as pl
from jax.experimental.pallas import tpu as pltpu

F32 = jnp.float32
BF16 = jnp.bfloat16

D_MODEL = 1024
BATCH = 16
SEQ = 256
DEPTH = 4
DEC_BATCH = 4
DEC_SEQ = 1024
PAST_LEN = 512
GRID_W = 64
N_EVEN = (DEPTH + 1) // 2
N_ODD = DEPTH // 2
H_A = 4
DK_A = 128
DV_A = D_MODEL // 2 // H_A
F_A = H_A * DK_A
H_B = 8
KV_B = 2
G_B = H_B // KV_B
HD_B = D_MODEL // 2 // H_B
WINDOW = 128
H_C = 8
HD_C = D_MODEL // (2 * H_C)
D_FF = 4 * D_MODEL
ROPE_BASE = 10000.0
EPS = 1e-6
EVEN_COLS = 3 * F_A + 2 * H_A * DV_A + (H_B + 2 * KV_B) * HD_B
ODD_W = H_C * 2 * HD_C
D_MIX = D_MODEL

N_CTX_TOK = BATCH * SEQ
N_LAT_TOK = DEC_BATCH * DEC_SEQ
N_TOK = N_CTX_TOK + N_LAT_TOK
MOD_ROWS = 8
N_MOD = 6 * D_MODEL

HEAD_GROUP = 64
HGRN_CHUNK = 128
HGRN_LEVELS = 7
HGRN_SPLIT = 3
LOG2_E = math.log2(math.e)
VMEM_LIMIT = 48 * 1024 * 1024

PROJ_TM = 1024
PROJ_TN = 512
PROJ_RC = 128
MLP_TM = 1024
MLP_TK = 1024
MLP_RC = 256
ADA_TN = 1536
ATT_TQ = 256
ONES_ROWS = 16


def _silu(x):
    return x * jax.nn.sigmoid(x)


def _nt_dot(a, b):
    return lax.dot_general(a, b, (((1,), (1,)), ((), ())), preferred_element_type=F32)


def _pipelined(n, first, second):
    cur = first(0)
    for i in range(n):
        nxt = first(i + 1) if i + 1 < n else None
        second(i, cur)
        cur = nxt


def _params(sem):
    return pltpu.CompilerParams(dimension_semantics=sem, vmem_limit_bytes=VMEM_LIMIT)


def _aliased_call(kernel, *, grid, in_specs, args, out_specs, out_shape, carried, sem, name,
                  scratch_shapes=()):
    n_in = len(args)
    extra = [buf for buf in carried if buf is not None]
    aliases = {}
    for k, buf in enumerate(carried):
        if buf is not None:
            aliases[n_in + len(aliases)] = k
    n_extra = len(extra)

    def body(*refs):
        kernel(*refs[:n_in], *refs[n_in + n_extra:])

    return pl.pallas_call(
        body,
        grid=grid,
        in_specs=list(in_specs) + [pl.BlockSpec(memory_space=pl.ANY)] * n_extra,
        out_specs=out_specs,
        out_shape=out_shape,
        input_output_aliases=aliases,
        scratch_shapes=list(scratch_shapes),
        compiler_params=_params(sem),
        name=name,
    )(*args, *extra)


def _mods_kernel(c_ref, w_ref, b_ref, o_ref):
    s = _silu(c_ref[...]).astype(BF16)
    o_ref[...] = jnp.dot(s, w_ref[...].astype(BF16), preferred_element_type=F32) + b_ref[...]


def _mods_call(c_all, w_ada, b_ada):
    return pl.pallas_call(
        _mods_kernel,
        grid=(DEPTH, N_MOD // ADA_TN),
        in_specs=[
            pl.BlockSpec((MOD_ROWS, D_MODEL), lambda l, j: (0, 0)),
            pl.BlockSpec((None, D_MODEL, ADA_TN), lambda l, j: (l, 0, j)),
            pl.BlockSpec((None, 1, ADA_TN), lambda l, j: (l, 0, j)),
        ],
        out_specs=pl.BlockSpec((None, MOD_ROWS, ADA_TN), lambda l, j: (l, 0, j)),
        out_shape=jax.ShapeDtypeStruct((DEPTH, MOD_ROWS, N_MOD), F32),
        compiler_params=_params(("parallel", "parallel")),
        name="ada_mods",
    )(c_all, w_ada, b_ada)


def _norm_mod(x, nw, sc, sh):
    ms = jnp.mean(x * x, axis=-1, keepdims=True)
    return (x * lax.rsqrt(ms + EPS) * nw) * (1.0 + sc) + sh


def _group_rms(y, w_t, bd_ref):
    yy = (y * y).astype(BF16)
    bw = bd_ref.shape[0]
    parts = [jnp.dot(yy[:, s:s + bw], bd_ref[...], preferred_element_type=F32)
             for s in range(0, y.shape[1], bw)]
    ss = parts[0] if len(parts) == 1 else jnp.concatenate(parts, axis=1)
    return y * lax.rsqrt(ss * (1.0 / HEAD_GROUP) + EPS) * w_t


def _rope(y, cos, sin):
    n = y.shape[1]
    lane = lax.broadcasted_iota(jnp.int32, y.shape, 1)
    nxt = pltpu.roll(y, n - 1, axis=1)
    prv = pltpu.roll(y, 1, axis=1)
    swapped = jnp.where((lane & 1) == 0, nxt, prv)
    return y * cos + swapped * sin


def _lower_bounds(lbl_ref, jl):
    rows = [lbl_ref[pl.ds(2 * m, 2), :] for m in range(N_EVEN)]
    mx = functools.reduce(jnp.maximum, rows)
    es = [jnp.exp(r - mx) for r in rows]
    den = functools.reduce(lambda a, b: a + b, es)
    sm = [e / den for e in es]
    cs = sm[0]
    for m in range(1, jl + 1):
        cs = cs + sm[m]
    return cs - sm[0]


def _proj_kernel(x_ref, mods_ref, nw_ref, w_ref, wkv_ref, lbl_ref, qn_ref, kn_ref, cos_ref,
                 sin_ref, bd_ref, o_ref, h_scr, w_scr, *, kinds, jl, lat):
    j = pl.program_id(0)
    i = pl.program_id(1)
    tm, tn = o_ref.shape
    kvw = 2 * KV_B * HD_B

    @pl.when(j == 0)
    def _():
        row = 1 + i if lat else 0
        sh = mods_ref[pl.ds(row, 1), 0:D_MODEL]
        sc = mods_ref[pl.ds(row, 1), D_MODEL:2 * D_MODEL]
        h_scr[i] = _norm_mod(x_ref[...], nw_ref[...], sc, sh).astype(BF16)

    def finish(kind, y, rows):
        if kind == "silu_scale":
            return _silu(y) * (DK_A ** -0.5)
        if kind in ("loggate0", "loggate1"):
            d = int(kind[-1])
            lb = _lower_bounds(lbl_ref, jl)[d:d + 1, :]
            return jnp.log2(lb + (1.0 - lb) * jax.nn.sigmoid(y))
        if kind == "ident":
            return y
        if kind == "silu":
            return _silu(y)
        if kind in ("qnorm", "knorm"):
            w_t = qn_ref[...] if kind == "qnorm" else kn_ref[...]
            r = _group_rms(y, w_t, bd_ref)
            if lat:
                r = _rope(r, cos_ref[rows, :], sin_ref[rows, :])
            return r * (HEAD_GROUP ** -0.5 * LOG2_E) if kind == "qnorm" else r
        if kind == "kv":
            kn = _group_rms(y, kn_ref[:, 0:kvw], bd_ref)
            if lat:
                kn = _rope(kn, cos_ref[rows, 0:kvw], sin_ref[rows, 0:kvw])
            lane = lax.broadcasted_iota(jnp.int32, y.shape, 1)
            return jnp.where(lane < KV_B * HD_B, kn, y)
        raise ValueError(kind)

    def run(kind):
        @pl.when(i == 0)
        def _():
            if kind == "kv":
                w_scr[:, 0:kvw] = wkv_ref[...].astype(BF16)
            else:
                w_scr[...] = w_ref[...].astype(BF16)

        def first(r):
            w = w_scr[:, 0:kvw] if kind == "kv" else w_scr[...]
            return jnp.dot(h_scr[i, r * PROJ_RC:(r + 1) * PROJ_RC, :], w,
                           preferred_element_type=F32)

        def second(r, y):
            rows = slice(r * PROJ_RC, (r + 1) * PROJ_RC)
            if kind == "kv":
                o_ref[rows, 0:kvw] = finish(kind, y, rows)
                o_ref[rows, kvw:tn] = jnp.zeros((PROJ_RC, tn - kvw), F32)
            else:
                o_ref[rows, :] = finish(kind, y, rows)

        _pipelined(tm // PROJ_RC, first, second)

    for jj, kind in enumerate(kinds):
        pl.when(j == jj)(functools.partial(run, kind))


def _proj_call(x_src, x_tile0, lat, p_prev, mods, li, norm_w, w, jl, lbl, qn_t, kn_t, cos_t, sin_t,
               bd, kinds, name):
    tm, tn = PROJ_TM, PROJ_TN
    assert tm == DEC_SEQ
    n_ctx_tiles = N_CTX_TOK // tm
    n_tiles = (N_LAT_TOK if lat else N_CTX_TOK) // tm
    tile0 = n_ctx_tiles if lat else 0
    if not lat:
        cos_t = sin_t = jnp.zeros((8, 128), F32)
    n_main = sum(1 for k in kinds if k != "kv")
    n_cols = tn * len(kinds)
    kvw = 2 * KV_B * HD_B
    kv_blk = (n_main * tn) // kvw if "kv" in kinds else 0
    const = lambda j, i: (0, 0)
    (out,) = _aliased_call(
        functools.partial(_proj_kernel, kinds=kinds, jl=jl, lat=lat),
        grid=(len(kinds), n_tiles),
        in_specs=[
            pl.BlockSpec((tm, D_MODEL),
                         lambda j, i: (x_tile0 + jnp.where(j == 0, i, n_tiles - 1), 0)),
            pl.BlockSpec((None, MOD_ROWS, N_MOD), lambda j, i: (li, 0, 0)),
            pl.BlockSpec((None, None, 1, D_MODEL), lambda j, i: (li, 0, 0, 0)),
            pl.BlockSpec((None, D_MODEL, tn), lambda j, i: (jl, 0, jnp.minimum(j, n_main - 1))),
            pl.BlockSpec((None, D_MODEL, kvw), lambda j, i: (jl, 0, kv_blk)),
            pl.BlockSpec(lbl.shape, const),
            pl.BlockSpec((1, tn), const),
            pl.BlockSpec((1, tn), const),
            pl.BlockSpec(cos_t.shape, const),
            pl.BlockSpec(sin_t.shape, const),
            pl.BlockSpec(bd.shape, const),
        ],
        args=[x_src, mods, norm_w, w, w, lbl, qn_t, kn_t, cos_t, sin_t, bd],
        out_specs=[pl.BlockSpec((tm, tn), lambda j, i: (tile0 + i, j))],
        out_shape=[jax.ShapeDtypeStruct((N_TOK, n_cols), F32)],
        carried=[p_prev], sem=("arbitrary", "arbitrary"), name=name,
        scratch_shapes=[pltpu.VMEM((n_tiles, tm, D_MODEL), BF16),
                        pltpu.VMEM((D_MODEL, tn), BF16)])
    return out


def _hgrn_tables(rev):
    c = HGRN_CHUNK
    t = np.arange(c)
    w = (t[None, :] <= t[:, None]) if not rev else (t[None, :] >= t[:, None])
    ws = np.concatenate([w.astype(np.float32)] * HGRN_SPLIT, axis=1)
    x = t[:, None] ^ t[None, :]
    lv = np.where(x > 0, np.floor(np.log2(np.maximum(x, 1))).astype(np.int32), HGRN_LEVELS)
    causal = (t[None, :] < t[:, None]) if not rev else (t[None, :] > t[:, None])
    lv = np.where(causal | (x == 0), lv, -1).astype(np.int32)
    return jnp.asarray(ws, BF16), jnp.asarray(lv)


def _hgrn_level_exponents(cum, rev):
    c = HGRN_CHUNK
    sub_rows = 8
    c3 = cum.reshape(c // sub_rows, sub_rows, DK_A)
    sub = lax.broadcasted_iota(jnp.int32, c3.shape, 1)
    out = []
    for l in range(1, HGRN_LEVELS):
        hb = 1 << l
        if 2 * hb <= sub_rows:
            r = None
            for b0 in range(0, sub_rows, 2 * hb):
                idx = b0 + (hb if rev else hb - 1)
                rk = c3[:, idx:idx + 1, :]
                r = rk if r is None else jnp.where(sub < b0, r, rk)
            d = c3 - r
            bit = (sub & hb) != 0
            q_role = jnp.logical_not(bit) if rev else bit
            out.append(jnp.where(q_role, d, -d).reshape(c, DK_A))
        else:
            pieces = []
            for b0 in range(0, c, 2 * hb):
                mid = b0 + hb
                ridx = mid if rev else mid - 1
                r = cum[ridx:ridx + 1, :]
                lo = cum[b0:mid]
                hi = cum[mid:b0 + 2 * hb]
                pieces += [lo - r, r - hi] if rev else [r - lo, hi - r]
            out.append(jnp.concatenate(pieces, axis=0))
    return out


def _hgrn_level_operands(l, q, k, f, z, rev, row):
    c = HGRN_CHUNK
    hb = 1 << l
    if hb >= 8:
        zero = jnp.zeros((hb, DK_A), F32)
        qparts, kparts = [], []
        for b0 in range(0, c, 2 * hb):
            lo, hi = slice(b0, b0 + hb), slice(b0 + hb, b0 + 2 * hb)
            if rev:
                qparts += [q[lo] * z[lo], zero]
                kparts += [zero, k[hi] * z[hi]]
            else:
                qparts += [zero, q[hi] * z[hi]]
                kparts += [k[lo] * z[lo], zero]
        return (jnp.concatenate(qparts, axis=0).astype(BF16),
                jnp.concatenate(kparts, axis=0).astype(BF16))
    bit = ((row >> l) & 1) == 1
    q_role = jnp.logical_not(bit) if rev else bit
    ql = jnp.where(q_role, q * (f if l == 0 else z), 0.0).astype(BF16)
    kl = jnp.where(q_role, 0.0, k if l == 0 else k * z).astype(BF16)
    return ql, kl


def _hgrn_chunks(chains):
    c = HGRN_CHUNK
    row = lax.broadcasted_iota(jnp.int32, (c, DK_A), 0)
    cums = []
    for q, g, v, st, w_ref, lv_ref, rev in chains:
        terms = []
        rem = g
        for _ in range(HGRN_SPLIT):
            term = rem.astype(BF16)
            terms.append(term)
            rem = rem - term.astype(F32)
        cums.append(jnp.dot(w_ref[...], jnp.concatenate(terms, axis=0),
                            preferred_element_type=F32))
    work = []
    for (q, g, v, st, w_ref, lv_ref, rev), cum in zip(chains, cums):
        f = jnp.exp2(g)
        k = 1.0 - f
        last = 0 if rev else c - 1
        total_e = cum[last:last + 1, :]
        qd = (q * jnp.exp2(cum)).astype(BF16)
        kd = (k * jnp.exp2(total_e - cum)).astype(BF16)
        o = _nt_dot(qd, st.astype(BF16))
        st_new = (st * jnp.exp2(total_e)
                  + jnp.dot(v.T.astype(BF16), kd, preferred_element_type=F32))
        zs = [None] + [jnp.exp2(e) for e in _hgrn_level_exponents(cum, rev)]
        work.append((k, f, zs, o, st_new))
    accs = [None] * len(chains)
    for l in reversed(range(HGRN_LEVELS)):
        for i, ((q, g, v, st, w_ref, lv_ref, rev), (k, f, zs, o, st_new)) in enumerate(
                zip(chains, work)):
            ql, kl = _hgrn_level_operands(l, q, k, f, zs[l], rev, row)
            a_l = _nt_dot(ql, kl)
            accs[i] = a_l if accs[i] is None else jnp.where(lv_ref[...] == l, a_l, accs[i])
    outs = []
    for (q, g, v, st, w_ref, lv_ref, rev), (k, f, zs, o, st_new), a in zip(chains, work, accs):
        a = jnp.where(lv_ref[...] == HGRN_LEVELS, jnp.sum(q * k, axis=-1, keepdims=True), a)
        outs.append((o + jnp.dot(a.astype(BF16), v.astype(BF16), preferred_element_type=F32),
                     st_new))
    return outs


def _hgrn_kernel(*refs, n_chunks, has_init, emit_state):
    refs = list(refs)
    q_ref, gf_ref, gb_ref, v_ref, sg_ref, nw_ref, wf_ref, wb_ref, lvf_ref, lvb_ref = refs[:10]
    pos = 10
    s0_ref = None
    if has_init:
        s0_ref = refs[pos]
        pos += 1
    o_ref = refs[pos]
    pos += 1
    so_ref = None
    if emit_state:
        so_ref = refs[pos]
        pos += 1
    of_scr, ob_scr, st_scr = refs[pos:pos + 3]

    for d in range(2):
        for h in range(H_A):
            if has_init:
                st_scr[d, h] = s0_ref[d, h].T
            else:
                st_scr[d, h] = jnp.zeros((DV_A, DK_A), F32)

    def body(c, carry):
        chains, dests = [], []
        for h in range(H_A):
            cols = slice(h * DK_A, (h + 1) * DK_A)
            for d, (g_ref, w_ref, lv_ref, scr) in enumerate(
                    ((gf_ref, wf_ref, lvf_ref, of_scr), (gb_ref, wb_ref, lvb_ref, ob_scr))):
                cc = c if d == 0 else n_chunks - 1 - c
                r0 = pl.multiple_of(cc * HGRN_CHUNK, HGRN_CHUNK)
                rows = pl.ds(r0, HGRN_CHUNK)
                chains.append((q_ref[rows, cols], g_ref[rows, cols], v_ref[rows, cols],
                               st_scr[d, h], w_ref, lv_ref, d == 1))
                dests.append((scr, rows, cols, d, h))
        for (o, st), (scr, rows, cols, d, h) in zip(_hgrn_chunks(chains), dests):
            st_scr[d, h] = st
            scr[rows, cols] = o
        return carry

    lax.fori_loop(0, n_chunks, body, 0)
    for h in range(H_A):
        cols = slice(h * DV_A, (h + 1) * DV_A)
        o = of_scr[:, cols] + ob_scr[:, cols]
        y = o * lax.rsqrt(jnp.mean(o * o, axis=-1, keepdims=True) + EPS) * nw_ref[...]
        o_ref[:, cols] = (y * sg_ref[:, cols]).astype(o_ref.dtype)
    if emit_state:
        for d in range(2):
            for h in range(H_A):
                so_ref[d, h] = st_scr[d, h].T


def _hgrn_call(p, nw, tabs, seq_len, n_seq, row_blk0, s0, jl, mix, state_out, name):
    wf, lvf, wb, lvb = tabs
    has_init = s0 is not None
    emit_state = s0 is None
    const = lambda b: (0, 0)
    blk = (seq_len, F_A)
    state_spec = pl.BlockSpec((None, None, 2, H_A, DK_A, DV_A), lambda b: (b, jl, 0, 0, 0, 0))
    in_specs = [pl.BlockSpec(blk, (lambda b, part=part: (row_blk0 + b, part))) for part in range(5)]
    in_specs += [
        pl.BlockSpec((None, 1, DV_A), lambda b: (jl, 0, 0)),
        pl.BlockSpec(wf.shape, const), pl.BlockSpec(wb.shape, const),
        pl.BlockSpec(lvf.shape, const), pl.BlockSpec(lvb.shape, const),
    ]
    args = [p, p, p, p, p, nw, wf, wb, lvf, lvb]
    if has_init:
        in_specs.append(state_spec)
        args.append(s0)
    out_shape = [jax.ShapeDtypeStruct((N_TOK, D_MIX), BF16)]
    out_specs = [pl.BlockSpec((seq_len, H_A * DV_A), lambda b: (row_blk0 + b, 0))]
    carried = [mix]
    if emit_state:
        out_shape.append(jax.ShapeDtypeStruct((BATCH, N_EVEN, 2, H_A, DK_A, DV_A), F32))
        out_specs.append(state_spec)
        carried.append(state_out)
    return _aliased_call(
        functools.partial(_hgrn_kernel, n_chunks=seq_len // HGRN_CHUNK, has_init=has_init,
                          emit_state=emit_state),
        grid=(n_seq,), in_specs=in_specs, args=args, out_specs=out_specs,
        out_shape=out_shape, carried=carried, sem=("parallel",), name=name,
        scratch_shapes=[pltpu.VMEM((seq_len, H_A * DV_A), F32),
                        pltpu.VMEM((seq_len, H_A * DV_A), F32),
                        pltpu.VMEM((2, H_A, DV_A, DK_A), F32)])


def _swa_ctx_kernel(q_ref, kv_ref, sink_ref, o_ref, kc_ref, vc_ref):
    kv = kv_ref[...]
    k32 = kv[:, 0:KV_B * HD_B]
    v32 = kv[:, KV_B * HD_B:2 * KV_B * HD_B]
    kc_ref[...] = k32
    vc_ref[...] = v32
    k = k32.astype(BF16)
    vt32 = v32.T
    vts = [_with_ones_rows(vt32[n * HD_B:(n + 1) * HD_B]) for n in range(KV_B)]
    q = q_ref[...].astype(BF16)
    ksl = [slice((h // G_B) * HD_B, (h // G_B + 1) * HD_B) for h in range(H_B)]
    sts = [_nt_dot(k[:, ksl[h]], q[:, h * HD_B:(h + 1) * HD_B]) for h in range(H_B)]
    ps, sinks = [], []
    for h in range(H_B):
        sink = sink_ref[0:1, h:h + 1] * LOG2_E
        m = jnp.maximum(jnp.max(sts[h], axis=0, keepdims=True), sink)
        ps.append(jnp.exp2(sts[h] - m).astype(BF16))
        sinks.append(jnp.exp2(sink - m))
    outs = []
    for h in range(H_B):
        ota = jnp.dot(vts[h // G_B], ps[h], preferred_element_type=F32)
        outs.append(ota[0:HD_B] / (ota[HD_B:HD_B + 1] + sinks[h]))
    o_ref[...] = jnp.concatenate(outs, axis=0).T.astype(o_ref.dtype)


def _swa_ctx_call(p, sink, jl, mix, k_out, v_out, name):
    qcol = (3 * F_A + 2 * H_A * DV_A) // PROJ_TN
    cache_spec = pl.BlockSpec((None, None, SEQ, KV_B * HD_B), lambda b: (b, jl, 0, 0))
    cache_shape = jax.ShapeDtypeStruct((BATCH, N_EVEN, SEQ, KV_B * HD_B), F32)
    return _aliased_call(
        _swa_ctx_kernel,
        grid=(BATCH,),
        in_specs=[
            pl.BlockSpec((SEQ, H_B * HD_B), lambda b: (b, qcol)),
            pl.BlockSpec((SEQ, PROJ_TN), lambda b: (b, qcol + 1)),
            pl.BlockSpec((None, 1, H_B), lambda b: (jl, 0, 0)),
        ],
        args=[p, p, sink],
        out_specs=[pl.BlockSpec((SEQ, H_B * HD_B), lambda b: (b, 1)), cache_spec, cache_spec],
        out_shape=[jax.ShapeDtypeStruct((N_TOK, D_MIX), BF16), cache_shape, cache_shape],
        carried=[mix, k_out, v_out], sem=("parallel",), name=name)


def _swa_lat_kernel(q_ref, kv_ref, ck_ref, cv_ref, sink_ref, o_ref):
    qi = pl.program_id(1)
    tq = q_ref.shape[0]
    span = tq + 2 * WINDOW
    ws = pl.multiple_of(jnp.clip(qi * tq - WINDOW, 0, DEC_SEQ - span), WINDOW)
    kvw = kv_ref[pl.ds(ws, span), :]
    kw = kvw[:, 0:KV_B * HD_B].astype(BF16)
    vwt = kvw[:, KV_B * HD_B:2 * KV_B * HD_B].T.astype(BF16)
    kc = ck_ref[...].astype(BF16)
    vct = cv_ref[...].T.astype(BF16)
    q = q_ref[...].astype(BF16)
    t_k = ws + lax.broadcasted_iota(jnp.int32, (span, tq), 0)
    t_q = qi * tq + lax.broadcasted_iota(jnp.int32, (span, tq), 1)
    valid = jnp.abs(t_q - t_k) <= WINDOW
    outs = []

    def first(h):
        qh = q[:, h * HD_B:(h + 1) * HD_B]
        ksl = slice((h // G_B) * HD_B, (h // G_B + 1) * HD_B)
        return _nt_dot(kw[:, ksl], qh), _nt_dot(kc[:, ksl], qh)

    def second(h, scores):
        ksl = slice((h // G_B) * HD_B, (h // G_B + 1) * HD_B)
        s_w = jnp.where(valid, scores[0], -jnp.inf)
        s_c = scores[1]
        sink = sink_ref[0:1, h:h + 1] * LOG2_E
        m = jnp.maximum(jnp.maximum(jnp.max(s_w, axis=0, keepdims=True),
                                    jnp.max(s_c, axis=0, keepdims=True)), sink)
        p_w = jnp.exp2(s_w - m)
        p_c = jnp.exp2(s_c - m)
        den = (jnp.sum(p_w, axis=0, keepdims=True) + jnp.sum(p_c, axis=0, keepdims=True)
               + jnp.exp2(sink - m))
        ot = (jnp.dot(vwt[ksl, :], p_w.astype(BF16), preferred_element_type=F32)
              + jnp.dot(vct[ksl, :], p_c.astype(BF16), preferred_element_type=F32))
        outs.append(ot / den)

    _pipelined(H_B, first, second)
    o_ref[...] = jnp.concatenate(outs, axis=0).T.astype(o_ref.dtype)


def _swa_lat_call(p, ck, cv, sink, jl, mix, name):
    tq = ATT_TQ
    qcol = (3 * F_A + 2 * H_A * DV_A) // PROJ_TN
    nq = DEC_SEQ // tq
    q_blk0 = N_CTX_TOK // tq
    s_blk0 = N_CTX_TOK // DEC_SEQ
    cache_spec = pl.BlockSpec((None, None, PAST_LEN, KV_B * HD_B), lambda b, i: (b, jl, 0, 0))
    (out,) = _aliased_call(
        _swa_lat_kernel,
        grid=(DEC_BATCH, nq),
        in_specs=[
            pl.BlockSpec((tq, H_B * HD_B), lambda b, i: (q_blk0 + b * nq + i, qcol)),
            pl.BlockSpec((DEC_SEQ, PROJ_TN), lambda b, i: (s_blk0 + b, qcol + 1)),
            cache_spec, cache_spec,
            pl.BlockSpec((None, 1, H_B), lambda b, i: (jl, 0, 0)),
        ],
        args=[p, p, ck, cv, sink],
        out_specs=[pl.BlockSpec((tq, H_B * HD_B), lambda b, i: (q_blk0 + b * nq + i, 1))],
        out_shape=[jax.ShapeDtypeStruct((N_TOK, D_MIX), BF16)],
        carried=[mix], sem=("parallel", "parallel"), name=name)
    return out


def _diff_lambda(lp_ref, lam_init):
    lp = lp_ref[...]
    a = jnp.sum(lp[0:1] * lp[1:2], axis=-1, keepdims=True)
    b = jnp.sum(lp[2:3] * lp[3:4], axis=-1, keepdims=True)
    return jnp.exp(a) - jnp.exp(b) + lam_init


def _diff_scores(q, k_parts):
    return [[_nt_dot(kp[:, c * HD_C:(c + 1) * HD_C], q[:, c * HD_C:(c + 1) * HD_C])
             for kp in k_parts] for c in range(2)]


def _diff_finish(scores, vt_parts, lam, lam_init, sw_t):
    hw = 2 * HD_C
    comps = []
    for c in range(2):
        ss = scores[c]
        m = functools.reduce(jnp.maximum, [jnp.max(s, axis=0, keepdims=True) for s in ss])
        ota = functools.reduce(
            lambda a, b: a + b,
            [jnp.dot(vt, jnp.exp2(s - m).astype(BF16), preferred_element_type=F32)
             for s, vt in zip(ss, vt_parts)])
        comps.append(ota[0:hw] / ota[hw:hw + 1])
    ot = comps[0] - lam * comps[1]
    yt = ot * lax.rsqrt(jnp.mean(ot * ot, axis=0, keepdims=True) + EPS) * sw_t
    return (yt * (1.0 - lam_init)).T


def _with_ones_rows(vt):
    return jnp.concatenate([vt, jnp.ones((ONES_ROWS, vt.shape[1]), F32)], axis=0).astype(BF16)


def _diff_ctx_kernel(q_ref, k_ref, v_ref, lp_ref, sw_ref, o_ref, kc_ref, vc_ref, *, lam_init):
    lam = _diff_lambda(lp_ref, lam_init)
    hw = 2 * HD_C
    vts = {}

    def first(h):
        sl = slice(h * hw, (h + 1) * hw)
        k32 = k_ref[:, sl]
        v32 = v_ref[:, sl]
        kc_ref[:, sl] = k32
        vc_ref[:, sl] = v32
        vts[h] = _with_ones_rows(v32.T)
        return _diff_scores(q_ref[:, sl].astype(BF16), [k32.astype(BF16)])

    def second(h, scores):
        y = _diff_finish(scores, [vts.pop(h)], lam, lam_init, sw_ref[...])
        o_ref[:, h * hw:(h + 1) * hw] = y.astype(o_ref.dtype)

    _pipelined(H_C, first, second)


def _diff_ctx_call(p, lp, sw, lam_init, jl, k_out, v_out, name):
    cache_spec = pl.BlockSpec((None, None, SEQ, ODD_W), lambda b: (b, jl, 0, 0))
    cache_shape = jax.ShapeDtypeStruct((BATCH, N_ODD, SEQ, ODD_W), F32)
    return _aliased_call(
        functools.partial(_diff_ctx_kernel, lam_init=lam_init),
        grid=(BATCH,),
        in_specs=[
            pl.BlockSpec((SEQ, ODD_W), lambda b: (b, 0)),
            pl.BlockSpec((SEQ, ODD_W), lambda b: (b, 1)),
            pl.BlockSpec((SEQ, ODD_W), lambda b: (b, 2)),
            pl.BlockSpec((None, 4, HD_C), lambda b: (jl, 0, 0)),
            pl.BlockSpec((None, 2 * HD_C, ATT_TQ), lambda b: (jl, 0, 0)),
        ],
        args=[p, p, p, lp, sw],
        out_specs=[pl.BlockSpec((SEQ, ODD_W), lambda b: (b, 0)), cache_spec, cache_spec],
        out_shape=[jax.ShapeDtypeStruct((N_TOK, D_MIX), BF16), cache_shape, cache_shape],
        carried=[None, k_out, v_out], sem=("parallel",), name=name)


def _diff_lat_kernel(q_ref, k_ref, v_ref, ck_ref, cv_ref, lp_ref, sw_ref, o_ref, *, lam_init):
    lam = _diff_lambda(lp_ref, lam_init)
    k_parts = [k_ref[...].astype(BF16), ck_ref[...].astype(BF16)]
    vt_parts = [_with_ones_rows(v_ref[...].T), _with_ones_rows(cv_ref[...].T)]
    tq = ATT_TQ

    def first(i):
        return _diff_scores(q_ref[i * tq:(i + 1) * tq, :].astype(BF16), k_parts)

    def second(i, scores):
        y = _diff_finish(scores, vt_parts, lam, lam_init, sw_ref[...])
        o_ref[i * tq:(i + 1) * tq, :] = y.astype(o_ref.dtype)

    _pipelined(q_ref.shape[0] // tq, first, second)


def _diff_lat_call(p, ck, cv, lp, sw, lam_init, jl, mix, name):
    hw = 2 * HD_C
    s_blk0 = N_CTX_TOK // DEC_SEQ
    cache_spec = pl.BlockSpec((None, None, PAST_LEN, hw), lambda b, h: (b, jl, 0, h))
    (out,) = _aliased_call(
        functools.partial(_diff_lat_kernel, lam_init=lam_init),
        grid=(DEC_BATCH, H_C),
        in_specs=[
            pl.BlockSpec((DEC_SEQ, hw), lambda b, h: (s_blk0 + b, h)),
            pl.BlockSpec((DEC_SEQ, hw), lambda b, h: (s_blk0 + b, H_C + h)),
            pl.BlockSpec((DEC_SEQ, hw), lambda b, h: (s_blk0 + b, 2 * H_C + h)),
            cache_spec, cache_spec,
            pl.BlockSpec((None, 4, HD_C), lambda b, h: (jl, 0, 0)),
            pl.BlockSpec((None, hw, ATT_TQ), lambda b, h: (jl, 0, 0)),
        ],
        args=[p, p, p, ck, cv, lp, sw],
        out_specs=[pl.BlockSpec((DEC_SEQ, hw), lambda b, h: (s_blk0 + b, h))],
        out_shape=[jax.ShapeDtypeStruct((N_TOK, D_MIX), BF16)],
        carried=[mix], sem=("parallel", "parallel"), name=name)
    return out


def _mlp_kernel(x_ref, mix_ref, wo_ref, mods_ref, nw_ref, w1_ref, w2_ref, o_ref,
                h_scr, *, lat):
    i = pl.program_id(0)
    k = pl.program_id(1)
    row = 1 + i if lat else 0

    def mod(a):
        return mods_ref[pl.ds(row, 1), a * D_MODEL:(a + 1) * D_MODEL]

    @pl.when(k == 0)
    def _():
        def first(r):
            return jnp.dot(mix_ref[r * MLP_RC:(r + 1) * MLP_RC, :], wo_ref[...],
                           preferred_element_type=F32)

        def second(r, y):
            rows = slice(r * MLP_RC, (r + 1) * MLP_RC)
            x1 = x_ref[rows, :] + mod(2) * y
            o_ref[rows, :] = x1
            h_scr[rows, :] = _norm_mod(x1, nw_ref[...], mod(4), mod(3)).astype(BF16)

        _pipelined(x_ref.shape[0] // MLP_RC, first, second)

    u = jnp.dot(h_scr[...], w1_ref[...].astype(BF16), preferred_element_type=F32)
    u = jnp.square(jnp.maximum(u, 0.0)).astype(BF16)
    o_ref[...] += mod(5) * jnp.dot(u, w2_ref[...].astype(BF16), preferred_element_type=F32)


def _mlp_call(x_src, x_tile0, lat, mix, wo, jl, mods, li, norm_w, w1, w2, out_prev, out_rows,
              out_tile0, name):
    tm, tk = MLP_TM, MLP_TK
    n_tiles = (N_LAT_TOK if lat else N_CTX_TOK) // tm
    mix_tile0 = N_CTX_TOK // tm if lat else 0
    (out,) = _aliased_call(
        functools.partial(_mlp_kernel, lat=lat),
        grid=(n_tiles, D_FF // tk),
        in_specs=[
            pl.BlockSpec((tm, D_MODEL), lambda i, k: (x_tile0 + i, 0)),
            pl.BlockSpec((tm, D_MIX), lambda i, k: (mix_tile0 + i, 0)),
            pl.BlockSpec((None, D_MIX, D_MODEL), lambda i, k: (jl, 0, 0)),
            pl.BlockSpec((None, MOD_ROWS, N_MOD), lambda i, k: (li, 0, 0)),
            pl.BlockSpec((None, None, 1, D_MODEL), lambda i, k: (li, 1, 0, 0)),
            pl.BlockSpec((None, D_MODEL, tk), lambda i, k: (li, 0, k)),
            pl.BlockSpec((None, tk, D_MODEL), lambda i, k: (li, k, 0)),
        ],
        args=[x_src, mix, wo, mods, norm_w, w1, w2],
        out_specs=[pl.BlockSpec((tm, D_MODEL), lambda i, k: (out_tile0 + i, 0))],
        out_shape=[jax.ShapeDtypeStruct((out_rows, D_MODEL), F32)],
        carried=[out_prev], sem=("parallel", "arbitrary"), name=name,
        scratch_shapes=[pltpu.VMEM((tm, D_MODEL), BF16)])
    return out


def _rope_tables(width):
    t = np.arange(DEC_SEQ)
    half = HEAD_GROUP // 2
    inv = ROPE_BASE ** (-np.arange(0, half, 2, dtype=np.float64) / half)
    ang = np.concatenate([(t // GRID_W)[:, None] * inv, (t % GRID_W)[:, None] * inv], axis=-1)
    cos = np.repeat(np.cos(ang), 2, axis=-1)
    sin = np.repeat(np.sin(ang), 2, axis=-1)
    sign = np.tile(np.array([-1.0, 1.0]), HEAD_GROUP // 2)
    reps = width // HEAD_GROUP
    return (jnp.asarray(np.tile(cos, (1, reps)), F32),
            jnp.asarray(np.tile(sin * sign, (1, reps)), F32))


def _block_diag_ones(n):
    g = np.arange(n) // HEAD_GROUP
    return jnp.asarray(g[:, None] == g[None, :], BF16)


def _tile_row(w, width):
    return jnp.tile(w.astype(F32), width // w.shape[0])[None, :]


def _lambda_init(li):
    return 0.8 - 0.6 * math.exp(-0.3 * li)


def kernel(x_prompt, x_sample, cache_k_swa, cache_v_swa, state_hgrn, cache_k_diff, cache_v_diff, c, c_ctx, norm_w, w_ada, b_ada, w_in_even, w_out_even, hgrn_lb_logits, hgrn_norm_w, swa_qnorm_w, swa_knorm_w, swa_sink, w_in_odd, w_out_odd, diff_qnorm_w, diff_knorm_w, diff_lambda_p, diff_subln_w, w_mlp1, w_mlp2):
    assert PROJ_TM == MLP_TM
    n_ctx_tiles = N_CTX_TOK // PROJ_TM
    x = None
    x_ctx0 = x_prompt.reshape(N_CTX_TOK, D_MODEL)
    x_lat0 = x_sample.reshape(N_LAT_TOK, D_MODEL)
    c_all = jnp.concatenate(
        [c_ctx[None, :], c, jnp.zeros((MOD_ROWS - 1 - DEC_BATCH, D_MODEL), F32)], axis=0)
    mods = _mods_call(c_all, w_ada, b_ada.reshape(DEPTH, 1, N_MOD))

    cos_t, sin_t = _rope_tables(PROJ_TN)
    bd = _block_diag_ones(256)
    tabs_f = _hgrn_tables(False)
    tabs_b = _hgrn_tables(True)
    hgrn_tabs = (tabs_f[0], tabs_f[1], tabs_b[0], tabs_b[1])
    lbl = hgrn_lb_logits.astype(F32).reshape(N_EVEN * 2, F_A)
    norm_w4 = norm_w.astype(F32).reshape(DEPTH, 2, 1, D_MODEL)

    w_out_even_b = w_out_even.astype(BF16)
    w_out_odd_b = w_out_odd.astype(BF16)

    ck_swa = cache_k_swa.reshape(DEC_BATCH, N_EVEN, PAST_LEN, KV_B * HD_B)
    cv_swa = cache_v_swa.reshape(DEC_BATCH, N_EVEN, PAST_LEN, KV_B * HD_B)
    ck_diff = cache_k_diff.reshape(DEC_BATCH, N_ODD, PAST_LEN, ODD_W)
    cv_diff = cache_v_diff.reshape(DEC_BATCH, N_ODD, PAST_LEN, ODD_W)
    hgrn_nw = hgrn_norm_w.astype(F32).reshape(N_EVEN, 1, DV_A)
    sink = swa_sink.astype(F32).reshape(N_EVEN, 1, H_B)
    lam_p = diff_lambda_p.astype(F32)
    assert ATT_TQ == SEQ
    subln = jnp.broadcast_to(diff_subln_w.astype(F32)[:, :, None], (N_ODD, 2 * HD_C, ATT_TQ))

    even_kinds = ("silu_scale", "loggate0", "loggate1", "ident", "silu", "qnorm", "kv")
    odd_kinds = ("qnorm", "qnorm", "knorm", "knorm", "ident", "ident")

    k_swa = v_swa = states = k_diff = v_diff = None
    for li in range(DEPTH):
        j = li // 2
        srcs = ((x_ctx0, 0), (x_lat0, 0)) if li == 0 else ((x, 0), (x, n_ctx_tiles))
        if li % 2 == 0:
            p = None
            for lat, (src, t0) in enumerate(srcs):
                p = _proj_call(src, t0, bool(lat), p, mods, li, norm_w4, w_in_even, j, lbl,
                               _tile_row(swa_qnorm_w[j], PROJ_TN),
                               _tile_row(swa_knorm_w[j], PROJ_TN),
                               cos_t, sin_t, bd, even_kinds, f"proj_even{j}_{lat}")
            mix, states = _hgrn_call(p, hgrn_nw, hgrn_tabs, SEQ, BATCH, 0, None, j, None, states,
                                     f"hgrn_ctx{j}")
            (mix,) = _hgrn_call(p, hgrn_nw, hgrn_tabs, DEC_SEQ, DEC_BATCH, N_CTX_TOK // DEC_SEQ,
                                state_hgrn, j, mix, None, f"hgrn_lat{j}")
            mix, k_swa, v_swa = _swa_ctx_call(p, sink, j, mix, k_swa, v_swa, f"swa_ctx{j}")
            mix = _swa_lat_call(p, ck_swa, cv_swa, sink, j, mix, f"swa_lat{j}")
            wo = w_out_even_b
        else:
            p = None
            for lat, (src, t0) in enumerate(srcs):
                p = _proj_call(src, t0, bool(lat), p, mods, li, norm_w4, w_in_odd, j, lbl,
                               _tile_row(diff_qnorm_w[j], PROJ_TN),
                               _tile_row(diff_knorm_w[j], PROJ_TN),
                               cos_t, sin_t, bd, odd_kinds, f"proj_odd{j}_{lat}")
            lam_init = _lambda_init(li)
            mix, k_diff, v_diff = _diff_ctx_call(p, lam_p, subln, lam_init, j, k_diff, v_diff,
                                                 f"diff_ctx{j}")
            mix = _diff_lat_call(p, ck_diff, cv_diff, lam_p, subln, lam_init, j, mix,
                                 f"diff_lat{j}")
            wo = w_out_odd_b
        last = li == DEPTH - 1
        outs = []
        x_next = None
        for lat, (src, t0) in enumerate(srcs):
            rows = (N_LAT_TOK if lat else N_CTX_TOK) if last else N_TOK
            out_t0 = 0 if last else lat * n_ctx_tiles
            x_next = _mlp_call(src, t0, bool(lat), mix, wo, j, mods, li, norm_w4, w_mlp1, w_mlp2,
                               None if last else x_next, rows, out_t0, f"mlp{li}_{lat}")
            outs.append(x_next)
        x = x_next

    y_prompt = outs[0].reshape(BATCH, SEQ, D_MODEL)
    y_sample = outs[1].reshape(DEC_BATCH, DEC_SEQ, D_MODEL)
    return (y_prompt, y_sample,
            k_swa.reshape(BATCH, N_EVEN, SEQ, KV_B, HD_B),
            v_swa.reshape(BATCH, N_EVEN, SEQ, KV_B, HD_B),
            states,
            k_diff.reshape(BATCH, N_ODD, SEQ, H_C, 2, HD_C),
            v_diff.reshape(BATCH, N_ODD, SEQ, H_C, 2 * HD_C))
```

```python
import functools
import math

import numpy as np
import jax
import jax.numpy as jnp
from jax import lax
from jax.experimental import pallas as pl
from jax.experimental.pallas import tpu as pltpu

F32 = jnp.float32
BF16 = jnp.bfloat16

D_MODEL = 1024
BATCH = 16
SEQ = 256
DEPTH = 4
DEC_BATCH = 4
DEC_SEQ = 1024
PAST_LEN = 512
GRID_W = 64
N_EVEN = (DEPTH + 1) // 2
N_ODD = DEPTH // 2
H_A = 4
DK_A = 128
DV_A = D_MODEL // 2 // H_A
F_A = H_A * DK_A
H_B = 8
KV_B = 2
G_B = H_B // KV_B
HD_B = D_MODEL // 2 // H_B
WINDOW = 128
H_C = 8
HD_C = D_MODEL // (2 * H_C)
D_FF = 4 * D_MODEL
ROPE_BASE = 10000.0
EPS = 1e-6
EVEN_COLS = 3 * F_A + 2 * H_A * DV_A + (H_B + 2 * KV_B) * HD_B
ODD_W = H_C * 2 * HD_C
D_MIX = D_MODEL

N_CTX_TOK = BATCH * SEQ
N_LAT_TOK = DEC_BATCH * DEC_SEQ
N_TOK = N_CTX_TOK + N_LAT_TOK
MOD_ROWS = 8
N_MOD = 6 * D_MODEL

HEAD_GROUP = 64
HGRN_CHUNK = 128
HGRN_LEVELS = 7
HGRN_SPLIT = 3
LOG2_E = math.log2(math.e)
VMEM_LIMIT = 48 * 1024 * 1024

PROJ_TM = 1024
PROJ_TN = 512
PROJ_TILES = 2
PROJ_RC = 128
MLP_TM = 1024
MLP_TK = 1024
MLP_RC = 256
ADA_TN = 1536
ATT_TQ = 256
DIFF_LAT_HEADS = 2
ONES_ROWS = 16


def _silu(x):
    return x * jax.nn.sigmoid(x)


def _nt_dot(a, b):
    return lax.dot_general(a, b, (((1,), (1,)), ((), ())), preferred_element_type=F32)


def _pipelined(n, first, second):
    cur = first(0)
    for i in range(n):
        nxt = first(i + 1) if i + 1 < n else None
        second(i, cur)
        cur = nxt


def _params(sem):
    return pltpu.CompilerParams(dimension_semantics=sem, vmem_limit_bytes=VMEM_LIMIT)


def _aliased_call(kernel, *, grid, in_specs, args, out_specs, out_shape, carried, sem, name,
                  scratch_shapes=()):
    n_in = len(args)
    extra = [buf for buf in carried if buf is not None]
    aliases = {}
    for k, buf in enumerate(carried):
        if buf is not None:
            aliases[n_in + len(aliases)] = k
    n_extra = len(extra)

    def body(*refs):
        kernel(*refs[:n_in], *refs[n_in + n_extra:])

    return pl.pallas_call(
        body,
        grid=grid,
        in_specs=list(in_specs) + [pl.BlockSpec(memory_space=pl.ANY)] * n_extra,
        out_specs=out_specs,
        out_shape=out_shape,
        input_output_aliases=aliases,
        scratch_shapes=list(scratch_shapes),
        compiler_params=_params(sem),
        name=name,
    )(*args, *extra)


def _mods_kernel(c_ref, w_ref, b_ref, o_ref):
    s = _silu(c_ref[...]).astype(BF16)
    o_ref[...] = jnp.dot(s, w_ref[...].astype(BF16), preferred_element_type=F32) + b_ref[...]


def _mods_call(c_all, w_ada, b_ada):
    return pl.pallas_call(
        _mods_kernel,
        grid=(DEPTH, N_MOD // ADA_TN),
        in_specs=[
            pl.BlockSpec((MOD_ROWS, D_MODEL), lambda l, j: (0, 0)),
            pl.BlockSpec((None, D_MODEL, ADA_TN), lambda l, j: (l, 0, j)),
            pl.BlockSpec((None, 1, ADA_TN), lambda l, j: (l, 0, j)),
        ],
        out_specs=pl.BlockSpec((None, MOD_ROWS, ADA_TN), lambda l, j: (l, 0, j)),
        out_shape=jax.ShapeDtypeStruct((DEPTH, MOD_ROWS, N_MOD), F32),
        compiler_params=_params(("parallel", "parallel")),
        name="ada_mods",
    )(c_all, w_ada, b_ada)


def _norm_mod(x, nw, sc, sh):
    ms = jnp.mean(x * x, axis=-1, keepdims=True)
    return (x * lax.rsqrt(ms + EPS) * nw) * (1.0 + sc) + sh


def _group_rms(y, w_t, bd_ref):
    yy = (y * y).astype(BF16)
    bw = bd_ref.shape[0]
    parts = [jnp.dot(yy[:, s:s + bw], bd_ref[...], preferred_element_type=F32)
             for s in range(0, y.shape[1], bw)]
    ss = parts[0] if len(parts) == 1 else jnp.concatenate(parts, axis=1)
    return y * lax.rsqrt(ss * (1.0 / HEAD_GROUP) + EPS) * w_t


def _rope(y, cos, sin):
    n = y.shape[1]
    lane = lax.broadcasted_iota(jnp.int32, y.shape, 1)
    nxt = pltpu.roll(y, n - 1, axis=1)
    prv = pltpu.roll(y, 1, axis=1)
    swapped = jnp.where((lane & 1) == 0, nxt, prv)
    return y * cos + swapped * sin


def _lower_bounds(lbl_ref, jl):
    rows = [lbl_ref[pl.ds(2 * m, 2), :] for m in range(N_EVEN)]
    mx = functools.reduce(jnp.maximum, rows)
    es = [jnp.exp(r - mx) for r in rows]
    den = functools.reduce(lambda a, b: a + b, es)
    sm = [e / den for e in es]
    cs = sm[0]
    for m in range(1, jl + 1):
        cs = cs + sm[m]
    return cs - sm[0]


def _proj_kernel(x_ref, mods_ref, nw_ref, w_ref, wkv_ref, lbl_ref, qn_ref, kn_ref, cos_ref,
                 sin_ref, bd_ref, o_ref, h_scr, w_scr, *, kinds, jl, lat):
    j = pl.program_id(0)
    i = pl.program_id(1)
    tm, tn = PROJ_TM, o_ref.shape[1]
    kvw = 2 * KV_B * HD_B

    @pl.when(j == 0)
    def _():
        for t in range(PROJ_TILES):
            tile = PROJ_TILES * i + t
            row = 1 + tile if lat else 0
            sh = mods_ref[pl.ds(row, 1), 0:D_MODEL]
            sc = mods_ref[pl.ds(row, 1), D_MODEL:2 * D_MODEL]
            h_scr[tile] = _norm_mod(x_ref[t * tm:(t + 1) * tm, :], nw_ref[...], sc,
                                    sh).astype(BF16)

    def finish(kind, y, rows):
        if kind == "silu_scale":
            return _silu(y) * (DK_A ** -0.5)
        if kind in ("loggate0", "loggate1"):
            d = int(kind[-1])
            lb = _lower_bounds(lbl_ref, jl)[d:d + 1, :]
            return jnp.log2(lb + (1.0 - lb) * jax.nn.sigmoid(y))
        if kind == "ident":
            return y
        if kind == "silu":
            return _silu(y)
        if kind in ("qnorm", "knorm"):
            w_t = qn_ref[...] if kind == "qnorm" else kn_ref[...]
            r = _group_rms(y, w_t, bd_ref)
            if lat:
                r = _rope(r, cos_ref[rows, :], sin_ref[rows, :])
            return r * (HEAD_GROUP ** -0.5 * LOG2_E) if kind == "qnorm" else r
        if kind == "kv":
            kn = _group_rms(y, kn_ref[:, 0:kvw], bd_ref)
            if lat:
                kn = _rope(kn, cos_ref[rows, 0:kvw], sin_ref[rows, 0:kvw])
            lane = lax.broadcasted_iota(jnp.int32, y.shape, 1)
            return jnp.where(lane < KV_B * HD_B, kn, y)
        raise ValueError(kind)

    def run(kind):
        @pl.when(i == 0)
        def _():
            if kind == "kv":
                w_scr[:, 0:kvw] = wkv_ref[...].astype(BF16)
            else:
                w_scr[...] = w_ref[...].astype(BF16)

        per_tile = tm // PROJ_RC

        def first(r):
            t, c = divmod(r, per_tile)
            w = w_scr[:, 0:kvw] if kind == "kv" else w_scr[...]
            return jnp.dot(h_scr[PROJ_TILES * i + t, c * PROJ_RC:(c + 1) * PROJ_RC, :], w,
                           preferred_element_type=F32)

        def second(r, y):
            c = r % per_tile
            rows = slice(r * PROJ_RC, (r + 1) * PROJ_RC)
            seq_rows = slice(c * PROJ_RC, (c + 1) * PROJ_RC)
            if kind == "kv":
                o_ref[rows, 0:kvw] = finish(kind, y, seq_rows)
                o_ref[rows, kvw:tn] = jnp.zeros((PROJ_RC, tn - kvw), F32)
            else:
                o_ref[rows, :] = finish(kind, y, seq_rows)

        _pipelined(PROJ_TILES * per_tile, first, second)

    for jj, kind in enumerate(kinds):
        pl.when(j == jj)(functools.partial(run, kind))


def _proj_call(x_src, x_tile0, lat, p_prev, mods, li, norm_w, w, jl, lbl, qn_t, kn_t, cos_t, sin_t,
               bd, kinds, name):
    tm, tn = PROJ_TM, PROJ_TN
    assert tm == DEC_SEQ
    n_ctx_tiles = N_CTX_TOK // tm
    n_tiles = (N_LAT_TOK if lat else N_CTX_TOK) // tm
    tile0 = n_ctx_tiles if lat else 0
    if not lat:
        cos_t = sin_t = jnp.zeros((8, 128), F32)
    n_main = sum(1 for k in kinds if k != "kv")
    n_cols = tn * len(kinds)
    kvw = 2 * KV_B * HD_B
    kv_blk = (n_main * tn) // kvw if "kv" in kinds else 0
    const = lambda j, i: (0, 0)
    n_blk = n_tiles // PROJ_TILES
    blk = PROJ_TILES * tm
    x_blk0, out_blk0 = x_tile0 // PROJ_TILES, tile0 // PROJ_TILES
    assert x_tile0 % PROJ_TILES == 0 and tile0 % PROJ_TILES == 0
    (out,) = _aliased_call(
        functools.partial(_proj_kernel, kinds=kinds, jl=jl, lat=lat),
        grid=(len(kinds), n_blk),
        in_specs=[
            pl.BlockSpec((blk, D_MODEL),
                         lambda j, i: (x_blk0 + jnp.where(j == 0, i, n_blk - 1), 0)),
            pl.BlockSpec((None, MOD_ROWS, N_MOD), lambda j, i: (li, 0, 0)),
            pl.BlockSpec((None, None, 1, D_MODEL), lambda j, i: (li, 0, 0, 0)),
            pl.BlockSpec((None, D_MODEL, tn), lambda j, i: (jl, 0, jnp.minimum(j, n_main - 1))),
            pl.BlockSpec((None, D_MODEL, kvw), lambda j, i: (jl, 0, kv_blk)),
            pl.BlockSpec(lbl.shape, const),
            pl.BlockSpec((1, tn), const),
            pl.BlockSpec((1, tn), const),
            pl.BlockSpec(cos_t.shape, const),
            pl.BlockSpec(sin_t.shape, const),
            pl.BlockSpec(bd.shape, const),
        ],
        args=[x_src, mods, norm_w, w, w, lbl, qn_t, kn_t, cos_t, sin_t, bd],
        out_specs=[pl.BlockSpec((blk, tn), lambda j, i: (out_blk0 + i, j))],
        out_shape=[jax.ShapeDtypeStruct((N_TOK, n_cols), F32)],
        carried=[p_prev], sem=("arbitrary", "arbitrary"), name=name,
        scratch_shapes=[pltpu.VMEM((n_tiles, tm, D_MODEL), BF16),
                        pltpu.VMEM((D_MODEL, tn), BF16)])
    return out


def _hgrn_tables(rev):
    c = HGRN_CHUNK
    t = np.arange(c)
    w = (t[None, :] <= t[:, None]) if not rev else (t[None, :] >= t[:, None])
    ws = np.concatenate([w.astype(np.float32)] * HGRN_SPLIT, axis=1)
    x = t[:, None] ^ t[None, :]
    lv = np.where(x > 0, np.floor(np.log2(np.maximum(x, 1))).astype(np.int32), HGRN_LEVELS)
    causal = (t[None, :] < t[:, None]) if not rev else (t[None, :] > t[:, None])
    lv = np.where(causal | (x == 0), lv, -1).astype(np.int32)
    return jnp.asarray(ws, BF16), jnp.asarray(lv)


def _hgrn_level_exponents(cum, rev):
    c = HGRN_CHUNK
    sub_rows = 8
    c3 = cum.reshape(c // sub_rows, sub_rows, DK_A)
    sub = lax.broadcasted_iota(jnp.int32, c3.shape, 1)
    out = []
    for l in range(1, HGRN_LEVELS):
        hb = 1 << l
        if 2 * hb <= sub_rows:
            r = None
            for b0 in range(0, sub_rows, 2 * hb):
                idx = b0 + (hb if rev else hb - 1)
                rk = c3[:, idx:idx + 1, :]
                r = rk if r is None else jnp.where(sub < b0, r, rk)
            d = c3 - r
            bit = (sub & hb) != 0
            q_role = jnp.logical_not(bit) if rev else bit
            out.append(jnp.where(q_role, d, -d).reshape(c, DK_A))
        else:
            pieces = []
            for b0 in range(0, c, 2 * hb):
                mid = b0 + hb
                ridx = mid if rev else mid - 1
                r = cum[ridx:ridx + 1, :]
                lo = cum[b0:mid]
                hi = cum[mid:b0 + 2 * hb]
                pieces += [lo - r, r - hi] if rev else [r - lo, hi - r]
            out.append(jnp.concatenate(pieces, axis=0))
    return out


def _hgrn_level_operands(l, q, k, f, z, rev, row):
    c = HGRN_CHUNK
    hb = 1 << l
    if hb >= 8:
        zero = jnp.zeros((hb, DK_A), F32)
        qparts, kparts = [], []
        for b0 in range(0, c, 2 * hb):
            lo, hi = slice(b0, b0 + hb), slice(b0 + hb, b0 + 2 * hb)
            if rev:
                qparts += [q[lo] * z[lo], zero]
                kparts += [zero, k[hi] * z[hi]]
            else:
                qparts += [zero, q[hi] * z[hi]]
                kparts += [k[lo] * z[lo], zero]
        return (jnp.concatenate(qparts, axis=0).astype(BF16),
                jnp.concatenate(kparts, axis=0).astype(BF16))
    bit = ((row >> l) & 1) == 1
    q_role = jnp.logical_not(bit) if rev else bit
    ql = jnp.where(q_role, q * (f if l == 0 else z), 0.0).astype(BF16)
    kl = jnp.where(q_role, 0.0, k if l == 0 else k * z).astype(BF16)
    return ql, kl


def _hgrn_chunks(chains):
    c = HGRN_CHUNK
    row = lax.broadcasted_iota(jnp.int32, (c, DK_A), 0)
    cums = []
    for q, g, v, st, w_ref, lv_ref, rev in chains:
        terms = []
        rem = g
        for _ in range(HGRN_SPLIT):
            term = rem.astype(BF16)
            terms.append(term)
            rem = rem - term.astype(F32)
        cums.append(jnp.dot(w_ref[...], jnp.concatenate(terms, axis=0),
                            preferred_element_type=F32))
    work = []
    for (q, g, v, st, w_ref, lv_ref, rev), cum in zip(chains, cums):
        f = jnp.exp2(g)
        k = 1.0 - f
        last = 0 if rev else c - 1
        total_e = cum[last:last + 1, :]
        qd = (q * jnp.exp2(cum)).astype(BF16)
        kd = (k * jnp.exp2(total_e - cum)).astype(BF16)
        o = _nt_dot(qd, st.astype(BF16))
        st_new = (st * jnp.exp2(total_e)
                  + jnp.dot(v.T.astype(BF16), kd, preferred_element_type=F32))
        zs = [None] + [jnp.exp2(e) for e in _hgrn_level_exponents(cum, rev)]
        work.append((k, f, zs, o, st_new))
    accs = [None] * len(chains)
    for l in reversed(range(HGRN_LEVELS)):
        for i, ((q, g, v, st, w_ref, lv_ref, rev), (k, f, zs, o, st_new)) in enumerate(
                zip(chains, work)):
            ql, kl = _hgrn_level_operands(l, q, k, f, zs[l], rev, row)
            a_l = _nt_dot(ql, kl)
            accs[i] = a_l if accs[i] is None else jnp.where(lv_ref[...] == l, a_l, accs[i])
    outs = []
    for (q, g, v, st, w_ref, lv_ref, rev), (k, f, zs, o, st_new), a in zip(chains, work, accs):
        a = jnp.where(lv_ref[...] == HGRN_LEVELS, jnp.sum(q * k, axis=-1, keepdims=True), a)
        outs.append((o + jnp.dot(a.astype(BF16), v.astype(BF16), preferred_element_type=F32),
                     st_new))
    return outs


def _hgrn_kernel(*refs, n_chunks, has_init, emit_state):
    refs = list(refs)
    q_ref, gf_ref, gb_ref, v_ref, sg_ref, nw_ref, wf_ref, wb_ref, lvf_ref, lvb_ref = refs[:10]
    pos = 10
    s0_ref = None
    if has_init:
        s0_ref = refs[pos]
        pos += 1
    o_ref = refs[pos]
    pos += 1
    so_ref = None
    if emit_state:
        so_ref = refs[pos]
        pos += 1
    of_scr, ob_scr, st_scr = refs[pos:pos + 3]

    for d in range(2):
        for h in range(H_A):
            if has_init:
                st_scr[d, h] = s0_ref[d, h].T
            else:
                st_scr[d, h] = jnp.zeros((DV_A, DK_A), F32)

    def body(c, carry):
        chains, dests = [], []
        for h in range(H_A):
            cols = slice(h * DK_A, (h + 1) * DK_A)
            for d, (g_ref, w_ref, lv_ref, scr) in enumerate(
                    ((gf_ref, wf_ref, lvf_ref, of_scr), (gb_ref, wb_ref, lvb_ref, ob_scr))):
                cc = c if d == 0 else n_chunks - 1 - c
                r0 = pl.multiple_of(cc * HGRN_CHUNK, HGRN_CHUNK)
                rows = pl.ds(r0, HGRN_CHUNK)
                chains.append((q_ref[rows, cols], g_ref[rows, cols], v_ref[rows, cols],
                               st_scr[d, h], w_ref, lv_ref, d == 1))
                dests.append((scr, rows, cols, d, h))
        for (o, st), (scr, rows, cols, d, h) in zip(_hgrn_chunks(chains), dests):
            st_scr[d, h] = st
            scr[rows, cols] = o
        return carry

    lax.fori_loop(0, n_chunks, body, 0)
    for h in range(H_A):
        cols = slice(h * DV_A, (h + 1) * DV_A)
        o = of_scr[:, cols] + ob_scr[:, cols]
        y = o * lax.rsqrt(jnp.mean(o * o, axis=-1, keepdims=True) + EPS) * nw_ref[...]
        o_ref[:, cols] = (y * sg_ref[:, cols]).astype(o_ref.dtype)
    if emit_state:
        for d in range(2):
            for h in range(H_A):
                so_ref[d, h] = st_scr[d, h].T


def _hgrn_call(p, nw, tabs, seq_len, n_seq, row_blk0, s0, jl, mix, state_out, name):
    wf, lvf, wb, lvb = tabs
    has_init = s0 is not None
    emit_state = s0 is None
    const = lambda b: (0, 0)
    blk = (seq_len, F_A)
    state_spec = pl.BlockSpec((None, None, 2, H_A, DK_A, DV_A), lambda b: (b, jl, 0, 0, 0, 0))
    in_specs = [pl.BlockSpec(blk, (lambda b, part=part: (row_blk0 + b, part))) for part in range(5)]
    in_specs += [
        pl.BlockSpec((None, 1, DV_A), lambda b: (jl, 0, 0)),
        pl.BlockSpec(wf.shape, const), pl.BlockSpec(wb.shape, const),
        pl.BlockSpec(lvf.shape, const), pl.BlockSpec(lvb.shape, const),
    ]
    args = [p, p, p, p, p, nw, wf, wb, lvf, lvb]
    if has_init:
        in_specs.append(state_spec)
        args.append(s0)
    out_shape = [jax.ShapeDtypeStruct((N_TOK, D_MIX), BF16)]
    out_specs = [pl.BlockSpec((seq_len, H_A * DV_A), lambda b: (row_blk0 + b, 0))]
    carried = [mix]
    if emit_state:
        out_shape.append(jax.ShapeDtypeStruct((BATCH, N_EVEN, 2, H_A, DK_A, DV_A), F32))
        out_specs.append(state_spec)
        carried.append(state_out)
    return _aliased_call(
        functools.partial(_hgrn_kernel, n_chunks=seq_len // HGRN_CHUNK, has_init=has_init,
                          emit_state=emit_state),
        grid=(n_seq,), in_specs=in_specs, args=args, out_specs=out_specs,
        out_shape=out_shape, carried=carried, sem=("parallel",), name=name,
        scratch_shapes=[pltpu.VMEM((seq_len, H_A * DV_A), F32),
                        pltpu.VMEM((seq_len, H_A * DV_A), F32),
                        pltpu.VMEM((2, H_A, DV_A, DK_A), F32)])


def _swa_ctx_kernel(q_ref, kv_ref, sink_ref, o_ref, kc_ref, vc_ref):
    kv = kv_ref[...]
    k32 = kv[:, 0:KV_B * HD_B]
    v32 = kv[:, KV_B * HD_B:2 * KV_B * HD_B]
    kc_ref[...] = k32
    vc_ref[...] = v32
    k = k32.astype(BF16)
    vt32 = v32.T
    vts = [_with_ones_rows(vt32[n * HD_B:(n + 1) * HD_B]) for n in range(KV_B)]
    q = q_ref[...].astype(BF16)
    ksl = [slice((h // G_B) * HD_B, (h // G_B + 1) * HD_B) for h in range(H_B)]
    sts = [_nt_dot(k[:, ksl[h]], q[:, h * HD_B:(h + 1) * HD_B]) for h in range(H_B)]
    ps, sinks = [], []
    for h in range(H_B):
        sink = sink_ref[0:1, h:h + 1] * LOG2_E
        m = jnp.maximum(jnp.max(sts[h], axis=0, keepdims=True), sink)
        ps.append(jnp.exp2(sts[h] - m).astype(BF16))
        sinks.append(jnp.exp2(sink - m))
    outs = []
    for h in range(H_B):
        ota = jnp.dot(vts[h // G_B], ps[h], preferred_element_type=F32)
        outs.append(ota[0:HD_B] / (ota[HD_B:HD_B + 1] + sinks[h]))
    o_ref[...] = jnp.concatenate(outs, axis=0).T.astype(o_ref.dtype)


def _swa_ctx_call(p, sink, jl, mix, k_out, v_out, name):
    qcol = (3 * F_A + 2 * H_A * DV_A) // PROJ_TN
    cache_spec = pl.BlockSpec((None, None, SEQ, KV_B * HD_B), lambda b: (b, jl, 0, 0))
    cache_shape = jax.ShapeDtypeStruct((BATCH, N_EVEN, SEQ, KV_B * HD_B), F32)
    return _aliased_call(
        _swa_ctx_kernel,
        grid=(BATCH,),
        in_specs=[
            pl.BlockSpec((SEQ, H_B * HD_B), lambda b: (b, qcol)),
            pl.BlockSpec((SEQ, PROJ_TN), lambda b: (b, qcol + 1)),
            pl.BlockSpec((None, 1, H_B), lambda b: (jl, 0, 0)),
        ],
        args=[p, p, sink],
        out_specs=[pl.BlockSpec((SEQ, H_B * HD_B), lambda b: (b, 1)), cache_spec, cache_spec],
        out_shape=[jax.ShapeDtypeStruct((N_TOK, D_MIX), BF16), cache_shape, cache_shape],
        carried=[mix, k_out, v_out], sem=("parallel",), name=name)


def _swa_lat_kernel(q_ref, kv_ref, ck_ref, cv_ref, sink_ref, o_ref):
    qi = pl.program_id(1)
    tq = q_ref.shape[0]
    span = tq + 2 * WINDOW
    ws = pl.multiple_of(jnp.clip(qi * tq - WINDOW, 0, DEC_SEQ - span), WINDOW)
    kvw = kv_ref[pl.ds(ws, span), :]
    kw = kvw[:, 0:KV_B * HD_B].astype(BF16)
    vwt = kvw[:, KV_B * HD_B:2 * KV_B * HD_B].T.astype(BF16)
    kc = ck_ref[...].astype(BF16)
    vct = cv_ref[...].T.astype(BF16)
    q = q_ref[...].astype(BF16)
    t_k = ws + lax.broadcasted_iota(jnp.int32, (span, tq), 0)
    t_q = qi * tq + lax.broadcasted_iota(jnp.int32, (span, tq), 1)
    valid = jnp.abs(t_q - t_k) <= WINDOW
    outs = []

    def first(h):
        qh = q[:, h * HD_B:(h + 1) * HD_B]
        ksl = slice((h // G_B) * HD_B, (h // G_B + 1) * HD_B)
        return _nt_dot(kw[:, ksl], qh), _nt_dot(kc[:, ksl], qh)

    def second(h, scores):
        ksl = slice((h // G_B) * HD_B, (h // G_B + 1) * HD_B)
        s_w = jnp.where(valid, scores[0], -jnp.inf)
        s_c = scores[1]
        sink = sink_ref[0:1, h:h + 1] * LOG2_E
        m = jnp.maximum(jnp.maximum(jnp.max(s_w, axis=0, keepdims=True),
                                    jnp.max(s_c, axis=0, keepdims=True)), sink)
        p_w = jnp.exp2(s_w - m)
        p_c = jnp.exp2(s_c - m)
        den = (jnp.sum(p_w, axis=0, keepdims=True) + jnp.sum(p_c, axis=0, keepdims=True)
               + jnp.exp2(sink - m))
        ot = (jnp.dot(vwt[ksl, :], p_w.astype(BF16), preferred_element_type=F32)
              + jnp.dot(vct[ksl, :], p_c.astype(BF16), preferred_element_type=F32))
        outs.append(ot / den)

    _pipelined(H_B, first, second)
    o_ref[...] = jnp.concatenate(outs, axis=0).T.astype(o_ref.dtype)


def _swa_lat_call(p, ck, cv, sink, jl, mix, name):
    tq = ATT_TQ
    qcol = (3 * F_A + 2 * H_A * DV_A) // PROJ_TN
    nq = DEC_SEQ // tq
    q_blk0 = N_CTX_TOK // tq
    s_blk0 = N_CTX_TOK // DEC_SEQ
    cache_spec = pl.BlockSpec((None, None, PAST_LEN, KV_B * HD_B), lambda b, i: (b, jl, 0, 0))
    (out,) = _aliased_call(
        _swa_lat_kernel,
        grid=(DEC_BATCH, nq),
        in_specs=[
            pl.BlockSpec((tq, H_B * HD_B), lambda b, i: (q_blk0 + b * nq + i, qcol)),
            pl.BlockSpec((DEC_SEQ, PROJ_TN), lambda b, i: (s_blk0 + b, qcol + 1)),
            cache_spec, cache_spec,
            pl.BlockSpec((None, 1, H_B), lambda b, i: (jl, 0, 0)),
        ],
        args=[p, p, ck, cv, sink],
        out_specs=[pl.BlockSpec((tq, H_B * HD_B), lambda b, i: (q_blk0 + b * nq + i, 1))],
        out_shape=[jax.ShapeDtypeStruct((N_TOK, D_MIX), BF16)],
        carried=[mix], sem=("parallel", "parallel"), name=name)
    return out


def _diff_lambda(lp_ref, lam_init):
    lp = lp_ref[...]
    a = jnp.sum(lp[0:1] * lp[1:2], axis=-1, keepdims=True)
    b = jnp.sum(lp[2:3] * lp[3:4], axis=-1, keepdims=True)
    return jnp.exp(a) - jnp.exp(b) + lam_init


def _diff_scores(q, k_parts):
    return [[_nt_dot(kp[:, c * HD_C:(c + 1) * HD_C], q[:, c * HD_C:(c + 1) * HD_C])
             for kp in k_parts] for c in range(2)]


def _diff_finish(scores, vt_parts, lam, lam_init, sw_t):
    hw = 2 * HD_C
    comps = []
    for c in range(2):
        ss = scores[c]
        m = functools.reduce(jnp.maximum, [jnp.max(s, axis=0, keepdims=True) for s in ss])
        ota = functools.reduce(
            lambda a, b: a + b,
            [jnp.dot(vt, jnp.exp2(s - m).astype(BF16), preferred_element_type=F32)
             for s, vt in zip(ss, vt_parts)])
        comps.append(ota[0:hw] / ota[hw:hw + 1])
    ot = comps[0] - lam * comps[1]
    yt = ot * lax.rsqrt(jnp.mean(ot * ot, axis=0, keepdims=True) + EPS) * sw_t
    return (yt * (1.0 - lam_init)).T


def _with_ones_rows(vt):
    return jnp.concatenate([vt, jnp.ones((ONES_ROWS, vt.shape[1]), F32)], axis=0).astype(BF16)


def _diff_ctx_kernel(q_ref, k_ref, v_ref, lp_ref, sw_ref, o_ref, kc_ref, vc_ref, *, lam_init):
    lam = _diff_lambda(lp_ref, lam_init)
    hw = 2 * HD_C
    vts = {}

    def first(h):
        sl = slice(h * hw, (h + 1) * hw)
        k32 = k_ref[:, sl]
        v32 = v_ref[:, sl]
        kc_ref[:, sl] = k32
        vc_ref[:, sl] = v32
        vts[h] = _with_ones_rows(v32.T)
        return _diff_scores(q_ref[:, sl].astype(BF16), [k32.astype(BF16)])

    def second(h, scores):
        y = _diff_finish(scores, [vts.pop(h)], lam, lam_init, sw_ref[...])
        o_ref[:, h * hw:(h + 1) * hw] = y.astype(o_ref.dtype)

    _pipelined(H_C, first, second)


def _diff_ctx_call(p, lp, sw, lam_init, jl, k_out, v_out, name):
    cache_spec = pl.BlockSpec((None, None, SEQ, ODD_W), lambda b: (b, jl, 0, 0))
    cache_shape = jax.ShapeDtypeStruct((BATCH, N_ODD, SEQ, ODD_W), F32)
    return _aliased_call(
        functools.partial(_diff_ctx_kernel, lam_init=lam_init),
        grid=(BATCH,),
        in_specs=[
            pl.BlockSpec((SEQ, ODD_W), lambda b: (b, 0)),
            pl.BlockSpec((SEQ, ODD_W), lambda b: (b, 1)),
            pl.BlockSpec((SEQ, ODD_W), lambda b: (b, 2)),
            pl.BlockSpec((None, 4, HD_C), lambda b: (jl, 0, 0)),
            pl.BlockSpec((None, 2 * HD_C, ATT_TQ), lambda b: (jl, 0, 0)),
        ],
        args=[p, p, p, lp, sw],
        out_specs=[pl.BlockSpec((SEQ, ODD_W), lambda b: (b, 0)), cache_spec, cache_spec],
        out_shape=[jax.ShapeDtypeStruct((N_TOK, D_MIX), BF16), cache_shape, cache_shape],
        carried=[None, k_out, v_out], sem=("parallel",), name=name)


def _diff_lat_kernel(q_ref, k_ref, v_ref, ck_ref, cv_ref, lp_ref, sw_ref, o_ref, *, lam_init):
    lam = _diff_lambda(lp_ref, lam_init)
    hw = 2 * HD_C
    tq = ATT_TQ
    nq = q_ref.shape[0] // tq
    heads = []
    for g in range(q_ref.shape[1] // hw):
        sl = slice(g * hw, (g + 1) * hw)
        heads.append(([k_ref[:, sl].astype(BF16), ck_ref[:, sl].astype(BF16)],
                      [_with_ones_rows(v_ref[:, sl].T), _with_ones_rows(cv_ref[:, sl].T)]))

    def first(n):
        g, i = divmod(n, nq)
        q = q_ref[i * tq:(i + 1) * tq, g * hw:(g + 1) * hw].astype(BF16)
        return _diff_scores(q, heads[g][0])

    def second(n, scores):
        g, i = divmod(n, nq)
        y = _diff_finish(scores, heads[g][1], lam, lam_init, sw_ref[...])
        o_ref[i * tq:(i + 1) * tq, g * hw:(g + 1) * hw] = y.astype(o_ref.dtype)

    _pipelined(len(heads) * nq, first, second)


def _diff_lat_call(p, ck, cv, lp, sw, lam_init, jl, mix, name):
    hw = 2 * HD_C
    gw = DIFF_LAT_HEADS * hw
    n_g = H_C // DIFF_LAT_HEADS
    s_blk0 = N_CTX_TOK // DEC_SEQ
    cache_spec = pl.BlockSpec((None, None, PAST_LEN, gw), lambda b, h: (b, jl, 0, h))
    (out,) = _aliased_call(
        functools.partial(_diff_lat_kernel, lam_init=lam_init),
        grid=(DEC_BATCH, n_g),
        in_specs=[
            pl.BlockSpec((DEC_SEQ, gw), lambda b, h: (s_blk0 + b, h)),
            pl.BlockSpec((DEC_SEQ, gw), lambda b, h: (s_blk0 + b, n_g + h)),
            pl.BlockSpec((DEC_SEQ, gw), lambda b, h: (s_blk0 + b, 2 * n_g + h)),
            cache_spec, cache_spec,
            pl.BlockSpec((None, 4, HD_C), lambda b, h: (jl, 0, 0)),
            pl.BlockSpec((None, hw, ATT_TQ), lambda b, h: (jl, 0, 0)),
        ],
        args=[p, p, p, ck, cv, lp, sw],
        out_specs=[pl.BlockSpec((DEC_SEQ, gw), lambda b, h: (s_blk0 + b, h))],
        out_shape=[jax.ShapeDtypeStruct((N_TOK, D_MIX), BF16)],
        carried=[mix], sem=("parallel", "parallel"), name=name)
    return out


def _mlp_kernel(x_ref, mix_ref, wo_ref, mods_ref, nw_ref, w1_ref, w2_ref, o_ref,
                h_scr, *, lat):
    i = pl.program_id(0)
    k = pl.program_id(1)
    row = 1 + i if lat else 0

    def mod(a):
        return mods_ref[pl.ds(row, 1), a * D_MODEL:(a + 1) * D_MODEL]

    @pl.when(k == 0)
    def _():
        def first(r):
            return jnp.dot(mix_ref[r * MLP_RC:(r + 1) * MLP_RC, :], wo_ref[...],
                           preferred_element_type=F32)

        def second(r, y):
            rows = slice(r * MLP_RC, (r + 1) * MLP_RC)
            x1 = x_ref[rows, :] + mod(2) * y
            o_ref[rows, :] = x1
            h_scr[rows, :] = _norm_mod(x1, nw_ref[...], mod(4), mod(3)).astype(BF16)

        _pipelined(x_ref.shape[0] // MLP_RC, first, second)

    u = jnp.dot(h_scr[...], w1_ref[...].astype(BF16), preferred_element_type=F32)
    u = jnp.square(jnp.maximum(u, 0.0)).astype(BF16)
    o_ref[...] += mod(5) * jnp.dot(u, w2_ref[...].astype(BF16), preferred_element_type=F32)


def _mlp_call(x_src, x_tile0, lat, mix, wo, jl, mods, li, norm_w, w1, w2, out_prev, out_rows,
              out_tile0, name):
    tm, tk = MLP_TM, MLP_TK
    n_tiles = (N_LAT_TOK if lat else N_CTX_TOK) // tm
    mix_tile0 = N_CTX_TOK // tm if lat else 0
    (out,) = _aliased_call(
        functools.partial(_mlp_kernel, lat=lat),
        grid=(n_tiles, D_FF // tk),
        in_specs=[
            pl.BlockSpec((tm, D_MODEL), lambda i, k: (x_tile0 + i, 0)),
            pl.BlockSpec((tm, D_MIX), lambda i, k: (mix_tile0 + i, 0)),
            pl.BlockSpec((None, D_MIX, D_MODEL), lambda i, k: (jl, 0, 0)),
            pl.BlockSpec((None, MOD_ROWS, N_MOD), lambda i, k: (li, 0, 0)),
            pl.BlockSpec((None, None, 1, D_MODEL), lambda i, k: (li, 1, 0, 0)),
            pl.BlockSpec((None, D_MODEL, tk), lambda i, k: (li, 0, k)),
            pl.BlockSpec((None, tk, D_MODEL), lambda i, k: (li, k, 0)),
        ],
        args=[x_src, mix, wo, mods, norm_w, w1, w2],
        out_specs=[pl.BlockSpec((tm, D_MODEL), lambda i, k: (out_tile0 + i, 0))],
        out_shape=[jax.ShapeDtypeStruct((out_rows, D_MODEL), F32)],
        carried=[out_prev], sem=("parallel", "arbitrary"), name=name,
        scratch_shapes=[pltpu.VMEM((tm, D_MODEL), BF16)])
    return out


def _rope_tables(width):
    t = np.arange(DEC_SEQ)
    half = HEAD_GROUP // 2
    inv = ROPE_BASE ** (-np.arange(0, half, 2, dtype=np.float64) / half)
    ang = np.concatenate([(t // GRID_W)[:, None] * inv, (t % GRID_W)[:, None] * inv], axis=-1)
    cos = np.repeat(np.cos(ang), 2, axis=-1)
    sin = np.repeat(np.sin(ang), 2, axis=-1)
    sign = np.tile(np.array([-1.0, 1.0]), HEAD_GROUP // 2)
    reps = width // HEAD_GROUP
    return (jnp.asarray(np.tile(cos, (1, reps)), F32),
            jnp.asarray(np.tile(sin * sign, (1, reps)), F32))


def _block_diag_ones(n):
    g = np.arange(n) // HEAD_GROUP
    return jnp.asarray(g[:, None] == g[None, :], BF16)


def _tile_row(w, width):
    return jnp.tile(w.astype(F32), width // w.shape[0])[None, :]


def _lambda_init(li):
    return 0.8 - 0.6 * math.exp(-0.3 * li)


def kernel(x_prompt, x_sample, cache_k_swa, cache_v_swa, state_hgrn, cache_k_diff, cache_v_diff, c, c_ctx, norm_w, w_ada, b_ada, w_in_even, w_out_even, hgrn_lb_logits, hgrn_norm_w, swa_qnorm_w, swa_knorm_w, swa_sink, w_in_odd, w_out_odd, diff_qnorm_w, diff_knorm_w, diff_lambda_p, diff_subln_w, w_mlp1, w_mlp2):
    assert PROJ_TM == MLP_TM
    n_ctx_tiles = N_CTX_TOK // PROJ_TM
    x = None
    x_ctx0 = x_prompt.reshape(N_CTX_TOK, D_MODEL)
    x_lat0 = x_sample.reshape(N_LAT_TOK, D_MODEL)
    c_all = jnp.concatenate(
        [c_ctx[None, :], c, jnp.zeros((MOD_ROWS - 1 - DEC_BATCH, D_MODEL), F32)], axis=0)
    mods = _mods_call(c_all, w_ada, b_ada.reshape(DEPTH, 1, N_MOD))

    cos_t, sin_t = _rope_tables(PROJ_TN)
    bd = _block_diag_ones(256)
    tabs_f = _hgrn_tables(False)
    tabs_b = _hgrn_tables(True)
    hgrn_tabs = (tabs_f[0], tabs_f[1], tabs_b[0], tabs_b[1])
    lbl = hgrn_lb_logits.astype(F32).reshape(N_EVEN * 2, F_A)
    norm_w4 = norm_w.astype(F32).reshape(DEPTH, 2, 1, D_MODEL)

    w_out_even_b = w_out_even.astype(BF16)
    w_out_odd_b = w_out_odd.astype(BF16)

    ck_swa = cache_k_swa.reshape(DEC_BATCH, N_EVEN, PAST_LEN, KV_B * HD_B)
    cv_swa = cache_v_swa.reshape(DEC_BATCH, N_EVEN, PAST_LEN, KV_B * HD_B)
    ck_diff = cache_k_diff.reshape(DEC_BATCH, N_ODD, PAST_LEN, ODD_W)
    cv_diff = cache_v_diff.reshape(DEC_BATCH, N_ODD, PAST_LEN, ODD_W)
    hgrn_nw = hgrn_norm_w.astype(F32).reshape(N_EVEN, 1, DV_A)
    sink = swa_sink.astype(F32).reshape(N_EVEN, 1, H_B)
    lam_p = diff_lambda_p.astype(F32)
    assert ATT_TQ == SEQ
    subln = jnp.broadcast_to(diff_subln_w.astype(F32)[:, :, None], (N_ODD, 2 * HD_C, ATT_TQ))

    even_kinds = ("silu_scale", "loggate0", "loggate1", "ident", "silu", "qnorm", "kv")
    odd_kinds = ("qnorm", "qnorm", "knorm", "knorm", "ident", "ident")

    k_swa = v_swa = states = k_diff = v_diff = None
    for li in range(DEPTH):
        j = li // 2
        srcs = ((x_ctx0, 0), (x_lat0, 0)) if li == 0 else ((x, 0), (x, n_ctx_tiles))
        if li % 2 == 0:
            p = None
            for lat, (src, t0) in enumerate(srcs):
                p = _proj_call(src, t0, bool(lat), p, mods, li, norm_w4, w_in_even, j, lbl,
                               _tile_row(swa_qnorm_w[j], PROJ_TN),
                               _tile_row(swa_knorm_w[j], PROJ_TN),
                               cos_t, sin_t, bd, even_kinds, f"proj_even{j}_{lat}")
            mix, states = _hgrn_call(p, hgrn_nw, hgrn_tabs, SEQ, BATCH, 0, None, j, None, states,
                                     f"hgrn_ctx{j}")
            (mix,) = _hgrn_call(p, hgrn_nw, hgrn_tabs, DEC_SEQ, DEC_BATCH, N_CTX_TOK // DEC_SEQ,
                                state_hgrn, j, mix, None, f"hgrn_lat{j}")
            mix, k_swa, v_swa = _swa_ctx_call(p, sink, j, mix, k_swa, v_swa, f"swa_ctx{j}")
            mix = _swa_lat_call(p, ck_swa, cv_swa, sink, j, mix, f"swa_lat{j}")
            wo = w_out_even_b
        else:
            p = None
            for lat, (src, t0) in enumerate(srcs):
                p = _proj_call(src, t0, bool(lat), p, mods, li, norm_w4, w_in_odd, j, lbl,
                               _tile_row(diff_qnorm_w[j], PROJ_TN),
                               _tile_row(diff_knorm_w[j], PROJ_TN),
                               cos_t, sin_t, bd, odd_kinds, f"proj_odd{j}_{lat}")
            lam_init = _lambda_init(li)
            mix, k_diff, v_diff = _diff_ctx_call(p, lam_p, subln, lam_init, j, k_diff, v_diff,
                                                 f"diff_ctx{j}")
            mix = _diff_lat_call(p, ck_diff, cv_diff, lam_p, subln, lam_init, j, mix,
                                 f"diff_lat{j}")
            wo = w_out_odd_b
        last = li == DEPTH - 1
        outs = []
        x_next = None
        for lat, (src, t0) in enumerate(srcs):
            rows = (N_LAT_TOK if lat else N_CTX_TOK) if last else N_TOK
            out_t0 = 0 if last else lat * n_ctx_tiles
            x_next = _mlp_call(src, t0, bool(lat), mix, wo, j, mods, li, norm_w4, w_mlp1, w_mlp2,
                               None if last else x_next, rows, out_t0, f"mlp{li}_{lat}")
            outs.append(x_next)
        x = x_next

    y_prompt = outs[0].reshape(BATCH, SEQ, D_MODEL)
    y_sample = outs[1].reshape(DEC_BATCH, DEC_SEQ, D_MODEL)
    return (y_prompt, y_sample,
            k_swa.reshape(BATCH, N_EVEN, SEQ, KV_B, HD_B),
            v_swa.reshape(BATCH, N_EVEN, SEQ, KV_B, HD_B),
            states,
            k_diff.reshape(BATCH, N_ODD, SEQ, H_C, 2, HD_C),
            v_diff.reshape(BATCH, N_ODD, SEQ, H_C, 2 * HD_C))
```

```python
import functools
import math

import numpy as np
import jax
import jax.numpy as jnp
from jax import lax
from jax.experimental import pallas as pl
from jax.experimental.pallas import tpu as pltpu

F32 = jnp.float32
BF16 = jnp.bfloat16

D_MODEL = 1024
BATCH = 16
SEQ = 256
DEPTH = 4
DEC_BATCH = 4
DEC_SEQ = 1024
PAST_LEN = 512
GRID_W = 64
N_EVEN = (DEPTH + 1) // 2
N_ODD = DEPTH // 2
H_A = 4
DK_A = 128
DV_A = D_MODEL // 2 // H_A
F_A = H_A * DK_A
H_B = 8
KV_B = 2
G_B = H_B // KV_B
HD_B = D_MODEL // 2 // H_B
WINDOW = 128
H_C = 8
HD_C = D_MODEL // (2 * H_C)
D_FF = 4 * D_MODEL
ROPE_BASE = 10000.0
EPS = 1e-6
EVEN_COLS = 3 * F_A + 2 * H_A * DV_A + (H_B + 2 * KV_B) * HD_B
ODD_W = H_C * 2 * HD_C
D_MIX = D_MODEL

N_CTX_TOK = BATCH * SEQ
N_LAT_TOK = DEC_BATCH * DEC_SEQ
N_TOK = N_CTX_TOK + N_LAT_TOK
MOD_ROWS = 8
N_MOD = 6 * D_MODEL

HEAD_GROUP = 64
HGRN_CHUNK = 128
HGRN_LEVELS = 7
HGRN_SPLIT = 3
LOG2_E = math.log2(math.e)
VMEM_LIMIT = 48 * 1024 * 1024

PROJ_TM = 1024
PROJ_TN = 512
ROPE_TW = 128
PROJ_TILES = 2
PROJ_RC = 128
MLP_TM = 1024
MLP_TK = 1024
MLP_RC = 256
ADA_TN = 1536
ATT_TQ = 256
DIFF_LAT_HEADS = 2
ONES_ROWS = 16


def _silu(x):
    return x * jax.nn.sigmoid(x)


def _nt_dot(a, b):
    return lax.dot_general(a, b, (((1,), (1,)), ((), ())), preferred_element_type=F32)


def _pipelined(n, first, second):
    cur = first(0)
    for i in range(n):
        nxt = first(i + 1) if i + 1 < n else None
        second(i, cur)
        cur = nxt


def _params(sem):
    return pltpu.CompilerParams(dimension_semantics=sem, vmem_limit_bytes=VMEM_LIMIT)


def _aliased_call(kernel, *, grid, in_specs, args, out_specs, out_shape, carried, sem, name,
                  scratch_shapes=()):
    n_in = len(args)
    extra = [buf for buf in carried if buf is not None]
    aliases = {}
    for k, buf in enumerate(carried):
        if buf is not None:
            aliases[n_in + len(aliases)] = k
    n_extra = len(extra)

    def body(*refs):
        kernel(*refs[:n_in], *refs[n_in + n_extra:])

    return pl.pallas_call(
        body,
        grid=grid,
        in_specs=list(in_specs) + [pl.BlockSpec(memory_space=pl.ANY)] * n_extra,
        out_specs=out_specs,
        out_shape=out_shape,
        input_output_aliases=aliases,
        scratch_shapes=list(scratch_shapes),
        compiler_params=_params(sem),
        name=name,
    )(*args, *extra)


def _mods_kernel(c_ref, w_ref, b_ref, o_ref):
    s = _silu(c_ref[...]).astype(BF16)
    o_ref[...] = jnp.dot(s, w_ref[...].astype(BF16), preferred_element_type=F32) + b_ref[...]


def _mods_call(c_all, w_ada, b_ada):
    return pl.pallas_call(
        _mods_kernel,
        grid=(DEPTH, N_MOD // ADA_TN),
        in_specs=[
            pl.BlockSpec((MOD_ROWS, D_MODEL), lambda l, j: (0, 0)),
            pl.BlockSpec((None, D_MODEL, ADA_TN), lambda l, j: (l, 0, j)),
            pl.BlockSpec((None, 1, ADA_TN), lambda l, j: (l, 0, j)),
        ],
        out_specs=pl.BlockSpec((None, MOD_ROWS, ADA_TN), lambda l, j: (l, 0, j)),
        out_shape=jax.ShapeDtypeStruct((DEPTH, MOD_ROWS, N_MOD), F32),
        compiler_params=_params(("parallel", "parallel")),
        name="ada_mods",
    )(c_all, w_ada, b_ada)


def _norm_mod(x, nw, sc, sh):
    ms = jnp.mean(x * x, axis=-1, keepdims=True)
    return (x * lax.rsqrt(ms + EPS) * nw) * (1.0 + sc) + sh


def _group_rms(y, w_t, bd_ref):
    yy = (y * y).astype(BF16)
    bw = bd_ref.shape[0]
    parts = [jnp.dot(yy[:, s:s + bw], bd_ref[...], preferred_element_type=F32)
             for s in range(0, y.shape[1], bw)]
    ss = parts[0] if len(parts) == 1 else jnp.concatenate(parts, axis=1)
    return y * lax.rsqrt(ss * (1.0 / HEAD_GROUP) + EPS) * w_t


def _rope(y, cos, sin):
    n = y.shape[1]
    lane = lax.broadcasted_iota(jnp.int32, y.shape, 1)
    nxt = pltpu.roll(y, n - 1, axis=1)
    prv = pltpu.roll(y, 1, axis=1)
    swapped = jnp.where((lane & 1) == 0, nxt, prv)
    return y * cos + swapped * sin


def _lower_bounds(lbl_ref, jl):
    rows = [lbl_ref[pl.ds(2 * m, 2), :] for m in range(N_EVEN)]
    mx = functools.reduce(jnp.maximum, rows)
    es = [jnp.exp(r - mx) for r in rows]
    den = functools.reduce(lambda a, b: a + b, es)
    sm = [e / den for e in es]
    cs = sm[0]
    for m in range(1, jl + 1):
        cs = cs + sm[m]
    return cs - sm[0]


def _proj_kernel(x_ref, mods_ref, nw_ref, w_ref, wkv_ref, lbl_ref, qn_ref, kn_ref, cos_ref,
                 sin_ref, bd_ref, o_ref, h_scr, w_scr, *, kinds, jl, lat):
    j = pl.program_id(0)
    i = pl.program_id(1)
    tm, tn = PROJ_TM, o_ref.shape[1]
    kvw = 2 * KV_B * HD_B

    @pl.when(j == 0)
    def _():
        for t in range(PROJ_TILES):
            tile = PROJ_TILES * i + t
            row = 1 + tile if lat else 0
            sh = mods_ref[pl.ds(row, 1), 0:D_MODEL]
            sc = mods_ref[pl.ds(row, 1), D_MODEL:2 * D_MODEL]
            h_scr[tile] = _norm_mod(x_ref[t * tm:(t + 1) * tm, :], nw_ref[...], sc,
                                    sh).astype(BF16)

    def finish(kind, y, rows):
        if kind == "silu_scale":
            return _silu(y) * (DK_A ** -0.5)
        if kind in ("loggate0", "loggate1"):
            d = int(kind[-1])
            lb = _lower_bounds(lbl_ref, jl)[d:d + 1, :]
            return jnp.log2(lb + (1.0 - lb) * jax.nn.sigmoid(y))
        if kind == "ident":
            return y
        if kind == "silu":
            return _silu(y)
        if kind in ("qnorm", "knorm"):
            w_t = qn_ref[...] if kind == "qnorm" else kn_ref[...]
            r = _group_rms(y, w_t, bd_ref)
            if lat:
                r = _rope(r, jnp.tile(cos_ref[rows, :], (1, tn // ROPE_TW)),
                          jnp.tile(sin_ref[rows, :], (1, tn // ROPE_TW)))
            return r * (HEAD_GROUP ** -0.5 * LOG2_E) if kind == "qnorm" else r
        if kind == "kv":
            kn = _group_rms(y, kn_ref[:, 0:kvw], bd_ref)
            if lat:
                kn = _rope(kn, jnp.tile(cos_ref[rows, :], (1, kvw // ROPE_TW)),
                           jnp.tile(sin_ref[rows, :], (1, kvw // ROPE_TW)))
            lane = lax.broadcasted_iota(jnp.int32, y.shape, 1)
            return jnp.where(lane < KV_B * HD_B, kn, y)
        raise ValueError(kind)

    def run(kind):
        @pl.when(i == 0)
        def _():
            if kind == "kv":
                w_scr[:, 0:kvw] = wkv_ref[...].astype(BF16)
            else:
                w_scr[...] = w_ref[...].astype(BF16)

        per_tile = tm // PROJ_RC

        def first(r):
            t, c = divmod(r, per_tile)
            w = w_scr[:, 0:kvw] if kind == "kv" else w_scr[...]
            return jnp.dot(h_scr[PROJ_TILES * i + t, c * PROJ_RC:(c + 1) * PROJ_RC, :], w,
                           preferred_element_type=F32)

        def second(r, y):
            c = r % per_tile
            rows = slice(r * PROJ_RC, (r + 1) * PROJ_RC)
            seq_rows = slice(c * PROJ_RC, (c + 1) * PROJ_RC)
            if kind == "kv":
                o_ref[rows, 0:kvw] = finish(kind, y, seq_rows)
                o_ref[rows, kvw:tn] = jnp.zeros((PROJ_RC, tn - kvw), F32)
            else:
                o_ref[rows, :] = finish(kind, y, seq_rows)

        _pipelined(PROJ_TILES * per_tile, first, second)

    for jj, kind in enumerate(kinds):
        pl.when(j == jj)(functools.partial(run, kind))


def _proj_call(x_src, x_tile0, lat, p_prev, mods, li, norm_w, w, jl, lbl, qn_t, kn_t, cos_t, sin_t,
               bd, kinds, name):
    tm, tn = PROJ_TM, PROJ_TN
    assert tm == DEC_SEQ
    n_ctx_tiles = N_CTX_TOK // tm
    n_tiles = (N_LAT_TOK if lat else N_CTX_TOK) // tm
    tile0 = n_ctx_tiles if lat else 0
    if not lat:
        cos_t = sin_t = jnp.zeros((8, 128), F32)
    n_main = sum(1 for k in kinds if k != "kv")
    n_cols = tn * len(kinds)
    kvw = 2 * KV_B * HD_B
    kv_blk = (n_main * tn) // kvw if "kv" in kinds else 0
    const = lambda j, i: (0, 0)
    n_blk = n_tiles // PROJ_TILES
    blk = PROJ_TILES * tm
    x_blk0, out_blk0 = x_tile0 // PROJ_TILES, tile0 // PROJ_TILES
    assert x_tile0 % PROJ_TILES == 0 and tile0 % PROJ_TILES == 0
    (out,) = _aliased_call(
        functools.partial(_proj_kernel, kinds=kinds, jl=jl, lat=lat),
        grid=(len(kinds), n_blk),
        in_specs=[
            pl.BlockSpec((blk, D_MODEL),
                         lambda j, i: (x_blk0 + jnp.where(j == 0, i, n_blk - 1), 0)),
            pl.BlockSpec((None, MOD_ROWS, N_MOD), lambda j, i: (li, 0, 0)),
            pl.BlockSpec((None, None, 1, D_MODEL), lambda j, i: (li, 0, 0, 0)),
            pl.BlockSpec((None, D_MODEL, tn), lambda j, i: (jl, 0, jnp.minimum(j, n_main - 1))),
            pl.BlockSpec((None, D_MODEL, kvw), lambda j, i: (jl, 0, kv_blk)),
            pl.BlockSpec(lbl.shape, const),
            pl.BlockSpec((1, tn), const),
            pl.BlockSpec((1, tn), const),
            pl.BlockSpec(cos_t.shape, const),
            pl.BlockSpec(sin_t.shape, const),
            pl.BlockSpec(bd.shape, const),
        ],
        args=[x_src, mods, norm_w, w, w, lbl, qn_t, kn_t, cos_t, sin_t, bd],
        out_specs=[pl.BlockSpec((blk, tn), lambda j, i: (out_blk0 + i, j))],
        out_shape=[jax.ShapeDtypeStruct((N_TOK, n_cols), F32)],
        carried=[p_prev], sem=("arbitrary", "arbitrary"), name=name,
        scratch_shapes=[pltpu.VMEM((n_tiles, tm, D_MODEL), BF16),
                        pltpu.VMEM((D_MODEL, tn), BF16)])
    return out


def _hgrn_tables(rev):
    c = HGRN_CHUNK
    t = np.arange(c)
    w = (t[None, :] <= t[:, None]) if not rev else (t[None, :] >= t[:, None])
    ws = np.concatenate([w.astype(np.float32)] * HGRN_SPLIT, axis=1)
    x = t[:, None] ^ t[None, :]
    lv = np.where(x > 0, np.floor(np.log2(np.maximum(x, 1))).astype(np.int32), HGRN_LEVELS)
    causal = (t[None, :] < t[:, None]) if not rev else (t[None, :] > t[:, None])
    lv = np.where(causal | (x == 0), lv, -1).astype(np.int32)
    return jnp.asarray(ws, BF16), jnp.asarray(lv)


def _hgrn_level_exponents(cum, rev):
    c = HGRN_CHUNK
    sub_rows = 8
    c3 = cum.reshape(c // sub_rows, sub_rows, DK_A)
    sub = lax.broadcasted_iota(jnp.int32, c3.shape, 1)
    out = []
    for l in range(1, HGRN_LEVELS):
        hb = 1 << l
        if 2 * hb <= sub_rows:
            r = None
            for b0 in range(0, sub_rows, 2 * hb):
                idx = b0 + (hb if rev else hb - 1)
                rk = c3[:, idx:idx + 1, :]
                r = rk if r is None else jnp.where(sub < b0, r, rk)
            d = c3 - r
            bit = (sub & hb) != 0
            q_role = jnp.logical_not(bit) if rev else bit
            out.append(jnp.where(q_role, d, -d).reshape(c, DK_A))
        else:
            pieces = []
            for b0 in range(0, c, 2 * hb):
                mid = b0 + hb
                ridx = mid if rev else mid - 1
                r = cum[ridx:ridx + 1, :]
                lo = cum[b0:mid]
                hi = cum[mid:b0 + 2 * hb]
                pieces += [lo - r, r - hi] if rev else [r - lo, hi - r]
            out.append(jnp.concatenate(pieces, axis=0))
    return out


def _hgrn_level_operands(l, q, k, f, z, rev, row):
    c = HGRN_CHUNK
    hb = 1 << l
    if hb >= 8:
        zero = jnp.zeros((hb, DK_A), F32)
        qparts, kparts = [], []
        for b0 in range(0, c, 2 * hb):
            lo, hi = slice(b0, b0 + hb), slice(b0 + hb, b0 + 2 * hb)
            if rev:
                qparts += [q[lo] * z[lo], zero]
                kparts += [zero, k[hi] * z[hi]]
            else:
                qparts += [zero, q[hi] * z[hi]]
                kparts += [k[lo] * z[lo], zero]
        return (jnp.concatenate(qparts, axis=0).astype(BF16),
                jnp.concatenate(kparts, axis=0).astype(BF16))
    bit = ((row >> l) & 1) == 1
    q_role = jnp.logical_not(bit) if rev else bit
    ql = jnp.where(q_role, q * (f if l == 0 else z), 0.0).astype(BF16)
    kl = jnp.where(q_role, 0.0, k if l == 0 else k * z).astype(BF16)
    return ql, kl


def _hgrn_chunks(chains):
    c = HGRN_CHUNK
    row = lax.broadcasted_iota(jnp.int32, (c, DK_A), 0)
    cums = []
    for q, g, v, st, w_ref, lv_ref, rev in chains:
        terms = []
        rem = g
        for _ in range(HGRN_SPLIT):
            term = rem.astype(BF16)
            terms.append(term)
            rem = rem - term.astype(F32)
        cums.append(jnp.dot(w_ref[...], jnp.concatenate(terms, axis=0),
                            preferred_element_type=F32))
    work = []
    for (q, g, v, st, w_ref, lv_ref, rev), cum in zip(chains, cums):
        f = jnp.exp2(g)
        k = 1.0 - f
        last = 0 if rev else c - 1
        total_e = cum[last:last + 1, :]
        qd = (q * jnp.exp2(cum)).astype(BF16)
        kd = (k * jnp.exp2(total_e - cum)).astype(BF16)
        o = _nt_dot(qd, st.astype(BF16))
        st_new = (st * jnp.exp2(total_e)
                  + jnp.dot(v.T.astype(BF16), kd, preferred_element_type=F32))
        zs = [None] + [jnp.exp2(e) for e in _hgrn_level_exponents(cum, rev)]
        work.append((k, f, zs, o, st_new))
    accs = [None] * len(chains)
    for l in reversed(range(HGRN_LEVELS)):
        for i, ((q, g, v, st, w_ref, lv_ref, rev), (k, f, zs, o, st_new)) in enumerate(
                zip(chains, work)):
            ql, kl = _hgrn_level_operands(l, q, k, f, zs[l], rev, row)
            a_l = _nt_dot(ql, kl)
            accs[i] = a_l if accs[i] is None else jnp.where(lv_ref[...] == l, a_l, accs[i])
    outs = []
    for (q, g, v, st, w_ref, lv_ref, rev), (k, f, zs, o, st_new), a in zip(chains, work, accs):
        a = jnp.where(lv_ref[...] == HGRN_LEVELS, jnp.sum(q * k, axis=-1, keepdims=True), a)
        outs.append((o + jnp.dot(a.astype(BF16), v.astype(BF16), preferred_element_type=F32),
                     st_new))
    return outs


def _hgrn_kernel(*refs, n_chunks, has_init, emit_state):
    refs = list(refs)
    q_ref, gf_ref, gb_ref, v_ref, sg_ref, nw_ref, wf_ref, wb_ref, lvf_ref, lvb_ref = refs[:10]
    pos = 10
    s0_ref = None
    if has_init:
        s0_ref = refs[pos]
        pos += 1
    o_ref = refs[pos]
    pos += 1
    so_ref = None
    if emit_state:
        so_ref = refs[pos]
        pos += 1
    of_scr, ob_scr, st_scr = refs[pos:pos + 3]

    for d in range(2):
        for h in range(H_A):
            if has_init:
                st_scr[d, h] = s0_ref[d, h].T
            else:
                st_scr[d, h] = jnp.zeros((DV_A, DK_A), F32)

    def body(c, carry):
        chains, dests = [], []
        for h in range(H_A):
            cols = slice(h * DK_A, (h + 1) * DK_A)
            for d, (g_ref, w_ref, lv_ref, scr) in enumerate(
                    ((gf_ref, wf_ref, lvf_ref, of_scr), (gb_ref, wb_ref, lvb_ref, ob_scr))):
                cc = c if d == 0 else n_chunks - 1 - c
                r0 = pl.multiple_of(cc * HGRN_CHUNK, HGRN_CHUNK)
                rows = pl.ds(r0, HGRN_CHUNK)
                chains.append((q_ref[rows, cols], g_ref[rows, cols], v_ref[rows, cols],
                               st_scr[d, h], w_ref, lv_ref, d == 1))
                dests.append((scr, rows, cols, d, h))
        for (o, st), (scr, rows, cols, d, h) in zip(_hgrn_chunks(chains), dests):
            st_scr[d, h] = st
            scr[rows, cols] = o
        return carry

    lax.fori_loop(0, n_chunks, body, 0)
    for h in range(H_A):
        cols = slice(h * DV_A, (h + 1) * DV_A)
        o = of_scr[:, cols] + ob_scr[:, cols]
        y = o * lax.rsqrt(jnp.mean(o * o, axis=-1, keepdims=True) + EPS) * nw_ref[...]
        o_ref[:, cols] = (y * sg_ref[:, cols]).astype(o_ref.dtype)
    if emit_state:
        for d in range(2):
            for h in range(H_A):
                so_ref[d, h] = st_scr[d, h].T


def _hgrn_call(p, nw, tabs, seq_len, n_seq, row_blk0, s0, jl, mix, state_out, name):
    wf, lvf, wb, lvb = tabs
    has_init = s0 is not None
    emit_state = s0 is None
    const = lambda b: (0, 0)
    blk = (seq_len, F_A)
    state_spec = pl.BlockSpec((None, None, 2, H_A, DK_A, DV_A), lambda b: (b, jl, 0, 0, 0, 0))
    in_specs = [pl.BlockSpec(blk, (lambda b, part=part: (row_blk0 + b, part))) for part in range(5)]
    in_specs += [
        pl.BlockSpec((None, 1, DV_A), lambda b: (jl, 0, 0)),
        pl.BlockSpec(wf.shape, const), pl.BlockSpec(wb.shape, const),
        pl.BlockSpec(lvf.shape, const), pl.BlockSpec(lvb.shape, const),
    ]
    args = [p, p, p, p, p, nw, wf, wb, lvf, lvb]
    if has_init:
        in_specs.append(state_spec)
        args.append(s0)
    out_shape = [jax.ShapeDtypeStruct((N_TOK, D_MIX), BF16)]
    out_specs = [pl.BlockSpec((seq_len, H_A * DV_A), lambda b: (row_blk0 + b, 0))]
    carried = [mix]
    if emit_state:
        out_shape.append(jax.ShapeDtypeStruct((BATCH, N_EVEN, 2, H_A, DK_A, DV_A), F32))
        out_specs.append(state_spec)
        carried.append(state_out)
    return _aliased_call(
        functools.partial(_hgrn_kernel, n_chunks=seq_len // HGRN_CHUNK, has_init=has_init,
                          emit_state=emit_state),
        grid=(n_seq,), in_specs=in_specs, args=args, out_specs=out_specs,
        out_shape=out_shape, carried=carried, sem=("parallel",), name=name,
        scratch_shapes=[pltpu.VMEM((seq_len, H_A * DV_A), F32),
                        pltpu.VMEM((seq_len, H_A * DV_A), F32),
                        pltpu.VMEM((2, H_A, DV_A, DK_A), F32)])


def _swa_ctx_kernel(q_ref, kv_ref, sink_ref, o_ref, kc_ref, vc_ref):
    kv = kv_ref[...]
    k32 = kv[:, 0:KV_B * HD_B]
    v32 = kv[:, KV_B * HD_B:2 * KV_B * HD_B]
    kc_ref[...] = k32
    vc_ref[...] = v32
    k = k32.astype(BF16)
    vt32 = v32.T
    vts = [_with_ones_rows(vt32[n * HD_B:(n + 1) * HD_B]) for n in range(KV_B)]
    q = q_ref[...].astype(BF16)
    ksl = [slice((h // G_B) * HD_B, (h // G_B + 1) * HD_B) for h in range(H_B)]
    sts = [_nt_dot(k[:, ksl[h]], q[:, h * HD_B:(h + 1) * HD_B]) for h in range(H_B)]
    ps, sinks = [], []
    for h in range(H_B):
        sink = sink_ref[0:1, h:h + 1] * LOG2_E
        m = jnp.maximum(jnp.max(sts[h], axis=0, keepdims=True), sink)
        ps.append(jnp.exp2(sts[h] - m).astype(BF16))
        sinks.append(jnp.exp2(sink - m))
    outs = []
    for h in range(H_B):
        ota = jnp.dot(vts[h // G_B], ps[h], preferred_element_type=F32)
        outs.append(ota[0:HD_B] / (ota[HD_B:HD_B + 1] + sinks[h]))
    o_ref[...] = jnp.concatenate(outs, axis=0).T.astype(o_ref.dtype)


def _swa_ctx_call(p, sink, jl, mix, k_out, v_out, name):
    qcol = (3 * F_A + 2 * H_A * DV_A) // PROJ_TN
    cache_spec = pl.BlockSpec((None, None, SEQ, KV_B * HD_B), lambda b: (b, jl, 0, 0))
    cache_shape = jax.ShapeDtypeStruct((BATCH, N_EVEN, SEQ, KV_B * HD_B), F32)
    return _aliased_call(
        _swa_ctx_kernel,
        grid=(BATCH,),
        in_specs=[
            pl.BlockSpec((SEQ, H_B * HD_B), lambda b: (b, qcol)),
            pl.BlockSpec((SEQ, PROJ_TN), lambda b: (b, qcol + 1)),
            pl.BlockSpec((None, 1, H_B), lambda b: (jl, 0, 0)),
        ],
        args=[p, p, sink],
        out_specs=[pl.BlockSpec((SEQ, H_B * HD_B), lambda b: (b, 1)), cache_spec, cache_spec],
        out_shape=[jax.ShapeDtypeStruct((N_TOK, D_MIX), BF16), cache_shape, cache_shape],
        carried=[mix, k_out, v_out], sem=("parallel",), name=name)


def _swa_lat_kernel(q_ref, kv_ref, ck_ref, cv_ref, sink_ref, o_ref):
    qi = pl.program_id(1)
    tq = q_ref.shape[0]
    span = tq + 2 * WINDOW
    ws = pl.multiple_of(jnp.clip(qi * tq - WINDOW, 0, DEC_SEQ - span), WINDOW)
    kvw = kv_ref[pl.ds(ws, span), :]
    kw = kvw[:, 0:KV_B * HD_B].astype(BF16)
    vwt = kvw[:, KV_B * HD_B:2 * KV_B * HD_B].T.astype(BF16)
    kc = ck_ref[...].astype(BF16)
    vct = cv_ref[...].T.astype(BF16)
    q = q_ref[...].astype(BF16)
    t_k = ws + lax.broadcasted_iota(jnp.int32, (span, tq), 0)
    t_q = qi * tq + lax.broadcasted_iota(jnp.int32, (span, tq), 1)
    valid = jnp.abs(t_q - t_k) <= WINDOW
    outs = []

    def first(h):
        qh = q[:, h * HD_B:(h + 1) * HD_B]
        ksl = slice((h // G_B) * HD_B, (h // G_B + 1) * HD_B)
        return _nt_dot(kw[:, ksl], qh), _nt_dot(kc[:, ksl], qh)

    def second(h, scores):
        ksl = slice((h // G_B) * HD_B, (h // G_B + 1) * HD_B)
        s_w = jnp.where(valid, scores[0], -jnp.inf)
        s_c = scores[1]
        sink = sink_ref[0:1, h:h + 1] * LOG2_E
        m = jnp.maximum(jnp.maximum(jnp.max(s_w, axis=0, keepdims=True),
                                    jnp.max(s_c, axis=0, keepdims=True)), sink)
        p_w = jnp.exp2(s_w - m)
        p_c = jnp.exp2(s_c - m)
        den = (jnp.sum(p_w, axis=0, keepdims=True) + jnp.sum(p_c, axis=0, keepdims=True)
               + jnp.exp2(sink - m))
        ot = (jnp.dot(vwt[ksl, :], p_w.astype(BF16), preferred_element_type=F32)
              + jnp.dot(vct[ksl, :], p_c.astype(BF16), preferred_element_type=F32))
        outs.append(ot / den)

    _pipelined(H_B, first, second)
    o_ref[...] = jnp.concatenate(outs, axis=0).T.astype(o_ref.dtype)


def _swa_lat_call(p, ck, cv, sink, jl, mix, name):
    tq = ATT_TQ
    qcol = (3 * F_A + 2 * H_A * DV_A) // PROJ_TN
    nq = DEC_SEQ // tq
    q_blk0 = N_CTX_TOK // tq
    s_blk0 = N_CTX_TOK // DEC_SEQ
    cache_spec = pl.BlockSpec((None, None, PAST_LEN, KV_B * HD_B), lambda b, i: (b, jl, 0, 0))
    (out,) = _aliased_call(
        _swa_lat_kernel,
        grid=(DEC_BATCH, nq),
        in_specs=[
            pl.BlockSpec((tq, H_B * HD_B), lambda b, i: (q_blk0 + b * nq + i, qcol)),
            pl.BlockSpec((DEC_SEQ, PROJ_TN), lambda b, i: (s_blk0 + b, qcol + 1)),
            cache_spec, cache_spec,
            pl.BlockSpec((None, 1, H_B), lambda b, i: (jl, 0, 0)),
        ],
        args=[p, p, ck, cv, sink],
        out_specs=[pl.BlockSpec((tq, H_B * HD_B), lambda b, i: (q_blk0 + b * nq + i, 1))],
        out_shape=[jax.ShapeDtypeStruct((N_TOK, D_MIX), BF16)],
        carried=[mix], sem=("parallel", "parallel"), name=name)
    return out


def _diff_lambda(lp_ref, lam_init):
    lp = lp_ref[...]
    a = jnp.sum(lp[0:1] * lp[1:2], axis=-1, keepdims=True)
    b = jnp.sum(lp[2:3] * lp[3:4], axis=-1, keepdims=True)
    return jnp.exp(a) - jnp.exp(b) + lam_init


def _diff_scores(q, k_parts):
    return [[_nt_dot(kp[:, c * HD_C:(c + 1) * HD_C], q[:, c * HD_C:(c + 1) * HD_C])
             for kp in k_parts] for c in range(2)]


def _diff_finish(scores, vt_parts, lam, lam_init, sw_t):
    hw = 2 * HD_C
    comps = []
    for c in range(2):
        ss = scores[c]
        m = functools.reduce(jnp.maximum, [jnp.max(s, axis=0, keepdims=True) for s in ss])
        ota = functools.reduce(
            lambda a, b: a + b,
            [jnp.dot(vt, jnp.exp2(s - m).astype(BF16), preferred_element_type=F32)
             for s, vt in zip(ss, vt_parts)])
        comps.append(ota[0:hw] / ota[hw:hw + 1])
    ot = comps[0] - lam * comps[1]
    yt = ot * lax.rsqrt(jnp.mean(ot * ot, axis=0, keepdims=True) + EPS) * sw_t
    return (yt * (1.0 - lam_init)).T


def _with_ones_rows(vt):
    return jnp.concatenate([vt, jnp.ones((ONES_ROWS, vt.shape[1]), F32)], axis=0).astype(BF16)


def _diff_ctx_kernel(q_ref, k_ref, v_ref, lp_ref, sw_ref, o_ref, kc_ref, vc_ref, *, lam_init):
    lam = _diff_lambda(lp_ref, lam_init)
    hw = 2 * HD_C
    vts = {}

    def first(h):
        sl = slice(h * hw, (h + 1) * hw)
        k32 = k_ref[:, sl]
        v32 = v_ref[:, sl]
        kc_ref[:, sl] = k32
        vc_ref[:, sl] = v32
        vts[h] = _with_ones_rows(v32.T)
        return _diff_scores(q_ref[:, sl].astype(BF16), [k32.astype(BF16)])

    def second(h, scores):
        y = _diff_finish(scores, [vts.pop(h)], lam, lam_init, sw_ref[...])
        o_ref[:, h * hw:(h + 1) * hw] = y.astype(o_ref.dtype)

    _pipelined(H_C, first, second)


def _diff_ctx_call(p, lp, sw, lam_init, jl, k_out, v_out, name):
    cache_spec = pl.BlockSpec((None, None, SEQ, ODD_W), lambda b: (b, jl, 0, 0))
    cache_shape = jax.ShapeDtypeStruct((BATCH, N_ODD, SEQ, ODD_W), F32)
    return _aliased_call(
        functools.partial(_diff_ctx_kernel, lam_init=lam_init),
        grid=(BATCH,),
        in_specs=[
            pl.BlockSpec((SEQ, ODD_W), lambda b: (b, 0)),
            pl.BlockSpec((SEQ, ODD_W), lambda b: (b, 1)),
            pl.BlockSpec((SEQ, ODD_W), lambda b: (b, 2)),
            pl.BlockSpec((None, 4, HD_C), lambda b: (jl, 0, 0)),
            pl.BlockSpec((None, 2 * HD_C, ATT_TQ), lambda b: (jl, 0, 0)),
        ],
        args=[p, p, p, lp, sw],
        out_specs=[pl.BlockSpec((SEQ, ODD_W), lambda b: (b, 0)), cache_spec, cache_spec],
        out_shape=[jax.ShapeDtypeStruct((N_TOK, D_MIX), BF16), cache_shape, cache_shape],
        carried=[None, k_out, v_out], sem=("parallel",), name=name)


def _diff_lat_kernel(q_ref, k_ref, v_ref, ck_ref, cv_ref, lp_ref, sw_ref, o_ref, *, lam_init):
    lam = _diff_lambda(lp_ref, lam_init)
    hw = 2 * HD_C
    tq = ATT_TQ
    nq = q_ref.shape[0] // tq
    heads = []
    for g in range(q_ref.shape[1] // hw):
        sl = slice(g * hw, (g + 1) * hw)
        heads.append(([k_ref[:, sl].astype(BF16), ck_ref[:, sl].astype(BF16)],
                      [_with_ones_rows(v_ref[:, sl].T), _with_ones_rows(cv_ref[:, sl].T)]))

    def first(n):
        g, i = divmod(n, nq)
        q = q_ref[i * tq:(i + 1) * tq, g * hw:(g + 1) * hw].astype(BF16)
        return _diff_scores(q, heads[g][0])

    def second(n, scores):
        g, i = divmod(n, nq)
        y = _diff_finish(scores, heads[g][1], lam, lam_init, sw_ref[...])
        o_ref[i * tq:(i + 1) * tq, g * hw:(g + 1) * hw] = y.astype(o_ref.dtype)

    _pipelined(len(heads) * nq, first, second)


def _diff_lat_call(p, ck, cv, lp, sw, lam_init, jl, mix, name):
    hw = 2 * HD_C
    gw = DIFF_LAT_HEADS * hw
    n_g = H_C // DIFF_LAT_HEADS
    s_blk0 = N_CTX_TOK // DEC_SEQ
    cache_spec = pl.BlockSpec((None, None, PAST_LEN, gw), lambda b, h: (b, jl, 0, h))
    (out,) = _aliased_call(
        functools.partial(_diff_lat_kernel, lam_init=lam_init),
        grid=(DEC_BATCH, n_g),
        in_specs=[
            pl.BlockSpec((DEC_SEQ, gw), lambda b, h: (s_blk0 + b, h)),
            pl.BlockSpec((DEC_SEQ, gw), lambda b, h: (s_blk0 + b, n_g + h)),
            pl.BlockSpec((DEC_SEQ, gw), lambda b, h: (s_blk0 + b, 2 * n_g + h)),
            cache_spec, cache_spec,
            pl.BlockSpec((None, 4, HD_C), lambda b, h: (jl, 0, 0)),
            pl.BlockSpec((None, hw, ATT_TQ), lambda b, h: (jl, 0, 0)),
        ],
        args=[p, p, p, ck, cv, lp, sw],
        out_specs=[pl.BlockSpec((DEC_SEQ, gw), lambda b, h: (s_blk0 + b, h))],
        out_shape=[jax.ShapeDtypeStruct((N_TOK, D_MIX), BF16)],
        carried=[mix], sem=("parallel", "parallel"), name=name)
    return out


def _mlp_kernel(x_ref, mix_ref, wo_ref, mods_ref, nw_ref, w1_ref, w2_ref, o_ref,
                h_scr, *, lat):
    i = pl.program_id(0)
    k = pl.program_id(1)
    row = 1 + i if lat else 0

    def mod(a):
        return mods_ref[pl.ds(row, 1), a * D_MODEL:(a + 1) * D_MODEL]

    @pl.when(k == 0)
    def _():
        def first(r):
            return jnp.dot(mix_ref[r * MLP_RC:(r + 1) * MLP_RC, :], wo_ref[...],
                           preferred_element_type=F32)

        def second(r, y):
            rows = slice(r * MLP_RC, (r + 1) * MLP_RC)
            x1 = x_ref[rows, :] + mod(2) * y
            o_ref[rows, :] = x1
            h_scr[rows, :] = _norm_mod(x1, nw_ref[...], mod(4), mod(3)).astype(BF16)

        _pipelined(x_ref.shape[0] // MLP_RC, first, second)

    u = jnp.dot(h_scr[...], w1_ref[...].astype(BF16), preferred_element_type=F32)
    u = jnp.square(jnp.maximum(u, 0.0)).astype(BF16)
    o_ref[...] += mod(5) * jnp.dot(u, w2_ref[...].astype(BF16), preferred_element_type=F32)


def _mlp_call(x_src, x_tile0, lat, mix, wo, jl, mods, li, norm_w, w1, w2, out_prev, out_rows,
              out_tile0, name):
    tm, tk = MLP_TM, MLP_TK
    n_tiles = (N_LAT_TOK if lat else N_CTX_TOK) // tm
    mix_tile0 = N_CTX_TOK // tm if lat else 0
    (out,) = _aliased_call(
        functools.partial(_mlp_kernel, lat=lat),
        grid=(n_tiles, D_FF // tk),
        in_specs=[
            pl.BlockSpec((tm, D_MODEL), lambda i, k: (x_tile0 + i, 0)),
            pl.BlockSpec((tm, D_MIX), lambda i, k: (mix_tile0 + i, 0)),
            pl.BlockSpec((None, D_MIX, D_MODEL), lambda i, k: (jl, 0, 0)),
            pl.BlockSpec((None, MOD_ROWS, N_MOD), lambda i, k: (li, 0, 0)),
            pl.BlockSpec((None, None, 1, D_MODEL), lambda i, k: (li, 1, 0, 0)),
            pl.BlockSpec((None, D_MODEL, tk), lambda i, k: (li, 0, k)),
            pl.BlockSpec((None, tk, D_MODEL), lambda i, k: (li, k, 0)),
        ],
        args=[x_src, mix, wo, mods, norm_w, w1, w2],
        out_specs=[pl.BlockSpec((tm, D_MODEL), lambda i, k: (out_tile0 + i, 0))],
        out_shape=[jax.ShapeDtypeStruct((out_rows, D_MODEL), F32)],
        carried=[out_prev], sem=("parallel", "arbitrary"), name=name,
        scratch_shapes=[pltpu.VMEM((tm, D_MODEL), BF16)])
    return out


def _rope_tables(width):
    t = np.arange(DEC_SEQ)
    half = HEAD_GROUP // 2
    inv = ROPE_BASE ** (-np.arange(0, half, 2, dtype=np.float64) / half)
    ang = np.concatenate([(t // GRID_W)[:, None] * inv, (t % GRID_W)[:, None] * inv], axis=-1)
    cos = np.repeat(np.cos(ang), 2, axis=-1)
    sin = np.repeat(np.sin(ang), 2, axis=-1)
    sign = np.tile(np.array([-1.0, 1.0]), HEAD_GROUP // 2)
    reps = width // HEAD_GROUP
    return (jnp.asarray(np.tile(cos, (1, reps)), F32),
            jnp.asarray(np.tile(sin * sign, (1, reps)), F32))


def _block_diag_ones(n):
    g = np.arange(n) // HEAD_GROUP
    return jnp.asarray(g[:, None] == g[None, :], BF16)


def _tile_row(w, width):
    return jnp.tile(w.astype(F32), width // w.shape[0])[None, :]


def _lambda_init(li):
    return 0.8 - 0.6 * math.exp(-0.3 * li)


def kernel(x_prompt, x_sample, cache_k_swa, cache_v_swa, state_hgrn, cache_k_diff, cache_v_diff, c, c_ctx, norm_w, w_ada, b_ada, w_in_even, w_out_even, hgrn_lb_logits, hgrn_norm_w, swa_qnorm_w, swa_knorm_w, swa_sink, w_in_odd, w_out_odd, diff_qnorm_w, diff_knorm_w, diff_lambda_p, diff_subln_w, w_mlp1, w_mlp2):
    assert PROJ_TM == MLP_TM
    n_ctx_tiles = N_CTX_TOK // PROJ_TM
    x = None
    x_ctx0 = x_prompt.reshape(N_CTX_TOK, D_MODEL)
    x_lat0 = x_sample.reshape(N_LAT_TOK, D_MODEL)
    c_all = jnp.concatenate(
        [c_ctx[None, :], c, jnp.zeros((MOD_ROWS - 1 - DEC_BATCH, D_MODEL), F32)], axis=0)
    mods = _mods_call(c_all, w_ada, b_ada.reshape(DEPTH, 1, N_MOD))

    cos_t, sin_t = _rope_tables(ROPE_TW)
    bd = _block_diag_ones(256)
    tabs_f = _hgrn_tables(False)
    tabs_b = _hgrn_tables(True)
    hgrn_tabs = (tabs_f[0], tabs_f[1], tabs_b[0], tabs_b[1])
    lbl = hgrn_lb_logits.astype(F32).reshape(N_EVEN * 2, F_A)
    norm_w4 = norm_w.astype(F32).reshape(DEPTH, 2, 1, D_MODEL)

    w_out_even_b = w_out_even.astype(BF16)
    w_out_odd_b = w_out_odd.astype(BF16)

    ck_swa = cache_k_swa.reshape(DEC_BATCH, N_EVEN, PAST_LEN, KV_B * HD_B)
    cv_swa = cache_v_swa.reshape(DEC_BATCH, N_EVEN, PAST_LEN, KV_B * HD_B)
    ck_diff = cache_k_diff.reshape(DEC_BATCH, N_ODD, PAST_LEN, ODD_W)
    cv_diff = cache_v_diff.reshape(DEC_BATCH, N_ODD, PAST_LEN, ODD_W)
    hgrn_nw = hgrn_norm_w.astype(F32).reshape(N_EVEN, 1, DV_A)
    sink = swa_sink.astype(F32).reshape(N_EVEN, 1, H_B)
    lam_p = diff_lambda_p.astype(F32)
    assert ATT_TQ == SEQ
    subln = jnp.broadcast_to(diff_subln_w.astype(F32)[:, :, None], (N_ODD, 2 * HD_C, ATT_TQ))

    even_kinds = ("silu_scale", "loggate0", "loggate1", "ident", "silu", "qnorm", "kv")
    odd_kinds = ("qnorm", "qnorm", "knorm", "knorm", "ident", "ident")

    k_swa = v_swa = states = k_diff = v_diff = None
    for li in range(DEPTH):
        j = li // 2
        srcs = ((x_ctx0, 0), (x_lat0, 0)) if li == 0 else ((x, 0), (x, n_ctx_tiles))
        if li % 2 == 0:
            p = None
            for lat, (src, t0) in enumerate(srcs):
                p = _proj_call(src, t0, bool(lat), p, mods, li, norm_w4, w_in_even, j, lbl,
                               _tile_row(swa_qnorm_w[j], PROJ_TN),
                               _tile_row(swa_knorm_w[j], PROJ_TN),
                               cos_t, sin_t, bd, even_kinds, f"proj_even{j}_{lat}")
            mix, states = _hgrn_call(p, hgrn_nw, hgrn_tabs, SEQ, BATCH, 0, None, j, None, states,
                                     f"hgrn_ctx{j}")
            (mix,) = _hgrn_call(p, hgrn_nw, hgrn_tabs, DEC_SEQ, DEC_BATCH, N_CTX_TOK // DEC_SEQ,
                                state_hgrn, j, mix, None, f"hgrn_lat{j}")
            mix, k_swa, v_swa = _swa_ctx_call(p, sink, j, mix, k_swa, v_swa, f"swa_ctx{j}")
            mix = _swa_lat_call(p, ck_swa, cv_swa, sink, j, mix, f"swa_lat{j}")
            wo = w_out_even_b
        else:
            p = None
            for lat, (src, t0) in enumerate(srcs):
                p = _proj_call(src, t0, bool(lat), p, mods, li, norm_w4, w_in_odd, j, lbl,
                               _tile_row(diff_qnorm_w[j], PROJ_TN),
                               _tile_row(diff_knorm_w[j], PROJ_TN),
                               cos_t, sin_t, bd, odd_kinds, f"proj_odd{j}_{lat}")
            lam_init = _lambda_init(li)
            mix, k_diff, v_diff = _diff_ctx_call(p, lam_p, subln, lam_init, j, k_diff, v_diff,
                                                 f"diff_ctx{j}")
            mix = _diff_lat_call(p, ck_diff, cv_diff, lam_p, subln, lam_init, j, mix,
                                 f"diff_lat{j}")
            wo = w_out_odd_b
        last = li == DEPTH - 1
        outs = []
        x_next = None
        for lat, (src, t0) in enumerate(srcs):
            rows = (N_LAT_TOK if lat else N_CTX_TOK) if last else N_TOK
            out_t0 = 0 if last else lat * n_ctx_tiles
            x_next = _mlp_call(src, t0, bool(lat), mix, wo, j, mods, li, norm_w4, w_mlp1, w_mlp2,
                               None if last else x_next, rows, out_t0, f"mlp{li}_{lat}")
            outs.append(x_next)
        x = x_next

    y_prompt = outs[0].reshape(BATCH, SEQ, D_MODEL)
    y_sample = outs[1].reshape(DEC_BATCH, DEC_SEQ, D_MODEL)
    return (y_prompt, y_sample,
            k_swa.reshape(BATCH, N_EVEN, SEQ, KV_B, HD_B),
            v_swa.reshape(BATCH, N_EVEN, SEQ, KV_B, HD_B),
            states,
            k_diff.reshape(BATCH, N_ODD, SEQ, H_C, 2, HD_C),
            v_diff.reshape(BATCH, N_ODD, SEQ, H_C, 2 * HD_C))
```

```python
import functools
import math

import numpy as np
import jax
import jax.numpy as jnp
from jax import lax
from jax.experimental import pallas as pl
from jax.experimental.pallas import tpu as pltpu

F32 = jnp.float32
BF16 = jnp.bfloat16

D_MODEL = 1024
BATCH = 16
SEQ = 256
DEPTH = 4
DEC_BATCH = 4
DEC_SEQ = 1024
PAST_LEN = 512
GRID_W = 64
N_EVEN = (DEPTH + 1) // 2
N_ODD = DEPTH // 2
H_A = 4
DK_A = 128
DV_A = D_MODEL // 2 // H_A
F_A = H_A * DK_A
H_B = 8
KV_B = 2
G_B = H_B // KV_B
HD_B = D_MODEL // 2 // H_B
WINDOW = 128
H_C = 8
HD_C = D_MODEL // (2 * H_C)
D_FF = 4 * D_MODEL
ROPE_BASE = 10000.0
EPS = 1e-6
EVEN_COLS = 3 * F_A + 2 * H_A * DV_A + (H_B + 2 * KV_B) * HD_B
ODD_W = H_C * 2 * HD_C
D_MIX = D_MODEL

N_CTX_TOK = BATCH * SEQ
N_LAT_TOK = DEC_BATCH * DEC_SEQ
N_TOK = N_CTX_TOK + N_LAT_TOK
MOD_ROWS = 8
N_MOD = 6 * D_MODEL

HEAD_GROUP = 64
HGRN_CHUNK = 128
HGRN_LEVELS = 7
HGRN_SPLIT = 3
LOG2_E = math.log2(math.e)
VMEM_LIMIT = 48 * 1024 * 1024

PROJ_TM = 1024
PROJ_TN = 512
ROPE_TW = 128
PROJ_TILES_CTX = 2
PROJ_TILES_LAT = 1
PROJ_RC = 128
MLP_TM = 1024
MLP_TK = 1024
MLP_RC = 256
ADA_TN = 1536
ATT_TQ = 256
DIFF_LAT_HEADS = 2
ONES_ROWS = 16


def _silu(x):
    return x * jax.nn.sigmoid(x)


def _nt_dot(a, b):
    return lax.dot_general(a, b, (((1,), (1,)), ((), ())), preferred_element_type=F32)


def _pipelined(n, first, second):
    cur = first(0)
    for i in range(n):
        nxt = first(i + 1) if i + 1 < n else None
        second(i, cur)
        cur = nxt


def _params(sem):
    return pltpu.CompilerParams(dimension_semantics=sem, vmem_limit_bytes=VMEM_LIMIT)


def _aliased_call(kernel, *, grid, in_specs, args, out_specs, out_shape, carried, sem, name,
                  scratch_shapes=()):
    n_in = len(args)
    extra = [buf for buf in carried if buf is not None]
    aliases = {}
    for k, buf in enumerate(carried):
        if buf is not None:
            aliases[n_in + len(aliases)] = k
    n_extra = len(extra)

    def body(*refs):
        kernel(*refs[:n_in], *refs[n_in + n_extra:])

    return pl.pallas_call(
        body,
        grid=grid,
        in_specs=list(in_specs) + [pl.BlockSpec(memory_space=pl.ANY)] * n_extra,
        out_specs=out_specs,
        out_shape=out_shape,
        input_output_aliases=aliases,
        scratch_shapes=list(scratch_shapes),
        compiler_params=_params(sem),
        name=name,
    )(*args, *extra)


def _mods_kernel(c_ref, w_ref, b_ref, o_ref):
    s = _silu(c_ref[...]).astype(BF16)
    o_ref[...] = jnp.dot(s, w_ref[...].astype(BF16), preferred_element_type=F32) + b_ref[...]


def _mods_call(c_all, w_ada, b_ada):
    return pl.pallas_call(
        _mods_kernel,
        grid=(DEPTH, N_MOD // ADA_TN),
        in_specs=[
            pl.BlockSpec((MOD_ROWS, D_MODEL), lambda l, j: (0, 0)),
            pl.BlockSpec((None, D_MODEL, ADA_TN), lambda l, j: (l, 0, j)),
            pl.BlockSpec((None, 1, ADA_TN), lambda l, j: (l, 0, j)),
        ],
        out_specs=pl.BlockSpec((None, MOD_ROWS, ADA_TN), lambda l, j: (l, 0, j)),
        out_shape=jax.ShapeDtypeStruct((DEPTH, MOD_ROWS, N_MOD), F32),
        compiler_params=_params(("parallel", "parallel")),
        name="ada_mods",
    )(c_all, w_ada, b_ada)


def _norm_mod(x, nw, sc, sh):
    ms = jnp.mean(x * x, axis=-1, keepdims=True)
    return (x * lax.rsqrt(ms + EPS) * nw) * (1.0 + sc) + sh


def _group_rms(y, w_t, bd_ref):
    yy = (y * y).astype(BF16)
    bw = bd_ref.shape[0]
    parts = [jnp.dot(yy[:, s:s + bw], bd_ref[...], preferred_element_type=F32)
             for s in range(0, y.shape[1], bw)]
    ss = parts[0] if len(parts) == 1 else jnp.concatenate(parts, axis=1)
    return y * lax.rsqrt(ss * (1.0 / HEAD_GROUP) + EPS) * w_t


def _rope(y, cos, sin):
    n = y.shape[1]
    lane = lax.broadcasted_iota(jnp.int32, y.shape, 1)
    nxt = pltpu.roll(y, n - 1, axis=1)
    prv = pltpu.roll(y, 1, axis=1)
    swapped = jnp.where((lane & 1) == 0, nxt, prv)
    return y * cos + swapped * sin


def _lower_bounds(lbl_ref, jl):
    rows = [lbl_ref[pl.ds(2 * m, 2), :] for m in range(N_EVEN)]
    mx = functools.reduce(jnp.maximum, rows)
    es = [jnp.exp(r - mx) for r in rows]
    den = functools.reduce(lambda a, b: a + b, es)
    sm = [e / den for e in es]
    cs = sm[0]
    for m in range(1, jl + 1):
        cs = cs + sm[m]
    return cs - sm[0]


def _proj_kernel(x_ref, mods_ref, nw_ref, w_ref, wkv_ref, lbl_ref, qn_ref, kn_ref, cos_ref,
                 sin_ref, bd_ref, o_ref, h_scr, w_scr, *, kinds, jl, lat):
    tiles = PROJ_TILES_LAT if lat else PROJ_TILES_CTX
    j = pl.program_id(0)
    i = pl.program_id(1)
    tm, tn = PROJ_TM, o_ref.shape[1]
    kvw = 2 * KV_B * HD_B

    @pl.when(j == 0)
    def _():
        for t in range(tiles):
            tile = tiles * i + t
            row = 1 + tile if lat else 0
            sh = mods_ref[pl.ds(row, 1), 0:D_MODEL]
            sc = mods_ref[pl.ds(row, 1), D_MODEL:2 * D_MODEL]
            h_scr[tile] = _norm_mod(x_ref[t * tm:(t + 1) * tm, :], nw_ref[...], sc,
                                    sh).astype(BF16)

    def finish(kind, y, rows):
        if kind == "silu_scale":
            return _silu(y) * (DK_A ** -0.5)
        if kind in ("loggate0", "loggate1"):
            d = int(kind[-1])
            lb = _lower_bounds(lbl_ref, jl)[d:d + 1, :]
            return jnp.log2(lb + (1.0 - lb) * jax.nn.sigmoid(y))
        if kind == "ident":
            return y
        if kind == "silu":
            return _silu(y)
        if kind in ("qnorm", "knorm"):
            w_t = qn_ref[...] if kind == "qnorm" else kn_ref[...]
            r = _group_rms(y, w_t, bd_ref)
            if lat:
                r = _rope(r, jnp.tile(cos_ref[rows, :], (1, tn // ROPE_TW)),
                          jnp.tile(sin_ref[rows, :], (1, tn // ROPE_TW)))
            return r * (HEAD_GROUP ** -0.5 * LOG2_E) if kind == "qnorm" else r
        if kind == "kv":
            kn = _group_rms(y, kn_ref[:, 0:kvw], bd_ref)
            if lat:
                kn = _rope(kn, jnp.tile(cos_ref[rows, :], (1, kvw // ROPE_TW)),
                           jnp.tile(sin_ref[rows, :], (1, kvw // ROPE_TW)))
            lane = lax.broadcasted_iota(jnp.int32, y.shape, 1)
            return jnp.where(lane < KV_B * HD_B, kn, y)
        raise ValueError(kind)

    def run(kind):
        @pl.when(i == 0)
        def _():
            if kind == "kv":
                w_scr[:, 0:kvw] = wkv_ref[...].astype(BF16)
            else:
                w_scr[...] = w_ref[...].astype(BF16)

        per_tile = tm // PROJ_RC

        def first(r):
            t, c = divmod(r, per_tile)
            w = w_scr[:, 0:kvw] if kind == "kv" else w_scr[...]
            return jnp.dot(h_scr[tiles * i + t, c * PROJ_RC:(c + 1) * PROJ_RC, :], w,
                           preferred_element_type=F32)

        def second(r, y):
            c = r % per_tile
            rows = slice(r * PROJ_RC, (r + 1) * PROJ_RC)
            seq_rows = slice(c * PROJ_RC, (c + 1) * PROJ_RC)
            if kind == "kv":
                o_ref[rows, 0:kvw] = finish(kind, y, seq_rows)
                o_ref[rows, kvw:tn] = jnp.zeros((PROJ_RC, tn - kvw), F32)
            else:
                o_ref[rows, :] = finish(kind, y, seq_rows)

        _pipelined(tiles * per_tile, first, second)

    for jj, kind in enumerate(kinds):
        pl.when(j == jj)(functools.partial(run, kind))


def _proj_call(x_src, x_tile0, lat, p_prev, mods, li, norm_w, w, jl, lbl, qn_t, kn_t, cos_t, sin_t,
               bd, kinds, name):
    tm, tn = PROJ_TM, PROJ_TN
    assert tm == DEC_SEQ
    n_ctx_tiles = N_CTX_TOK // tm
    n_tiles = (N_LAT_TOK if lat else N_CTX_TOK) // tm
    tile0 = n_ctx_tiles if lat else 0
    if not lat:
        cos_t = sin_t = jnp.zeros((8, 128), F32)
    n_main = sum(1 for k in kinds if k != "kv")
    n_cols = tn * len(kinds)
    kvw = 2 * KV_B * HD_B
    kv_blk = (n_main * tn) // kvw if "kv" in kinds else 0
    const = lambda j, i: (0, 0)
    tiles = PROJ_TILES_LAT if lat else PROJ_TILES_CTX
    n_blk = n_tiles // tiles
    blk = tiles * tm
    x_blk0, out_blk0 = x_tile0 // tiles, tile0 // tiles
    assert x_tile0 % tiles == 0 and tile0 % tiles == 0
    (out,) = _aliased_call(
        functools.partial(_proj_kernel, kinds=kinds, jl=jl, lat=lat),
        grid=(len(kinds), n_blk),
        in_specs=[
            pl.BlockSpec((blk, D_MODEL),
                         lambda j, i: (x_blk0 + jnp.where(j == 0, i, n_blk - 1), 0)),
            pl.BlockSpec((None, MOD_ROWS, N_MOD), lambda j, i: (li, 0, 0)),
            pl.BlockSpec((None, None, 1, D_MODEL), lambda j, i: (li, 0, 0, 0)),
            pl.BlockSpec((None, D_MODEL, tn), lambda j, i: (jl, 0, jnp.minimum(j, n_main - 1))),
            pl.BlockSpec((None, D_MODEL, kvw), lambda j, i: (jl, 0, kv_blk)),
            pl.BlockSpec(lbl.shape, const),
            pl.BlockSpec((1, tn), const),
            pl.BlockSpec((1, tn), const),
            pl.BlockSpec(cos_t.shape, const),
            pl.BlockSpec(sin_t.shape, const),
            pl.BlockSpec(bd.shape, const),
        ],
        args=[x_src, mods, norm_w, w, w, lbl, qn_t, kn_t, cos_t, sin_t, bd],
        out_specs=[pl.BlockSpec((blk, tn), lambda j, i: (out_blk0 + i, j))],
        out_shape=[jax.ShapeDtypeStruct((N_TOK, n_cols), F32)],
        carried=[p_prev], sem=("arbitrary", "arbitrary"), name=name,
        scratch_shapes=[pltpu.VMEM((n_tiles, tm, D_MODEL), BF16),
                        pltpu.VMEM((D_MODEL, tn), BF16)])
    return out


def _hgrn_tables(rev):
    c = HGRN_CHUNK
    t = np.arange(c)
    w = (t[None, :] <= t[:, None]) if not rev else (t[None, :] >= t[:, None])
    ws = np.concatenate([w.astype(np.float32)] * HGRN_SPLIT, axis=1)
    x = t[:, None] ^ t[None, :]
    lv = np.where(x > 0, np.floor(np.log2(np.maximum(x, 1))).astype(np.int32), HGRN_LEVELS)
    causal = (t[None, :] < t[:, None]) if not rev else (t[None, :] > t[:, None])
    lv = np.where(causal | (x == 0), lv, -1).astype(np.int32)
    return jnp.asarray(ws, BF16), jnp.asarray(lv)


def _hgrn_level_exponents(cum, rev):
    c = HGRN_CHUNK
    sub_rows = 8
    c3 = cum.reshape(c // sub_rows, sub_rows, DK_A)
    sub = lax.broadcasted_iota(jnp.int32, c3.shape, 1)
    out = []
    for l in range(1, HGRN_LEVELS):
        hb = 1 << l
        if 2 * hb <= sub_rows:
            r = None
            for b0 in range(0, sub_rows, 2 * hb):
                idx = b0 + (hb if rev else hb - 1)
                rk = c3[:, idx:idx + 1, :]
                r = rk if r is None else jnp.where(sub < b0, r, rk)
            d = c3 - r
            bit = (sub & hb) != 0
            q_role = jnp.logical_not(bit) if rev else bit
            out.append(jnp.where(q_role, d, -d).reshape(c, DK_A))
        else:
            pieces = []
            for b0 in range(0, c, 2 * hb):
                mid = b0 + hb
                ridx = mid if rev else mid - 1
                r = cum[ridx:ridx + 1, :]
                lo = cum[b0:mid]
                hi = cum[mid:b0 + 2 * hb]
                pieces += [lo - r, r - hi] if rev else [r - lo, hi - r]
            out.append(jnp.concatenate(pieces, axis=0))
    return out


def _hgrn_level_operands(l, q, k, f, z, rev, row):
    c = HGRN_CHUNK
    hb = 1 << l
    if hb >= 8:
        zero = jnp.zeros((hb, DK_A), F32)
        qparts, kparts = [], []
        for b0 in range(0, c, 2 * hb):
            lo, hi = slice(b0, b0 + hb), slice(b0 + hb, b0 + 2 * hb)
            if rev:
                qparts += [q[lo] * z[lo], zero]
                kparts += [zero, k[hi] * z[hi]]
            else:
                qparts += [zero, q[hi] * z[hi]]
                kparts += [k[lo] * z[lo], zero]
        return (jnp.concatenate(qparts, axis=0).astype(BF16),
                jnp.concatenate(kparts, axis=0).astype(BF16))
    bit = ((row >> l) & 1) == 1
    q_role = jnp.logical_not(bit) if rev else bit
    ql = jnp.where(q_role, q * (f if l == 0 else z), 0.0).astype(BF16)
    kl = jnp.where(q_role, 0.0, k if l == 0 else k * z).astype(BF16)
    return ql, kl


def _hgrn_chunks(chains):
    c = HGRN_CHUNK
    row = lax.broadcasted_iota(jnp.int32, (c, DK_A), 0)
    cums = []
    for q, g, v, st, w_ref, lv_ref, rev in chains:
        terms = []
        rem = g
        for _ in range(HGRN_SPLIT):
            term = rem.astype(BF16)
            terms.append(term)
            rem = rem - term.astype(F32)
        cums.append(jnp.dot(w_ref[...], jnp.concatenate(terms, axis=0),
                            preferred_element_type=F32))
    work = []
    for (q, g, v, st, w_ref, lv_ref, rev), cum in zip(chains, cums):
        f = jnp.exp2(g)
        k = 1.0 - f
        last = 0 if rev else c - 1
        total_e = cum[last:last + 1, :]
        qd = (q * jnp.exp2(cum)).astype(BF16)
        kd = (k * jnp.exp2(total_e - cum)).astype(BF16)
        o = _nt_dot(qd, st.astype(BF16))
        st_new = (st * jnp.exp2(total_e)
                  + jnp.dot(v.T.astype(BF16), kd, preferred_element_type=F32))
        zs = [None] + [jnp.exp2(e) for e in _hgrn_level_exponents(cum, rev)]
        work.append((k, f, zs, o, st_new))
    accs = [None] * len(chains)
    for l in reversed(range(HGRN_LEVELS)):
        for i, ((q, g, v, st, w_ref, lv_ref, rev), (k, f, zs, o, st_new)) in enumerate(
                zip(chains, work)):
            ql, kl = _hgrn_level_operands(l, q, k, f, zs[l], rev, row)
            a_l = _nt_dot(ql, kl)
            accs[i] = a_l if accs[i] is None else jnp.where(lv_ref[...] == l, a_l, accs[i])
    outs = []
    for (q, g, v, st, w_ref, lv_ref, rev), (k, f, zs, o, st_new), a in zip(chains, work, accs):
        a = jnp.where(lv_ref[...] == HGRN_LEVELS, jnp.sum(q * k, axis=-1, keepdims=True), a)
        outs.append((o + jnp.dot(a.astype(BF16), v.astype(BF16), preferred_element_type=F32),
                     st_new))
    return outs


def _hgrn_kernel(*refs, n_chunks, has_init, emit_state):
    refs = list(refs)
    q_ref, gf_ref, gb_ref, v_ref, sg_ref, nw_ref, wf_ref, wb_ref, lvf_ref, lvb_ref = refs[:10]
    pos = 10
    s0_ref = None
    if has_init:
        s0_ref = refs[pos]
        pos += 1
    o_ref = refs[pos]
    pos += 1
    so_ref = None
    if emit_state:
        so_ref = refs[pos]
        pos += 1
    of_scr, ob_scr, st_scr = refs[pos:pos + 3]

    for d in range(2):
        for h in range(H_A):
            if has_init:
                st_scr[d, h] = s0_ref[d, h].T
            else:
                st_scr[d, h] = jnp.zeros((DV_A, DK_A), F32)

    def body(c, carry):
        chains, dests = [], []
        for h in range(H_A):
            cols = slice(h * DK_A, (h + 1) * DK_A)
            for d, (g_ref, w_ref, lv_ref, scr) in enumerate(
                    ((gf_ref, wf_ref, lvf_ref, of_scr), (gb_ref, wb_ref, lvb_ref, ob_scr))):
                cc = c if d == 0 else n_chunks - 1 - c
                r0 = pl.multiple_of(cc * HGRN_CHUNK, HGRN_CHUNK)
                rows = pl.ds(r0, HGRN_CHUNK)
                chains.append((q_ref[rows, cols], g_ref[rows, cols], v_ref[rows, cols],
                               st_scr[d, h], w_ref, lv_ref, d == 1))
                dests.append((scr, rows, cols, d, h))
        for (o, st), (scr, rows, cols, d, h) in zip(_hgrn_chunks(chains), dests):
            st_scr[d, h] = st
            scr[rows, cols] = o
        return carry

    lax.fori_loop(0, n_chunks, body, 0)
    for h in range(H_A):
        cols = slice(h * DV_A, (h + 1) * DV_A)
        o = of_scr[:, cols] + ob_scr[:, cols]
        y = o * lax.rsqrt(jnp.mean(o * o, axis=-1, keepdims=True) + EPS) * nw_ref[...]
        o_ref[:, cols] = (y * sg_ref[:, cols]).astype(o_ref.dtype)
    if emit_state:
        for d in range(2):
            for h in range(H_A):
                so_ref[d, h] = st_scr[d, h].T


def _hgrn_call(p, nw, tabs, seq_len, n_seq, row_blk0, s0, jl, mix, state_out, name):
    wf, lvf, wb, lvb = tabs
    has_init = s0 is not None
    emit_state = s0 is None
    const = lambda b: (0, 0)
    blk = (seq_len, F_A)
    state_spec = pl.BlockSpec((None, None, 2, H_A, DK_A, DV_A), lambda b: (b, jl, 0, 0, 0, 0))
    in_specs = [pl.BlockSpec(blk, (lambda b, part=part: (row_blk0 + b, part))) for part in range(5)]
    in_specs += [
        pl.BlockSpec((None, 1, DV_A), lambda b: (jl, 0, 0)),
        pl.BlockSpec(wf.shape, const), pl.BlockSpec(wb.shape, const),
        pl.BlockSpec(lvf.shape, const), pl.BlockSpec(lvb.shape, const),
    ]
    args = [p, p, p, p, p, nw, wf, wb, lvf, lvb]
    if has_init:
        in_specs.append(state_spec)
        args.append(s0)
    out_shape = [jax.ShapeDtypeStruct((N_TOK, D_MIX), BF16)]
    out_specs = [pl.BlockSpec((seq_len, H_A * DV_A), lambda b: (row_blk0 + b, 0))]
    carried = [mix]
    if emit_state:
        out_shape.append(jax.ShapeDtypeStruct((BATCH, N_EVEN, 2, H_A, DK_A, DV_A), F32))
        out_specs.append(state_spec)
        carried.append(state_out)
    return _aliased_call(
        functools.partial(_hgrn_kernel, n_chunks=seq_len // HGRN_CHUNK, has_init=has_init,
                          emit_state=emit_state),
        grid=(n_seq,), in_specs=in_specs, args=args, out_specs=out_specs,
        out_shape=out_shape, carried=carried, sem=("parallel",), name=name,
        scratch_shapes=[pltpu.VMEM((seq_len, H_A * DV_A), F32),
                        pltpu.VMEM((seq_len, H_A * DV_A), F32),
                        pltpu.VMEM((2, H_A, DV_A, DK_A), F32)])


def _swa_ctx_kernel(q_ref, kv_ref, sink_ref, o_ref, kc_ref, vc_ref):
    kv = kv_ref[...]
    k32 = kv[:, 0:KV_B * HD_B]
    v32 = kv[:, KV_B * HD_B:2 * KV_B * HD_B]
    kc_ref[...] = k32
    vc_ref[...] = v32
    k = k32.astype(BF16)
    vt32 = v32.T
    vts = [_with_ones_rows(vt32[n * HD_B:(n + 1) * HD_B]) for n in range(KV_B)]
    q = q_ref[...].astype(BF16)
    ksl = [slice((h // G_B) * HD_B, (h // G_B + 1) * HD_B) for h in range(H_B)]
    sts = [_nt_dot(k[:, ksl[h]], q[:, h * HD_B:(h + 1) * HD_B]) for h in range(H_B)]
    ps, sinks = [], []
    for h in range(H_B):
        sink = sink_ref[0:1, h:h + 1] * LOG2_E
        m = jnp.maximum(jnp.max(sts[h], axis=0, keepdims=True), sink)
        ps.append(jnp.exp2(sts[h] - m).astype(BF16))
        sinks.append(jnp.exp2(sink - m))
    outs = []
    for h in range(H_B):
        ota = jnp.dot(vts[h // G_B], ps[h], preferred_element_type=F32)
        outs.append(ota[0:HD_B] / (ota[HD_B:HD_B + 1] + sinks[h]))
    o_ref[...] = jnp.concatenate(outs, axis=0).T.astype(o_ref.dtype)


def _swa_ctx_call(p, sink, jl, mix, k_out, v_out, name):
    qcol = (3 * F_A + 2 * H_A * DV_A) // PROJ_TN
    cache_spec = pl.BlockSpec((None, None, SEQ, KV_B * HD_B), lambda b: (b, jl, 0, 0))
    cache_shape = jax.ShapeDtypeStruct((BATCH, N_EVEN, SEQ, KV_B * HD_B), F32)
    return _aliased_call(
        _swa_ctx_kernel,
        grid=(BATCH,),
        in_specs=[
            pl.BlockSpec((SEQ, H_B * HD_B), lambda b: (b, qcol)),
            pl.BlockSpec((SEQ, PROJ_TN), lambda b: (b, qcol + 1)),
            pl.BlockSpec((None, 1, H_B), lambda b: (jl, 0, 0)),
        ],
        args=[p, p, sink],
        out_specs=[pl.BlockSpec((SEQ, H_B * HD_B), lambda b: (b, 1)), cache_spec, cache_spec],
        out_shape=[jax.ShapeDtypeStruct((N_TOK, D_MIX), BF16), cache_shape, cache_shape],
        carried=[mix, k_out, v_out], sem=("parallel",), name=name)


def _swa_lat_kernel(q_ref, kv_ref, ck_ref, cv_ref, sink_ref, o_ref):
    qi = pl.program_id(1)
    tq = q_ref.shape[0]
    span = tq + 2 * WINDOW
    ws = pl.multiple_of(jnp.clip(qi * tq - WINDOW, 0, DEC_SEQ - span), WINDOW)
    kvw = kv_ref[pl.ds(ws, span), :]
    kw = kvw[:, 0:KV_B * HD_B].astype(BF16)
    vwt = kvw[:, KV_B * HD_B:2 * KV_B * HD_B].T.astype(BF16)
    kc = ck_ref[...].astype(BF16)
    vct = cv_ref[...].T.astype(BF16)
    q = q_ref[...].astype(BF16)
    t_k = ws + lax.broadcasted_iota(jnp.int32, (span, tq), 0)
    t_q = qi * tq + lax.broadcasted_iota(jnp.int32, (span, tq), 1)
    valid = jnp.abs(t_q - t_k) <= WINDOW
    outs = []

    def first(h):
        qh = q[:, h * HD_B:(h + 1) * HD_B]
        ksl = slice((h // G_B) * HD_B, (h // G_B + 1) * HD_B)
        return _nt_dot(kw[:, ksl], qh), _nt_dot(kc[:, ksl], qh)

    def second(h, scores):
        ksl = slice((h // G_B) * HD_B, (h // G_B + 1) * HD_B)
        s_w = jnp.where(valid, scores[0], -jnp.inf)
        s_c = scores[1]
        sink = sink_ref[0:1, h:h + 1] * LOG2_E
        m = jnp.maximum(jnp.maximum(jnp.max(s_w, axis=0, keepdims=True),
                                    jnp.max(s_c, axis=0, keepdims=True)), sink)
        p_w = jnp.exp2(s_w - m)
        p_c = jnp.exp2(s_c - m)
        den = (jnp.sum(p_w, axis=0, keepdims=True) + jnp.sum(p_c, axis=0, keepdims=True)
               + jnp.exp2(sink - m))
        ot = (jnp.dot(vwt[ksl, :], p_w.astype(BF16), preferred_element_type=F32)
              + jnp.dot(vct[ksl, :], p_c.astype(BF16), preferred_element_type=F32))
        outs.append(ot / den)

    _pipelined(H_B, first, second)
    o_ref[...] = jnp.concatenate(outs, axis=0).T.astype(o_ref.dtype)


def _swa_lat_call(p, ck, cv, sink, jl, mix, name):
    tq = ATT_TQ
    qcol = (3 * F_A + 2 * H_A * DV_A) // PROJ_TN
    nq = DEC_SEQ // tq
    q_blk0 = N_CTX_TOK // tq
    s_blk0 = N_CTX_TOK // DEC_SEQ
    cache_spec = pl.BlockSpec((None, None, PAST_LEN, KV_B * HD_B), lambda b, i: (b, jl, 0, 0))
    (out,) = _aliased_call(
        _swa_lat_kernel,
        grid=(DEC_BATCH, nq),
        in_specs=[
            pl.BlockSpec((tq, H_B * HD_B), lambda b, i: (q_blk0 + b * nq + i, qcol)),
            pl.BlockSpec((DEC_SEQ, PROJ_TN), lambda b, i: (s_blk0 + b, qcol + 1)),
            cache_spec, cache_spec,
            pl.BlockSpec((None, 1, H_B), lambda b, i: (jl, 0, 0)),
        ],
        args=[p, p, ck, cv, sink],
        out_specs=[pl.BlockSpec((tq, H_B * HD_B), lambda b, i: (q_blk0 + b * nq + i, 1))],
        out_shape=[jax.ShapeDtypeStruct((N_TOK, D_MIX), BF16)],
        carried=[mix], sem=("parallel", "parallel"), name=name)
    return out


def _diff_lambda(lp_ref, lam_init):
    lp = lp_ref[...]
    a = jnp.sum(lp[0:1] * lp[1:2], axis=-1, keepdims=True)
    b = jnp.sum(lp[2:3] * lp[3:4], axis=-1, keepdims=True)
    return jnp.exp(a) - jnp.exp(b) + lam_init


def _diff_scores(q, k_parts):
    return [[_nt_dot(kp[:, c * HD_C:(c + 1) * HD_C], q[:, c * HD_C:(c + 1) * HD_C])
             for kp in k_parts] for c in range(2)]


def _diff_finish(scores, vt_parts, lam, lam_init, sw_t):
    hw = 2 * HD_C
    comps = []
    for c in range(2):
        ss = scores[c]
        m = functools.reduce(jnp.maximum, [jnp.max(s, axis=0, keepdims=True) for s in ss])
        ota = functools.reduce(
            lambda a, b: a + b,
            [jnp.dot(vt, jnp.exp2(s - m).astype(BF16), preferred_element_type=F32)
             for s, vt in zip(ss, vt_parts)])
        comps.append(ota[0:hw] / ota[hw:hw + 1])
    ot = comps[0] - lam * comps[1]
    yt = ot * lax.rsqrt(jnp.mean(ot * ot, axis=0, keepdims=True) + EPS) * sw_t
    return (yt * (1.0 - lam_init)).T


def _with_ones_rows(vt):
    return jnp.concatenate([vt, jnp.ones((ONES_ROWS, vt.shape[1]), F32)], axis=0).astype(BF16)


def _diff_ctx_kernel(q_ref, k_ref, v_ref, lp_ref, sw_ref, o_ref, kc_ref, vc_ref, *, lam_init):
    lam = _diff_lambda(lp_ref, lam_init)
    hw = 2 * HD_C
    vts = {}

    def first(h):
        sl = slice(h * hw, (h + 1) * hw)
        k32 = k_ref[:, sl]
        v32 = v_ref[:, sl]
        kc_ref[:, sl] = k32
        vc_ref[:, sl] = v32
        vts[h] = _with_ones_rows(v32.T)
        return _diff_scores(q_ref[:, sl].astype(BF16), [k32.astype(BF16)])

    def second(h, scores):
        y = _diff_finish(scores, [vts.pop(h)], lam, lam_init, sw_ref[...])
        o_ref[:, h * hw:(h + 1) * hw] = y.astype(o_ref.dtype)

    _pipelined(H_C, first, second)


def _diff_ctx_call(p, lp, sw, lam_init, jl, k_out, v_out, name):
    cache_spec = pl.BlockSpec((None, None, SEQ, ODD_W), lambda b: (b, jl, 0, 0))
    cache_shape = jax.ShapeDtypeStruct((BATCH, N_ODD, SEQ, ODD_W), F32)
    return _aliased_call(
        functools.partial(_diff_ctx_kernel, lam_init=lam_init),
        grid=(BATCH,),
        in_specs=[
            pl.BlockSpec((SEQ, ODD_W), lambda b: (b, 0)),
            pl.BlockSpec((SEQ, ODD_W), lambda b: (b, 1)),
            pl.BlockSpec((SEQ, ODD_W), lambda b: (b, 2)),
            pl.BlockSpec((None, 4, HD_C), lambda b: (jl, 0, 0)),
            pl.BlockSpec((None, 2 * HD_C, ATT_TQ), lambda b: (jl, 0, 0)),
        ],
        args=[p, p, p, lp, sw],
        out_specs=[pl.BlockSpec((SEQ, ODD_W), lambda b: (b, 0)), cache_spec, cache_spec],
        out_shape=[jax.ShapeDtypeStruct((N_TOK, D_MIX), BF16), cache_shape, cache_shape],
        carried=[None, k_out, v_out], sem=("parallel",), name=name)


def _diff_lat_kernel(q_ref, k_ref, v_ref, ck_ref, cv_ref, lp_ref, sw_ref, o_ref, *, lam_init):
    lam = _diff_lambda(lp_ref, lam_init)
    hw = 2 * HD_C
    tq = ATT_TQ
    nq = q_ref.shape[0] // tq
    heads = []
    for g in range(q_ref.shape[1] // hw):
        sl = slice(g * hw, (g + 1) * hw)
        heads.append(([k_ref[:, sl].astype(BF16), ck_ref[:, sl].astype(BF16)],
                      [_with_ones_rows(v_ref[:, sl].T), _with_ones_rows(cv_ref[:, sl].T)]))

    def first(n):
        g, i = divmod(n, nq)
        q = q_ref[i * tq:(i + 1) * tq, g * hw:(g + 1) * hw].astype(BF16)
        return _diff_scores(q, heads[g][0])

    def second(n, scores):
        g, i = divmod(n, nq)
        y = _diff_finish(scores, heads[g][1], lam, lam_init, sw_ref[...])
        o_ref[i * tq:(i + 1) * tq, g * hw:(g + 1) * hw] = y.astype(o_ref.dtype)

    _pipelined(len(heads) * nq, first, second)


def _diff_lat_call(p, ck, cv, lp, sw, lam_init, jl, mix, name):
    hw = 2 * HD_C
    gw = DIFF_LAT_HEADS * hw
    n_g = H_C // DIFF_LAT_HEADS
    s_blk0 = N_CTX_TOK // DEC_SEQ
    cache_spec = pl.BlockSpec((None, None, PAST_LEN, gw), lambda b, h: (b, jl, 0, h))
    (out,) = _aliased_call(
        functools.partial(_diff_lat_kernel, lam_init=lam_init),
        grid=(DEC_BATCH, n_g),
        in_specs=[
            pl.BlockSpec((DEC_SEQ, gw), lambda b, h: (s_blk0 + b, h)),
            pl.BlockSpec((DEC_SEQ, gw), lambda b, h: (s_blk0 + b, n_g + h)),
            pl.BlockSpec((DEC_SEQ, gw), lambda b, h: (s_blk0 + b, 2 * n_g + h)),
            cache_spec, cache_spec,
            pl.BlockSpec((None, 4, HD_C), lambda b, h: (jl, 0, 0)),
            pl.BlockSpec((None, hw, ATT_TQ), lambda b, h: (jl, 0, 0)),
        ],
        args=[p, p, p, ck, cv, lp, sw],
        out_specs=[pl.BlockSpec((DEC_SEQ, gw), lambda b, h: (s_blk0 + b, h))],
        out_shape=[jax.ShapeDtypeStruct((N_TOK, D_MIX), BF16)],
        carried=[mix], sem=("parallel", "parallel"), name=name)
    return out


def _mlp_kernel(x_ref, mix_ref, wo_ref, mods_ref, nw_ref, w1_ref, w2_ref, o_ref,
                h_scr, *, lat):
    i = pl.program_id(0)
    k = pl.program_id(1)
    row = 1 + i if lat else 0

    def mod(a):
        return mods_ref[pl.ds(row, 1), a * D_MODEL:(a + 1) * D_MODEL]

    @pl.when(k == 0)
    def _():
        def first(r):
            return jnp.dot(mix_ref[r * MLP_RC:(r + 1) * MLP_RC, :], wo_ref[...],
                           preferred_element_type=F32)

        def second(r, y):
            rows = slice(r * MLP_RC, (r + 1) * MLP_RC)
            x1 = x_ref[rows, :] + mod(2) * y
            o_ref[rows, :] = x1
            h_scr[rows, :] = _norm_mod(x1, nw_ref[...], mod(4), mod(3)).astype(BF16)

        _pipelined(x_ref.shape[0] // MLP_RC, first, second)

    u = jnp.dot(h_scr[...], w1_ref[...].astype(BF16), preferred_element_type=F32)
    u = jnp.square(jnp.maximum(u, 0.0)).astype(BF16)
    o_ref[...] += mod(5) * jnp.dot(u, w2_ref[...].astype(BF16), preferred_element_type=F32)


def _mlp_call(x_src, x_tile0, lat, mix, wo, jl, mods, li, norm_w, w1, w2, out_prev, out_rows,
              out_tile0, name):
    tm, tk = MLP_TM, MLP_TK
    n_tiles = (N_LAT_TOK if lat else N_CTX_TOK) // tm
    mix_tile0 = N_CTX_TOK // tm if lat else 0
    (out,) = _aliased_call(
        functools.partial(_mlp_kernel, lat=lat),
        grid=(n_tiles, D_FF // tk),
        in_specs=[
            pl.BlockSpec((tm, D_MODEL), lambda i, k: (x_tile0 + i, 0)),
            pl.BlockSpec((tm, D_MIX), lambda i, k: (mix_tile0 + i, 0)),
            pl.BlockSpec((None, D_MIX, D_MODEL), lambda i, k: (jl, 0, 0)),
            pl.BlockSpec((None, MOD_ROWS, N_MOD), lambda i, k: (li, 0, 0)),
            pl.BlockSpec((None, None, 1, D_MODEL), lambda i, k: (li, 1, 0, 0)),
            pl.BlockSpec((None, D_MODEL, tk), lambda i, k: (li, 0, k)),
            pl.BlockSpec((None, tk, D_MODEL), lambda i, k: (li, k, 0)),
        ],
        args=[x_src, mix, wo, mods, norm_w, w1, w2],
        out_specs=[pl.BlockSpec((tm, D_MODEL), lambda i, k: (out_tile0 + i, 0))],
        out_shape=[jax.ShapeDtypeStruct((out_rows, D_MODEL), F32)],
        carried=[out_prev], sem=("parallel", "arbitrary"), name=name,
        scratch_shapes=[pltpu.VMEM((tm, D_MODEL), BF16)])
    return out


def _rope_tables(width):
    t = np.arange(DEC_SEQ)
    half = HEAD_GROUP // 2
    inv = ROPE_BASE ** (-np.arange(0, half, 2, dtype=np.float64) / half)
    ang = np.concatenate([(t // GRID_W)[:, None] * inv, (t % GRID_W)[:, None] * inv], axis=-1)
    cos = np.repeat(np.cos(ang), 2, axis=-1)
    sin = np.repeat(np.sin(ang), 2, axis=-1)
    sign = np.tile(np.array([-1.0, 1.0]), HEAD_GROUP // 2)
    reps = width // HEAD_GROUP
    return (jnp.asarray(np.tile(cos, (1, reps)), F32),
            jnp.asarray(np.tile(sin * sign, (1, reps)), F32))


def _block_diag_ones(n):
    g = np.arange(n) // HEAD_GROUP
    return jnp.asarray(g[:, None] == g[None, :], BF16)


def _tile_row(w, width):
    return jnp.tile(w.astype(F32), width // w.shape[0])[None, :]


def _lambda_init(li):
    return 0.8 - 0.6 * math.exp(-0.3 * li)


def kernel(x_prompt, x_sample, cache_k_swa, cache_v_swa, state_hgrn, cache_k_diff, cache_v_diff, c, c_ctx, norm_w, w_ada, b_ada, w_in_even, w_out_even, hgrn_lb_logits, hgrn_norm_w, swa_qnorm_w, swa_knorm_w, swa_sink, w_in_odd, w_out_odd, diff_qnorm_w, diff_knorm_w, diff_lambda_p, diff_subln_w, w_mlp1, w_mlp2):
    assert PROJ_TM == MLP_TM
    n_ctx_tiles = N_CTX_TOK // PROJ_TM
    x = None
    x_ctx0 = x_prompt.reshape(N_CTX_TOK, D_MODEL)
    x_lat0 = x_sample.reshape(N_LAT_TOK, D_MODEL)
    c_all = jnp.concatenate(
        [c_ctx[None, :], c, jnp.zeros((MOD_ROWS - 1 - DEC_BATCH, D_MODEL), F32)], axis=0)
    mods = _mods_call(c_all, w_ada, b_ada.reshape(DEPTH, 1, N_MOD))

    cos_t, sin_t = _rope_tables(ROPE_TW)
    bd = _block_diag_ones(256)
    tabs_f = _hgrn_tables(False)
    tabs_b = _hgrn_tables(True)
    hgrn_tabs = (tabs_f[0], tabs_f[1], tabs_b[0], tabs_b[1])
    lbl = hgrn_lb_logits.astype(F32).reshape(N_EVEN * 2, F_A)
    norm_w4 = norm_w.astype(F32).reshape(DEPTH, 2, 1, D_MODEL)

    w_out_even_b = w_out_even.astype(BF16)
    w_out_odd_b = w_out_odd.astype(BF16)

    ck_swa = cache_k_swa.reshape(DEC_BATCH, N_EVEN, PAST_LEN, KV_B * HD_B)
    cv_swa = cache_v_swa.reshape(DEC_BATCH, N_EVEN, PAST_LEN, KV_B * HD_B)
    ck_diff = cache_k_diff.reshape(DEC_BATCH, N_ODD, PAST_LEN, ODD_W)
    cv_diff = cache_v_diff.reshape(DEC_BATCH, N_ODD, PAST_LEN, ODD_W)
    hgrn_nw = hgrn_norm_w.astype(F32).reshape(N_EVEN, 1, DV_A)
    sink = swa_sink.astype(F32).reshape(N_EVEN, 1, H_B)
    lam_p = diff_lambda_p.astype(F32)
    assert ATT_TQ == SEQ
    subln = jnp.broadcast_to(diff_subln_w.astype(F32)[:, :, None], (N_ODD, 2 * HD_C, ATT_TQ))

    even_kinds = ("silu_scale", "loggate0", "loggate1", "ident", "silu", "qnorm", "kv")
    odd_kinds = ("qnorm", "qnorm", "knorm", "knorm", "ident", "ident")

    k_swa = v_swa = states = k_diff = v_diff = None
    for li in range(DEPTH):
        j = li // 2
        srcs = ((x_ctx0, 0), (x_lat0, 0)) if li == 0 else ((x, 0), (x, n_ctx_tiles))
        if li % 2 == 0:
            p = None
            for lat, (src, t0) in enumerate(srcs):
                p = _proj_call(src, t0, bool(lat), p, mods, li, norm_w4, w_in_even, j, lbl,
                               _tile_row(swa_qnorm_w[j], PROJ_TN),
                               _tile_row(swa_knorm_w[j], PROJ_TN),
                               cos_t, sin_t, bd, even_kinds, f"proj_even{j}_{lat}")
            mix, states = _hgrn_call(p, hgrn_nw, hgrn_tabs, SEQ, BATCH, 0, None, j, None, states,
                                     f"hgrn_ctx{j}")
            (mix,) = _hgrn_call(p, hgrn_nw, hgrn_tabs, DEC_SEQ, DEC_BATCH, N_CTX_TOK // DEC_SEQ,
                                state_hgrn, j, mix, None, f"hgrn_lat{j}")
            mix, k_swa, v_swa = _swa_ctx_call(p, sink, j, mix, k_swa, v_swa, f"swa_ctx{j}")
            mix = _swa_lat_call(p, ck_swa, cv_swa, sink, j, mix, f"swa_lat{j}")
            wo = w_out_even_b
        else:
            p = None
            for lat, (src, t0) in enumerate(srcs):
                p = _proj_call(src, t0, bool(lat), p, mods, li, norm_w4, w_in_odd, j, lbl,
                               _tile_row(diff_qnorm_w[j], PROJ_TN),
                               _tile_row(diff_knorm_w[j], PROJ_TN),
                               cos_t, sin_t, bd, odd_kinds, f"proj_odd{j}_{lat}")
            lam_init = _lambda_init(li)
            mix, k_diff, v_diff = _diff_ctx_call(p, lam_p, subln, lam_init, j, k_diff, v_diff,
                                                 f"diff_ctx{j}")
            mix = _diff_lat_call(p, ck_diff, cv_diff, lam_p, subln, lam_init, j, mix,
                                 f"diff_lat{j}")
            wo = w_out_odd_b
        last = li == DEPTH - 1
        outs = []
        x_next = None
        for lat, (src, t0) in enumerate(srcs):
            rows = (N_LAT_TOK if lat else N_CTX_TOK) if last else N_TOK
            out_t0 = 0 if last else lat * n_ctx_tiles
            x_next = _mlp_call(src, t0, bool(lat), mix, wo, j, mods, li, norm_w4, w_mlp1, w_mlp2,
                               None if last else x_next, rows, out_t0, f"mlp{li}_{lat}")
            outs.append(x_next)
        x = x_next

    y_prompt = outs[0].reshape(BATCH, SEQ, D_MODEL)
    y_sample = outs[1].reshape(DEC_BATCH, DEC_SEQ, D_MODEL)
    return (y_prompt, y_sample,
            k_swa.reshape(BATCH, N_EVEN, SEQ, KV_B, HD_B),
            v_swa.reshape(BATCH, N_EVEN, SEQ, KV_B, HD_B),
            states,
            k_diff.reshape(BATCH, N_ODD, SEQ, H_C, 2, HD_C),
            v_diff.reshape(BATCH, N_ODD, SEQ, H_C, 2 * HD_C))
```

```python
import functools
import math

import numpy as np
import jax
import jax.numpy as jnp
from jax import lax
from jax.experimental import pallas as pl
from jax.experimental.pallas import tpu as pltpu

F32 = jnp.float32
BF16 = jnp.bfloat16

D_MODEL = 1024
BATCH = 16
SEQ = 256
DEPTH = 4
DEC_BATCH = 4
DEC_SEQ = 1024
PAST_LEN = 512
GRID_W = 64
N_EVEN = (DEPTH + 1) // 2
N_ODD = DEPTH // 2
H_A = 4
DK_A = 128
DV_A = D_MODEL // 2 // H_A
F_A = H_A * DK_A
H_B = 8
KV_B = 2
G_B = H_B // KV_B
HD_B = D_MODEL // 2 // H_B
WINDOW = 128
H_C = 8
HD_C = D_MODEL // (2 * H_C)
D_FF = 4 * D_MODEL
ROPE_BASE = 10000.0
EPS = 1e-6
EVEN_COLS = 3 * F_A + 2 * H_A * DV_A + (H_B + 2 * KV_B) * HD_B
ODD_W = H_C * 2 * HD_C
D_MIX = D_MODEL

N_CTX_TOK = BATCH * SEQ
N_LAT_TOK = DEC_BATCH * DEC_SEQ
N_TOK = N_CTX_TOK + N_LAT_TOK
MOD_ROWS = 8
N_MOD = 6 * D_MODEL

HEAD_GROUP = 64
HGRN_CHUNK = 128
HGRN_LEVELS = 7
HGRN_SPLIT = 3
LOG2_E = math.log2(math.e)
VMEM_LIMIT = 48 * 1024 * 1024

PROJ_TM = 1024
PROJ_TN = 512
ROPE_TW = 128
PROJ_TILES_CTX = 2
PROJ_TILES_LAT = 1
PROJ_RC = 128
MLP_TM = 1024
MLP_TK = 1024
MLP_RC = 256
ADA_TN = 1536
ATT_TQ = 256
CTX_SEQS = 2
DIFF_LAT_HEADS = 2
ONES_ROWS = 16


def _silu(x):
    return x * jax.nn.sigmoid(x)


def _nt_dot(a, b):
    return lax.dot_general(a, b, (((1,), (1,)), ((), ())), preferred_element_type=F32)


def _pipelined(n, first, second):
    cur = first(0)
    for i in range(n):
        nxt = first(i + 1) if i + 1 < n else None
        second(i, cur)
        cur = nxt


def _params(sem):
    return pltpu.CompilerParams(dimension_semantics=sem, vmem_limit_bytes=VMEM_LIMIT)


def _aliased_call(kernel, *, grid, in_specs, args, out_specs, out_shape, carried, sem, name,
                  scratch_shapes=()):
    n_in = len(args)
    extra = [buf for buf in carried if buf is not None]
    aliases = {}
    for k, buf in enumerate(carried):
        if buf is not None:
            aliases[n_in + len(aliases)] = k
    n_extra = len(extra)

    def body(*refs):
        kernel(*refs[:n_in], *refs[n_in + n_extra:])

    return pl.pallas_call(
        body,
        grid=grid,
        in_specs=list(in_specs) + [pl.BlockSpec(memory_space=pl.ANY)] * n_extra,
        out_specs=out_specs,
        out_shape=out_shape,
        input_output_aliases=aliases,
        scratch_shapes=list(scratch_shapes),
        compiler_params=_params(sem),
        name=name,
    )(*args, *extra)


def _mods_kernel(c_ref, w_ref, b_ref, o_ref):
    s = _silu(c_ref[...]).astype(BF16)
    o_ref[...] = jnp.dot(s, w_ref[...].astype(BF16), preferred_element_type=F32) + b_ref[...]


def _mods_call(c_all, w_ada, b_ada):
    return pl.pallas_call(
        _mods_kernel,
        grid=(DEPTH, N_MOD // ADA_TN),
        in_specs=[
            pl.BlockSpec((MOD_ROWS, D_MODEL), lambda l, j: (0, 0)),
            pl.BlockSpec((None, D_MODEL, ADA_TN), lambda l, j: (l, 0, j)),
            pl.BlockSpec((None, 1, ADA_TN), lambda l, j: (l, 0, j)),
        ],
        out_specs=pl.BlockSpec((None, MOD_ROWS, ADA_TN), lambda l, j: (l, 0, j)),
        out_shape=jax.ShapeDtypeStruct((DEPTH, MOD_ROWS, N_MOD), F32),
        compiler_params=_params(("parallel", "parallel")),
        name="ada_mods",
    )(c_all, w_ada, b_ada)


def _norm_mod(x, nw, sc, sh):
    ms = jnp.mean(x * x, axis=-1, keepdims=True)
    return (x * lax.rsqrt(ms + EPS) * nw) * (1.0 + sc) + sh


def _group_rms(y, w_t, bd_ref):
    yy = (y * y).astype(BF16)
    bw = bd_ref.shape[0]
    parts = [jnp.dot(yy[:, s:s + bw], bd_ref[...], preferred_element_type=F32)
             for s in range(0, y.shape[1], bw)]
    ss = parts[0] if len(parts) == 1 else jnp.concatenate(parts, axis=1)
    return y * lax.rsqrt(ss * (1.0 / HEAD_GROUP) + EPS) * w_t


def _rope(y, cos, sin):
    n = y.shape[1]
    lane = lax.broadcasted_iota(jnp.int32, y.shape, 1)
    nxt = pltpu.roll(y, n - 1, axis=1)
    prv = pltpu.roll(y, 1, axis=1)
    swapped = jnp.where((lane & 1) == 0, nxt, prv)
    return y * cos + swapped * sin


def _lower_bounds(lbl_ref, jl):
    rows = [lbl_ref[pl.ds(2 * m, 2), :] for m in range(N_EVEN)]
    mx = functools.reduce(jnp.maximum, rows)
    es = [jnp.exp(r - mx) for r in rows]
    den = functools.reduce(lambda a, b: a + b, es)
    sm = [e / den for e in es]
    cs = sm[0]
    for m in range(1, jl + 1):
        cs = cs + sm[m]
    return cs - sm[0]


def _proj_kernel(x_ref, mods_ref, nw_ref, w_ref, wkv_ref, lbl_ref, qn_ref, kn_ref, cos_ref,
                 sin_ref, bd_ref, o_ref, *rest, kinds, jl, lat, cache_tiles, tiles):
    cache_refs, (h_scr, w_scr) = rest[:-2], rest[-2:]
    j = pl.program_id(0)
    i = pl.program_id(1)
    tm, tn = PROJ_TM, o_ref.shape[1]
    kvw = 2 * KV_B * HD_B

    @pl.when(j == 0)
    def _():
        for t in range(tiles):
            tile = tiles * i + t
            row = 1 + tile if lat else 0
            sh = mods_ref[pl.ds(row, 1), 0:D_MODEL]
            sc = mods_ref[pl.ds(row, 1), D_MODEL:2 * D_MODEL]
            h_scr[tile] = _norm_mod(x_ref[t * tm:(t + 1) * tm, :], nw_ref[...], sc,
                                    sh).astype(BF16)

    def finish(kind, y, rows):
        if kind == "silu_scale":
            return _silu(y) * (DK_A ** -0.5)
        if kind in ("loggate0", "loggate1"):
            d = int(kind[-1])
            lb = _lower_bounds(lbl_ref, jl)[d:d + 1, :]
            return jnp.log2(lb + (1.0 - lb) * jax.nn.sigmoid(y))
        if kind == "ident":
            return y
        if kind == "silu":
            return _silu(y)
        if kind in ("qnorm", "knorm"):
            w_t = qn_ref[...] if kind == "qnorm" else kn_ref[...]
            r = _group_rms(y, w_t, bd_ref)
            if lat:
                r = _rope(r, jnp.tile(cos_ref[rows, :], (1, tn // ROPE_TW)),
                          jnp.tile(sin_ref[rows, :], (1, tn // ROPE_TW)))
            return r * (HEAD_GROUP ** -0.5 * LOG2_E) if kind == "qnorm" else r
        if kind == "kv":
            kn = _group_rms(y, kn_ref[:, 0:kvw], bd_ref)
            if lat:
                kn = _rope(kn, jnp.tile(cos_ref[rows, :], (1, kvw // ROPE_TW)),
                           jnp.tile(sin_ref[rows, :], (1, kvw // ROPE_TW)))
            lane = lax.broadcasted_iota(jnp.int32, y.shape, 1)
            return jnp.where(lane < KV_B * HD_B, kn, y)
        raise ValueError(kind)

    def run(kind, jj):
        @pl.when(i == 0)
        def _():
            if kind == "kv":
                w_scr[:, 0:kvw] = wkv_ref[...].astype(BF16)
            else:
                w_scr[...] = w_ref[...].astype(BF16)

        per_tile = tm // PROJ_RC

        def first(r):
            t, c = divmod(r, per_tile)
            w = w_scr[:, 0:kvw] if kind == "kv" else w_scr[...]
            return jnp.dot(h_scr[tiles * i + t, c * PROJ_RC:(c + 1) * PROJ_RC, :], w,
                           preferred_element_type=F32)

        def second(r, y):
            c = r % per_tile
            rows = slice(r * PROJ_RC, (r + 1) * PROJ_RC)
            seq_rows = slice(c * PROJ_RC, (c + 1) * PROJ_RC)
            val = finish(kind, y, seq_rows)
            if kind == "kv":
                o_ref[rows, 0:kvw] = val.astype(o_ref.dtype)
                o_ref[rows, kvw:tn] = jnp.zeros((PROJ_RC, tn - kvw), o_ref.dtype)
            else:
                o_ref[rows, :] = val.astype(o_ref.dtype)
            if jj in cache_tiles:
                seq, off = divmod(r * PROJ_RC, SEQ)
                cache_refs[cache_tiles[jj][0]][seq, off:off + PROJ_RC, :] = val

        _pipelined(tiles * per_tile, first, second)

    for jj, kind in enumerate(kinds):
        pl.when(j == jj)(functools.partial(run, kind, jj))


def _proj_call(x_src, x_tile0, lat, p_prev, mods, li, norm_w, w, jl, lbl, qn_t, kn_t, cos_t, sin_t,
               bd, kinds, name, out_dtype=F32, caches=None):
    tm, tn = PROJ_TM, PROJ_TN
    assert tm == DEC_SEQ
    n_ctx_tiles = N_CTX_TOK // tm
    n_tiles = (N_LAT_TOK if lat else N_CTX_TOK) // tm
    tile0 = n_ctx_tiles if lat else 0
    if not lat:
        cos_t = sin_t = jnp.zeros((8, 128), F32)
    n_main = sum(1 for k in kinds if k != "kv")
    n_cols = tn * len(kinds)
    kvw = 2 * KV_B * HD_B
    kv_blk = (n_main * tn) // kvw if "kv" in kinds else 0
    const = lambda j, i: (0, 0)
    tiles = PROJ_TILES_LAT if lat or caches is not None else PROJ_TILES_CTX
    n_blk = n_tiles // tiles
    blk = tiles * tm
    x_blk0, out_blk0 = x_tile0 // tiles, tile0 // tiles
    assert x_tile0 % tiles == 0 and tile0 % tiles == 0
    out_specs = [pl.BlockSpec((blk, tn), lambda j, i: (out_blk0 + i, j))]
    out_shape = [jax.ShapeDtypeStruct((N_TOK, n_cols), out_dtype)]
    carried = [p_prev]
    cache_tiles = {}
    if caches is not None:
        assert not lat
        j0, bufs = caches
        n_layers = N_ODD
        for c, buf in enumerate(bufs):
            lo, hi = j0 + 2 * c, j0 + 2 * c + 1
            cache_tiles[lo], cache_tiles[hi] = (c, 0), (c, 1)

            def cache_idx(j, i, lo=lo, hi=hi):
                row_blk = jnp.where(j < lo, 0, jnp.where(j > hi, n_blk - 1, i))
                return (row_blk, jl, 0, jnp.clip(j - lo, 0, 1))

            out_specs.append(pl.BlockSpec((blk // SEQ, None, SEQ, tn), cache_idx))
            out_shape.append(jax.ShapeDtypeStruct((BATCH, n_layers, SEQ, 2 * tn), F32))
            carried.append(buf)
    return _aliased_call(
        functools.partial(_proj_kernel, kinds=kinds, jl=jl, lat=lat, cache_tiles=cache_tiles,
                          tiles=tiles),
        grid=(len(kinds), n_blk),
        in_specs=[
            pl.BlockSpec((blk, D_MODEL),
                         lambda j, i: (x_blk0 + jnp.where(j == 0, i, n_blk - 1), 0)),
            pl.BlockSpec((None, MOD_ROWS, N_MOD), lambda j, i: (li, 0, 0)),
            pl.BlockSpec((None, None, 1, D_MODEL), lambda j, i: (li, 0, 0, 0)),
            pl.BlockSpec((None, D_MODEL, tn), lambda j, i: (jl, 0, jnp.minimum(j, n_main - 1))),
            pl.BlockSpec((None, D_MODEL, kvw), lambda j, i: (jl, 0, kv_blk)),
            pl.BlockSpec(lbl.shape, const),
            pl.BlockSpec((1, tn), const),
            pl.BlockSpec((1, tn), const),
            pl.BlockSpec(cos_t.shape, const),
            pl.BlockSpec(sin_t.shape, const),
            pl.BlockSpec(bd.shape, const),
        ],
        args=[x_src, mods, norm_w, w, w, lbl, qn_t, kn_t, cos_t, sin_t, bd],
        out_specs=out_specs, out_shape=out_shape, carried=carried,
        sem=("arbitrary", "arbitrary"), name=name,
        scratch_shapes=[pltpu.VMEM((n_tiles, tm, D_MODEL), BF16),
                        pltpu.VMEM((D_MODEL, tn), BF16)])


def _hgrn_tables(rev):
    c = HGRN_CHUNK
    t = np.arange(c)
    w = (t[None, :] <= t[:, None]) if not rev else (t[None, :] >= t[:, None])
    ws = np.concatenate([w.astype(np.float32)] * HGRN_SPLIT, axis=1)
    x = t[:, None] ^ t[None, :]
    lv = np.where(x > 0, np.floor(np.log2(np.maximum(x, 1))).astype(np.int32), HGRN_LEVELS)
    causal = (t[None, :] < t[:, None]) if not rev else (t[None, :] > t[:, None])
    lv = np.where(causal | (x == 0), lv, -1).astype(np.int32)
    return jnp.asarray(ws, BF16), jnp.asarray(lv)


def _hgrn_level_exponents(cum, rev):
    c = HGRN_CHUNK
    sub_rows = 8
    c3 = cum.reshape(c // sub_rows, sub_rows, DK_A)
    sub = lax.broadcasted_iota(jnp.int32, c3.shape, 1)
    out = []
    for l in range(1, HGRN_LEVELS):
        hb = 1 << l
        if 2 * hb <= sub_rows:
            r = None
            for b0 in range(0, sub_rows, 2 * hb):
                idx = b0 + (hb if rev else hb - 1)
                rk = c3[:, idx:idx + 1, :]
                r = rk if r is None else jnp.where(sub < b0, r, rk)
            d = c3 - r
            bit = (sub & hb) != 0
            q_role = jnp.logical_not(bit) if rev else bit
            out.append(jnp.where(q_role, d, -d).reshape(c, DK_A))
        else:
            pieces = []
            for b0 in range(0, c, 2 * hb):
                mid = b0 + hb
                ridx = mid if rev else mid - 1
                r = cum[ridx:ridx + 1, :]
                lo = cum[b0:mid]
                hi = cum[mid:b0 + 2 * hb]
                pieces += [lo - r, r - hi] if rev else [r - lo, hi - r]
            out.append(jnp.concatenate(pieces, axis=0))
    return out


def _hgrn_level_operands(l, q, k, f, z, rev, row):
    c = HGRN_CHUNK
    hb = 1 << l
    if hb >= 8:
        zero = jnp.zeros((hb, DK_A), F32)
        qparts, kparts = [], []
        for b0 in range(0, c, 2 * hb):
            lo, hi = slice(b0, b0 + hb), slice(b0 + hb, b0 + 2 * hb)
            if rev:
                qparts += [q[lo] * z[lo], zero]
                kparts += [zero, k[hi] * z[hi]]
            else:
                qparts += [zero, q[hi] * z[hi]]
                kparts += [k[lo] * z[lo], zero]
        return (jnp.concatenate(qparts, axis=0).astype(BF16),
                jnp.concatenate(kparts, axis=0).astype(BF16))
    bit = ((row >> l) & 1) == 1
    q_role = jnp.logical_not(bit) if rev else bit
    ql = jnp.where(q_role, q * (f if l == 0 else z), 0.0).astype(BF16)
    kl = jnp.where(q_role, 0.0, k if l == 0 else k * z).astype(BF16)
    return ql, kl


def _hgrn_chunks(chains):
    c = HGRN_CHUNK
    row = lax.broadcasted_iota(jnp.int32, (c, DK_A), 0)
    cums = []
    for q, g, v, st, w_ref, lv_ref, rev in chains:
        terms = []
        rem = g
        for _ in range(HGRN_SPLIT):
            term = rem.astype(BF16)
            terms.append(term)
            rem = rem - term.astype(F32)
        cums.append(jnp.dot(w_ref[...], jnp.concatenate(terms, axis=0),
                            preferred_element_type=F32))
    work = []
    for (q, g, v, st, w_ref, lv_ref, rev), cum in zip(chains, cums):
        f = jnp.exp2(g)
        k = 1.0 - f
        last = 0 if rev else c - 1
        total_e = cum[last:last + 1, :]
        qd = (q * jnp.exp2(cum)).astype(BF16)
        kd = (k * jnp.exp2(total_e - cum)).astype(BF16)
        o = _nt_dot(qd, st.astype(BF16))
        st_new = (st * jnp.exp2(total_e)
                  + jnp.dot(v.T.astype(BF16), kd, preferred_element_type=F32))
        zs = [None] + [jnp.exp2(e) for e in _hgrn_level_exponents(cum, rev)]
        work.append((k, f, zs, o, st_new))
    accs = [None] * len(chains)
    for l in reversed(range(HGRN_LEVELS)):
        for i, ((q, g, v, st, w_ref, lv_ref, rev), (k, f, zs, o, st_new)) in enumerate(
                zip(chains, work)):
            ql, kl = _hgrn_level_operands(l, q, k, f, zs[l], rev, row)
            a_l = _nt_dot(ql, kl)
            accs[i] = a_l if accs[i] is None else jnp.where(lv_ref[...] == l, a_l, accs[i])
    outs = []
    for (q, g, v, st, w_ref, lv_ref, rev), (k, f, zs, o, st_new), a in zip(chains, work, accs):
        a = jnp.where(lv_ref[...] == HGRN_LEVELS, jnp.sum(q * k, axis=-1, keepdims=True), a)
        outs.append((o + jnp.dot(a.astype(BF16), v.astype(BF16), preferred_element_type=F32),
                     st_new))
    return outs


def _hgrn_kernel(*refs, n_chunks, has_init, emit_state):
    refs = list(refs)
    q_ref, gf_ref, gb_ref, v_ref, sg_ref, nw_ref, wf_ref, wb_ref, lvf_ref, lvb_ref = refs[:10]
    pos = 10
    s0_ref = None
    if has_init:
        s0_ref = refs[pos]
        pos += 1
    o_ref = refs[pos]
    pos += 1
    so_ref = None
    if emit_state:
        so_ref = refs[pos]
        pos += 1
    of_scr, ob_scr, st_scr = refs[pos:pos + 3]

    for d in range(2):
        for h in range(H_A):
            if has_init:
                st_scr[d, h] = s0_ref[d, h].T
            else:
                st_scr[d, h] = jnp.zeros((DV_A, DK_A), F32)

    def body(c, carry):
        chains, dests = [], []
        for h in range(H_A):
            cols = slice(h * DK_A, (h + 1) * DK_A)
            for d, (g_ref, w_ref, lv_ref, scr) in enumerate(
                    ((gf_ref, wf_ref, lvf_ref, of_scr), (gb_ref, wb_ref, lvb_ref, ob_scr))):
                cc = c if d == 0 else n_chunks - 1 - c
                r0 = pl.multiple_of(cc * HGRN_CHUNK, HGRN_CHUNK)
                rows = pl.ds(r0, HGRN_CHUNK)
                chains.append((q_ref[rows, cols], g_ref[rows, cols], v_ref[rows, cols],
                               st_scr[d, h], w_ref, lv_ref, d == 1))
                dests.append((scr, rows, cols, d, h))
        for (o, st), (scr, rows, cols, d, h) in zip(_hgrn_chunks(chains), dests):
            st_scr[d, h] = st
            scr[rows, cols] = o
        return carry

    lax.fori_loop(0, n_chunks, body, 0)
    for h in range(H_A):
        cols = slice(h * DV_A, (h + 1) * DV_A)
        o = of_scr[:, cols] + ob_scr[:, cols]
        y = o * lax.rsqrt(jnp.mean(o * o, axis=-1, keepdims=True) + EPS) * nw_ref[...]
        o_ref[:, cols] = (y * sg_ref[:, cols]).astype(o_ref.dtype)
    if emit_state:
        for d in range(2):
            for h in range(H_A):
                so_ref[d, h] = st_scr[d, h].T


def _hgrn_call(p, nw, tabs, seq_len, n_seq, row_blk0, s0, jl, mix, state_out, name):
    wf, lvf, wb, lvb = tabs
    has_init = s0 is not None
    emit_state = s0 is None
    const = lambda b: (0, 0)
    blk = (seq_len, F_A)
    state_spec = pl.BlockSpec((None, None, 2, H_A, DK_A, DV_A), lambda b: (b, jl, 0, 0, 0, 0))
    in_specs = [pl.BlockSpec(blk, (lambda b, part=part: (row_blk0 + b, part))) for part in range(5)]
    in_specs += [
        pl.BlockSpec((None, 1, DV_A), lambda b: (jl, 0, 0)),
        pl.BlockSpec(wf.shape, const), pl.BlockSpec(wb.shape, const),
        pl.BlockSpec(lvf.shape, const), pl.BlockSpec(lvb.shape, const),
    ]
    args = [p, p, p, p, p, nw, wf, wb, lvf, lvb]
    if has_init:
        in_specs.append(state_spec)
        args.append(s0)
    out_shape = [jax.ShapeDtypeStruct((N_TOK, D_MIX), BF16)]
    out_specs = [pl.BlockSpec((seq_len, H_A * DV_A), lambda b: (row_blk0 + b, 0))]
    carried = [mix]
    if emit_state:
        out_shape.append(jax.ShapeDtypeStruct((BATCH, N_EVEN, 2, H_A, DK_A, DV_A), F32))
        out_specs.append(state_spec)
        carried.append(state_out)
    return _aliased_call(
        functools.partial(_hgrn_kernel, n_chunks=seq_len // HGRN_CHUNK, has_init=has_init,
                          emit_state=emit_state),
        grid=(n_seq,), in_specs=in_specs, args=args, out_specs=out_specs,
        out_shape=out_shape, carried=carried, sem=("parallel",), name=name,
        scratch_shapes=[pltpu.VMEM((seq_len, H_A * DV_A), F32),
                        pltpu.VMEM((seq_len, H_A * DV_A), F32),
                        pltpu.VMEM((2, H_A, DV_A, DK_A), F32)])


def _swa_ctx_kernel(q_ref, kv_ref, sink_ref, o_ref, kc_ref, vc_ref):
    for s in range(CTX_SEQS):
        rows = slice(s * SEQ, (s + 1) * SEQ)
        kv = kv_ref[rows, :]
        k32 = kv[:, 0:KV_B * HD_B]
        v32 = kv[:, KV_B * HD_B:2 * KV_B * HD_B]
        kc_ref[s] = k32
        vc_ref[s] = v32
        k = k32.astype(BF16)
        vt32 = v32.T
        vts = [_with_ones_rows(vt32[n * HD_B:(n + 1) * HD_B]) for n in range(KV_B)]
        q = q_ref[rows, :].astype(BF16)
        ksl = [slice((h // G_B) * HD_B, (h // G_B + 1) * HD_B) for h in range(H_B)]
        sts = [_nt_dot(k[:, ksl[h]], q[:, h * HD_B:(h + 1) * HD_B]) for h in range(H_B)]
        ps, sinks = [], []
        for h in range(H_B):
            sink = sink_ref[0:1, h:h + 1] * LOG2_E
            m = jnp.maximum(jnp.max(sts[h], axis=0, keepdims=True), sink)
            ps.append(jnp.exp2(sts[h] - m).astype(BF16))
            sinks.append(jnp.exp2(sink - m))
        outs = []
        for h in range(H_B):
            ota = jnp.dot(vts[h // G_B], ps[h], preferred_element_type=F32)
            outs.append(ota[0:HD_B] / (ota[HD_B:HD_B + 1] + sinks[h]))
        o_ref[rows, :] = jnp.concatenate(outs, axis=0).T.astype(o_ref.dtype)


def _swa_ctx_call(p, sink, jl, mix, k_out, v_out, name):
    qcol = (3 * F_A + 2 * H_A * DV_A) // PROJ_TN
    rows = CTX_SEQS * SEQ
    cache_spec = pl.BlockSpec((CTX_SEQS, None, SEQ, KV_B * HD_B), lambda b: (b, jl, 0, 0))
    cache_shape = jax.ShapeDtypeStruct((BATCH, N_EVEN, SEQ, KV_B * HD_B), F32)
    return _aliased_call(
        _swa_ctx_kernel,
        grid=(BATCH // CTX_SEQS,),
        in_specs=[
            pl.BlockSpec((rows, H_B * HD_B), lambda b: (b, qcol)),
            pl.BlockSpec((rows, PROJ_TN), lambda b: (b, qcol + 1)),
            pl.BlockSpec((None, 1, H_B), lambda b: (jl, 0, 0)),
        ],
        args=[p, p, sink],
        out_specs=[pl.BlockSpec((rows, H_B * HD_B), lambda b: (b, 1)), cache_spec, cache_spec],
        out_shape=[jax.ShapeDtypeStruct((N_TOK, D_MIX), BF16), cache_shape, cache_shape],
        carried=[mix, k_out, v_out], sem=("parallel",), name=name)


def _swa_lat_kernel(q_ref, kv_ref, ck_ref, cv_ref, sink_ref, o_ref):
    qi = pl.program_id(1)
    tq = q_ref.shape[0]
    span = tq + 2 * WINDOW
    ws = pl.multiple_of(jnp.clip(qi * tq - WINDOW, 0, DEC_SEQ - span), WINDOW)
    kvw = kv_ref[pl.ds(ws, span), :]
    kw = kvw[:, 0:KV_B * HD_B].astype(BF16)
    vwt = kvw[:, KV_B * HD_B:2 * KV_B * HD_B].T.astype(BF16)
    kc = ck_ref[...].astype(BF16)
    vct = cv_ref[...].T.astype(BF16)
    q = q_ref[...].astype(BF16)
    t_k = ws + lax.broadcasted_iota(jnp.int32, (span, tq), 0)
    t_q = qi * tq + lax.broadcasted_iota(jnp.int32, (span, tq), 1)
    valid = jnp.abs(t_q - t_k) <= WINDOW
    outs = []

    def first(h):
        qh = q[:, h * HD_B:(h + 1) * HD_B]
        ksl = slice((h // G_B) * HD_B, (h // G_B + 1) * HD_B)
        return _nt_dot(kw[:, ksl], qh), _nt_dot(kc[:, ksl], qh)

    def second(h, scores):
        ksl = slice((h // G_B) * HD_B, (h // G_B + 1) * HD_B)
        s_w = jnp.where(valid, scores[0], -jnp.inf)
        s_c = scores[1]
        sink = sink_ref[0:1, h:h + 1] * LOG2_E
        m = jnp.maximum(jnp.maximum(jnp.max(s_w, axis=0, keepdims=True),
                                    jnp.max(s_c, axis=0, keepdims=True)), sink)
        p_w = jnp.exp2(s_w - m)
        p_c = jnp.exp2(s_c - m)
        den = (jnp.sum(p_w, axis=0, keepdims=True) + jnp.sum(p_c, axis=0, keepdims=True)
               + jnp.exp2(sink - m))
        ot = (jnp.dot(vwt[ksl, :], p_w.astype(BF16), preferred_element_type=F32)
              + jnp.dot(vct[ksl, :], p_c.astype(BF16), preferred_element_type=F32))
        outs.append(ot / den)

    _pipelined(H_B, first, second)
    o_ref[...] = jnp.concatenate(outs, axis=0).T.astype(o_ref.dtype)


def _swa_lat_call(p, ck, cv, sink, jl, mix, name):
    tq = ATT_TQ
    qcol = (3 * F_A + 2 * H_A * DV_A) // PROJ_TN
    nq = DEC_SEQ // tq
    q_blk0 = N_CTX_TOK // tq
    s_blk0 = N_CTX_TOK // DEC_SEQ
    cache_spec = pl.BlockSpec((None, None, PAST_LEN, KV_B * HD_B), lambda b, i: (b, jl, 0, 0))
    (out,) = _aliased_call(
        _swa_lat_kernel,
        grid=(DEC_BATCH, nq),
        in_specs=[
            pl.BlockSpec((tq, H_B * HD_B), lambda b, i: (q_blk0 + b * nq + i, qcol)),
            pl.BlockSpec((DEC_SEQ, PROJ_TN), lambda b, i: (s_blk0 + b, qcol + 1)),
            cache_spec, cache_spec,
            pl.BlockSpec((None, 1, H_B), lambda b, i: (jl, 0, 0)),
        ],
        args=[p, p, ck, cv, sink],
        out_specs=[pl.BlockSpec((tq, H_B * HD_B), lambda b, i: (q_blk0 + b * nq + i, 1))],
        out_shape=[jax.ShapeDtypeStruct((N_TOK, D_MIX), BF16)],
        carried=[mix], sem=("parallel", "parallel"), name=name)
    return out


def _diff_lambda(lp_ref, lam_init):
    lp = lp_ref[...]
    a = jnp.sum(lp[0:1] * lp[1:2], axis=-1, keepdims=True)
    b = jnp.sum(lp[2:3] * lp[3:4], axis=-1, keepdims=True)
    return jnp.exp(a) - jnp.exp(b) + lam_init


def _diff_scores(q, k_parts):
    return [[_nt_dot(kp[:, c * HD_C:(c + 1) * HD_C], q[:, c * HD_C:(c + 1) * HD_C])
             for kp in k_parts] for c in range(2)]


def _diff_finish(scores, vt_parts, lam, lam_init, sw_t):
    hw = 2 * HD_C
    comps = []
    for c in range(2):
        ss = scores[c]
        m = functools.reduce(jnp.maximum, [jnp.max(s, axis=0, keepdims=True) for s in ss])
        ota = functools.reduce(
            lambda a, b: a + b,
            [jnp.dot(vt, jnp.exp2(s - m).astype(BF16), preferred_element_type=F32)
             for s, vt in zip(ss, vt_parts)])
        comps.append(ota[0:hw] / ota[hw:hw + 1])
    ot = comps[0] - lam * comps[1]
    yt = ot * lax.rsqrt(jnp.mean(ot * ot, axis=0, keepdims=True) + EPS) * sw_t
    return (yt * (1.0 - lam_init)).T


def _with_ones_rows(vt):
    return jnp.concatenate([vt, jnp.ones((ONES_ROWS, vt.shape[1]), F32)], axis=0).astype(BF16)


def _diff_ctx_kernel(q_ref, k_ref, v_ref, lp_ref, sw_ref, o_ref, *, lam_init):
    lam = _diff_lambda(lp_ref, lam_init)
    hw = 2 * HD_C
    vts = {}

    def first(n):
        s, h = divmod(n, H_C)
        rows = slice(s * SEQ, (s + 1) * SEQ)
        sl = slice(h * hw, (h + 1) * hw)
        vts[n] = _with_ones_rows(v_ref[rows, sl].astype(F32).T)
        return _diff_scores(q_ref[rows, sl].astype(BF16), [k_ref[rows, sl].astype(BF16)])

    def second(n, scores):
        s, h = divmod(n, H_C)
        y = _diff_finish(scores, [vts.pop(n)], lam, lam_init, sw_ref[...])
        o_ref[s * SEQ:(s + 1) * SEQ, h * hw:(h + 1) * hw] = y.astype(o_ref.dtype)

    _pipelined(CTX_SEQS * H_C, first, second)


def _diff_ctx_call(p, lp, sw, lam_init, jl, name):
    rows = CTX_SEQS * SEQ
    (out,) = _aliased_call(
        functools.partial(_diff_ctx_kernel, lam_init=lam_init),
        grid=(BATCH // CTX_SEQS,),
        in_specs=[
            pl.BlockSpec((rows, ODD_W), lambda b: (b, 0)),
            pl.BlockSpec((rows, ODD_W), lambda b: (b, 1)),
            pl.BlockSpec((rows, ODD_W), lambda b: (b, 2)),
            pl.BlockSpec((None, 4, HD_C), lambda b: (jl, 0, 0)),
            pl.BlockSpec((None, 2 * HD_C, ATT_TQ), lambda b: (jl, 0, 0)),
        ],
        args=[p, p, p, lp, sw],
        out_specs=[pl.BlockSpec((rows, ODD_W), lambda b: (b, 0))],
        out_shape=[jax.ShapeDtypeStruct((N_TOK, D_MIX), BF16)],
        carried=[None], sem=("parallel",), name=name)
    return out


def _diff_lat_kernel(q_ref, k_ref, v_ref, ck_ref, cv_ref, lp_ref, sw_ref, o_ref, *, lam_init):
    lam = _diff_lambda(lp_ref, lam_init)
    hw = 2 * HD_C
    tq = ATT_TQ
    nq = q_ref.shape[0] // tq
    heads = []
    for g in range(q_ref.shape[1] // hw):
        sl = slice(g * hw, (g + 1) * hw)
        heads.append(([k_ref[:, sl].astype(BF16), ck_ref[:, sl].astype(BF16)],
                      [_with_ones_rows(v_ref[:, sl].astype(F32).T),
                       _with_ones_rows(cv_ref[:, sl].T)]))

    def first(n):
        g, i = divmod(n, nq)
        q = q_ref[i * tq:(i + 1) * tq, g * hw:(g + 1) * hw].astype(BF16)
        return _diff_scores(q, heads[g][0])

    def second(n, scores):
        g, i = divmod(n, nq)
        y = _diff_finish(scores, heads[g][1], lam, lam_init, sw_ref[...])
        o_ref[i * tq:(i + 1) * tq, g * hw:(g + 1) * hw] = y.astype(o_ref.dtype)

    _pipelined(len(heads) * nq, first, second)


def _diff_lat_call(p, ck, cv, lp, sw, lam_init, jl, mix, name):
    hw = 2 * HD_C
    gw = DIFF_LAT_HEADS * hw
    n_g = H_C // DIFF_LAT_HEADS
    s_blk0 = N_CTX_TOK // DEC_SEQ
    cache_spec = pl.BlockSpec((None, None, PAST_LEN, gw), lambda b, h: (b, jl, 0, h))
    (out,) = _aliased_call(
        functools.partial(_diff_lat_kernel, lam_init=lam_init),
        grid=(DEC_BATCH, n_g),
        in_specs=[
            pl.BlockSpec((DEC_SEQ, gw), lambda b, h: (s_blk0 + b, h)),
            pl.BlockSpec((DEC_SEQ, gw), lambda b, h: (s_blk0 + b, n_g + h)),
            pl.BlockSpec((DEC_SEQ, gw), lambda b, h: (s_blk0 + b, 2 * n_g + h)),
            cache_spec, cache_spec,
            pl.BlockSpec((None, 4, HD_C), lambda b, h: (jl, 0, 0)),
            pl.BlockSpec((None, hw, ATT_TQ), lambda b, h: (jl, 0, 0)),
        ],
        args=[p, p, p, ck, cv, lp, sw],
        out_specs=[pl.BlockSpec((DEC_SEQ, gw), lambda b, h: (s_blk0 + b, h))],
        out_shape=[jax.ShapeDtypeStruct((N_TOK, D_MIX), BF16)],
        carried=[mix], sem=("parallel", "parallel"), name=name)
    return out


def _mlp_kernel(x_ref, mix_ref, wo_ref, mods_ref, nw_ref, w1_ref, w2_ref, o_ref,
                h_scr, *, group):
    i = pl.program_id(0)
    k = pl.program_id(1)
    n_ctx_tiles = N_CTX_TOK // x_ref.shape[0]
    row = {"ctx": 0, "lat": 1 + i,
           "all": jnp.where(i >= n_ctx_tiles, 1 + i - n_ctx_tiles, 0)}[group]

    def mod(a):
        return mods_ref[pl.ds(row, 1), a * D_MODEL:(a + 1) * D_MODEL]

    @pl.when(k == 0)
    def _():
        def first(r):
            return jnp.dot(mix_ref[r * MLP_RC:(r + 1) * MLP_RC, :], wo_ref[...],
                           preferred_element_type=F32)

        def second(r, y):
            rows = slice(r * MLP_RC, (r + 1) * MLP_RC)
            x1 = x_ref[rows, :] + mod(2) * y
            o_ref[rows, :] = x1
            h_scr[rows, :] = _norm_mod(x1, nw_ref[...], mod(4), mod(3)).astype(BF16)

        _pipelined(x_ref.shape[0] // MLP_RC, first, second)

    u = jnp.dot(h_scr[...], w1_ref[...].astype(BF16), preferred_element_type=F32)
    u = jnp.square(jnp.maximum(u, 0.0)).astype(BF16)
    o_ref[...] += mod(5) * jnp.dot(u, w2_ref[...].astype(BF16), preferred_element_type=F32)


def _mlp_call(x_src, x_tile0, lat, mix, wo, jl, mods, li, norm_w, w1, w2, out_prev, out_rows,
              out_tile0, name):
    tm, tk = MLP_TM, MLP_TK
    group = {False: "ctx", True: "lat", None: "all"}[lat]
    n_tiles = {"ctx": N_CTX_TOK, "lat": N_LAT_TOK, "all": N_TOK}[group] // tm
    mix_tile0 = N_CTX_TOK // tm if group == "lat" else 0
    (out,) = _aliased_call(
        functools.partial(_mlp_kernel, group=group),
        grid=(n_tiles, D_FF // tk),
        in_specs=[
            pl.BlockSpec((tm, D_MODEL), lambda i, k: (x_tile0 + i, 0)),
            pl.BlockSpec((tm, D_MIX), lambda i, k: (mix_tile0 + i, 0)),
            pl.BlockSpec((None, D_MIX, D_MODEL), lambda i, k: (jl, 0, 0)),
            pl.BlockSpec((None, MOD_ROWS, N_MOD), lambda i, k: (li, 0, 0)),
            pl.BlockSpec((None, None, 1, D_MODEL), lambda i, k: (li, 1, 0, 0)),
            pl.BlockSpec((None, D_MODEL, tk), lambda i, k: (li, 0, k)),
            pl.BlockSpec((None, tk, D_MODEL), lambda i, k: (li, k, 0)),
        ],
        args=[x_src, mix, wo, mods, norm_w, w1, w2],
        out_specs=[pl.BlockSpec((tm, D_MODEL), lambda i, k: (out_tile0 + i, 0))],
        out_shape=[jax.ShapeDtypeStruct((out_rows, D_MODEL), F32)],
        carried=[out_prev], sem=("parallel", "arbitrary"), name=name,
        scratch_shapes=[pltpu.VMEM((tm, D_MODEL), BF16)])
    return out


def _rope_tables(width):
    t = np.arange(DEC_SEQ)
    half = HEAD_GROUP // 2
    inv = ROPE_BASE ** (-np.arange(0, half, 2, dtype=np.float64) / half)
    ang = np.concatenate([(t // GRID_W)[:, None] * inv, (t % GRID_W)[:, None] * inv], axis=-1)
    cos = np.repeat(np.cos(ang), 2, axis=-1)
    sin = np.repeat(np.sin(ang), 2, axis=-1)
    sign = np.tile(np.array([-1.0, 1.0]), HEAD_GROUP // 2)
    reps = width // HEAD_GROUP
    return (jnp.asarray(np.tile(cos, (1, reps)), F32),
            jnp.asarray(np.tile(sin * sign, (1, reps)), F32))


def _block_diag_ones(n):
    g = np.arange(n) // HEAD_GROUP
    return jnp.asarray(g[:, None] == g[None, :], BF16)


def _tile_row(w, width):
    return jnp.tile(w.astype(F32), width // w.shape[0])[None, :]


def _lambda_init(li):
    return 0.8 - 0.6 * math.exp(-0.3 * li)


def kernel(x_prompt, x_sample, cache_k_swa, cache_v_swa, state_hgrn, cache_k_diff, cache_v_diff, c, c_ctx, norm_w, w_ada, b_ada, w_in_even, w_out_even, hgrn_lb_logits, hgrn_norm_w, swa_qnorm_w, swa_knorm_w, swa_sink, w_in_odd, w_out_odd, diff_qnorm_w, diff_knorm_w, diff_lambda_p, diff_subln_w, w_mlp1, w_mlp2):
    assert PROJ_TM == MLP_TM
    n_ctx_tiles = N_CTX_TOK // PROJ_TM
    x = None
    x_ctx0 = x_prompt.reshape(N_CTX_TOK, D_MODEL)
    x_lat0 = x_sample.reshape(N_LAT_TOK, D_MODEL)
    c_all = jnp.concatenate(
        [c_ctx[None, :], c, jnp.zeros((MOD_ROWS - 1 - DEC_BATCH, D_MODEL), F32)], axis=0)
    mods = _mods_call(c_all, w_ada, b_ada.reshape(DEPTH, 1, N_MOD))

    cos_t, sin_t = _rope_tables(ROPE_TW)
    bd = _block_diag_ones(256)
    tabs_f = _hgrn_tables(False)
    tabs_b = _hgrn_tables(True)
    hgrn_tabs = (tabs_f[0], tabs_f[1], tabs_b[0], tabs_b[1])
    lbl = hgrn_lb_logits.astype(F32).reshape(N_EVEN * 2, F_A)
    norm_w4 = norm_w.astype(F32).reshape(DEPTH, 2, 1, D_MODEL)

    w_out_even_b = w_out_even.astype(BF16)
    w_out_odd_b = w_out_odd.astype(BF16)

    ck_swa = cache_k_swa.reshape(DEC_BATCH, N_EVEN, PAST_LEN, KV_B * HD_B)
    cv_swa = cache_v_swa.reshape(DEC_BATCH, N_EVEN, PAST_LEN, KV_B * HD_B)
    ck_diff = cache_k_diff.reshape(DEC_BATCH, N_ODD, PAST_LEN, ODD_W)
    cv_diff = cache_v_diff.reshape(DEC_BATCH, N_ODD, PAST_LEN, ODD_W)
    hgrn_nw = hgrn_norm_w.astype(F32).reshape(N_EVEN, 1, DV_A)
    sink = swa_sink.astype(F32).reshape(N_EVEN, 1, H_B)
    lam_p = diff_lambda_p.astype(F32)
    assert ATT_TQ == SEQ
    subln = jnp.broadcast_to(diff_subln_w.astype(F32)[:, :, None], (N_ODD, 2 * HD_C, ATT_TQ))

    even_kinds = ("silu_scale", "loggate0", "loggate1", "ident", "silu", "qnorm", "kv")
    odd_kinds = ("qnorm", "qnorm", "knorm", "knorm", "ident", "ident")

    k_swa = v_swa = states = k_diff = v_diff = None
    for li in range(DEPTH):
        j = li // 2
        srcs = ((x_ctx0, 0), (x_lat0, 0)) if li == 0 else ((x, 0), (x, n_ctx_tiles))
        if li % 2 == 0:
            p = None
            for lat, (src, t0) in enumerate(srcs):
                (p,) = _proj_call(src, t0, bool(lat), p, mods, li, norm_w4, w_in_even, j, lbl,
                                  _tile_row(swa_qnorm_w[j], PROJ_TN),
                                  _tile_row(swa_knorm_w[j], PROJ_TN),
                                  cos_t, sin_t, bd, even_kinds, f"proj_even{j}_{lat}")
            mix, states = _hgrn_call(p, hgrn_nw, hgrn_tabs, SEQ, BATCH, 0, None, j, None, states,
                                     f"hgrn_ctx{j}")
            (mix,) = _hgrn_call(p, hgrn_nw, hgrn_tabs, DEC_SEQ, DEC_BATCH, N_CTX_TOK // DEC_SEQ,
                                state_hgrn, j, mix, None, f"hgrn_lat{j}")
            mix, k_swa, v_swa = _swa_ctx_call(p, sink, j, mix, k_swa, v_swa, f"swa_ctx{j}")
            mix = _swa_lat_call(p, ck_swa, cv_swa, sink, j, mix, f"swa_lat{j}")
            wo = w_out_even_b
        else:
            p = None
            for lat, (src, t0) in enumerate(srcs):
                res = _proj_call(src, t0, bool(lat), p, mods, li, norm_w4, w_in_odd, j, lbl,
                                 _tile_row(diff_qnorm_w[j], PROJ_TN),
                                 _tile_row(diff_knorm_w[j], PROJ_TN),
                                 cos_t, sin_t, bd, odd_kinds, f"proj_odd{j}_{lat}",
                                 out_dtype=BF16,
                                 caches=None if lat else (2, (k_diff, v_diff)))
                if lat:
                    (p,) = res
                else:
                    p, k_diff, v_diff = res
            lam_init = _lambda_init(li)
            mix = _diff_ctx_call(p, lam_p, subln, lam_init, j, f"diff_ctx{j}")
            mix = _diff_lat_call(p, ck_diff, cv_diff, lam_p, subln, lam_init, j, mix,
                                 f"diff_lat{j}")
            wo = w_out_odd_b
        last = li == DEPTH - 1
        if li == 0 or last:
            outs = []
            x_next = None
            for lat, (src, t0) in enumerate(srcs):
                rows = (N_LAT_TOK if lat else N_CTX_TOK) if last else N_TOK
                out_t0 = 0 if last else lat * n_ctx_tiles
                x_next = _mlp_call(src, t0, bool(lat), mix, wo, j, mods, li, norm_w4, w_mlp1,
                                   w_mlp2, None if last else x_next, rows, out_t0,
                                   f"mlp{li}_{lat}")
                outs.append(x_next)
            x = x_next
        else:
            x = _mlp_call(x, 0, None, mix, wo, j, mods, li, norm_w4, w_mlp1, w_mlp2, None, N_TOK,
                          0, f"mlp{li}")

    y_prompt = outs[0].reshape(BATCH, SEQ, D_MODEL)
    y_sample = outs[1].reshape(DEC_BATCH, DEC_SEQ, D_MODEL)
    return (y_prompt, y_sample,
            k_swa.reshape(BATCH, N_EVEN, SEQ, KV_B, HD_B),
            v_swa.reshape(BATCH, N_EVEN, SEQ, KV_B, HD_B),
            states,
            k_diff.reshape(BATCH, N_ODD, SEQ, H_C, 2, HD_C),
            v_diff.reshape(BATCH, N_ODD, SEQ, H_C, 2 * HD_C))
```

```python
import functools
import math

import numpy as np
import jax
import jax.numpy as jnp
from jax import lax
from jax.experimental import pallas as pl
from jax.experimental.pallas import tpu as pltpu

F32 = jnp.float32
BF16 = jnp.bfloat16

D_MODEL = 1024
BATCH = 16
SEQ = 256
DEPTH = 4
DEC_BATCH = 4
DEC_SEQ = 1024
PAST_LEN = 512
GRID_W = 64
N_EVEN = (DEPTH + 1) // 2
N_ODD = DEPTH // 2
H_A = 4
DK_A = 128
DV_A = D_MODEL // 2 // H_A
F_A = H_A * DK_A
H_B = 8
KV_B = 2
G_B = H_B // KV_B
HD_B = D_MODEL // 2 // H_B
WINDOW = 128
H_C = 8
HD_C = D_MODEL // (2 * H_C)
D_FF = 4 * D_MODEL
ROPE_BASE = 10000.0
EPS = 1e-6
EVEN_COLS = 3 * F_A + 2 * H_A * DV_A + (H_B + 2 * KV_B) * HD_B
ODD_W = H_C * 2 * HD_C
D_MIX = D_MODEL

N_CTX_TOK = BATCH * SEQ
N_LAT_TOK = DEC_BATCH * DEC_SEQ
N_TOK = N_CTX_TOK + N_LAT_TOK
MOD_ROWS = 8
N_MOD = 6 * D_MODEL

HEAD_GROUP = 64
HGRN_CHUNK = 128
HGRN_LEVELS = 7
HGRN_SPLIT = 3
LOG2_E = math.log2(math.e)
VMEM_LIMIT = 48 * 1024 * 1024

PROJ_TM = 1024
PROJ_TN = 512
ROPE_TW = 128
PROJ_TILES_CTX = 2
PROJ_TILES_LAT = 2
PROJ_RC = 128
MLP_TM = 1024
MLP_TK = 1024
MLP_RC = 256
ADA_TN = 1536
ATT_TQ = 256
HGRN_CTX_SEQS = 2
CTX_SEQS = 2
SWA_CTX_SEQS = 4
DIFF_LAT_HEADS = 4
ONES_ROWS = 16


def _silu(x):
    return x * jax.nn.sigmoid(x)


def _nt_dot(a, b):
    return lax.dot_general(a, b, (((1,), (1,)), ((), ())), preferred_element_type=F32)


def _pipelined(n, first, second):
    cur = first(0)
    for i in range(n):
        nxt = first(i + 1) if i + 1 < n else None
        second(i, cur)
        cur = nxt


def _params(sem):
    return pltpu.CompilerParams(dimension_semantics=sem, vmem_limit_bytes=VMEM_LIMIT)


def _aliased_call(kernel, *, grid, in_specs, args, out_specs, out_shape, carried, sem, name,
                  scratch_shapes=()):
    n_in = len(args)
    extra = [buf for buf in carried if buf is not None]
    aliases = {}
    for k, buf in enumerate(carried):
        if buf is not None:
            aliases[n_in + len(aliases)] = k
    n_extra = len(extra)

    def body(*refs):
        kernel(*refs[:n_in], *refs[n_in + n_extra:])

    return pl.pallas_call(
        body,
        grid=grid,
        in_specs=list(in_specs) + [pl.BlockSpec(memory_space=pl.ANY)] * n_extra,
        out_specs=out_specs,
        out_shape=out_shape,
        input_output_aliases=aliases,
        scratch_shapes=list(scratch_shapes),
        compiler_params=_params(sem),
        name=name,
    )(*args, *extra)


def _mods_kernel(c_ref, w_ref, b_ref, o_ref):
    s = _silu(c_ref[...]).astype(BF16)
    o_ref[...] = jnp.dot(s, w_ref[...].astype(BF16), preferred_element_type=F32) + b_ref[...]


def _mods_call(c_all, w_ada, b_ada):
    return pl.pallas_call(
        _mods_kernel,
        grid=(DEPTH, N_MOD // ADA_TN),
        in_specs=[
            pl.BlockSpec((MOD_ROWS, D_MODEL), lambda l, j: (0, 0)),
            pl.BlockSpec((None, D_MODEL, ADA_TN), lambda l, j: (l, 0, j)),
            pl.BlockSpec((None, 1, ADA_TN), lambda l, j: (l, 0, j)),
        ],
        out_specs=pl.BlockSpec((None, MOD_ROWS, ADA_TN), lambda l, j: (l, 0, j)),
        out_shape=jax.ShapeDtypeStruct((DEPTH, MOD_ROWS, N_MOD), F32),
        compiler_params=_params(("parallel", "parallel")),
        name="ada_mods",
    )(c_all, w_ada, b_ada)


def _norm_mod(x, nw, sc, sh):
    ms = jnp.mean(x * x, axis=-1, keepdims=True)
    return (x * lax.rsqrt(ms + EPS) * nw) * (1.0 + sc) + sh


def _group_rms(y, w_t, bd_ref):
    yy = (y * y).astype(BF16)
    bw = bd_ref.shape[0]
    parts = [jnp.dot(yy[:, s:s + bw], bd_ref[...], preferred_element_type=F32)
             for s in range(0, y.shape[1], bw)]
    ss = parts[0] if len(parts) == 1 else jnp.concatenate(parts, axis=1)
    return y * lax.rsqrt(ss * (1.0 / HEAD_GROUP) + EPS) * w_t


def _rope(y, cos, sin):
    n = y.shape[1]
    lane = lax.broadcasted_iota(jnp.int32, y.shape, 1)
    nxt = pltpu.roll(y, n - 1, axis=1)
    prv = pltpu.roll(y, 1, axis=1)
    swapped = jnp.where((lane & 1) == 0, nxt, prv)
    return y * cos + swapped * sin


def _lower_bounds(lbl_ref, jl):
    rows = [lbl_ref[pl.ds(2 * m, 2), :] for m in range(N_EVEN)]
    mx = functools.reduce(jnp.maximum, rows)
    es = [jnp.exp(r - mx) for r in rows]
    den = functools.reduce(lambda a, b: a + b, es)
    sm = [e / den for e in es]
    cs = sm[0]
    for m in range(1, jl + 1):
        cs = cs + sm[m]
    return cs - sm[0]


def _proj_kernel(x_ref, mods_ref, nw_ref, w_ref, wkv_ref, lbl_ref, qn_ref, kn_ref, cos_ref,
                 sin_ref, bd_ref, o_ref, h_scr, w_scr, *, kinds, jl, lat):
    tiles = PROJ_TILES_LAT if lat else PROJ_TILES_CTX
    j = pl.program_id(0)
    i = pl.program_id(1)
    tm, tn = PROJ_TM, o_ref.shape[1]
    kvw = 2 * KV_B * HD_B

    @pl.when(j == 0)
    def _():
        for t in range(tiles):
            tile = tiles * i + t
            row = 1 + tile if lat else 0
            sh = mods_ref[pl.ds(row, 1), 0:D_MODEL]
            sc = mods_ref[pl.ds(row, 1), D_MODEL:2 * D_MODEL]
            h_scr[tile] = _norm_mod(x_ref[t * tm:(t + 1) * tm, :], nw_ref[...], sc,
                                    sh).astype(BF16)

    def finish(kind, y, rows):
        if kind == "silu_scale":
            return _silu(y) * (DK_A ** -0.5)
        if kind in ("loggate0", "loggate1"):
            d = int(kind[-1])
            lb = _lower_bounds(lbl_ref, jl)[d:d + 1, :]
            return jnp.log2(lb + (1.0 - lb) * jax.nn.sigmoid(y))
        if kind == "ident":
            return y
        if kind == "silu":
            return _silu(y)
        if kind in ("qnorm", "knorm"):
            w_t = qn_ref[...] if kind == "qnorm" else kn_ref[...]
            r = _group_rms(y, w_t, bd_ref)
            if lat:
                r = _rope(r, jnp.tile(cos_ref[rows, :], (1, tn // ROPE_TW)),
                          jnp.tile(sin_ref[rows, :], (1, tn // ROPE_TW)))
            return r * (HEAD_GROUP ** -0.5 * LOG2_E) if kind == "qnorm" else r
        if kind == "kv":
            kn = _group_rms(y, kn_ref[:, 0:kvw], bd_ref)
            if lat:
                kn = _rope(kn, jnp.tile(cos_ref[rows, :], (1, kvw // ROPE_TW)),
                           jnp.tile(sin_ref[rows, :], (1, kvw // ROPE_TW)))
            lane = lax.broadcasted_iota(jnp.int32, y.shape, 1)
            return jnp.where(lane < KV_B * HD_B, kn, y)
        raise ValueError(kind)

    def run(kind):
        @pl.when(i == 0)
        def _():
            if kind == "kv":
                w_scr[:, 0:kvw] = wkv_ref[...].astype(BF16)
            else:
                w_scr[...] = w_ref[...].astype(BF16)

        per_tile = tm // PROJ_RC

        def first(r):
            t, c = divmod(r, per_tile)
            w = w_scr[:, 0:kvw] if kind == "kv" else w_scr[...]
            return jnp.dot(h_scr[tiles * i + t, c * PROJ_RC:(c + 1) * PROJ_RC, :], w,
                           preferred_element_type=F32)

        def second(r, y):
            c = r % per_tile
            rows = slice(r * PROJ_RC, (r + 1) * PROJ_RC)
            seq_rows = slice(c * PROJ_RC, (c + 1) * PROJ_RC)
            if kind == "kv":
                o_ref[rows, 0:kvw] = finish(kind, y, seq_rows)
                o_ref[rows, kvw:tn] = jnp.zeros((PROJ_RC, tn - kvw), F32)
            else:
                o_ref[rows, :] = finish(kind, y, seq_rows)

        _pipelined(tiles * per_tile, first, second)

    for jj, kind in enumerate(kinds):
        pl.when(j == jj)(functools.partial(run, kind))


def _proj_call(x_src, x_tile0, lat, mods, li, norm_w, w, jl, lbl, qn_t, kn_t, cos_t, sin_t,
               bd, kinds, name):
    tm, tn = PROJ_TM, PROJ_TN
    assert tm == DEC_SEQ
    n_rows = N_LAT_TOK if lat else N_CTX_TOK
    n_tiles = n_rows // tm
    if not lat:
        cos_t = sin_t = jnp.zeros((8, 128), F32)
    n_main = sum(1 for k in kinds if k != "kv")
    n_cols = tn * len(kinds)
    kvw = 2 * KV_B * HD_B
    kv_blk = (n_main * tn) // kvw if "kv" in kinds else 0
    const = lambda j, i: (0, 0)
    tiles = PROJ_TILES_LAT if lat else PROJ_TILES_CTX
    n_blk = n_tiles // tiles
    blk = tiles * tm
    x_blk0 = x_tile0 // tiles
    assert x_tile0 % tiles == 0
    return pl.pallas_call(
        functools.partial(_proj_kernel, kinds=kinds, jl=jl, lat=lat),
        grid=(len(kinds), n_blk),
        in_specs=[
            pl.BlockSpec((blk, D_MODEL),
                         lambda j, i: (x_blk0 + jnp.where(j == 0, i, n_blk - 1), 0)),
            pl.BlockSpec((None, MOD_ROWS, N_MOD), lambda j, i: (li, 0, 0)),
            pl.BlockSpec((None, None, 1, D_MODEL), lambda j, i: (li, 0, 0, 0)),
            pl.BlockSpec((None, D_MODEL, tn), lambda j, i: (jl, 0, jnp.minimum(j, n_main - 1))),
            pl.BlockSpec((None, D_MODEL, kvw), lambda j, i: (jl, 0, kv_blk)),
            pl.BlockSpec(lbl.shape, const),
            pl.BlockSpec((1, tn), const),
            pl.BlockSpec((1, tn), const),
            pl.BlockSpec(cos_t.shape, const),
            pl.BlockSpec(sin_t.shape, const),
            pl.BlockSpec(bd.shape, const),
        ],
        out_specs=pl.BlockSpec((blk, tn), lambda j, i: (i, j)),
        out_shape=jax.ShapeDtypeStruct((n_rows, n_cols), F32),
        scratch_shapes=[pltpu.VMEM((n_tiles, tm, D_MODEL), BF16),
                        pltpu.VMEM((D_MODEL, tn), BF16)],
        compiler_params=_params(("arbitrary", "arbitrary")),
        name=name,
    )(x_src, mods, norm_w, w, w, lbl, qn_t, kn_t, cos_t, sin_t, bd)


def _hgrn_tables(rev):
    c = HGRN_CHUNK
    t = np.arange(c)
    w = (t[None, :] <= t[:, None]) if not rev else (t[None, :] >= t[:, None])
    ws = np.concatenate([w.astype(np.float32)] * HGRN_SPLIT, axis=1)
    x = t[:, None] ^ t[None, :]
    lv = np.where(x > 0, np.floor(np.log2(np.maximum(x, 1))).astype(np.int32), HGRN_LEVELS)
    causal = (t[None, :] < t[:, None]) if not rev else (t[None, :] > t[:, None])
    lv = np.where(causal | (x == 0), lv, -1).astype(np.int32)
    return jnp.asarray(ws, BF16), jnp.asarray(lv)


def _hgrn_level_exponents(cum, rev):
    c = HGRN_CHUNK
    sub_rows = 8
    c3 = cum.reshape(c // sub_rows, sub_rows, DK_A)
    sub = lax.broadcasted_iota(jnp.int32, c3.shape, 1)
    out = []
    for l in range(1, HGRN_LEVELS):
        hb = 1 << l
        if 2 * hb <= sub_rows:
            r = None
            for b0 in range(0, sub_rows, 2 * hb):
                idx = b0 + (hb if rev else hb - 1)
                rk = c3[:, idx:idx + 1, :]
                r = rk if r is None else jnp.where(sub < b0, r, rk)
            d = c3 - r
            bit = (sub & hb) != 0
            q_role = jnp.logical_not(bit) if rev else bit
            out.append(jnp.where(q_role, d, -d).reshape(c, DK_A))
        else:
            pieces = []
            for b0 in range(0, c, 2 * hb):
                mid = b0 + hb
                ridx = mid if rev else mid - 1
                r = cum[ridx:ridx + 1, :]
                lo = cum[b0:mid]
                hi = cum[mid:b0 + 2 * hb]
                pieces += [lo - r, r - hi] if rev else [r - lo, hi - r]
            out.append(jnp.concatenate(pieces, axis=0))
    return out


def _hgrn_level_operands(l, q, k, f, z, rev, row):
    c = HGRN_CHUNK
    hb = 1 << l
    if hb >= 8:
        zero = jnp.zeros((hb, DK_A), F32)
        qparts, kparts = [], []
        for b0 in range(0, c, 2 * hb):
            lo, hi = slice(b0, b0 + hb), slice(b0 + hb, b0 + 2 * hb)
            if rev:
                qparts += [q[lo] * z[lo], zero]
                kparts += [zero, k[hi] * z[hi]]
            else:
                qparts += [zero, q[hi] * z[hi]]
                kparts += [k[lo] * z[lo], zero]
        return (jnp.concatenate(qparts, axis=0).astype(BF16),
                jnp.concatenate(kparts, axis=0).astype(BF16))
    bit = ((row >> l) & 1) == 1
    q_role = jnp.logical_not(bit) if rev else bit
    ql = jnp.where(q_role, q * (f if l == 0 else z), 0.0).astype(BF16)
    kl = jnp.where(q_role, 0.0, k if l == 0 else k * z).astype(BF16)
    return ql, kl


def _hgrn_chunks(chains):
    c = HGRN_CHUNK
    row = lax.broadcasted_iota(jnp.int32, (c, DK_A), 0)
    cums = []
    for q, g, v, st, w_ref, lv_ref, rev in chains:
        terms = []
        rem = g
        for _ in range(HGRN_SPLIT):
            term = rem.astype(BF16)
            terms.append(term)
            rem = rem - term.astype(F32)
        cums.append(jnp.dot(w_ref[...], jnp.concatenate(terms, axis=0),
                            preferred_element_type=F32))
    work = []
    for (q, g, v, st, w_ref, lv_ref, rev), cum in zip(chains, cums):
        f = jnp.exp2(g)
        k = 1.0 - f
        last = 0 if rev else c - 1
        total_e = cum[last:last + 1, :]
        qd = (q * jnp.exp2(cum)).astype(BF16)
        kd = (k * jnp.exp2(total_e - cum)).astype(BF16)
        o = _nt_dot(qd, st.astype(BF16))
        st_new = (st * jnp.exp2(total_e)
                  + jnp.dot(v.T.astype(BF16), kd, preferred_element_type=F32))
        zs = [None] + [jnp.exp2(e) for e in _hgrn_level_exponents(cum, rev)]
        work.append((k, f, zs, o, st_new))
    accs = [None] * len(chains)
    for l in reversed(range(HGRN_LEVELS)):
        for i, ((q, g, v, st, w_ref, lv_ref, rev), (k, f, zs, o, st_new)) in enumerate(
                zip(chains, work)):
            ql, kl = _hgrn_level_operands(l, q, k, f, zs[l], rev, row)
            a_l = _nt_dot(ql, kl)
            accs[i] = a_l if accs[i] is None else jnp.where(lv_ref[...] == l, a_l, accs[i])
    outs = []
    for (q, g, v, st, w_ref, lv_ref, rev), (k, f, zs, o, st_new), a in zip(chains, work, accs):
        a = jnp.where(lv_ref[...] == HGRN_LEVELS, jnp.sum(q * k, axis=-1, keepdims=True), a)
        outs.append((o + jnp.dot(a.astype(BF16), v.astype(BF16), preferred_element_type=F32),
                     st_new))
    return outs


def _hgrn_kernel(*refs, n_chunks, seqs, has_init, emit_state):
    refs = list(refs)
    q_ref, gf_ref, gb_ref, v_ref, sg_ref, nw_ref, wf_ref, wb_ref, lvf_ref, lvb_ref = refs[:10]
    pos = 10
    s0_ref = None
    if has_init:
        s0_ref = refs[pos]
        pos += 1
    o_ref = refs[pos]
    pos += 1
    so_ref = None
    if emit_state:
        so_ref = refs[pos]
        pos += 1
    of_scr, ob_scr, st_scr = refs[pos:pos + 3]

    seq_len = n_chunks * HGRN_CHUNK
    for s in range(seqs):
        for d in range(2):
            for h in range(H_A):
                if has_init:
                    st_scr[s, d, h] = s0_ref[s, d, h].T
                else:
                    st_scr[s, d, h] = jnp.zeros((DV_A, DK_A), F32)

    def body(c, carry):
        chains, dests = [], []
        for s in range(seqs):
            for h in range(H_A):
                cols = slice(h * DK_A, (h + 1) * DK_A)
                for d, (g_ref, w_ref, lv_ref, scr) in enumerate(
                        ((gf_ref, wf_ref, lvf_ref, of_scr), (gb_ref, wb_ref, lvb_ref, ob_scr))):
                    cc = c if d == 0 else n_chunks - 1 - c
                    r0 = pl.multiple_of(s * seq_len + cc * HGRN_CHUNK, HGRN_CHUNK)
                    rows = pl.ds(r0, HGRN_CHUNK)
                    chains.append((q_ref[rows, cols], g_ref[rows, cols], v_ref[rows, cols],
                                   st_scr[s, d, h], w_ref, lv_ref, d == 1))
                    dests.append((scr, rows, cols, s, d, h))
        for (o, st), (scr, rows, cols, s, d, h) in zip(_hgrn_chunks(chains), dests):
            st_scr[s, d, h] = st
            scr[rows, cols] = o
        return carry

    lax.fori_loop(0, n_chunks, body, 0)
    for h in range(H_A):
        cols = slice(h * DV_A, (h + 1) * DV_A)
        o = of_scr[:, cols] + ob_scr[:, cols]
        y = o * lax.rsqrt(jnp.mean(o * o, axis=-1, keepdims=True) + EPS) * nw_ref[...]
        o_ref[:, cols] = (y * sg_ref[:, cols]).astype(o_ref.dtype)
    if emit_state:
        for s in range(seqs):
            for d in range(2):
                for h in range(H_A):
                    so_ref[s, d, h] = st_scr[s, d, h].T


def _hgrn_call(p, nw, tabs, seq_len, n_seq, row_blk0, s0, jl, mix, state_out, name):
    wf, lvf, wb, lvb = tabs
    has_init = s0 is not None
    emit_state = s0 is None
    seqs = HGRN_CTX_SEQS if emit_state else 1
    assert row_blk0 % seqs == 0 and n_seq % seqs == 0
    blk0 = row_blk0 // seqs
    const = lambda b: (0, 0)
    blk = (seqs * seq_len, F_A)
    state_spec = pl.BlockSpec((seqs, None, 2, H_A, DK_A, DV_A), lambda b: (b, jl, 0, 0, 0, 0))
    in_specs = [pl.BlockSpec(blk, (lambda b, part=part: (b, part))) for part in range(5)]
    in_specs += [
        pl.BlockSpec((None, 1, DV_A), lambda b: (jl, 0, 0)),
        pl.BlockSpec(wf.shape, const), pl.BlockSpec(wb.shape, const),
        pl.BlockSpec(lvf.shape, const), pl.BlockSpec(lvb.shape, const),
    ]
    args = [p, p, p, p, p, nw, wf, wb, lvf, lvb]
    if has_init:
        in_specs.append(state_spec)
        args.append(s0)
    out_shape = [jax.ShapeDtypeStruct((N_TOK, D_MIX), BF16)]
    out_specs = [pl.BlockSpec((seqs * seq_len, H_A * DV_A), lambda b: (blk0 + b, 0))]
    carried = [mix]
    if emit_state:
        out_shape.append(jax.ShapeDtypeStruct((BATCH, N_EVEN, 2, H_A, DK_A, DV_A), F32))
        out_specs.append(state_spec)
        carried.append(state_out)
    return _aliased_call(
        functools.partial(_hgrn_kernel, n_chunks=seq_len // HGRN_CHUNK, seqs=seqs,
                          has_init=has_init, emit_state=emit_state),
        grid=(n_seq // seqs,), in_specs=in_specs, args=args, out_specs=out_specs,
        out_shape=out_shape, carried=carried, sem=("parallel",), name=name,
        scratch_shapes=[pltpu.VMEM((seqs * seq_len, H_A * DV_A), F32),
                        pltpu.VMEM((seqs * seq_len, H_A * DV_A), F32),
                        pltpu.VMEM((seqs, 2, H_A, DV_A, DK_A), F32)])


def _swa_ctx_kernel(q_ref, kv_ref, sink_ref, o_ref, kc_ref, vc_ref):
    for s in range(SWA_CTX_SEQS):
        rows = slice(s * SEQ, (s + 1) * SEQ)
        kv = kv_ref[rows, :]
        k32 = kv[:, 0:KV_B * HD_B]
        v32 = kv[:, KV_B * HD_B:2 * KV_B * HD_B]
        kc_ref[s] = k32
        vc_ref[s] = v32
        k = k32.astype(BF16)
        vt32 = v32.T
        vts = [_with_ones_rows(vt32[n * HD_B:(n + 1) * HD_B]) for n in range(KV_B)]
        q = q_ref[rows, :].astype(BF16)
        ksl = [slice((h // G_B) * HD_B, (h // G_B + 1) * HD_B) for h in range(H_B)]
        sts = [_nt_dot(k[:, ksl[h]], q[:, h * HD_B:(h + 1) * HD_B]) for h in range(H_B)]
        ps, sinks = [], []
        for h in range(H_B):
            sink = sink_ref[0:1, h:h + 1] * LOG2_E
            m = jnp.maximum(jnp.max(sts[h], axis=0, keepdims=True), sink)
            ps.append(jnp.exp2(sts[h] - m).astype(BF16))
            sinks.append(jnp.exp2(sink - m))
        outs = []
        for h in range(H_B):
            ota = jnp.dot(vts[h // G_B], ps[h], preferred_element_type=F32)
            outs.append(ota[0:HD_B] / (ota[HD_B:HD_B + 1] + sinks[h]))
        o_ref[rows, :] = jnp.concatenate(outs, axis=0).T.astype(o_ref.dtype)


def _swa_ctx_call(p, sink, jl, mix, k_out, v_out, name):
    qcol = (3 * F_A + 2 * H_A * DV_A) // PROJ_TN
    rows = SWA_CTX_SEQS * SEQ
    cache_spec = pl.BlockSpec((SWA_CTX_SEQS, None, SEQ, KV_B * HD_B), lambda b: (b, jl, 0, 0))
    cache_shape = jax.ShapeDtypeStruct((BATCH, N_EVEN, SEQ, KV_B * HD_B), F32)
    return _aliased_call(
        _swa_ctx_kernel,
        grid=(BATCH // SWA_CTX_SEQS,),
        in_specs=[
            pl.BlockSpec((rows, H_B * HD_B), lambda b: (b, qcol)),
            pl.BlockSpec((rows, PROJ_TN), lambda b: (b, qcol + 1)),
            pl.BlockSpec((None, 1, H_B), lambda b: (jl, 0, 0)),
        ],
        args=[p, p, sink],
        out_specs=[pl.BlockSpec((rows, H_B * HD_B), lambda b: (b, 1)), cache_spec, cache_spec],
        out_shape=[jax.ShapeDtypeStruct((N_TOK, D_MIX), BF16), cache_shape, cache_shape],
        carried=[mix, k_out, v_out], sem=("parallel",), name=name)


def _swa_lat_kernel(q_ref, kv_ref, ck_ref, cv_ref, sink_ref, o_ref):
    qi = pl.program_id(1)
    tq = q_ref.shape[0]
    span = tq + 2 * WINDOW
    ws = pl.multiple_of(jnp.clip(qi * tq - WINDOW, 0, DEC_SEQ - span), WINDOW)
    kvw = kv_ref[pl.ds(ws, span), :]
    kw = kvw[:, 0:KV_B * HD_B].astype(BF16)
    vwt = kvw[:, KV_B * HD_B:2 * KV_B * HD_B].T.astype(BF16)
    kc = ck_ref[...].astype(BF16)
    vct = cv_ref[...].T.astype(BF16)
    q = q_ref[...].astype(BF16)
    t_k = ws + lax.broadcasted_iota(jnp.int32, (span, tq), 0)
    t_q = qi * tq + lax.broadcasted_iota(jnp.int32, (span, tq), 1)
    valid = jnp.abs(t_q - t_k) <= WINDOW
    outs = []

    def first(h):
        qh = q[:, h * HD_B:(h + 1) * HD_B]
        ksl = slice((h // G_B) * HD_B, (h // G_B + 1) * HD_B)
        return _nt_dot(kw[:, ksl], qh), _nt_dot(kc[:, ksl], qh)

    def second(h, scores):
        ksl = slice((h // G_B) * HD_B, (h // G_B + 1) * HD_B)
        s_w = jnp.where(valid, scores[0], -jnp.inf)
        s_c = scores[1]
        sink = sink_ref[0:1, h:h + 1] * LOG2_E
        m = jnp.maximum(jnp.maximum(jnp.max(s_w, axis=0, keepdims=True),
                                    jnp.max(s_c, axis=0, keepdims=True)), sink)
        p_w = jnp.exp2(s_w - m)
        p_c = jnp.exp2(s_c - m)
        den = (jnp.sum(p_w, axis=0, keepdims=True) + jnp.sum(p_c, axis=0, keepdims=True)
               + jnp.exp2(sink - m))
        ot = (jnp.dot(vwt[ksl, :], p_w.astype(BF16), preferred_element_type=F32)
              + jnp.dot(vct[ksl, :], p_c.astype(BF16), preferred_element_type=F32))
        outs.append(ot / den)

    _pipelined(H_B, first, second)
    o_ref[...] = jnp.concatenate(outs, axis=0).T.astype(o_ref.dtype)


def _swa_lat_call(p, ck, cv, sink, jl, mix, name):
    tq = ATT_TQ
    qcol = (3 * F_A + 2 * H_A * DV_A) // PROJ_TN
    nq = DEC_SEQ // tq
    q_blk0 = N_CTX_TOK // tq
    cache_spec = pl.BlockSpec((None, None, PAST_LEN, KV_B * HD_B), lambda b, i: (b, jl, 0, 0))
    (out,) = _aliased_call(
        _swa_lat_kernel,
        grid=(DEC_BATCH, nq),
        in_specs=[
            pl.BlockSpec((tq, H_B * HD_B), lambda b, i: (b * nq + i, qcol)),
            pl.BlockSpec((DEC_SEQ, PROJ_TN), lambda b, i: (b, qcol + 1)),
            cache_spec, cache_spec,
            pl.BlockSpec((None, 1, H_B), lambda b, i: (jl, 0, 0)),
        ],
        args=[p, p, ck, cv, sink],
        out_specs=[pl.BlockSpec((tq, H_B * HD_B), lambda b, i: (q_blk0 + b * nq + i, 1))],
        out_shape=[jax.ShapeDtypeStruct((N_TOK, D_MIX), BF16)],
        carried=[mix], sem=("parallel", "parallel"), name=name)
    return out


def _diff_lambda(lp_ref, lam_init):
    lp = lp_ref[...]
    a = jnp.sum(lp[0:1] * lp[1:2], axis=-1, keepdims=True)
    b = jnp.sum(lp[2:3] * lp[3:4], axis=-1, keepdims=True)
    return jnp.exp(a) - jnp.exp(b) + lam_init


def _diff_scores(q, k_parts):
    return [[_nt_dot(kp[:, c * HD_C:(c + 1) * HD_C], q[:, c * HD_C:(c + 1) * HD_C])
             for kp in k_parts] for c in range(2)]


def _diff_finish(scores, vt_parts, lam, lam_init, sw_t):
    hw = 2 * HD_C
    comps = []
    for c in range(2):
        ss = scores[c]
        m = functools.reduce(jnp.maximum, [jnp.max(s, axis=0, keepdims=True) for s in ss])
        ota = functools.reduce(
            lambda a, b: a + b,
            [jnp.dot(vt, jnp.exp2(s - m).astype(BF16), preferred_element_type=F32)
             for s, vt in zip(ss, vt_parts)])
        comps.append(ota[0:hw] / ota[hw:hw + 1])
    ot = comps[0] - lam * comps[1]
    yt = ot * lax.rsqrt(jnp.mean(ot * ot, axis=0, keepdims=True) + EPS) * sw_t
    return (yt * (1.0 - lam_init)).T


def _with_ones_rows(vt):
    return jnp.concatenate([vt, jnp.ones((ONES_ROWS, vt.shape[1]), F32)], axis=0).astype(BF16)


def _diff_ctx_kernel(q_ref, k_ref, v_ref, lp_ref, sw_ref, o_ref, kc_ref, vc_ref, *, lam_init):
    lam = _diff_lambda(lp_ref, lam_init)
    hw = 2 * HD_C
    vts = {}

    def first(n):
        s, h = divmod(n, H_C)
        rows = slice(s * SEQ, (s + 1) * SEQ)
        sl = slice(h * hw, (h + 1) * hw)
        k32 = k_ref[rows, sl]
        v32 = v_ref[rows, sl]
        kc_ref[s, :, sl] = k32
        vc_ref[s, :, sl] = v32
        vts[n] = _with_ones_rows(v32.T)
        return _diff_scores(q_ref[rows, sl].astype(BF16), [k32.astype(BF16)])

    def second(n, scores):
        s, h = divmod(n, H_C)
        y = _diff_finish(scores, [vts.pop(n)], lam, lam_init, sw_ref[...])
        o_ref[s * SEQ:(s + 1) * SEQ, h * hw:(h + 1) * hw] = y.astype(o_ref.dtype)

    _pipelined(CTX_SEQS * H_C, first, second)


def _diff_ctx_call(p, lp, sw, lam_init, jl, k_out, v_out, name):
    rows = CTX_SEQS * SEQ
    cache_spec = pl.BlockSpec((CTX_SEQS, None, SEQ, ODD_W), lambda b: (b, jl, 0, 0))
    cache_shape = jax.ShapeDtypeStruct((BATCH, N_ODD, SEQ, ODD_W), F32)
    return _aliased_call(
        functools.partial(_diff_ctx_kernel, lam_init=lam_init),
        grid=(BATCH // CTX_SEQS,),
        in_specs=[
            pl.BlockSpec((rows, ODD_W), lambda b: (b, 0)),
            pl.BlockSpec((rows, ODD_W), lambda b: (b, 1)),
            pl.BlockSpec((rows, ODD_W), lambda b: (b, 2)),
            pl.BlockSpec((None, 4, HD_C), lambda b: (jl, 0, 0)),
            pl.BlockSpec((None, 2 * HD_C, ATT_TQ), lambda b: (jl, 0, 0)),
        ],
        args=[p, p, p, lp, sw],
        out_specs=[pl.BlockSpec((rows, ODD_W), lambda b: (b, 0)), cache_spec, cache_spec],
        out_shape=[jax.ShapeDtypeStruct((N_TOK, D_MIX), BF16), cache_shape, cache_shape],
        carried=[None, k_out, v_out], sem=("parallel",), name=name)


def _diff_lat_kernel(q_ref, k_ref, v_ref, ck_ref, cv_ref, lp_ref, sw_ref, o_ref, *, lam_init):
    lam = _diff_lambda(lp_ref, lam_init)
    hw = 2 * HD_C
    tq = ATT_TQ
    nq = q_ref.shape[0] // tq
    heads = []
    for g in range(q_ref.shape[1] // hw):
        sl = slice(g * hw, (g + 1) * hw)
        heads.append(([k_ref[:, sl].astype(BF16), ck_ref[:, sl].astype(BF16)],
                      [_with_ones_rows(v_ref[:, sl].T), _with_ones_rows(cv_ref[:, sl].T)]))

    def first(n):
        g, i = divmod(n, nq)
        q = q_ref[i * tq:(i + 1) * tq, g * hw:(g + 1) * hw].astype(BF16)
        return _diff_scores(q, heads[g][0])

    def second(n, scores):
        g, i = divmod(n, nq)
        y = _diff_finish(scores, heads[g][1], lam, lam_init, sw_ref[...])
        o_ref[i * tq:(i + 1) * tq, g * hw:(g + 1) * hw] = y.astype(o_ref.dtype)

    _pipelined(len(heads) * nq, first, second)


def _diff_lat_call(p, ck, cv, lp, sw, lam_init, jl, mix, name):
    hw = 2 * HD_C
    gw = DIFF_LAT_HEADS * hw
    n_g = H_C // DIFF_LAT_HEADS
    s_blk0 = N_CTX_TOK // DEC_SEQ
    cache_spec = pl.BlockSpec((None, None, PAST_LEN, gw), lambda b, h: (b, jl, 0, h))
    (out,) = _aliased_call(
        functools.partial(_diff_lat_kernel, lam_init=lam_init),
        grid=(DEC_BATCH, n_g),
        in_specs=[
            pl.BlockSpec((DEC_SEQ, gw), lambda b, h: (b, h)),
            pl.BlockSpec((DEC_SEQ, gw), lambda b, h: (b, n_g + h)),
            pl.BlockSpec((DEC_SEQ, gw), lambda b, h: (b, 2 * n_g + h)),
            cache_spec, cache_spec,
            pl.BlockSpec((None, 4, HD_C), lambda b, h: (jl, 0, 0)),
            pl.BlockSpec((None, hw, ATT_TQ), lambda b, h: (jl, 0, 0)),
        ],
        args=[p, p, p, ck, cv, lp, sw],
        out_specs=[pl.BlockSpec((DEC_SEQ, gw), lambda b, h: (s_blk0 + b, h))],
        out_shape=[jax.ShapeDtypeStruct((N_TOK, D_MIX), BF16)],
        carried=[mix], sem=("parallel", "parallel"), name=name)
    return out


def _mlp_kernel(x_ref, mix_ref, wo_ref, mods_ref, nw_ref, w1_ref, w2_ref, o_ref,
                h_scr, *, group):
    i = pl.program_id(0)
    k = pl.program_id(1)
    n_ctx_tiles = N_CTX_TOK // x_ref.shape[0]
    row = {"ctx": 0, "lat": 1 + i,
           "all": jnp.where(i >= n_ctx_tiles, 1 + i - n_ctx_tiles, 0)}[group]

    def mod(a):
        return mods_ref[pl.ds(row, 1), a * D_MODEL:(a + 1) * D_MODEL]

    @pl.when(k == 0)
    def _():
        def first(r):
            return jnp.dot(mix_ref[r * MLP_RC:(r + 1) * MLP_RC, :], wo_ref[...],
                           preferred_element_type=F32)

        def second(r, y):
            rows = slice(r * MLP_RC, (r + 1) * MLP_RC)
            x1 = x_ref[rows, :] + mod(2) * y
            o_ref[rows, :] = x1
            h_scr[rows, :] = _norm_mod(x1, nw_ref[...], mod(4), mod(3)).astype(BF16)

        _pipelined(x_ref.shape[0] // MLP_RC, first, second)

    u = jnp.dot(h_scr[...], w1_ref[...].astype(BF16), preferred_element_type=F32)
    u = jnp.square(jnp.maximum(u, 0.0)).astype(BF16)
    o_ref[...] += mod(5) * jnp.dot(u, w2_ref[...].astype(BF16), preferred_element_type=F32)


def _mlp_call(x_src, x_tile0, lat, mix, wo, jl, mods, li, norm_w, w1, w2, out_prev, out_rows,
              out_tile0, name):
    tm, tk = MLP_TM, MLP_TK
    group = {False: "ctx", True: "lat", None: "all"}[lat]
    n_tiles = {"ctx": N_CTX_TOK, "lat": N_LAT_TOK, "all": N_TOK}[group] // tm
    mix_tile0 = N_CTX_TOK // tm if group == "lat" else 0
    (out,) = _aliased_call(
        functools.partial(_mlp_kernel, group=group),
        grid=(n_tiles, D_FF // tk),
        in_specs=[
            pl.BlockSpec((tm, D_MODEL), lambda i, k: (x_tile0 + i, 0)),
            pl.BlockSpec((tm, D_MIX), lambda i, k: (mix_tile0 + i, 0)),
            pl.BlockSpec((None, D_MIX, D_MODEL), lambda i, k: (jl, 0, 0)),
            pl.BlockSpec((None, MOD_ROWS, N_MOD), lambda i, k: (li, 0, 0)),
            pl.BlockSpec((None, None, 1, D_MODEL), lambda i, k: (li, 1, 0, 0)),
            pl.BlockSpec((None, D_MODEL, tk), lambda i, k: (li, 0, k)),
            pl.BlockSpec((None, tk, D_MODEL), lambda i, k: (li, k, 0)),
        ],
        args=[x_src, mix, wo, mods, norm_w, w1, w2],
        out_specs=[pl.BlockSpec((tm, D_MODEL), lambda i, k: (out_tile0 + i, 0))],
        out_shape=[jax.ShapeDtypeStruct((out_rows, D_MODEL), F32)],
        carried=[out_prev], sem=("parallel", "arbitrary"), name=name,
        scratch_shapes=[pltpu.VMEM((tm, D_MODEL), BF16)])
    return out


def _rope_tables(width):
    t = np.arange(DEC_SEQ)
    half = HEAD_GROUP // 2
    inv = ROPE_BASE ** (-np.arange(0, half, 2, dtype=np.float64) / half)
    ang = np.concatenate([(t // GRID_W)[:, None] * inv, (t % GRID_W)[:, None] * inv], axis=-1)
    cos = np.repeat(np.cos(ang), 2, axis=-1)
    sin = np.repeat(np.sin(ang), 2, axis=-1)
    sign = np.tile(np.array([-1.0, 1.0]), HEAD_GROUP // 2)
    reps = width // HEAD_GROUP
    return (jnp.asarray(np.tile(cos, (1, reps)), F32),
            jnp.asarray(np.tile(sin * sign, (1, reps)), F32))


def _block_diag_ones(n):
    g = np.arange(n) // HEAD_GROUP
    return jnp.asarray(g[:, None] == g[None, :], BF16)


def _tile_row(w, width):
    return jnp.tile(w.astype(F32), width // w.shape[0])[None, :]


def _lambda_init(li):
    return 0.8 - 0.6 * math.exp(-0.3 * li)


def kernel(x_prompt, x_sample, cache_k_swa, cache_v_swa, state_hgrn, cache_k_diff, cache_v_diff, c, c_ctx, norm_w, w_ada, b_ada, w_in_even, w_out_even, hgrn_lb_logits, hgrn_norm_w, swa_qnorm_w, swa_knorm_w, swa_sink, w_in_odd, w_out_odd, diff_qnorm_w, diff_knorm_w, diff_lambda_p, diff_subln_w, w_mlp1, w_mlp2):
    assert PROJ_TM == MLP_TM
    n_ctx_tiles = N_CTX_TOK // PROJ_TM
    x = None
    x_ctx0 = x_prompt.reshape(N_CTX_TOK, D_MODEL)
    x_lat0 = x_sample.reshape(N_LAT_TOK, D_MODEL)
    c_all = jnp.concatenate(
        [c_ctx[None, :], c, jnp.zeros((MOD_ROWS - 1 - DEC_BATCH, D_MODEL), F32)], axis=0)
    mods = _mods_call(c_all, w_ada, b_ada.reshape(DEPTH, 1, N_MOD))

    cos_t, sin_t = _rope_tables(ROPE_TW)
    bd = _block_diag_ones(256)
    tabs_f = _hgrn_tables(False)
    tabs_b = _hgrn_tables(True)
    hgrn_tabs = (tabs_f[0], tabs_f[1], tabs_b[0], tabs_b[1])
    lbl = hgrn_lb_logits.astype(F32).reshape(N_EVEN * 2, F_A)
    norm_w4 = norm_w.astype(F32).reshape(DEPTH, 2, 1, D_MODEL)

    w_out_even_b = w_out_even.astype(BF16)
    w_out_odd_b = w_out_odd.astype(BF16)

    ck_swa = cache_k_swa.reshape(DEC_BATCH, N_EVEN, PAST_LEN, KV_B * HD_B)
    cv_swa = cache_v_swa.reshape(DEC_BATCH, N_EVEN, PAST_LEN, KV_B * HD_B)
    ck_diff = cache_k_diff.reshape(DEC_BATCH, N_ODD, PAST_LEN, ODD_W)
    cv_diff = cache_v_diff.reshape(DEC_BATCH, N_ODD, PAST_LEN, ODD_W)
    hgrn_nw = hgrn_norm_w.astype(F32).reshape(N_EVEN, 1, DV_A)
    sink = swa_sink.astype(F32).reshape(N_EVEN, 1, H_B)
    lam_p = diff_lambda_p.astype(F32)
    assert ATT_TQ == SEQ
    subln = jnp.broadcast_to(diff_subln_w.astype(F32)[:, :, None], (N_ODD, 2 * HD_C, ATT_TQ))

    even_kinds = ("silu_scale", "loggate0", "loggate1", "ident", "silu", "qnorm", "kv")
    odd_kinds = ("qnorm", "qnorm", "knorm", "knorm", "ident", "ident")

    k_swa = v_swa = states = k_diff = v_diff = None
    for li in range(DEPTH):
        j = li // 2
        srcs = ((x_ctx0, 0), (x_lat0, 0)) if li == 0 else ((x, 0), (x, n_ctx_tiles))
        if li % 2 == 0:
            p_ctx, p_lat = [
                _proj_call(src, t0, bool(lat), mods, li, norm_w4, w_in_even, j, lbl,
                           _tile_row(swa_qnorm_w[j], PROJ_TN),
                           _tile_row(swa_knorm_w[j], PROJ_TN),
                           cos_t, sin_t, bd, even_kinds, f"proj_even{j}_{lat}")
                for lat, (src, t0) in enumerate(srcs)]
            mix, states = _hgrn_call(p_ctx, hgrn_nw, hgrn_tabs, SEQ, BATCH, 0, None, j, None,
                                     states, f"hgrn_ctx{j}")
            (mix,) = _hgrn_call(p_lat, hgrn_nw, hgrn_tabs, DEC_SEQ, DEC_BATCH,
                                N_CTX_TOK // DEC_SEQ, state_hgrn, j, mix, None, f"hgrn_lat{j}")
            mix, k_swa, v_swa = _swa_ctx_call(p_ctx, sink, j, mix, k_swa, v_swa, f"swa_ctx{j}")
            mix = _swa_lat_call(p_lat, ck_swa, cv_swa, sink, j, mix, f"swa_lat{j}")
            wo = w_out_even_b
        else:
            p_ctx, p_lat = [
                _proj_call(src, t0, bool(lat), mods, li, norm_w4, w_in_odd, j, lbl,
                           _tile_row(diff_qnorm_w[j], PROJ_TN),
                           _tile_row(diff_knorm_w[j], PROJ_TN),
                           cos_t, sin_t, bd, odd_kinds, f"proj_odd{j}_{lat}")
                for lat, (src, t0) in enumerate(srcs)]
            lam_init = _lambda_init(li)
            mix, k_diff, v_diff = _diff_ctx_call(p_ctx, lam_p, subln, lam_init, j, k_diff,
                                                 v_diff, f"diff_ctx{j}")
            mix = _diff_lat_call(p_lat, ck_diff, cv_diff, lam_p, subln, lam_init, j, mix,
                                 f"diff_lat{j}")
            wo = w_out_odd_b
        last = li == DEPTH - 1
        if li == 0 or last:
            outs = []
            x_next = None
            for lat, (src, t0) in enumerate(srcs):
                rows = (N_LAT_TOK if lat else N_CTX_TOK) if last else N_TOK
                out_t0 = 0 if last else lat * n_ctx_tiles
                x_next = _mlp_call(src, t0, bool(lat), mix, wo, j, mods, li, norm_w4, w_mlp1,
                                   w_mlp2, None if last else x_next, rows, out_t0,
                                   f"mlp{li}_{lat}")
                outs.append(x_next)
            x = x_next
        else:
            x = _mlp_call(x, 0, None, mix, wo, j, mods, li, norm_w4, w_mlp1, w_mlp2, None, N_TOK,
                          0, f"mlp{li}")

    y_prompt = outs[0].reshape(BATCH, SEQ, D_MODEL)
    y_sample = outs[1].reshape(DEC_BATCH, DEC_SEQ, D_MODEL)
    return (y_prompt, y_sample,
            k_swa.reshape(BATCH, N_EVEN, SEQ, KV_B, HD_B),
            v_swa.reshape(BATCH, N_EVEN, SEQ, KV_B, HD_B),
            states,
            k_diff.reshape(BATCH, N_ODD, SEQ, H_C, 2, HD_C),
            v_diff.reshape(BATCH, N_ODD, SEQ, H_C, 2 * HD_C))
```

```python
import functools
import math

import numpy as np
import jax
import jax.numpy as jnp
from jax import lax
from jax.experimental import pallas as pl
from jax.experimental.pallas import tpu as pltpu

F32 = jnp.float32
BF16 = jnp.bfloat16

D_MODEL = 1024
BATCH = 16
SEQ = 256
DEPTH = 4
DEC_BATCH = 4
DEC_SEQ = 1024
PAST_LEN = 512
GRID_W = 64
N_EVEN = (DEPTH + 1) // 2
N_ODD = DEPTH // 2
H_A = 4
DK_A = 128
DV_A = D_MODEL // 2 // H_A
F_A = H_A * DK_A
H_B = 8
KV_B = 2
G_B = H_B // KV_B
HD_B = D_MODEL // 2 // H_B
WINDOW = 128
H_C = 8
HD_C = D_MODEL // (2 * H_C)
D_FF = 4 * D_MODEL
ROPE_BASE = 10000.0
EPS = 1e-6
EVEN_COLS = 3 * F_A + 2 * H_A * DV_A + (H_B + 2 * KV_B) * HD_B
ODD_W = H_C * 2 * HD_C
D_MIX = D_MODEL

N_CTX_TOK = BATCH * SEQ
N_LAT_TOK = DEC_BATCH * DEC_SEQ
N_TOK = N_CTX_TOK + N_LAT_TOK
MOD_ROWS = 8
N_MOD = 6 * D_MODEL

HEAD_GROUP = 64
HGRN_CHUNK = 128
HGRN_LEVELS = 7
HGRN_SPLIT = 3
LOG2_E = math.log2(math.e)
VMEM_LIMIT = 48 * 1024 * 1024

PROJ_TM = 1024
PROJ_TN = 512
ROPE_TW = 128
PROJ_TILES_CTX = 2
PROJ_TILES_LAT = 1
PROJ_RC = 128
MLP_TM = 1024
MLP_TK = 1024
MLP_RC = 256
ADA_TN = 3072
ATT_TQ = 256
HGRN_CTX_SEQS = 2
CTX_SEQS = 2
SWA_CTX_SEQS = 4
DIFF_LAT_HEADS = 4
ONES_ROWS = 16


def _silu(x):
    return x * jax.nn.sigmoid(x)


def _nt_dot(a, b):
    return lax.dot_general(a, b, (((1,), (1,)), ((), ())), preferred_element_type=F32)


def _pipelined(n, first, second):
    cur = first(0)
    for i in range(n):
        nxt = first(i + 1) if i + 1 < n else None
        second(i, cur)
        cur = nxt


def _params(sem):
    return pltpu.CompilerParams(dimension_semantics=sem, vmem_limit_bytes=VMEM_LIMIT)


def _aliased_call(kernel, *, grid, in_specs, args, out_specs, out_shape, carried, sem, name,
                  scratch_shapes=()):
    n_in = len(args)
    extra = [buf for buf in carried if buf is not None]
    aliases = {}
    for k, buf in enumerate(carried):
        if buf is not None:
            aliases[n_in + len(aliases)] = k
    n_extra = len(extra)

    def body(*refs):
        kernel(*refs[:n_in], *refs[n_in + n_extra:])

    return pl.pallas_call(
        body,
        grid=grid,
        in_specs=list(in_specs) + [pl.BlockSpec(memory_space=pl.ANY)] * n_extra,
        out_specs=out_specs,
        out_shape=out_shape,
        input_output_aliases=aliases,
        scratch_shapes=list(scratch_shapes),
        compiler_params=_params(sem),
        name=name,
    )(*args, *extra)


def _mods_kernel(c_ref, w_ref, b_ref, o_ref):
    s = _silu(c_ref[...]).astype(BF16)
    o_ref[...] = jnp.dot(s, w_ref[...].astype(BF16), preferred_element_type=F32) + b_ref[...]


def _mods_call(c_all, w_ada, b_ada):
    return pl.pallas_call(
        _mods_kernel,
        grid=(DEPTH, N_MOD // ADA_TN),
        in_specs=[
            pl.BlockSpec((MOD_ROWS, D_MODEL), lambda l, j: (0, 0)),
            pl.BlockSpec((None, D_MODEL, ADA_TN), lambda l, j: (l, 0, j)),
            pl.BlockSpec((None, 1, ADA_TN), lambda l, j: (l, 0, j)),
        ],
        out_specs=pl.BlockSpec((None, MOD_ROWS, ADA_TN), lambda l, j: (l, 0, j)),
        out_shape=jax.ShapeDtypeStruct((DEPTH, MOD_ROWS, N_MOD), F32),
        compiler_params=_params(("parallel", "parallel")),
        name="ada_mods",
    )(c_all, w_ada, b_ada)


def _norm_mod(x, nw, sc, sh):
    ms = jnp.mean(x * x, axis=-1, keepdims=True)
    return (x * lax.rsqrt(ms + EPS) * nw) * (1.0 + sc) + sh


def _group_rms(y, w_t, bd_ref):
    yy = (y * y).astype(BF16)
    bw = bd_ref.shape[0]
    parts = [jnp.dot(yy[:, s:s + bw], bd_ref[...], preferred_element_type=F32)
             for s in range(0, y.shape[1], bw)]
    ss = parts[0] if len(parts) == 1 else jnp.concatenate(parts, axis=1)
    return y * lax.rsqrt(ss * (1.0 / HEAD_GROUP) + EPS) * w_t


def _rope(y, cos, sin):
    n = y.shape[1]
    lane = lax.broadcasted_iota(jnp.int32, y.shape, 1)
    nxt = pltpu.roll(y, n - 1, axis=1)
    prv = pltpu.roll(y, 1, axis=1)
    swapped = jnp.where((lane & 1) == 0, nxt, prv)
    return y * cos + swapped * sin


def _lower_bounds(lbl_ref, jl):
    rows = [lbl_ref[pl.ds(2 * m, 2), :] for m in range(N_EVEN)]
    mx = functools.reduce(jnp.maximum, rows)
    es = [jnp.exp(r - mx) for r in rows]
    den = functools.reduce(lambda a, b: a + b, es)
    sm = [e / den for e in es]
    cs = sm[0]
    for m in range(1, jl + 1):
        cs = cs + sm[m]
    return cs - sm[0]


def _proj_kernel(x_ref, mods_ref, nw_ref, w_ref, wkv_ref, lbl_ref, qn_ref, kn_ref, cos_ref,
                 sin_ref, bd_ref, o_ref, h_scr, w_scr, *, kinds, jl, lat):
    tiles = PROJ_TILES_LAT if lat else PROJ_TILES_CTX
    j = pl.program_id(0)
    i = pl.program_id(1)
    tm, tn = PROJ_TM, o_ref.shape[1]
    kvw = 2 * KV_B * HD_B

    def normed_rows(t, c):
        row = 1 + tiles * i + t if lat else 0
        sh = mods_ref[pl.ds(row, 1), 0:D_MODEL]
        sc = mods_ref[pl.ds(row, 1), D_MODEL:2 * D_MODEL]
        r0 = t * tm + c * PROJ_RC
        return _norm_mod(x_ref[r0:r0 + PROJ_RC, :], nw_ref[...], sc, sh).astype(BF16)

    def finish(kind, y, rows):
        if kind == "silu_scale":
            return _silu(y) * (DK_A ** -0.5)
        if kind in ("loggate0", "loggate1"):
            d = int(kind[-1])
            lb = _lower_bounds(lbl_ref, jl)[d:d + 1, :]
            return jnp.log2(lb + (1.0 - lb) * jax.nn.sigmoid(y))
        if kind == "ident":
            return y
        if kind == "silu":
            return _silu(y)
        if kind in ("qnorm", "knorm"):
            w_t = qn_ref[...] if kind == "qnorm" else kn_ref[...]
            r = _group_rms(y, w_t, bd_ref)
            if lat:
                r = _rope(r, jnp.tile(cos_ref[rows, :], (1, tn // ROPE_TW)),
                          jnp.tile(sin_ref[rows, :], (1, tn // ROPE_TW)))
            return r * (HEAD_GROUP ** -0.5 * LOG2_E) if kind == "qnorm" else r
        if kind == "kv":
            kn = _group_rms(y, kn_ref[:, 0:kvw], bd_ref)
            if lat:
                kn = _rope(kn, jnp.tile(cos_ref[rows, :], (1, kvw // ROPE_TW)),
                           jnp.tile(sin_ref[rows, :], (1, kvw // ROPE_TW)))
            lane = lax.broadcasted_iota(jnp.int32, y.shape, 1)
            return jnp.where(lane < KV_B * HD_B, kn, y)
        raise ValueError(kind)

    def run(kind, jj):
        @pl.when(i == 0)
        def _():
            if kind == "kv":
                w_scr[:, 0:kvw] = wkv_ref[...].astype(BF16)
            else:
                w_scr[...] = w_ref[...].astype(BF16)

        per_tile = tm // PROJ_RC

        def first(r):
            t, c = divmod(r, per_tile)
            w = w_scr[:, 0:kvw] if kind == "kv" else w_scr[...]
            crows = slice(c * PROJ_RC, (c + 1) * PROJ_RC)
            if jj == 0:
                h = normed_rows(t, c)
                h_scr[tiles * i + t, crows, :] = h
            else:
                h = h_scr[tiles * i + t, crows, :]
            return jnp.dot(h, w, preferred_element_type=F32)

        def second(r, y):
            c = r % per_tile
            rows = slice(r * PROJ_RC, (r + 1) * PROJ_RC)
            seq_rows = slice(c * PROJ_RC, (c + 1) * PROJ_RC)
            if kind == "kv":
                o_ref[rows, 0:kvw] = finish(kind, y, seq_rows)
                o_ref[rows, kvw:tn] = jnp.zeros((PROJ_RC, tn - kvw), F32)
            else:
                o_ref[rows, :] = finish(kind, y, seq_rows)

        _pipelined(tiles * per_tile, first, second)

    for jj, kind in enumerate(kinds):
        pl.when(j == jj)(functools.partial(run, kind, jj))


def _proj_call(x_src, x_tile0, lat, p_prev, mods, li, norm_w, w, jl, lbl, qn_t, kn_t, cos_t, sin_t,
               bd, kinds, name):
    tm, tn = PROJ_TM, PROJ_TN
    assert tm == DEC_SEQ
    n_ctx_tiles = N_CTX_TOK // tm
    n_tiles = (N_LAT_TOK if lat else N_CTX_TOK) // tm
    tile0 = n_ctx_tiles if lat else 0
    if not lat:
        cos_t = sin_t = jnp.zeros((8, 128), F32)
    n_main = sum(1 for k in kinds if k != "kv")
    n_cols = tn * len(kinds)
    kvw = 2 * KV_B * HD_B
    kv_blk = (n_main * tn) // kvw if "kv" in kinds else 0
    const = lambda j, i: (0, 0)
    tiles = PROJ_TILES_LAT if lat else PROJ_TILES_CTX
    n_blk = n_tiles // tiles
    blk = tiles * tm
    x_blk0, out_blk0 = x_tile0 // tiles, tile0 // tiles
    assert x_tile0 % tiles == 0 and tile0 % tiles == 0
    (out,) = _aliased_call(
        functools.partial(_proj_kernel, kinds=kinds, jl=jl, lat=lat),
        grid=(len(kinds), n_blk),
        in_specs=[
            pl.BlockSpec((blk, D_MODEL),
                         lambda j, i: (x_blk0 + jnp.where(j == 0, i, n_blk - 1), 0)),
            pl.BlockSpec((None, MOD_ROWS, N_MOD), lambda j, i: (li, 0, 0)),
            pl.BlockSpec((None, None, 1, D_MODEL), lambda j, i: (li, 0, 0, 0)),
            pl.BlockSpec((None, D_MODEL, tn), lambda j, i: (jl, 0, jnp.minimum(j, n_main - 1))),
            pl.BlockSpec((None, D_MODEL, kvw), lambda j, i: (jl, 0, kv_blk)),
            pl.BlockSpec(lbl.shape, const),
            pl.BlockSpec((1, tn), const),
            pl.BlockSpec((1, tn), const),
            pl.BlockSpec(cos_t.shape, const),
            pl.BlockSpec(sin_t.shape, const),
            pl.BlockSpec(bd.shape, const),
        ],
        args=[x_src, mods, norm_w, w, w, lbl, qn_t, kn_t, cos_t, sin_t, bd],
        out_specs=[pl.BlockSpec((blk, tn), lambda j, i: (out_blk0 + i, j))],
        out_shape=[jax.ShapeDtypeStruct((N_TOK, n_cols), F32)],
        carried=[p_prev], sem=("arbitrary", "arbitrary"), name=name,
        scratch_shapes=[pltpu.VMEM((n_tiles, tm, D_MODEL), BF16),
                        pltpu.VMEM((D_MODEL, tn), BF16)])
    return out


def _hgrn_tables(rev):
    c = HGRN_CHUNK
    t = np.arange(c)
    w = (t[None, :] <= t[:, None]) if not rev else (t[None, :] >= t[:, None])
    ws = np.concatenate([w.astype(np.float32)] * HGRN_SPLIT, axis=1)
    x = t[:, None] ^ t[None, :]
    lv = np.where(x > 0, np.floor(np.log2(np.maximum(x, 1))).astype(np.int32), HGRN_LEVELS)
    causal = (t[None, :] < t[:, None]) if not rev else (t[None, :] > t[:, None])
    lv = np.where(causal | (x == 0), lv, -1).astype(np.int32)
    return jnp.asarray(ws, BF16), jnp.asarray(lv)


def _hgrn_level_exponents(cum, rev):
    c = HGRN_CHUNK
    sub_rows = 8
    c3 = cum.reshape(c // sub_rows, sub_rows, DK_A)
    sub = lax.broadcasted_iota(jnp.int32, c3.shape, 1)
    out = []
    for l in range(1, HGRN_LEVELS):
        hb = 1 << l
        if 2 * hb <= sub_rows:
            r = None
            for b0 in range(0, sub_rows, 2 * hb):
                idx = b0 + (hb if rev else hb - 1)
                rk = c3[:, idx:idx + 1, :]
                r = rk if r is None else jnp.where(sub < b0, r, rk)
            d = c3 - r
            bit = (sub & hb) != 0
            q_role = jnp.logical_not(bit) if rev else bit
            out.append(jnp.where(q_role, d, -d).reshape(c, DK_A))
        else:
            pieces = []
            for b0 in range(0, c, 2 * hb):
                mid = b0 + hb
                ridx = mid if rev else mid - 1
                r = cum[ridx:ridx + 1, :]
                lo = cum[b0:mid]
                hi = cum[mid:b0 + 2 * hb]
                pieces += [lo - r, r - hi] if rev else [r - lo, hi - r]
            out.append(jnp.concatenate(pieces, axis=0))
    return out


def _hgrn_level_operands(l, q, k, f, z, rev, row):
    c = HGRN_CHUNK
    hb = 1 << l
    if hb >= 8:
        zero = jnp.zeros((hb, DK_A), F32)
        qparts, kparts = [], []
        for b0 in range(0, c, 2 * hb):
            lo, hi = slice(b0, b0 + hb), slice(b0 + hb, b0 + 2 * hb)
            if rev:
                qparts += [q[lo] * z[lo], zero]
                kparts += [zero, k[hi] * z[hi]]
            else:
                qparts += [zero, q[hi] * z[hi]]
                kparts += [k[lo] * z[lo], zero]
        return (jnp.concatenate(qparts, axis=0).astype(BF16),
                jnp.concatenate(kparts, axis=0).astype(BF16))
    bit = ((row >> l) & 1) == 1
    q_role = jnp.logical_not(bit) if rev else bit
    ql = jnp.where(q_role, q * (f if l == 0 else z), 0.0).astype(BF16)
    kl = jnp.where(q_role, 0.0, k if l == 0 else k * z).astype(BF16)
    return ql, kl


def _hgrn_chunks(chains):
    c = HGRN_CHUNK
    row = lax.broadcasted_iota(jnp.int32, (c, DK_A), 0)
    cums = []
    for q, g, v, st, w_ref, lv_ref, rev in chains:
        terms = []
        rem = g
        for _ in range(HGRN_SPLIT):
            term = rem.astype(BF16)
            terms.append(term)
            rem = rem - term.astype(F32)
        cums.append(jnp.dot(w_ref[...], jnp.concatenate(terms, axis=0),
                            preferred_element_type=F32))
    work = []
    for (q, g, v, st, w_ref, lv_ref, rev), cum in zip(chains, cums):
        f = jnp.exp2(g)
        k = 1.0 - f
        last = 0 if rev else c - 1
        total_e = cum[last:last + 1, :]
        qd = (q * jnp.exp2(cum)).astype(BF16)
        kd = (k * jnp.exp2(total_e - cum)).astype(BF16)
        o = _nt_dot(qd, st.astype(BF16))
        st_new = (st * jnp.exp2(total_e)
                  + jnp.dot(v.T.astype(BF16), kd, preferred_element_type=F32))
        zs = [None] + [jnp.exp2(e) for e in _hgrn_level_exponents(cum, rev)]
        work.append((k, f, zs, o, st_new))
    accs = [None] * len(chains)
    for l in reversed(range(HGRN_LEVELS)):
        for i, ((q, g, v, st, w_ref, lv_ref, rev), (k, f, zs, o, st_new)) in enumerate(
                zip(chains, work)):
            ql, kl = _hgrn_level_operands(l, q, k, f, zs[l], rev, row)
            a_l = _nt_dot(ql, kl)
            accs[i] = a_l if accs[i] is None else jnp.where(lv_ref[...] == l, a_l, accs[i])
    outs = []
    for (q, g, v, st, w_ref, lv_ref, rev), (k, f, zs, o, st_new), a in zip(chains, work, accs):
        a = jnp.where(lv_ref[...] == HGRN_LEVELS, jnp.sum(q * k, axis=-1, keepdims=True), a)
        outs.append((o + jnp.dot(a.astype(BF16), v.astype(BF16), preferred_element_type=F32),
                     st_new))
    return outs


def _hgrn_kernel(*refs, n_chunks, seqs, has_init, emit_state):
    refs = list(refs)
    q_ref, gf_ref, gb_ref, v_ref, sg_ref, nw_ref, wf_ref, wb_ref, lvf_ref, lvb_ref = refs[:10]
    pos = 10
    s0_ref = None
    if has_init:
        s0_ref = refs[pos]
        pos += 1
    o_ref = refs[pos]
    pos += 1
    so_ref = None
    if emit_state:
        so_ref = refs[pos]
        pos += 1
    of_scr, ob_scr, st_scr = refs[pos:pos + 3]

    seq_len = n_chunks * HGRN_CHUNK
    for s in range(seqs):
        for d in range(2):
            for h in range(H_A):
                if has_init:
                    st_scr[s, d, h] = s0_ref[s, d, h].T
                else:
                    st_scr[s, d, h] = jnp.zeros((DV_A, DK_A), F32)

    def body(c, carry):
        chains, dests = [], []
        for s in range(seqs):
            for h in range(H_A):
                cols = slice(h * DK_A, (h + 1) * DK_A)
                for d, (g_ref, w_ref, lv_ref, scr) in enumerate(
                        ((gf_ref, wf_ref, lvf_ref, of_scr), (gb_ref, wb_ref, lvb_ref, ob_scr))):
                    cc = c if d == 0 else n_chunks - 1 - c
                    r0 = pl.multiple_of(s * seq_len + cc * HGRN_CHUNK, HGRN_CHUNK)
                    rows = pl.ds(r0, HGRN_CHUNK)
                    chains.append((q_ref[rows, cols], g_ref[rows, cols], v_ref[rows, cols],
                                   st_scr[s, d, h], w_ref, lv_ref, d == 1))
                    dests.append((scr, rows, cols, s, d, h))
        for (o, st), (scr, rows, cols, s, d, h) in zip(_hgrn_chunks(chains), dests):
            st_scr[s, d, h] = st
            scr[rows, cols] = o
        return carry

    lax.fori_loop(0, n_chunks, body, 0)
    for h in range(H_A):
        cols = slice(h * DV_A, (h + 1) * DV_A)
        o = of_scr[:, cols] + ob_scr[:, cols]
        y = o * lax.rsqrt(jnp.mean(o * o, axis=-1, keepdims=True) + EPS) * nw_ref[...]
        o_ref[:, cols] = (y * sg_ref[:, cols]).astype(o_ref.dtype)
    if emit_state:
        for s in range(seqs):
            for d in range(2):
                for h in range(H_A):
                    so_ref[s, d, h] = st_scr[s, d, h].T


def _hgrn_call(p, nw, tabs, seq_len, n_seq, row_blk0, s0, jl, mix, state_out, name):
    wf, lvf, wb, lvb = tabs
    has_init = s0 is not None
    emit_state = s0 is None
    seqs = HGRN_CTX_SEQS if emit_state else 1
    assert row_blk0 % seqs == 0 and n_seq % seqs == 0
    blk0 = row_blk0 // seqs
    const = lambda b: (0, 0)
    blk = (seqs * seq_len, F_A)
    state_spec = pl.BlockSpec((seqs, None, 2, H_A, DK_A, DV_A), lambda b: (b, jl, 0, 0, 0, 0))
    in_specs = [pl.BlockSpec(blk, (lambda b, part=part: (blk0 + b, part))) for part in range(5)]
    in_specs += [
        pl.BlockSpec((None, 1, DV_A), lambda b: (jl, 0, 0)),
        pl.BlockSpec(wf.shape, const), pl.BlockSpec(wb.shape, const),
        pl.BlockSpec(lvf.shape, const), pl.BlockSpec(lvb.shape, const),
    ]
    args = [p, p, p, p, p, nw, wf, wb, lvf, lvb]
    if has_init:
        in_specs.append(state_spec)
        args.append(s0)
    out_shape = [jax.ShapeDtypeStruct((N_TOK, D_MIX), BF16)]
    out_specs = [pl.BlockSpec((seqs * seq_len, H_A * DV_A), lambda b: (blk0 + b, 0))]
    carried = [mix]
    if emit_state:
        out_shape.append(jax.ShapeDtypeStruct((BATCH, N_EVEN, 2, H_A, DK_A, DV_A), F32))
        out_specs.append(state_spec)
        carried.append(state_out)
    return _aliased_call(
        functools.partial(_hgrn_kernel, n_chunks=seq_len // HGRN_CHUNK, seqs=seqs,
                          has_init=has_init, emit_state=emit_state),
        grid=(n_seq // seqs,), in_specs=in_specs, args=args, out_specs=out_specs,
        out_shape=out_shape, carried=carried, sem=("parallel",), name=name,
        scratch_shapes=[pltpu.VMEM((seqs * seq_len, H_A * DV_A), F32),
                        pltpu.VMEM((seqs * seq_len, H_A * DV_A), F32),
                        pltpu.VMEM((seqs, 2, H_A, DV_A, DK_A), F32)])


def _swa_ctx_kernel(q_ref, kv_ref, sink_ref, o_ref, kc_ref, vc_ref):
    for s in range(SWA_CTX_SEQS):
        rows = slice(s * SEQ, (s + 1) * SEQ)
        kv = kv_ref[rows, :]
        k32 = kv[:, 0:KV_B * HD_B]
        v32 = kv[:, KV_B * HD_B:2 * KV_B * HD_B]
        kc_ref[s] = k32
        vc_ref[s] = v32
        k = k32.astype(BF16)
        vt32 = v32.T
        vts = [_with_ones_rows(vt32[n * HD_B:(n + 1) * HD_B]) for n in range(KV_B)]
        q = q_ref[rows, :].astype(BF16)
        ksl = [slice((h // G_B) * HD_B, (h // G_B + 1) * HD_B) for h in range(H_B)]
        sts = [_nt_dot(k[:, ksl[h]], q[:, h * HD_B:(h + 1) * HD_B]) for h in range(H_B)]
        ps, sinks = [], []
        for h in range(H_B):
            sink = sink_ref[0:1, h:h + 1] * LOG2_E
            m = jnp.maximum(jnp.max(sts[h], axis=0, keepdims=True), sink)
            ps.append(jnp.exp2(sts[h] - m).astype(BF16))
            sinks.append(jnp.exp2(sink - m))
        outs = []
        for h in range(H_B):
            ota = jnp.dot(vts[h // G_B], ps[h], preferred_element_type=F32)
            outs.append(ota[0:HD_B] / (ota[HD_B:HD_B + 1] + sinks[h]))
        o_ref[rows, :] = jnp.concatenate(outs, axis=0).T.astype(o_ref.dtype)


def _swa_ctx_call(p, sink, jl, mix, k_out, v_out, name):
    qcol = (3 * F_A + 2 * H_A * DV_A) // PROJ_TN
    rows = SWA_CTX_SEQS * SEQ
    cache_spec = pl.BlockSpec((SWA_CTX_SEQS, None, SEQ, KV_B * HD_B), lambda b: (b, jl, 0, 0))
    cache_shape = jax.ShapeDtypeStruct((BATCH, N_EVEN, SEQ, KV_B * HD_B), F32)
    return _aliased_call(
        _swa_ctx_kernel,
        grid=(BATCH // SWA_CTX_SEQS,),
        in_specs=[
            pl.BlockSpec((rows, H_B * HD_B), lambda b: (b, qcol)),
            pl.BlockSpec((rows, PROJ_TN), lambda b: (b, qcol + 1)),
            pl.BlockSpec((None, 1, H_B), lambda b: (jl, 0, 0)),
        ],
        args=[p, p, sink],
        out_specs=[pl.BlockSpec((rows, H_B * HD_B), lambda b: (b, 1)), cache_spec, cache_spec],
        out_shape=[jax.ShapeDtypeStruct((N_TOK, D_MIX), BF16), cache_shape, cache_shape],
        carried=[mix, k_out, v_out], sem=("parallel",), name=name)


def _swa_lat_kernel(q_ref, kv_ref, ck_ref, cv_ref, sink_ref, o_ref):
    qi = pl.program_id(1)
    tq = q_ref.shape[0]
    span = tq + 2 * WINDOW
    ws = pl.multiple_of(jnp.clip(qi * tq - WINDOW, 0, DEC_SEQ - span), WINDOW)
    kvw = kv_ref[pl.ds(ws, span), :]
    kw = kvw[:, 0:KV_B * HD_B].astype(BF16)
    vwt = kvw[:, KV_B * HD_B:2 * KV_B * HD_B].T.astype(BF16)
    kc = ck_ref[...].astype(BF16)
    vct = cv_ref[...].T.astype(BF16)
    q = q_ref[...].astype(BF16)
    t_k = ws + lax.broadcasted_iota(jnp.int32, (span, tq), 0)
    t_q = qi * tq + lax.broadcasted_iota(jnp.int32, (span, tq), 1)
    valid = jnp.abs(t_q - t_k) <= WINDOW
    outs = []

    def first(h):
        qh = q[:, h * HD_B:(h + 1) * HD_B]
        ksl = slice((h // G_B) * HD_B, (h // G_B + 1) * HD_B)
        return _nt_dot(kw[:, ksl], qh), _nt_dot(kc[:, ksl], qh)

    def second(h, scores):
        ksl = slice((h // G_B) * HD_B, (h // G_B + 1) * HD_B)
        s_w = jnp.where(valid, scores[0], -jnp.inf)
        s_c = scores[1]
        sink = sink_ref[0:1, h:h + 1] * LOG2_E
        m = jnp.maximum(jnp.maximum(jnp.max(s_w, axis=0, keepdims=True),
                                    jnp.max(s_c, axis=0, keepdims=True)), sink)
        p_w = jnp.exp2(s_w - m)
        p_c = jnp.exp2(s_c - m)
        den = (jnp.sum(p_w, axis=0, keepdims=True) + jnp.sum(p_c, axis=0, keepdims=True)
               + jnp.exp2(sink - m))
        ot = (jnp.dot(vwt[ksl, :], p_w.astype(BF16), preferred_element_type=F32)
              + jnp.dot(vct[ksl, :], p_c.astype(BF16), preferred_element_type=F32))
        outs.append(ot / den)

    _pipelined(H_B, first, second)
    o_ref[...] = jnp.concatenate(outs, axis=0).T.astype(o_ref.dtype)


def _swa_lat_call(p, ck, cv, sink, jl, mix, name):
    tq = ATT_TQ
    qcol = (3 * F_A + 2 * H_A * DV_A) // PROJ_TN
    nq = DEC_SEQ // tq
    q_blk0 = N_CTX_TOK // tq
    s_blk0 = N_CTX_TOK // DEC_SEQ
    cache_spec = pl.BlockSpec((None, None, PAST_LEN, KV_B * HD_B), lambda b, i: (b, jl, 0, 0))
    (out,) = _aliased_call(
        _swa_lat_kernel,
        grid=(DEC_BATCH, nq),
        in_specs=[
            pl.BlockSpec((tq, H_B * HD_B), lambda b, i: (q_blk0 + b * nq + i, qcol)),
            pl.BlockSpec((DEC_SEQ, PROJ_TN), lambda b, i: (s_blk0 + b, qcol + 1)),
            cache_spec, cache_spec,
            pl.BlockSpec((None, 1, H_B), lambda b, i: (jl, 0, 0)),
        ],
        args=[p, p, ck, cv, sink],
        out_specs=[pl.BlockSpec((tq, H_B * HD_B), lambda b, i: (q_blk0 + b * nq + i, 1))],
        out_shape=[jax.ShapeDtypeStruct((N_TOK, D_MIX), BF16)],
        carried=[mix], sem=("parallel", "parallel"), name=name)
    return out


def _diff_lambda(lp_ref, lam_init):
    lp = lp_ref[...]
    a = jnp.sum(lp[0:1] * lp[1:2], axis=-1, keepdims=True)
    b = jnp.sum(lp[2:3] * lp[3:4], axis=-1, keepdims=True)
    return jnp.exp(a) - jnp.exp(b) + lam_init


def _diff_scores(q, k_parts):
    return [[_nt_dot(kp[:, c * HD_C:(c + 1) * HD_C], q[:, c * HD_C:(c + 1) * HD_C])
             for kp in k_parts] for c in range(2)]


def _diff_finish(scores, vt_parts, lam, lam_init, sw_t):
    hw = 2 * HD_C
    comps = []
    for c in range(2):
        ss = scores[c]
        m = functools.reduce(jnp.maximum, [jnp.max(s, axis=0, keepdims=True) for s in ss])
        ota = functools.reduce(
            lambda a, b: a + b,
            [jnp.dot(vt, jnp.exp2(s - m).astype(BF16), preferred_element_type=F32)
             for s, vt in zip(ss, vt_parts)])
        comps.append(ota[0:hw] / ota[hw:hw + 1])
    ot = comps[0] - lam * comps[1]
    yt = ot * lax.rsqrt(jnp.mean(ot * ot, axis=0, keepdims=True) + EPS) * sw_t
    return (yt * (1.0 - lam_init)).T


def _with_ones_rows(vt):
    return jnp.concatenate([vt, jnp.ones((ONES_ROWS, vt.shape[1]), F32)], axis=0).astype(BF16)


def _diff_ctx_kernel(q_ref, k_ref, v_ref, lp_ref, sw_ref, o_ref, kc_ref, vc_ref, *, lam_init):
    lam = _diff_lambda(lp_ref, lam_init)
    hw = 2 * HD_C
    vts = {}

    def first(n):
        s, h = divmod(n, H_C)
        rows = slice(s * SEQ, (s + 1) * SEQ)
        sl = slice(h * hw, (h + 1) * hw)
        k32 = k_ref[rows, sl]
        v32 = v_ref[rows, sl]
        kc_ref[s, :, sl] = k32
        vc_ref[s, :, sl] = v32
        vts[n] = _with_ones_rows(v32.T)
        return _diff_scores(q_ref[rows, sl].astype(BF16), [k32.astype(BF16)])

    def second(n, scores):
        s, h = divmod(n, H_C)
        y = _diff_finish(scores, [vts.pop(n)], lam, lam_init, sw_ref[...])
        o_ref[s * SEQ:(s + 1) * SEQ, h * hw:(h + 1) * hw] = y.astype(o_ref.dtype)

    _pipelined(CTX_SEQS * H_C, first, second)


def _diff_ctx_call(p, lp, sw, lam_init, jl, k_out, v_out, name):
    rows = CTX_SEQS * SEQ
    cache_spec = pl.BlockSpec((CTX_SEQS, None, SEQ, ODD_W), lambda b: (b, jl, 0, 0))
    cache_shape = jax.ShapeDtypeStruct((BATCH, N_ODD, SEQ, ODD_W), F32)
    return _aliased_call(
        functools.partial(_diff_ctx_kernel, lam_init=lam_init),
        grid=(BATCH // CTX_SEQS,),
        in_specs=[
            pl.BlockSpec((rows, ODD_W), lambda b: (b, 0)),
            pl.BlockSpec((rows, ODD_W), lambda b: (b, 1)),
            pl.BlockSpec((rows, ODD_W), lambda b: (b, 2)),
            pl.BlockSpec((None, 4, HD_C), lambda b: (jl, 0, 0)),
            pl.BlockSpec((None, 2 * HD_C, ATT_TQ), lambda b: (jl, 0, 0)),
        ],
        args=[p, p, p, lp, sw],
        out_specs=[pl.BlockSpec((rows, ODD_W), lambda b: (b, 0)), cache_spec, cache_spec],
        out_shape=[jax.ShapeDtypeStruct((N_TOK, D_MIX), BF16), cache_shape, cache_shape],
        carried=[None, k_out, v_out], sem=("parallel",), name=name)


def _diff_lat_kernel(q_ref, k_ref, v_ref, ck_ref, cv_ref, lp_ref, sw_ref, o_ref, *, lam_init):
    lam = _diff_lambda(lp_ref, lam_init)
    hw = 2 * HD_C
    tq = ATT_TQ
    nq = q_ref.shape[0] // tq
    heads = []
    for g in range(q_ref.shape[1] // hw):
        sl = slice(g * hw, (g + 1) * hw)
        heads.append(([k_ref[:, sl].astype(BF16), ck_ref[:, sl].astype(BF16)],
                      [_with_ones_rows(v_ref[:, sl].T), _with_ones_rows(cv_ref[:, sl].T)]))

    def first(n):
        g, i = divmod(n, nq)
        q = q_ref[i * tq:(i + 1) * tq, g * hw:(g + 1) * hw].astype(BF16)
        return _diff_scores(q, heads[g][0])

    def second(n, scores):
        g, i = divmod(n, nq)
        y = _diff_finish(scores, heads[g][1], lam, lam_init, sw_ref[...])
        o_ref[i * tq:(i + 1) * tq, g * hw:(g + 1) * hw] = y.astype(o_ref.dtype)

    _pipelined(len(heads) * nq, first, second)


def _diff_lat_call(p, ck, cv, lp, sw, lam_init, jl, mix, name):
    hw = 2 * HD_C
    gw = DIFF_LAT_HEADS * hw
    n_g = H_C // DIFF_LAT_HEADS
    s_blk0 = N_CTX_TOK // DEC_SEQ
    cache_spec = pl.BlockSpec((None, None, PAST_LEN, gw), lambda b, h: (b, jl, 0, h))
    (out,) = _aliased_call(
        functools.partial(_diff_lat_kernel, lam_init=lam_init),
        grid=(DEC_BATCH, n_g),
        in_specs=[
            pl.BlockSpec((DEC_SEQ, gw), lambda b, h: (s_blk0 + b, h)),
            pl.BlockSpec((DEC_SEQ, gw), lambda b, h: (s_blk0 + b, n_g + h)),
            pl.BlockSpec((DEC_SEQ, gw), lambda b, h: (s_blk0 + b, 2 * n_g + h)),
            cache_spec, cache_spec,
            pl.BlockSpec((None, 4, HD_C), lambda b, h: (jl, 0, 0)),
            pl.BlockSpec((None, hw, ATT_TQ), lambda b, h: (jl, 0, 0)),
        ],
        args=[p, p, p, ck, cv, lp, sw],
        out_specs=[pl.BlockSpec((DEC_SEQ, gw), lambda b, h: (s_blk0 + b, h))],
        out_shape=[jax.ShapeDtypeStruct((N_TOK, D_MIX), BF16)],
        carried=[mix], sem=("parallel", "parallel"), name=name)
    return out


def _mlp_kernel(x_ref, mix_ref, wo_ref, mods_ref, nw_ref, w1_ref, w2_ref, o_ref,
                h_scr, *, group):
    i = pl.program_id(0)
    k = pl.program_id(1)
    n_ctx_tiles = N_CTX_TOK // x_ref.shape[0]
    row = {"ctx": 0, "lat": 1 + i,
           "all": jnp.where(i >= n_ctx_tiles, 1 + i - n_ctx_tiles, 0)}[group]

    def mod(a):
        return mods_ref[pl.ds(row, 1), a * D_MODEL:(a + 1) * D_MODEL]

    @pl.when(k == 0)
    def _():
        def first(r):
            return jnp.dot(mix_ref[r * MLP_RC:(r + 1) * MLP_RC, :], wo_ref[...],
                           preferred_element_type=F32)

        def second(r, y):
            rows = slice(r * MLP_RC, (r + 1) * MLP_RC)
            x1 = x_ref[rows, :] + mod(2) * y
            o_ref[rows, :] = x1
            h_scr[rows, :] = _norm_mod(x1, nw_ref[...], mod(4), mod(3)).astype(BF16)

        _pipelined(x_ref.shape[0] // MLP_RC, first, second)

    u = jnp.dot(h_scr[...], w1_ref[...].astype(BF16), preferred_element_type=F32)
    u = jnp.square(jnp.maximum(u, 0.0)).astype(BF16)
    o_ref[...] += mod(5) * jnp.dot(u, w2_ref[...].astype(BF16), preferred_element_type=F32)


def _mlp_call(x_src, x_tile0, lat, mix, wo, jl, mods, li, norm_w, w1, w2, out_prev, out_rows,
              out_tile0, name):
    tm, tk = MLP_TM, MLP_TK
    group = {False: "ctx", True: "lat", None: "all"}[lat]
    n_tiles = {"ctx": N_CTX_TOK, "lat": N_LAT_TOK, "all": N_TOK}[group] // tm
    mix_tile0 = N_CTX_TOK // tm if group == "lat" else 0
    (out,) = _aliased_call(
        functools.partial(_mlp_kernel, group=group),
        grid=(n_tiles, D_FF // tk),
        in_specs=[
            pl.BlockSpec((tm, D_MODEL), lambda i, k: (x_tile0 + i, 0)),
            pl.BlockSpec((tm, D_MIX), lambda i, k: (mix_tile0 + i, 0)),
            pl.BlockSpec((None, D_MIX, D_MODEL), lambda i, k: (jl, 0, 0)),
            pl.BlockSpec((None, MOD_ROWS, N_MOD), lambda i, k: (li, 0, 0)),
            pl.BlockSpec((None, None, 1, D_MODEL), lambda i, k: (li, 1, 0, 0)),
            pl.BlockSpec((None, D_MODEL, tk), lambda i, k: (li, 0, k)),
            pl.BlockSpec((None, tk, D_MODEL), lambda i, k: (li, k, 0)),
        ],
        args=[x_src, mix, wo, mods, norm_w, w1, w2],
        out_specs=[pl.BlockSpec((tm, D_MODEL), lambda i, k: (out_tile0 + i, 0))],
        out_shape=[jax.ShapeDtypeStruct((out_rows, D_MODEL), F32)],
        carried=[out_prev], sem=("parallel", "arbitrary"), name=name,
        scratch_shapes=[pltpu.VMEM((tm, D_MODEL), BF16)])
    return out


def _rope_tables(width):
    t = np.arange(DEC_SEQ)
    half = HEAD_GROUP // 2
    inv = ROPE_BASE ** (-np.arange(0, half, 2, dtype=np.float64) / half)
    ang = np.concatenate([(t // GRID_W)[:, None] * inv, (t % GRID_W)[:, None] * inv], axis=-1)
    cos = np.repeat(np.cos(ang), 2, axis=-1)
    sin = np.repeat(np.sin(ang), 2, axis=-1)
    sign = np.tile(np.array([-1.0, 1.0]), HEAD_GROUP // 2)
    reps = width // HEAD_GROUP
    return (jnp.asarray(np.tile(cos, (1, reps)), F32),
            jnp.asarray(np.tile(sin * sign, (1, reps)), F32))


def _block_diag_ones(n):
    g = np.arange(n) // HEAD_GROUP
    return jnp.asarray(g[:, None] == g[None, :], BF16)


def _tile_row(w, width):
    return jnp.tile(w.astype(F32), width // w.shape[0])[None, :]


def _lambda_init(li):
    return 0.8 - 0.6 * math.exp(-0.3 * li)


def kernel(x_prompt, x_sample, cache_k_swa, cache_v_swa, state_hgrn, cache_k_diff, cache_v_diff, c, c_ctx, norm_w, w_ada, b_ada, w_in_even, w_out_even, hgrn_lb_logits, hgrn_norm_w, swa_qnorm_w, swa_knorm_w, swa_sink, w_in_odd, w_out_odd, diff_qnorm_w, diff_knorm_w, diff_lambda_p, diff_subln_w, w_mlp1, w_mlp2):
    assert PROJ_TM == MLP_TM
    n_ctx_tiles = N_CTX_TOK // PROJ_TM
    x = None
    x_ctx0 = x_prompt.reshape(N_CTX_TOK, D_MODEL)
    x_lat0 = x_sample.reshape(N_LAT_TOK, D_MODEL)
    c_all = jnp.concatenate(
        [c_ctx[None, :], c, jnp.zeros((MOD_ROWS - 1 - DEC_BATCH, D_MODEL), F32)], axis=0)
    mods = _mods_call(c_all, w_ada, b_ada.reshape(DEPTH, 1, N_MOD))

    cos_t, sin_t = _rope_tables(ROPE_TW)
    bd = _block_diag_ones(256)
    tabs_f = _hgrn_tables(False)
    tabs_b = _hgrn_tables(True)
    hgrn_tabs = (tabs_f[0], tabs_f[1], tabs_b[0], tabs_b[1])
    lbl = hgrn_lb_logits.astype(F32).reshape(N_EVEN * 2, F_A)
    norm_w4 = norm_w.astype(F32).reshape(DEPTH, 2, 1, D_MODEL)

    w_out_even_b = w_out_even.astype(BF16)
    w_out_odd_b = w_out_odd.astype(BF16)

    ck_swa = cache_k_swa.reshape(DEC_BATCH, N_EVEN, PAST_LEN, KV_B * HD_B)
    cv_swa = cache_v_swa.reshape(DEC_BATCH, N_EVEN, PAST_LEN, KV_B * HD_B)
    ck_diff = cache_k_diff.reshape(DEC_BATCH, N_ODD, PAST_LEN, ODD_W)
    cv_diff = cache_v_diff.reshape(DEC_BATCH, N_ODD, PAST_LEN, ODD_W)
    hgrn_nw = hgrn_norm_w.astype(F32).reshape(N_EVEN, 1, DV_A)
    sink = swa_sink.astype(F32).reshape(N_EVEN, 1, H_B)
    lam_p = diff_lambda_p.astype(F32)
    assert ATT_TQ == SEQ
    subln = jnp.broadcast_to(diff_subln_w.astype(F32)[:, :, None], (N_ODD, 2 * HD_C, ATT_TQ))

    even_kinds = ("silu_scale", "loggate0", "loggate1", "ident", "silu", "qnorm", "kv")
    odd_kinds = ("qnorm", "qnorm", "knorm", "knorm", "ident", "ident")

    k_swa = v_swa = states = k_diff = v_diff = None
    for li in range(DEPTH):
        j = li // 2
        srcs = ((x_ctx0, 0), (x_lat0, 0)) if li == 0 else ((x, 0), (x, n_ctx_tiles))
        if li % 2 == 0:
            p = None
            for lat, (src, t0) in enumerate(srcs):
                p = _proj_call(src, t0, bool(lat), p, mods, li, norm_w4, w_in_even, j, lbl,
                               _tile_row(swa_qnorm_w[j], PROJ_TN),
                               _tile_row(swa_knorm_w[j], PROJ_TN),
                               cos_t, sin_t, bd, even_kinds, f"proj_even{j}_{lat}")
            mix, states = _hgrn_call(p, hgrn_nw, hgrn_tabs, SEQ, BATCH, 0, None, j, None, states,
                                     f"hgrn_ctx{j}")
            (mix,) = _hgrn_call(p, hgrn_nw, hgrn_tabs, DEC_SEQ, DEC_BATCH, N_CTX_TOK // DEC_SEQ,
                                state_hgrn, j, mix, None, f"hgrn_lat{j}")
            mix, k_swa, v_swa = _swa_ctx_call(p, sink, j, mix, k_swa, v_swa, f"swa_ctx{j}")
            mix = _swa_lat_call(p, ck_swa, cv_swa, sink, j, mix, f"swa_lat{j}")
            wo = w_out_even_b
        else:
            p = None
            for lat, (src, t0) in enumerate(srcs):
                p = _proj_call(src, t0, bool(lat), p, mods, li, norm_w4, w_in_odd, j, lbl,
                               _tile_row(diff_qnorm_w[j], PROJ_TN),
                               _tile_row(diff_knorm_w[j], PROJ_TN),
                               cos_t, sin_t, bd, odd_kinds, f"proj_odd{j}_{lat}")
            lam_init = _lambda_init(li)
            mix, k_diff, v_diff = _diff_ctx_call(p, lam_p, subln, lam_init, j, k_diff, v_diff,
                                                 f"diff_ctx{j}")
            mix = _diff_lat_call(p, ck_diff, cv_diff, lam_p, subln, lam_init, j, mix,
                                 f"diff_lat{j}")
            wo = w_out_odd_b
        last = li == DEPTH - 1
        if li == 0 or last:
            outs = []
            x_next = None
            for lat, (src, t0) in enumerate(srcs):
                rows = (N_LAT_TOK if lat else N_CTX_TOK) if last else N_TOK
                out_t0 = 0 if last else lat * n_ctx_tiles
                x_next = _mlp_call(src, t0, bool(lat), mix, wo, j, mods, li, norm_w4, w_mlp1,
                                   w_mlp2, None if last else x_next, rows, out_t0,
                                   f"mlp{li}_{lat}")
                outs.append(x_next)
            x = x_next
        else:
            x = _mlp_call(x, 0, None, mix, wo, j, mods, li, norm_w4, w_mlp1, w_mlp2, None, N_TOK,
                          0, f"mlp{li}")

    y_prompt = outs[0].reshape(BATCH, SEQ, D_MODEL)
    y_sample = outs[1].reshape(DEC_BATCH, DEC_SEQ, D_MODEL)
    return (y_prompt, y_sample,
            k_swa.reshape(BATCH, N_EVEN, SEQ, KV_B, HD_B),
            v_swa.reshape(BATCH, N_EVEN, SEQ, KV_B, HD_B),
            states,
            k_diff.reshape(BATCH, N_ODD, SEQ, H_C, 2, HD_C),
            v_diff.reshape(BATCH, N_ODD, SEQ, H_C, 2 * HD_C))
```

```python
import functools
import math

import numpy as np
import jax
import jax.numpy as jnp
from jax import lax
from jax.experimental import pallas as pl
from jax.experimental.pallas import tpu as pltpu

F32 = jnp.float32
BF16 = jnp.bfloat16

D_MODEL = 1024
BATCH = 16
SEQ = 256
DEPTH = 4
DEC_BATCH = 4
DEC_SEQ = 1024
PAST_LEN = 512
GRID_W = 64
N_EVEN = (DEPTH + 1) // 2
N_ODD = DEPTH // 2
H_A = 4
DK_A = 128
DV_A = D_MODEL // 2 // H_A
F_A = H_A * DK_A
H_B = 8
KV_B = 2
G_B = H_B // KV_B
HD_B = D_MODEL // 2 // H_B
WINDOW = 128
H_C = 8
HD_C = D_MODEL // (2 * H_C)
D_FF = 4 * D_MODEL
ROPE_BASE = 10000.0
EPS = 1e-6
EVEN_COLS = 3 * F_A + 2 * H_A * DV_A + (H_B + 2 * KV_B) * HD_B
ODD_W = H_C * 2 * HD_C
D_MIX = D_MODEL

N_CTX_TOK = BATCH * SEQ
N_LAT_TOK = DEC_BATCH * DEC_SEQ
N_TOK = N_CTX_TOK + N_LAT_TOK
MOD_ROWS = 8
N_MOD = 6 * D_MODEL

HEAD_GROUP = 64
HGRN_CHUNK = 128
HGRN_LEVELS = 7
HGRN_SPLIT = 3
LOG2_E = math.log2(math.e)
VMEM_LIMIT = 48 * 1024 * 1024

PROJ_TM = 1024
PROJ_TN = 512
ROPE_TW = 128
PROJ_TILES_CTX = 2
PROJ_TILES_LAT = 1
PROJ_RC = 128
MLP_TM = 1024
MLP_TK = 1024
MLP_RC = 256
ADA_TN = 1536
ATT_TQ = 256
HGRN_CTX_SEQS = 2
CTX_SEQS = 2
SWA_CTX_SEQS = 4
DIFF_LAT_HEADS = 4
ONES_ROWS = 16


def _silu(x):
    return x * jax.nn.sigmoid(x)


def _nt_dot(a, b):
    return lax.dot_general(a, b, (((1,), (1,)), ((), ())), preferred_element_type=F32)


def _pipelined(n, first, second):
    cur = first(0)
    for i in range(n):
        nxt = first(i + 1) if i + 1 < n else None
        second(i, cur)
        cur = nxt


def _params(sem):
    return pltpu.CompilerParams(dimension_semantics=sem, vmem_limit_bytes=VMEM_LIMIT)


def _aliased_call(kernel, *, grid, in_specs, args, out_specs, out_shape, carried, sem, name,
                  scratch_shapes=()):
    n_in = len(args)
    extra = [buf for buf in carried if buf is not None]
    aliases = {}
    for k, buf in enumerate(carried):
        if buf is not None:
            aliases[n_in + len(aliases)] = k
    n_extra = len(extra)

    def body(*refs):
        kernel(*refs[:n_in], *refs[n_in + n_extra:])

    return pl.pallas_call(
        body,
        grid=grid,
        in_specs=list(in_specs) + [pl.BlockSpec(memory_space=pl.ANY)] * n_extra,
        out_specs=out_specs,
        out_shape=out_shape,
        input_output_aliases=aliases,
        scratch_shapes=list(scratch_shapes),
        compiler_params=_params(sem),
        name=name,
    )(*args, *extra)


def _mods_kernel(c_ref, w_ref, b_ref, o_ref):
    s = _silu(c_ref[...]).astype(BF16)
    o_ref[...] = jnp.dot(s, w_ref[...].astype(BF16), preferred_element_type=F32) + b_ref[...]


def _mods_call(c_all, w_ada, b_ada):
    return pl.pallas_call(
        _mods_kernel,
        grid=(DEPTH, N_MOD // ADA_TN),
        in_specs=[
            pl.BlockSpec((MOD_ROWS, D_MODEL), lambda l, j: (0, 0)),
            pl.BlockSpec((None, D_MODEL, ADA_TN), lambda l, j: (l, 0, j)),
            pl.BlockSpec((None, 1, ADA_TN), lambda l, j: (l, 0, j)),
        ],
        out_specs=pl.BlockSpec((None, MOD_ROWS, ADA_TN), lambda l, j: (l, 0, j)),
        out_shape=jax.ShapeDtypeStruct((DEPTH, MOD_ROWS, N_MOD), F32),
        compiler_params=_params(("parallel", "parallel")),
        name="ada_mods",
    )(c_all, w_ada, b_ada)


def _norm_mod(x, nw, sc, sh):
    ms = jnp.mean(x * x, axis=-1, keepdims=True)
    return (x * lax.rsqrt(ms + EPS) * nw) * (1.0 + sc) + sh


def _group_rms(y, w_t, bd_ref):
    yy = (y * y).astype(BF16)
    bw = bd_ref.shape[0]
    parts = [jnp.dot(yy[:, s:s + bw], bd_ref[...], preferred_element_type=F32)
             for s in range(0, y.shape[1], bw)]
    ss = parts[0] if len(parts) == 1 else jnp.concatenate(parts, axis=1)
    return y * lax.rsqrt(ss * (1.0 / HEAD_GROUP) + EPS) * w_t


def _rope(y, cos, sin):
    n = y.shape[1]
    lane = lax.broadcasted_iota(jnp.int32, y.shape, 1)
    nxt = pltpu.roll(y, n - 1, axis=1)
    prv = pltpu.roll(y, 1, axis=1)
    swapped = jnp.where((lane & 1) == 0, nxt, prv)
    return y * cos + swapped * sin


def _lower_bounds(lbl_ref, jl):
    rows = [lbl_ref[pl.ds(2 * m, 2), :] for m in range(N_EVEN)]
    mx = functools.reduce(jnp.maximum, rows)
    es = [jnp.exp(r - mx) for r in rows]
    den = functools.reduce(lambda a, b: a + b, es)
    sm = [e / den for e in es]
    cs = sm[0]
    for m in range(1, jl + 1):
        cs = cs + sm[m]
    return cs - sm[0]


def _proj_kernel(x_ref, mods_ref, nw_ref, w_ref, wkv_ref, lbl_ref, qn_ref, kn_ref, cos_ref,
                 sin_ref, bd_ref, o_ref, h_scr, w_scr, *, kinds, jl, lat):
    tiles = PROJ_TILES_LAT if lat else PROJ_TILES_CTX
    j = pl.program_id(0)
    i = pl.program_id(1)
    tm, tn = PROJ_TM, o_ref.shape[1]
    kvw = 2 * KV_B * HD_B

    @pl.when(j == 0)
    def _():
        for t in range(tiles):
            tile = tiles * i + t
            row = 1 + tile if lat else 0
            sh = mods_ref[pl.ds(row, 1), 0:D_MODEL]
            sc = mods_ref[pl.ds(row, 1), D_MODEL:2 * D_MODEL]
            h_scr[tile] = _norm_mod(x_ref[t * tm:(t + 1) * tm, :], nw_ref[...], sc,
                                    sh).astype(BF16)

    def finish(kind, y, rows):
        if kind == "silu_scale":
            return _silu(y) * (DK_A ** -0.5)
        if kind in ("loggate0", "loggate1"):
            d = int(kind[-1])
            lb = _lower_bounds(lbl_ref, jl)[d:d + 1, :]
            return jnp.log2(lb + (1.0 - lb) * jax.nn.sigmoid(y))
        if kind == "ident":
            return y
        if kind == "silu":
            return _silu(y)
        if kind in ("qnorm", "knorm"):
            w_t = qn_ref[...] if kind == "qnorm" else kn_ref[...]
            r = _group_rms(y, w_t, bd_ref)
            if lat:
                r = _rope(r, jnp.tile(cos_ref[rows, :], (1, tn // ROPE_TW)),
                          jnp.tile(sin_ref[rows, :], (1, tn // ROPE_TW)))
            return r * (HEAD_GROUP ** -0.5 * LOG2_E) if kind == "qnorm" else r
        if kind == "kv":
            kn = _group_rms(y, kn_ref[:, 0:kvw], bd_ref)
            if lat:
                kn = _rope(kn, jnp.tile(cos_ref[rows, :], (1, kvw // ROPE_TW)),
                           jnp.tile(sin_ref[rows, :], (1, kvw // ROPE_TW)))
            lane = lax.broadcasted_iota(jnp.int32, y.shape, 1)
            return jnp.where(lane < KV_B * HD_B, kn, y)
        raise ValueError(kind)

    def run(kind):
        @pl.when(i == 0)
        def _():
            if kind == "kv":
                w_scr[:, 0:kvw] = wkv_ref[...].astype(BF16)
            else:
                w_scr[...] = w_ref[...].astype(BF16)

        per_tile = tm // PROJ_RC

        def first(r):
            t, c = divmod(r, per_tile)
            w = w_scr[:, 0:kvw] if kind == "kv" else w_scr[...]
            return jnp.dot(h_scr[tiles * i + t, c * PROJ_RC:(c + 1) * PROJ_RC, :], w,
                           preferred_element_type=F32)

        def second(r, y):
            c = r % per_tile
            rows = slice(r * PROJ_RC, (r + 1) * PROJ_RC)
            seq_rows = slice(c * PROJ_RC, (c + 1) * PROJ_RC)
            if kind == "kv":
                o_ref[rows, 0:kvw] = finish(kind, y, seq_rows)
                o_ref[rows, kvw:tn] = jnp.zeros((PROJ_RC, tn - kvw), F32)
            else:
                o_ref[rows, :] = finish(kind, y, seq_rows)

        _pipelined(tiles * per_tile, first, second)

    for jj, kind in enumerate(kinds):
        pl.when(j == jj)(functools.partial(run, kind))


def _proj_call(x_src, x_tile0, lat, p_prev, mods, li, norm_w, w, jl, lbl, qn_t, kn_t, cos_t, sin_t,
               bd, kinds, name):
    tm, tn = PROJ_TM, PROJ_TN
    assert tm == DEC_SEQ
    n_ctx_tiles = N_CTX_TOK // tm
    n_tiles = (N_LAT_TOK if lat else N_CTX_TOK) // tm
    tile0 = n_ctx_tiles if lat else 0
    if not lat:
        cos_t = sin_t = jnp.zeros((8, 128), F32)
    n_main = sum(1 for k in kinds if k != "kv")
    n_cols = tn * len(kinds)
    kvw = 2 * KV_B * HD_B
    kv_blk = (n_main * tn) // kvw if "kv" in kinds else 0
    const = lambda j, i: (0, 0)
    tiles = PROJ_TILES_LAT if lat else PROJ_TILES_CTX
    n_blk = n_tiles // tiles
    blk = tiles * tm
    x_blk0, out_blk0 = x_tile0 // tiles, tile0 // tiles
    assert x_tile0 % tiles == 0 and tile0 % tiles == 0
    (out,) = _aliased_call(
        functools.partial(_proj_kernel, kinds=kinds, jl=jl, lat=lat),
        grid=(len(kinds), n_blk),
        in_specs=[
            pl.BlockSpec((blk, D_MODEL),
                         lambda j, i: (x_blk0 + jnp.where(j == 0, i, n_blk - 1), 0)),
            pl.BlockSpec((None, MOD_ROWS, N_MOD), lambda j, i: (li, 0, 0)),
            pl.BlockSpec((None, None, 1, D_MODEL), lambda j, i: (li, 0, 0, 0)),
            pl.BlockSpec((None, D_MODEL, tn), lambda j, i: (jl, 0, jnp.minimum(j, n_main - 1))),
            pl.BlockSpec((None, D_MODEL, kvw), lambda j, i: (jl, 0, kv_blk)),
            pl.BlockSpec(lbl.shape, const),
            pl.BlockSpec((1, tn), const),
            pl.BlockSpec((1, tn), const),
            pl.BlockSpec(cos_t.shape, const),
            pl.BlockSpec(sin_t.shape, const),
            pl.BlockSpec(bd.shape, const),
        ],
        args=[x_src, mods, norm_w, w, w, lbl, qn_t, kn_t, cos_t, sin_t, bd],
        out_specs=[pl.BlockSpec((blk, tn), lambda j, i: (out_blk0 + i, j))],
        out_shape=[jax.ShapeDtypeStruct((N_TOK, n_cols), F32)],
        carried=[p_prev], sem=("arbitrary", "arbitrary"), name=name,
        scratch_shapes=[pltpu.VMEM((n_tiles, tm, D_MODEL), BF16),
                        pltpu.VMEM((D_MODEL, tn), BF16)])
    return out


def _hgrn_tables(rev):
    c = HGRN_CHUNK
    t = np.arange(c)
    w = (t[None, :] <= t[:, None]) if not rev else (t[None, :] >= t[:, None])
    ws = np.concatenate([w.astype(np.float32)] * HGRN_SPLIT, axis=1)
    x = t[:, None] ^ t[None, :]
    lv = np.where(x > 0, np.floor(np.log2(np.maximum(x, 1))).astype(np.int32), HGRN_LEVELS)
    causal = (t[None, :] < t[:, None]) if not rev else (t[None, :] > t[:, None])
    lv = np.where(causal | (x == 0), lv, -1).astype(np.int32)
    return jnp.asarray(ws, BF16), jnp.asarray(lv)


def _hgrn_level_exponents(cum, rev):
    c = HGRN_CHUNK
    sub_rows = 8
    c3 = cum.reshape(c // sub_rows, sub_rows, DK_A)
    sub = lax.broadcasted_iota(jnp.int32, c3.shape, 1)
    out = []
    for l in range(1, HGRN_LEVELS):
        hb = 1 << l
        if 2 * hb <= sub_rows:
            r = None
            for b0 in range(0, sub_rows, 2 * hb):
                idx = b0 + (hb if rev else hb - 1)
                rk = c3[:, idx:idx + 1, :]
                r = rk if r is None else jnp.where(sub < b0, r, rk)
            d = c3 - r
            bit = (sub & hb) != 0
            q_role = jnp.logical_not(bit) if rev else bit
            out.append(jnp.where(q_role, d, -d).reshape(c, DK_A))
        else:
            pieces = []
            for b0 in range(0, c, 2 * hb):
                mid = b0 + hb
                ridx = mid if rev else mid - 1
                r = cum[ridx:ridx + 1, :]
                lo = cum[b0:mid]
                hi = cum[mid:b0 + 2 * hb]
                pieces += [lo - r, r - hi] if rev else [r - lo, hi - r]
            out.append(jnp.concatenate(pieces, axis=0))
    return out


def _hgrn_level_operands(l, q, k, f, z, rev, row):
    c = HGRN_CHUNK
    hb = 1 << l
    if hb >= 8:
        zero = jnp.zeros((hb, DK_A), F32)
        qparts, kparts = [], []
        for b0 in range(0, c, 2 * hb):
            lo, hi = slice(b0, b0 + hb), slice(b0 + hb, b0 + 2 * hb)
            if rev:
                qparts += [q[lo] * z[lo], zero]
                kparts += [zero, k[hi] * z[hi]]
            else:
                qparts += [zero, q[hi] * z[hi]]
                kparts += [k[lo] * z[lo], zero]
        return (jnp.concatenate(qparts, axis=0).astype(BF16),
                jnp.concatenate(kparts, axis=0).astype(BF16))
    bit = ((row >> l) & 1) == 1
    q_role = jnp.logical_not(bit) if rev else bit
    ql = jnp.where(q_role, q * (f if l == 0 else z), 0.0).astype(BF16)
    kl = jnp.where(q_role, 0.0, k if l == 0 else k * z).astype(BF16)
    return ql, kl


def _hgrn_chunks(chains):
    c = HGRN_CHUNK
    row = lax.broadcasted_iota(jnp.int32, (c, DK_A), 0)
    cums = []
    for q, g, v, st, w_ref, lv_ref, rev in chains:
        terms = []
        rem = g
        for _ in range(HGRN_SPLIT):
            term = rem.astype(BF16)
            terms.append(term)
            rem = rem - term.astype(F32)
        cums.append(jnp.dot(w_ref[...], jnp.concatenate(terms, axis=0),
                            preferred_element_type=F32))
    work = []
    for (q, g, v, st, w_ref, lv_ref, rev), cum in zip(chains, cums):
        f = jnp.exp2(g)
        k = 1.0 - f
        last = 0 if rev else c - 1
        total_e = cum[last:last + 1, :]
        qd = (q * jnp.exp2(cum)).astype(BF16)
        kd = (k * jnp.exp2(total_e - cum)).astype(BF16)
        o = _nt_dot(qd, st.astype(BF16))
        st_new = (st * jnp.exp2(total_e)
                  + jnp.dot(v.T.astype(BF16), kd, preferred_element_type=F32))
        zs = [None] + [jnp.exp2(e) for e in _hgrn_level_exponents(cum, rev)]
        work.append((k, f, zs, o, st_new))
    accs = [None] * len(chains)
    for l in reversed(range(HGRN_LEVELS)):
        for i, ((q, g, v, st, w_ref, lv_ref, rev), (k, f, zs, o, st_new)) in enumerate(
                zip(chains, work)):
            ql, kl = _hgrn_level_operands(l, q, k, f, zs[l], rev, row)
            a_l = _nt_dot(ql, kl)
            accs[i] = a_l if accs[i] is None else jnp.where(lv_ref[...] == l, a_l, accs[i])
    outs = []
    for (q, g, v, st, w_ref, lv_ref, rev), (k, f, zs, o, st_new), a in zip(chains, work, accs):
        a = jnp.where(lv_ref[...] == HGRN_LEVELS, jnp.sum(q * k, axis=-1, keepdims=True), a)
        outs.append((o + jnp.dot(a.astype(BF16), v.astype(BF16), preferred_element_type=F32),
                     st_new))
    return outs


def _hgrn_kernel(*refs, n_chunks, seqs, has_init, emit_state):
    refs = list(refs)
    q_ref, gf_ref, gb_ref, v_ref, sg_ref, nw_ref, wf_ref, wb_ref, lvf_ref, lvb_ref = refs[:10]
    pos = 10
    s0_ref = None
    if has_init:
        s0_ref = refs[pos]
        pos += 1
    o_ref = refs[pos]
    pos += 1
    so_ref = None
    if emit_state:
        so_ref = refs[pos]
        pos += 1
    of_scr, ob_scr, st_scr = refs[pos:pos + 3]

    seq_len = n_chunks * HGRN_CHUNK
    for s in range(seqs):
        for d in range(2):
            for h in range(H_A):
                if has_init:
                    st_scr[s, d, h] = s0_ref[s, d, h].T
                else:
                    st_scr[s, d, h] = jnp.zeros((DV_A, DK_A), F32)

    def body(c, carry):
        chains, dests = [], []
        for s in range(seqs):
            for h in range(H_A):
                cols = slice(h * DK_A, (h + 1) * DK_A)
                for d, (g_ref, w_ref, lv_ref, scr) in enumerate(
                        ((gf_ref, wf_ref, lvf_ref, of_scr), (gb_ref, wb_ref, lvb_ref, ob_scr))):
                    cc = c if d == 0 else n_chunks - 1 - c
                    r0 = pl.multiple_of(s * seq_len + cc * HGRN_CHUNK, HGRN_CHUNK)
                    rows = pl.ds(r0, HGRN_CHUNK)
                    chains.append((q_ref[rows, cols], g_ref[rows, cols], v_ref[rows, cols],
                                   st_scr[s, d, h], w_ref, lv_ref, d == 1))
                    dests.append((scr, rows, cols, s, d, h))
        for (o, st), (scr, rows, cols, s, d, h) in zip(_hgrn_chunks(chains), dests):
            st_scr[s, d, h] = st
            scr[rows, cols] = o
        return carry

    lax.fori_loop(0, n_chunks, body, 0)
    for h in range(H_A):
        cols = slice(h * DV_A, (h + 1) * DV_A)
        o = of_scr[:, cols] + ob_scr[:, cols]
        y = o * lax.rsqrt(jnp.mean(o * o, axis=-1, keepdims=True) + EPS) * nw_ref[...]
        o_ref[:, cols] = (y * sg_ref[:, cols]).astype(o_ref.dtype)
    if emit_state:
        for s in range(seqs):
            for d in range(2):
                for h in range(H_A):
                    so_ref[s, d, h] = st_scr[s, d, h].T


def _hgrn_call(p, nw, tabs, seq_len, n_seq, row_blk0, s0, jl, mix, state_out, name):
    wf, lvf, wb, lvb = tabs
    has_init = s0 is not None
    emit_state = s0 is None
    seqs = HGRN_CTX_SEQS if emit_state else 1
    assert row_blk0 % seqs == 0 and n_seq % seqs == 0
    blk0 = row_blk0 // seqs
    const = lambda b: (0, 0)
    blk = (seqs * seq_len, F_A)
    state_spec = pl.BlockSpec((seqs, None, 2, H_A, DK_A, DV_A), lambda b: (b, jl, 0, 0, 0, 0))
    in_specs = [pl.BlockSpec(blk, (lambda b, part=part: (blk0 + b, part))) for part in range(5)]
    in_specs += [
        pl.BlockSpec((None, 1, DV_A), lambda b: (jl, 0, 0)),
        pl.BlockSpec(wf.shape, const), pl.BlockSpec(wb.shape, const),
        pl.BlockSpec(lvf.shape, const), pl.BlockSpec(lvb.shape, const),
    ]
    args = [p, p, p, p, p, nw, wf, wb, lvf, lvb]
    if has_init:
        in_specs.append(state_spec)
        args.append(s0)
    out_shape = [jax.ShapeDtypeStruct((N_TOK, D_MIX), BF16)]
    out_specs = [pl.BlockSpec((seqs * seq_len, H_A * DV_A), lambda b: (blk0 + b, 0))]
    carried = [mix]
    if emit_state:
        out_shape.append(jax.ShapeDtypeStruct((BATCH, N_EVEN, 2, H_A, DK_A, DV_A), F32))
        out_specs.append(state_spec)
        carried.append(state_out)
    return _aliased_call(
        functools.partial(_hgrn_kernel, n_chunks=seq_len // HGRN_CHUNK, seqs=seqs,
                          has_init=has_init, emit_state=emit_state),
        grid=(n_seq // seqs,), in_specs=in_specs, args=args, out_specs=out_specs,
        out_shape=out_shape, carried=carried, sem=("parallel",), name=name,
        scratch_shapes=[pltpu.VMEM((seqs * seq_len, H_A * DV_A), F32),
                        pltpu.VMEM((seqs * seq_len, H_A * DV_A), F32),
                        pltpu.VMEM((seqs, 2, H_A, DV_A, DK_A), F32)])


def _swa_ctx_kernel(q_ref, kv_ref, sink_ref, o_ref, kc_ref, vc_ref):
    for s in range(SWA_CTX_SEQS):
        rows = slice(s * SEQ, (s + 1) * SEQ)
        kv = kv_ref[rows, :]
        k32 = kv[:, 0:KV_B * HD_B]
        v32 = kv[:, KV_B * HD_B:2 * KV_B * HD_B]
        kc_ref[s] = k32
        vc_ref[s] = v32
        k = k32.astype(BF16)
        vt32 = v32.T
        vts = [_with_ones_rows(vt32[n * HD_B:(n + 1) * HD_B]) for n in range(KV_B)]
        q = q_ref[rows, :].astype(BF16)
        ksl = [slice((h // G_B) * HD_B, (h // G_B + 1) * HD_B) for h in range(H_B)]
        sts = [_nt_dot(k[:, ksl[h]], q[:, h * HD_B:(h + 1) * HD_B]) for h in range(H_B)]
        ps, sinks = [], []
        for h in range(H_B):
            sink = sink_ref[0:1, h:h + 1] * LOG2_E
            m = jnp.maximum(jnp.max(sts[h], axis=0, keepdims=True), sink)
            ps.append(jnp.exp2(sts[h] - m).astype(BF16))
            sinks.append(jnp.exp2(sink - m))
        outs = []
        for h in range(H_B):
            ota = jnp.dot(vts[h // G_B], ps[h], preferred_element_type=F32)
            outs.append(ota[0:HD_B] / (ota[HD_B:HD_B + 1] + sinks[h]))
        o_ref[rows, :] = jnp.concatenate(outs, axis=0).T.astype(o_ref.dtype)


def _swa_ctx_call(p, sink, jl, mix, k_out, v_out, name):
    qcol = (3 * F_A + 2 * H_A * DV_A) // PROJ_TN
    rows = SWA_CTX_SEQS * SEQ
    cache_spec = pl.BlockSpec((SWA_CTX_SEQS, None, SEQ, KV_B * HD_B), lambda b: (b, jl, 0, 0))
    cache_shape = jax.ShapeDtypeStruct((BATCH, N_EVEN, SEQ, KV_B * HD_B), F32)
    return _aliased_call(
        _swa_ctx_kernel,
        grid=(BATCH // SWA_CTX_SEQS,),
        in_specs=[
            pl.BlockSpec((rows, H_B * HD_B), lambda b: (b, qcol)),
            pl.BlockSpec((rows, PROJ_TN), lambda b: (b, qcol + 1)),
            pl.BlockSpec((None, 1, H_B), lambda b: (jl, 0, 0)),
        ],
        args=[p, p, sink],
        out_specs=[pl.BlockSpec((rows, H_B * HD_B), lambda b: (b, 1)), cache_spec, cache_spec],
        out_shape=[jax.ShapeDtypeStruct((N_TOK, D_MIX), BF16), cache_shape, cache_shape],
        carried=[mix, k_out, v_out], sem=("parallel",), name=name)


def _swa_lat_kernel(q_ref, kv_ref, ck_ref, cv_ref, sink_ref, o_ref):
    qi = pl.program_id(1)
    tq = q_ref.shape[0]
    span = tq + 2 * WINDOW
    ws = pl.multiple_of(jnp.clip(qi * tq - WINDOW, 0, DEC_SEQ - span), WINDOW)
    kvw = kv_ref[pl.ds(ws, span), :]
    kw = kvw[:, 0:KV_B * HD_B].astype(BF16)
    vwt = kvw[:, KV_B * HD_B:2 * KV_B * HD_B].T.astype(BF16)
    kc = ck_ref[...].astype(BF16)
    vct = cv_ref[...].T.astype(BF16)
    q = q_ref[...].astype(BF16)
    t_k = ws + lax.broadcasted_iota(jnp.int32, (span, tq), 0)
    t_q = qi * tq + lax.broadcasted_iota(jnp.int32, (span, tq), 1)
    valid = jnp.abs(t_q - t_k) <= WINDOW
    outs = []

    def first(h):
        qh = q[:, h * HD_B:(h + 1) * HD_B]
        ksl = slice((h // G_B) * HD_B, (h // G_B + 1) * HD_B)
        return _nt_dot(kw[:, ksl], qh), _nt_dot(kc[:, ksl], qh)

    def second(h, scores):
        ksl = slice((h // G_B) * HD_B, (h // G_B + 1) * HD_B)
        s_w = jnp.where(valid, scores[0], -jnp.inf)
        s_c = scores[1]
        sink = sink_ref[0:1, h:h + 1] * LOG2_E
        m = jnp.maximum(jnp.maximum(jnp.max(s_w, axis=0, keepdims=True),
                                    jnp.max(s_c, axis=0, keepdims=True)), sink)
        p_w = jnp.exp2(s_w - m)
        p_c = jnp.exp2(s_c - m)
        den = (jnp.sum(p_w, axis=0, keepdims=True) + jnp.sum(p_c, axis=0, keepdims=True)
               + jnp.exp2(sink - m))
        ot = (jnp.dot(vwt[ksl, :], p_w.astype(BF16), preferred_element_type=F32)
              + jnp.dot(vct[ksl, :], p_c.astype(BF16), preferred_element_type=F32))
        outs.append(ot / den)

    _pipelined(H_B, first, second)
    o_ref[...] = jnp.concatenate(outs, axis=0).T.astype(o_ref.dtype)


def _swa_lat_call(p, ck, cv, sink, jl, mix, name):
    tq = ATT_TQ
    qcol = (3 * F_A + 2 * H_A * DV_A) // PROJ_TN
    nq = DEC_SEQ // tq
    q_blk0 = N_CTX_TOK // tq
    s_blk0 = N_CTX_TOK // DEC_SEQ
    cache_spec = pl.BlockSpec((None, None, PAST_LEN, KV_B * HD_B), lambda b, i: (b, jl, 0, 0))
    (out,) = _aliased_call(
        _swa_lat_kernel,
        grid=(DEC_BATCH, nq),
        in_specs=[
            pl.BlockSpec((tq, H_B * HD_B), lambda b, i: (q_blk0 + b * nq + i, qcol)),
            pl.BlockSpec((DEC_SEQ, PROJ_TN), lambda b, i: (s_blk0 + b, qcol + 1)),
            cache_spec, cache_spec,
            pl.BlockSpec((None, 1, H_B), lambda b, i: (jl, 0, 0)),
        ],
        args=[p, p, ck, cv, sink],
        out_specs=[pl.BlockSpec((tq, H_B * HD_B), lambda b, i: (q_blk0 + b * nq + i, 1))],
        out_shape=[jax.ShapeDtypeStruct((N_TOK, D_MIX), BF16)],
        carried=[mix], sem=("parallel", "parallel"), name=name)
    return out


def _diff_lambda(lp_ref, lam_init):
    lp = lp_ref[...]
    a = jnp.sum(lp[0:1] * lp[1:2], axis=-1, keepdims=True)
    b = jnp.sum(lp[2:3] * lp[3:4], axis=-1, keepdims=True)
    return jnp.exp(a) - jnp.exp(b) + lam_init


def _diff_scores(q, k_parts):
    return [[_nt_dot(kp[:, c * HD_C:(c + 1) * HD_C], q[:, c * HD_C:(c + 1) * HD_C])
             for kp in k_parts] for c in range(2)]


def _diff_finish(scores, vt_parts, lam, lam_init, sw_t):
    hw = 2 * HD_C
    comps = []
    for c in range(2):
        ss = scores[c]
        m = functools.reduce(jnp.maximum, [jnp.max(s, axis=0, keepdims=True) for s in ss])
        ota = functools.reduce(
            lambda a, b: a + b,
            [jnp.dot(vt, jnp.exp2(s - m).astype(BF16), preferred_element_type=F32)
             for s, vt in zip(ss, vt_parts)])
        comps.append(ota[0:hw] / ota[hw:hw + 1])
    ot = comps[0] - lam * comps[1]
    yt = ot * lax.rsqrt(jnp.mean(ot * ot, axis=0, keepdims=True) + EPS) * sw_t
    return (yt * (1.0 - lam_init)).T


def _with_ones_rows(vt):
    return jnp.concatenate([vt, jnp.ones((ONES_ROWS, vt.shape[1]), F32)], axis=0).astype(BF16)


def _diff_ctx_kernel(q_ref, k_ref, v_ref, lp_ref, sw_ref, o_ref, kc_ref, vc_ref, *, lam_init):
    lam = _diff_lambda(lp_ref, lam_init)
    hw = 2 * HD_C
    vts = {}

    def first(n):
        s, h = divmod(n, H_C)
        rows = slice(s * SEQ, (s + 1) * SEQ)
        sl = slice(h * hw, (h + 1) * hw)
        k32 = k_ref[rows, sl]
        v32 = v_ref[rows, sl]
        kc_ref[s, :, sl] = k32
        vc_ref[s, :, sl] = v32
        vts[n] = _with_ones_rows(v32.T)
        return _diff_scores(q_ref[rows, sl].astype(BF16), [k32.astype(BF16)])

    def second(n, scores):
        s, h = divmod(n, H_C)
        y = _diff_finish(scores, [vts.pop(n)], lam, lam_init, sw_ref[...])
        o_ref[s * SEQ:(s + 1) * SEQ, h * hw:(h + 1) * hw] = y.astype(o_ref.dtype)

    _pipelined(CTX_SEQS * H_C, first, second)


def _diff_ctx_call(p, lp, sw, lam_init, jl, k_out, v_out, name):
    rows = CTX_SEQS * SEQ
    cache_spec = pl.BlockSpec((CTX_SEQS, None, SEQ, ODD_W), lambda b: (b, jl, 0, 0))
    cache_shape = jax.ShapeDtypeStruct((BATCH, N_ODD, SEQ, ODD_W), F32)
    return _aliased_call(
        functools.partial(_diff_ctx_kernel, lam_init=lam_init),
        grid=(BATCH // CTX_SEQS,),
        in_specs=[
            pl.BlockSpec((rows, ODD_W), lambda b: (b, 0)),
            pl.BlockSpec((rows, ODD_W), lambda b: (b, 1)),
            pl.BlockSpec((rows, ODD_W), lambda b: (b, 2)),
            pl.BlockSpec((None, 4, HD_C), lambda b: (jl, 0, 0)),
            pl.BlockSpec((None, 2 * HD_C, ATT_TQ), lambda b: (jl, 0, 0)),
        ],
        args=[p, p, p, lp, sw],
        out_specs=[pl.BlockSpec((rows, ODD_W), lambda b: (b, 0)), cache_spec, cache_spec],
        out_shape=[jax.ShapeDtypeStruct((N_TOK, D_MIX), BF16), cache_shape, cache_shape],
        carried=[None, k_out, v_out], sem=("parallel",), name=name)


def _diff_lat_kernel(q_ref, k_ref, v_ref, ck_ref, cv_ref, lp_ref, sw_ref, o_ref, *, lam_init):
    lam = _diff_lambda(lp_ref, lam_init)
    hw = 2 * HD_C
    tq = ATT_TQ
    nq = q_ref.shape[0] // tq
    heads = []
    for g in range(q_ref.shape[1] // hw):
        sl = slice(g * hw, (g + 1) * hw)
        hd = pl.program_id(1) * (q_ref.shape[1] // hw) + g
        cv = cv_ref[pl.ds(hd, PAST_LEN, stride=H_C), :]
        heads.append(([k_ref[:, sl].astype(BF16), ck_ref[:, sl].astype(BF16)],
                      [_with_ones_rows(v_ref[:, sl].T), _with_ones_rows(cv.T)]))

    def first(n):
        g, i = divmod(n, nq)
        q = q_ref[i * tq:(i + 1) * tq, g * hw:(g + 1) * hw].astype(BF16)
        return _diff_scores(q, heads[g][0])

    def second(n, scores):
        g, i = divmod(n, nq)
        y = _diff_finish(scores, heads[g][1], lam, lam_init, sw_ref[...])
        o_ref[i * tq:(i + 1) * tq, g * hw:(g + 1) * hw] = y.astype(o_ref.dtype)

    _pipelined(len(heads) * nq, first, second)


def _diff_lat_call(p, ck, cv, lp, sw, lam_init, jl, mix, name):
    hw = 2 * HD_C
    gw = DIFF_LAT_HEADS * hw
    n_g = H_C // DIFF_LAT_HEADS
    s_blk0 = N_CTX_TOK // DEC_SEQ
    cache_spec = pl.BlockSpec((None, None, PAST_LEN, gw), lambda b, h: (b, jl, 0, h))
    (out,) = _aliased_call(
        functools.partial(_diff_lat_kernel, lam_init=lam_init),
        grid=(DEC_BATCH, n_g),
        in_specs=[
            pl.BlockSpec((DEC_SEQ, gw), lambda b, h: (s_blk0 + b, h)),
            pl.BlockSpec((DEC_SEQ, gw), lambda b, h: (s_blk0 + b, n_g + h)),
            pl.BlockSpec((DEC_SEQ, gw), lambda b, h: (s_blk0 + b, 2 * n_g + h)),
            cache_spec,
            pl.BlockSpec((None, None, PAST_LEN * H_C, hw), lambda b, h: (b, jl, 0, 0)),
            pl.BlockSpec((None, 4, HD_C), lambda b, h: (jl, 0, 0)),
            pl.BlockSpec((None, hw, ATT_TQ), lambda b, h: (jl, 0, 0)),
        ],
        args=[p, p, p, ck, cv, lp, sw],
        out_specs=[pl.BlockSpec((DEC_SEQ, gw), lambda b, h: (s_blk0 + b, h))],
        out_shape=[jax.ShapeDtypeStruct((N_TOK, D_MIX), BF16)],
        carried=[mix], sem=("parallel", "parallel"), name=name)
    return out


def _mlp_kernel(x_ref, mix_ref, wo_ref, mods_ref, nw_ref, w1_ref, w2_ref, o_ref,
                h_scr, *, group):
    i = pl.program_id(0)
    k = pl.program_id(1)
    n_ctx_tiles = N_CTX_TOK // x_ref.shape[0]
    row = {"ctx": 0, "lat": 1 + i,
           "all": jnp.where(i >= n_ctx_tiles, 1 + i - n_ctx_tiles, 0)}[group]

    def mod(a):
        return mods_ref[pl.ds(row, 1), a * D_MODEL:(a + 1) * D_MODEL]

    @pl.when(k == 0)
    def _():
        def first(r):
            return jnp.dot(mix_ref[r * MLP_RC:(r + 1) * MLP_RC, :], wo_ref[...],
                           preferred_element_type=F32)

        def second(r, y):
            rows = slice(r * MLP_RC, (r + 1) * MLP_RC)
            x1 = x_ref[rows, :] + mod(2) * y
            o_ref[rows, :] = x1
            h_scr[rows, :] = _norm_mod(x1, nw_ref[...], mod(4), mod(3)).astype(BF16)

        _pipelined(x_ref.shape[0] // MLP_RC, first, second)

    u = jnp.dot(h_scr[...], w1_ref[...].astype(BF16), preferred_element_type=F32)
    u = jnp.square(jnp.maximum(u, 0.0)).astype(BF16)
    o_ref[...] += mod(5) * jnp.dot(u, w2_ref[...].astype(BF16), preferred_element_type=F32)


def _mlp_call(x_src, x_tile0, lat, mix, wo, jl, mods, li, norm_w, w1, w2, out_prev, out_rows,
              out_tile0, name):
    tm, tk = MLP_TM, MLP_TK
    group = {False: "ctx", True: "lat", None: "all"}[lat]
    n_tiles = {"ctx": N_CTX_TOK, "lat": N_LAT_TOK, "all": N_TOK}[group] // tm
    mix_tile0 = N_CTX_TOK // tm if group == "lat" else 0
    (out,) = _aliased_call(
        functools.partial(_mlp_kernel, group=group),
        grid=(n_tiles, D_FF // tk),
        in_specs=[
            pl.BlockSpec((tm, D_MODEL), lambda i, k: (x_tile0 + i, 0)),
            pl.BlockSpec((tm, D_MIX), lambda i, k: (mix_tile0 + i, 0)),
            pl.BlockSpec((None, D_MIX, D_MODEL), lambda i, k: (jl, 0, 0)),
            pl.BlockSpec((None, MOD_ROWS, N_MOD), lambda i, k: (li, 0, 0)),
            pl.BlockSpec((None, None, 1, D_MODEL), lambda i, k: (li, 1, 0, 0)),
            pl.BlockSpec((None, D_MODEL, tk), lambda i, k: (li, 0, k)),
            pl.BlockSpec((None, tk, D_MODEL), lambda i, k: (li, k, 0)),
        ],
        args=[x_src, mix, wo, mods, norm_w, w1, w2],
        out_specs=[pl.BlockSpec((tm, D_MODEL), lambda i, k: (out_tile0 + i, 0))],
        out_shape=[jax.ShapeDtypeStruct((out_rows, D_MODEL), F32)],
        carried=[out_prev], sem=("parallel", "arbitrary"), name=name,
        scratch_shapes=[pltpu.VMEM((tm, D_MODEL), BF16)])
    return out


def _rope_tables(width):
    t = np.arange(DEC_SEQ)
    half = HEAD_GROUP // 2
    inv = ROPE_BASE ** (-np.arange(0, half, 2, dtype=np.float64) / half)
    ang = np.concatenate([(t // GRID_W)[:, None] * inv, (t % GRID_W)[:, None] * inv], axis=-1)
    cos = np.repeat(np.cos(ang), 2, axis=-1)
    sin = np.repeat(np.sin(ang), 2, axis=-1)
    sign = np.tile(np.array([-1.0, 1.0]), HEAD_GROUP // 2)
    reps = width // HEAD_GROUP
    return (jnp.asarray(np.tile(cos, (1, reps)), F32),
            jnp.asarray(np.tile(sin * sign, (1, reps)), F32))


def _block_diag_ones(n):
    g = np.arange(n) // HEAD_GROUP
    return jnp.asarray(g[:, None] == g[None, :], BF16)


def _tile_row(w, width):
    return jnp.tile(w.astype(F32), width // w.shape[0])[None, :]


def _lambda_init(li):
    return 0.8 - 0.6 * math.exp(-0.3 * li)


def kernel(x_prompt, x_sample, cache_k_swa, cache_v_swa, state_hgrn, cache_k_diff, cache_v_diff, c, c_ctx, norm_w, w_ada, b_ada, w_in_even, w_out_even, hgrn_lb_logits, hgrn_norm_w, swa_qnorm_w, swa_knorm_w, swa_sink, w_in_odd, w_out_odd, diff_qnorm_w, diff_knorm_w, diff_lambda_p, diff_subln_w, w_mlp1, w_mlp2):
    assert PROJ_TM == MLP_TM
    n_ctx_tiles = N_CTX_TOK // PROJ_TM
    x = None
    x_ctx0 = x_prompt.reshape(N_CTX_TOK, D_MODEL)
    x_lat0 = x_sample.reshape(N_LAT_TOK, D_MODEL)
    c_all = jnp.concatenate(
        [c_ctx[None, :], c, jnp.zeros((MOD_ROWS - 1 - DEC_BATCH, D_MODEL), F32)], axis=0)
    mods = _mods_call(c_all, w_ada, b_ada.reshape(DEPTH, 1, N_MOD))

    cos_t, sin_t = _rope_tables(ROPE_TW)
    bd = _block_diag_ones(256)
    tabs_f = _hgrn_tables(False)
    tabs_b = _hgrn_tables(True)
    hgrn_tabs = (tabs_f[0], tabs_f[1], tabs_b[0], tabs_b[1])
    lbl = hgrn_lb_logits.astype(F32).reshape(N_EVEN * 2, F_A)
    norm_w4 = norm_w.astype(F32).reshape(DEPTH, 2, 1, D_MODEL)

    w_out_even_b = w_out_even.astype(BF16)
    w_out_odd_b = w_out_odd.astype(BF16)

    ck_swa = cache_k_swa.reshape(DEC_BATCH, N_EVEN, PAST_LEN, KV_B * HD_B)
    cv_swa = cache_v_swa.reshape(DEC_BATCH, N_EVEN, PAST_LEN, KV_B * HD_B)
    ck_diff = cache_k_diff.reshape(DEC_BATCH, N_ODD, PAST_LEN, ODD_W)
    cv_diff = cache_v_diff.reshape(DEC_BATCH, N_ODD, PAST_LEN * H_C, 2 * HD_C)
    hgrn_nw = hgrn_norm_w.astype(F32).reshape(N_EVEN, 1, DV_A)
    sink = swa_sink.astype(F32).reshape(N_EVEN, 1, H_B)
    lam_p = diff_lambda_p.astype(F32)
    assert ATT_TQ == SEQ
    subln = jnp.broadcast_to(diff_subln_w.astype(F32)[:, :, None], (N_ODD, 2 * HD_C, ATT_TQ))

    even_kinds = ("silu_scale", "loggate0", "loggate1", "ident", "silu", "qnorm", "kv")
    odd_kinds = ("qnorm", "qnorm", "knorm", "knorm", "ident", "ident")

    k_swa = v_swa = states = k_diff = v_diff = None
    for li in range(DEPTH):
        j = li // 2
        srcs = ((x_ctx0, 0), (x_lat0, 0)) if li == 0 else ((x, 0), (x, n_ctx_tiles))
        if li % 2 == 0:
            p = None
            for lat, (src, t0) in enumerate(srcs):
                p = _proj_call(src, t0, bool(lat), p, mods, li, norm_w4, w_in_even, j, lbl,
                               _tile_row(swa_qnorm_w[j], PROJ_TN),
                               _tile_row(swa_knorm_w[j], PROJ_TN),
                               cos_t, sin_t, bd, even_kinds, f"proj_even{j}_{lat}")
            mix, states = _hgrn_call(p, hgrn_nw, hgrn_tabs, SEQ, BATCH, 0, None, j, None, states,
                                     f"hgrn_ctx{j}")
            (mix,) = _hgrn_call(p, hgrn_nw, hgrn_tabs, DEC_SEQ, DEC_BATCH, N_CTX_TOK // DEC_SEQ,
                                state_hgrn, j, mix, None, f"hgrn_lat{j}")
            mix, k_swa, v_swa = _swa_ctx_call(p, sink, j, mix, k_swa, v_swa, f"swa_ctx{j}")
            mix = _swa_lat_call(p, ck_swa, cv_swa, sink, j, mix, f"swa_lat{j}")
            wo = w_out_even_b
        else:
            p = None
            for lat, (src, t0) in enumerate(srcs):
                p = _proj_call(src, t0, bool(lat), p, mods, li, norm_w4, w_in_odd, j, lbl,
                               _tile_row(diff_qnorm_w[j], PROJ_TN),
                               _tile_row(diff_knorm_w[j], PROJ_TN),
                               cos_t, sin_t, bd, odd_kinds, f"proj_odd{j}_{lat}")
            lam_init = _lambda_init(li)
            mix, k_diff, v_diff = _diff_ctx_call(p, lam_p, subln, lam_init, j, k_diff, v_diff,
                                                 f"diff_ctx{j}")
            mix = _diff_lat_call(p, ck_diff, cv_diff, lam_p, subln, lam_init, j, mix,
                                 f"diff_lat{j}")
            wo = w_out_odd_b
        last = li == DEPTH - 1
        if li == 0 or last:
            outs = []
            x_next = None
            for lat, (src, t0) in enumerate(srcs):
                rows = (N_LAT_TOK if lat else N_CTX_TOK) if last else N_TOK
                out_t0 = 0 if last else lat * n_ctx_tiles
                x_next = _mlp_call(src, t0, bool(lat), mix, wo, j, mods, li, norm_w4, w_mlp1,
                                   w_mlp2, None if last else x_next, rows, out_t0,
                                   f"mlp{li}_{lat}")
                outs.append(x_next)
            x = x_next
        else:
            x = _mlp_call(x, 0, None, mix, wo, j, mods, li, norm_w4, w_mlp1, w_mlp2, None, N_TOK,
                          0, f"mlp{li}")

    y_prompt = outs[0].reshape(BATCH, SEQ, D_MODEL)
    y_sample = outs[1].reshape(DEC_BATCH, DEC_SEQ, D_MODEL)
    return (y_prompt, y_sample,
            k_swa.reshape(BATCH, N_EVEN, SEQ, KV_B, HD_B),
            v_swa.reshape(BATCH, N_EVEN, SEQ, KV_B, HD_B),
            states,
            k_diff.reshape(BATCH, N_ODD, SEQ, H_C, 2, HD_C),
            v_diff.reshape(BATCH, N_ODD, SEQ, H_C, 2 * HD_C))
```

```python
import functools
import math

import numpy as np
import jax
import jax.numpy as jnp
from jax import lax
from jax.experimental import pallas as pl
from jax.experimental.pallas import tpu as pltpu

F32 = jnp.float32
BF16 = jnp.bfloat16

D_MODEL = 1024
BATCH = 16
SEQ = 256
DEPTH = 4
DEC_BATCH = 4
DEC_SEQ = 1024
PAST_LEN = 512
GRID_W = 64
N_EVEN = (DEPTH + 1) // 2
N_ODD = DEPTH // 2
H_A = 4
DK_A = 128
DV_A = D_MODEL // 2 // H_A
F_A = H_A * DK_A
H_B = 8
KV_B = 2
G_B = H_B // KV_B
HD_B = D_MODEL // 2 // H_B
WINDOW = 128
H_C = 8
HD_C = D_MODEL // (2 * H_C)
D_FF = 4 * D_MODEL
ROPE_BASE = 10000.0
EPS = 1e-6
EVEN_COLS = 3 * F_A + 2 * H_A * DV_A + (H_B + 2 * KV_B) * HD_B
ODD_W = H_C * 2 * HD_C
D_MIX = D_MODEL

N_CTX_TOK = BATCH * SEQ
N_LAT_TOK = DEC_BATCH * DEC_SEQ
N_TOK = N_CTX_TOK + N_LAT_TOK
MOD_ROWS = 8
N_MOD = 6 * D_MODEL

HEAD_GROUP = 64
HGRN_CHUNK = 128
HGRN_LEVELS = 7
HGRN_SPLIT = 3
LOG2_E = math.log2(math.e)
VMEM_LIMIT = 48 * 1024 * 1024

PROJ_TM = 1024
PROJ_TN = 512
ROPE_TW = 128
PROJ_TILES_CTX = 2
PROJ_TILES_LAT = 1
PROJ_RC = 128
MLP_TM = 1024
MLP_TK = 1024
MLP_RC = 256
ADA_TN = 1536
ATT_TQ = 256
HGRN_CTX_SEQS = 2
CTX_SEQS = 2
SWA_CTX_SEQS = 4
DIFF_LAT_HEADS = 4
ONES_ROWS = 16


def _silu(x):
    return x * jax.nn.sigmoid(x)


def _nt_dot(a, b):
    return lax.dot_general(a, b, (((1,), (1,)), ((), ())), preferred_element_type=F32)


def _pipelined(n, first, second):
    cur = first(0)
    for i in range(n):
        nxt = first(i + 1) if i + 1 < n else None
        second(i, cur)
        cur = nxt


def _params(sem):
    return pltpu.CompilerParams(dimension_semantics=sem, vmem_limit_bytes=VMEM_LIMIT)


def _aliased_call(kernel, *, grid, in_specs, args, out_specs, out_shape, carried, sem, name,
                  scratch_shapes=()):
    n_in = len(args)
    extra = [buf for buf in carried if buf is not None]
    aliases = {}
    for k, buf in enumerate(carried):
        if buf is not None:
            aliases[n_in + len(aliases)] = k
    n_extra = len(extra)

    def body(*refs):
        kernel(*refs[:n_in], *refs[n_in + n_extra:])

    return pl.pallas_call(
        body,
        grid=grid,
        in_specs=list(in_specs) + [pl.BlockSpec(memory_space=pl.ANY)] * n_extra,
        out_specs=out_specs,
        out_shape=out_shape,
        input_output_aliases=aliases,
        scratch_shapes=list(scratch_shapes),
        compiler_params=_params(sem),
        name=name,
    )(*args, *extra)


def _mods_kernel(c_ref, w_ref, b_ref, o_ref):
    s = _silu(c_ref[...]).astype(BF16)
    o_ref[...] = jnp.dot(s, w_ref[...].astype(BF16), preferred_element_type=F32) + b_ref[...]


def _mods_call(c_all, w_ada, b_ada):
    return pl.pallas_call(
        _mods_kernel,
        grid=(DEPTH, N_MOD // ADA_TN),
        in_specs=[
            pl.BlockSpec((MOD_ROWS, D_MODEL), lambda l, j: (0, 0)),
            pl.BlockSpec((None, D_MODEL, ADA_TN), lambda l, j: (l, 0, j)),
            pl.BlockSpec((None, 1, ADA_TN), lambda l, j: (l, 0, j)),
        ],
        out_specs=pl.BlockSpec((None, MOD_ROWS, ADA_TN), lambda l, j: (l, 0, j)),
        out_shape=jax.ShapeDtypeStruct((DEPTH, MOD_ROWS, N_MOD), F32),
        compiler_params=_params(("parallel", "parallel")),
        name="ada_mods",
    )(c_all, w_ada, b_ada)


def _norm_mod(x, nw, sc, sh):
    ms = jnp.mean(x * x, axis=-1, keepdims=True)
    return (x * lax.rsqrt(ms + EPS) * nw) * (1.0 + sc) + sh


def _group_rms(y, w_t, bd_ref):
    yy = (y * y).astype(BF16)
    bw = bd_ref.shape[0]
    parts = [jnp.dot(yy[:, s:s + bw], bd_ref[...], preferred_element_type=F32)
             for s in range(0, y.shape[1], bw)]
    ss = parts[0] if len(parts) == 1 else jnp.concatenate(parts, axis=1)
    return y * lax.rsqrt(ss * (1.0 / HEAD_GROUP) + EPS) * w_t


def _rope(y, cos, sin):
    n = y.shape[1]
    lane = lax.broadcasted_iota(jnp.int32, y.shape, 1)
    nxt = pltpu.roll(y, n - 1, axis=1)
    prv = pltpu.roll(y, 1, axis=1)
    swapped = jnp.where((lane & 1) == 0, nxt, prv)
    return y * cos + swapped * sin


def _lower_bounds(lbl_ref, jl):
    rows = [lbl_ref[pl.ds(2 * m, 2), :] for m in range(N_EVEN)]
    mx = functools.reduce(jnp.maximum, rows)
    es = [jnp.exp(r - mx) for r in rows]
    den = functools.reduce(lambda a, b: a + b, es)
    sm = [e / den for e in es]
    cs = sm[0]
    for m in range(1, jl + 1):
        cs = cs + sm[m]
    return cs - sm[0]


def _proj_kernel(x_ref, mods_ref, nw_ref, w_ref, wkv_ref, lbl_ref, qn_ref, kn_ref, cos_ref,
                 sin_ref, bd_ref, o_ref, h_scr, w_scr, *, kinds, jl, lat):
    tiles = PROJ_TILES_LAT if lat else PROJ_TILES_CTX
    j = pl.program_id(0)
    i = pl.program_id(1)
    tm, tn = PROJ_TM, o_ref.shape[1]
    kvw = 2 * KV_B * HD_B

    @pl.when(j == 0)
    def _():
        for t in range(tiles):
            tile = tiles * i + t
            row = 1 + tile if lat else 0
            sh = mods_ref[pl.ds(row, 1), 0:D_MODEL]
            sc = mods_ref[pl.ds(row, 1), D_MODEL:2 * D_MODEL]
            h_scr[tile] = _norm_mod(x_ref[t * tm:(t + 1) * tm, :], nw_ref[...], sc,
                                    sh).astype(BF16)

    def finish(kind, y, rows):
        if kind == "silu_scale":
            return _silu(y) * (DK_A ** -0.5)
        if kind in ("loggate0", "loggate1"):
            d = int(kind[-1])
            lb = _lower_bounds(lbl_ref, jl)[d:d + 1, :]
            return jnp.log2(lb + (1.0 - lb) * jax.nn.sigmoid(y))
        if kind == "ident":
            return y
        if kind == "silu":
            return _silu(y)
        if kind in ("qnorm", "knorm"):
            w_t = qn_ref[...] if kind == "qnorm" else kn_ref[...]
            r = _group_rms(y, w_t, bd_ref)
            if lat:
                r = _rope(r, jnp.tile(cos_ref[rows, :], (1, tn // ROPE_TW)),
                          jnp.tile(sin_ref[rows, :], (1, tn // ROPE_TW)))
            return r * (HEAD_GROUP ** -0.5 * LOG2_E) if kind == "qnorm" else r
        if kind == "kv":
            kn = _group_rms(y, kn_ref[:, 0:kvw], bd_ref)
            if lat:
                kn = _rope(kn, jnp.tile(cos_ref[rows, :], (1, kvw // ROPE_TW)),
                           jnp.tile(sin_ref[rows, :], (1, kvw // ROPE_TW)))
            lane = lax.broadcasted_iota(jnp.int32, y.shape, 1)
            return jnp.where(lane < KV_B * HD_B, kn, y)
        raise ValueError(kind)

    def run(kind):
        @pl.when(i == 0)
        def _():
            if kind == "kv":
                w_scr[:, 0:kvw] = wkv_ref[...].astype(BF16)
            else:
                w_scr[...] = w_ref[...].astype(BF16)

        per_tile = tm // PROJ_RC

        def first(r):
            t, c = divmod(r, per_tile)
            w = w_scr[:, 0:kvw] if kind == "kv" else w_scr[...]
            return jnp.dot(h_scr[tiles * i + t, c * PROJ_RC:(c + 1) * PROJ_RC, :], w,
                           preferred_element_type=F32)

        def second(r, y):
            c = r % per_tile
            rows = slice(r * PROJ_RC, (r + 1) * PROJ_RC)
            seq_rows = slice(c * PROJ_RC, (c + 1) * PROJ_RC)
            if kind == "kv":
                o_ref[rows, 0:kvw] = finish(kind, y, seq_rows)
                o_ref[rows, kvw:tn] = jnp.zeros((PROJ_RC, tn - kvw), F32)
            else:
                o_ref[rows, :] = finish(kind, y, seq_rows)

        _pipelined(tiles * per_tile, first, second)

    for jj, kind in enumerate(kinds):
        pl.when(j == jj)(functools.partial(run, kind))


def _proj_call(x_src, x_tile0, lat, p_prev, mods, li, norm_w, w, jl, lbl, qn_t, kn_t, cos_t, sin_t,
               bd, kinds, name):
    tm, tn = PROJ_TM, PROJ_TN
    assert tm == DEC_SEQ
    n_ctx_tiles = N_CTX_TOK // tm
    n_tiles = (N_LAT_TOK if lat else N_CTX_TOK) // tm
    tile0 = n_ctx_tiles if lat else 0
    if not lat:
        cos_t = sin_t = jnp.zeros((8, 128), F32)
    n_main = sum(1 for k in kinds if k != "kv")
    n_cols = tn * len(kinds)
    kvw = 2 * KV_B * HD_B
    kv_blk = (n_main * tn) // kvw if "kv" in kinds else 0
    const = lambda j, i: (0, 0)
    tiles = PROJ_TILES_LAT if lat else PROJ_TILES_CTX
    n_blk = n_tiles // tiles
    blk = tiles * tm
    x_blk0, out_blk0 = x_tile0 // tiles, tile0 // tiles
    assert x_tile0 % tiles == 0 and tile0 % tiles == 0
    (out,) = _aliased_call(
        functools.partial(_proj_kernel, kinds=kinds, jl=jl, lat=lat),
        grid=(len(kinds), n_blk),
        in_specs=[
            pl.BlockSpec((blk, D_MODEL),
                         lambda j, i: (x_blk0 + jnp.where(j == 0, i, n_blk - 1), 0)),
            pl.BlockSpec((None, MOD_ROWS, N_MOD), lambda j, i: (li, 0, 0)),
            pl.BlockSpec((None, None, 1, D_MODEL), lambda j, i: (li, 0, 0, 0)),
            pl.BlockSpec((None, D_MODEL, tn), lambda j, i: (jl, 0, jnp.minimum(j, n_main - 1))),
            pl.BlockSpec((None, D_MODEL, kvw), lambda j, i: (jl, 0, kv_blk)),
            pl.BlockSpec(lbl.shape, const),
            pl.BlockSpec((1, tn), const),
            pl.BlockSpec((1, tn), const),
            pl.BlockSpec(cos_t.shape, const),
            pl.BlockSpec(sin_t.shape, const),
            pl.BlockSpec(bd.shape, const),
        ],
        args=[x_src, mods, norm_w, w, w, lbl, qn_t, kn_t, cos_t, sin_t, bd],
        out_specs=[pl.BlockSpec((blk, tn), lambda j, i: (out_blk0 + i, j))],
        out_shape=[jax.ShapeDtypeStruct((N_TOK, n_cols), F32)],
        carried=[p_prev], sem=("arbitrary", "arbitrary"), name=name,
        scratch_shapes=[pltpu.VMEM((n_tiles, tm, D_MODEL), BF16),
                        pltpu.VMEM((D_MODEL, tn), BF16)])
    return out


def _hgrn_tables(rev):
    c = HGRN_CHUNK
    t = np.arange(c)
    w = (t[None, :] <= t[:, None]) if not rev else (t[None, :] >= t[:, None])
    ws = np.concatenate([w.astype(np.float32)] * HGRN_SPLIT, axis=1)
    x = t[:, None] ^ t[None, :]
    lv = np.where(x > 0, np.floor(np.log2(np.maximum(x, 1))).astype(np.int32), HGRN_LEVELS)
    causal = (t[None, :] < t[:, None]) if not rev else (t[None, :] > t[:, None])
    lv = np.where(causal | (x == 0), lv, -1).astype(np.int32)
    return jnp.asarray(ws, BF16), jnp.asarray(lv)


def _hgrn_level_exponents(cum, rev):
    c = HGRN_CHUNK
    sub_rows = 8
    c3 = cum.reshape(c // sub_rows, sub_rows, DK_A)
    sub = lax.broadcasted_iota(jnp.int32, c3.shape, 1)
    out = []
    for l in range(1, HGRN_LEVELS):
        hb = 1 << l
        if 2 * hb <= sub_rows:
            r = None
            for b0 in range(0, sub_rows, 2 * hb):
                idx = b0 + (hb if rev else hb - 1)
                rk = c3[:, idx:idx + 1, :]
                r = rk if r is None else jnp.where(sub < b0, r, rk)
            d = c3 - r
            bit = (sub & hb) != 0
            q_role = jnp.logical_not(bit) if rev else bit
            out.append(jnp.where(q_role, d, -d).reshape(c, DK_A))
        else:
            pieces = []
            for b0 in range(0, c, 2 * hb):
                mid = b0 + hb
                ridx = mid if rev else mid - 1
                r = cum[ridx:ridx + 1, :]
                lo = cum[b0:mid]
                hi = cum[mid:b0 + 2 * hb]
                pieces += [lo - r, r - hi] if rev else [r - lo, hi - r]
            out.append(jnp.concatenate(pieces, axis=0))
    return out


def _hgrn_level_operands(l, q, k, f, z, rev, row):
    c = HGRN_CHUNK
    hb = 1 << l
    if hb >= 8:
        zero = jnp.zeros((hb, DK_A), F32)
        qparts, kparts = [], []
        for b0 in range(0, c, 2 * hb):
            lo, hi = slice(b0, b0 + hb), slice(b0 + hb, b0 + 2 * hb)
            if rev:
                qparts += [q[lo] * z[lo], zero]
                kparts += [zero, k[hi] * z[hi]]
            else:
                qparts += [zero, q[hi] * z[hi]]
                kparts += [k[lo] * z[lo], zero]
        return (jnp.concatenate(qparts, axis=0).astype(BF16),
                jnp.concatenate(kparts, axis=0).astype(BF16))
    bit = ((row >> l) & 1) == 1
    q_role = jnp.logical_not(bit) if rev else bit
    ql = jnp.where(q_role, q * (f if l == 0 else z), 0.0).astype(BF16)
    kl = jnp.where(q_role, 0.0, k if l == 0 else k * z).astype(BF16)
    return ql, kl


def _hgrn_chunks(chains):
    c = HGRN_CHUNK
    row = lax.broadcasted_iota(jnp.int32, (c, DK_A), 0)
    cums = []
    for q, g, v, st, w_ref, lv_ref, rev in chains:
        terms = []
        rem = g
        for _ in range(HGRN_SPLIT):
            term = rem.astype(BF16)
            terms.append(term)
            rem = rem - term.astype(F32)
        cums.append(jnp.dot(w_ref[...], jnp.concatenate(terms, axis=0),
                            preferred_element_type=F32))
    work = []
    for (q, g, v, st, w_ref, lv_ref, rev), cum in zip(chains, cums):
        f = jnp.exp2(g)
        k = 1.0 - f
        last = 0 if rev else c - 1
        total_e = cum[last:last + 1, :]
        qd = (q * jnp.exp2(cum)).astype(BF16)
        kd = (k * jnp.exp2(total_e - cum)).astype(BF16)
        o = _nt_dot(qd, st.astype(BF16))
        st_new = (st * jnp.exp2(total_e)
                  + jnp.dot(v.T.astype(BF16), kd, preferred_element_type=F32))
        zs = [None] + [jnp.exp2(e) for e in _hgrn_level_exponents(cum, rev)]
        work.append((k, f, zs, o, st_new))
    accs = [None] * len(chains)
    for l in reversed(range(HGRN_LEVELS)):
        for i, ((q, g, v, st, w_ref, lv_ref, rev), (k, f, zs, o, st_new)) in enumerate(
                zip(chains, work)):
            ql, kl = _hgrn_level_operands(l, q, k, f, zs[l], rev, row)
            a_l = _nt_dot(ql, kl)
            accs[i] = a_l if accs[i] is None else jnp.where(lv_ref[...] == l, a_l, accs[i])
    outs = []
    for (q, g, v, st, w_ref, lv_ref, rev), (k, f, zs, o, st_new), a in zip(chains, work, accs):
        a = jnp.where(lv_ref[...] == HGRN_LEVELS, jnp.sum(q * k, axis=-1, keepdims=True), a)
        outs.append((o + jnp.dot(a.astype(BF16), v.astype(BF16), preferred_element_type=F32),
                     st_new))
    return outs


def _hgrn_kernel(*refs, n_chunks, seqs, has_init, emit_state):
    refs = list(refs)
    q_ref, gf_ref, gb_ref, v_ref, sg_ref, nw_ref, wf_ref, wb_ref, lvf_ref, lvb_ref = refs[:10]
    pos = 10
    s0_ref = None
    if has_init:
        s0_ref = refs[pos]
        pos += 1
    o_ref = refs[pos]
    pos += 1
    so_ref = None
    if emit_state:
        so_ref = refs[pos]
        pos += 1
    of_scr, ob_scr, st_scr = refs[pos:pos + 3]

    seq_len = n_chunks * HGRN_CHUNK
    for s in range(seqs):
        for d in range(2):
            for h in range(H_A):
                if has_init:
                    st_scr[s, d, h] = s0_ref[s, d, h].T
                else:
                    st_scr[s, d, h] = jnp.zeros((DV_A, DK_A), F32)

    def body(c, carry):
        chains, dests = [], []
        for s in range(seqs):
            for h in range(H_A):
                cols = slice(h * DK_A, (h + 1) * DK_A)
                for d, (g_ref, w_ref, lv_ref, scr) in enumerate(
                        ((gf_ref, wf_ref, lvf_ref, of_scr), (gb_ref, wb_ref, lvb_ref, ob_scr))):
                    cc = c if d == 0 else n_chunks - 1 - c
                    r0 = pl.multiple_of(s * seq_len + cc * HGRN_CHUNK, HGRN_CHUNK)
                    rows = pl.ds(r0, HGRN_CHUNK)
                    chains.append((q_ref[rows, cols], g_ref[rows, cols], v_ref[rows, cols],
                                   st_scr[s, d, h], w_ref, lv_ref, d == 1))
                    dests.append((scr, rows, cols, s, d, h))
        for (o, st), (scr, rows, cols, s, d, h) in zip(_hgrn_chunks(chains), dests):
            st_scr[s, d, h] = st
            scr[rows, cols] = o
        return carry

    lax.fori_loop(0, n_chunks, body, 0)
    for h in range(H_A):
        cols = slice(h * DV_A, (h + 1) * DV_A)
        o = of_scr[:, cols] + ob_scr[:, cols]
        y = o * lax.rsqrt(jnp.mean(o * o, axis=-1, keepdims=True) + EPS) * nw_ref[...]
        o_ref[:, cols] = (y * sg_ref[:, cols]).astype(o_ref.dtype)
    if emit_state:
        for s in range(seqs):
            for d in range(2):
                for h in range(H_A):
                    so_ref[s, d, h] = st_scr[s, d, h].T


def _hgrn_call(p, nw, tabs, seq_len, n_seq, row_blk0, s0, jl, mix, state_out, name):
    wf, lvf, wb, lvb = tabs
    has_init = s0 is not None
    emit_state = s0 is None
    seqs = HGRN_CTX_SEQS if emit_state else 1
    assert row_blk0 % seqs == 0 and n_seq % seqs == 0
    blk0 = row_blk0 // seqs
    const = lambda b: (0, 0)
    blk = (seqs * seq_len, F_A)
    state_spec = pl.BlockSpec((seqs, None, 2, H_A, DK_A, DV_A), lambda b: (b, jl, 0, 0, 0, 0))
    in_specs = [pl.BlockSpec(blk, (lambda b, part=part: (blk0 + b, part))) for part in range(5)]
    in_specs += [
        pl.BlockSpec((None, 1, DV_A), lambda b: (jl, 0, 0)),
        pl.BlockSpec(wf.shape, const), pl.BlockSpec(wb.shape, const),
        pl.BlockSpec(lvf.shape, const), pl.BlockSpec(lvb.shape, const),
    ]
    args = [p, p, p, p, p, nw, wf, wb, lvf, lvb]
    if has_init:
        in_specs.append(state_spec)
        args.append(s0)
    out_shape = [jax.ShapeDtypeStruct((N_TOK, D_MIX), BF16)]
    out_specs = [pl.BlockSpec((seqs * seq_len, H_A * DV_A), lambda b: (blk0 + b, 0))]
    carried = [mix]
    if emit_state:
        out_shape.append(jax.ShapeDtypeStruct((BATCH, N_EVEN, 2, H_A, DK_A, DV_A), F32))
        out_specs.append(state_spec)
        carried.append(state_out)
    return _aliased_call(
        functools.partial(_hgrn_kernel, n_chunks=seq_len // HGRN_CHUNK, seqs=seqs,
                          has_init=has_init, emit_state=emit_state),
        grid=(n_seq // seqs,), in_specs=in_specs, args=args, out_specs=out_specs,
        out_shape=out_shape, carried=carried, sem=("parallel",), name=name,
        scratch_shapes=[pltpu.VMEM((seqs * seq_len, H_A * DV_A), F32),
                        pltpu.VMEM((seqs * seq_len, H_A * DV_A), F32),
                        pltpu.VMEM((seqs, 2, H_A, DV_A, DK_A), F32)])


def _swa_ctx_kernel(q_ref, kv_ref, sink_ref, o_ref, kc_ref, vc_ref):
    for s in range(SWA_CTX_SEQS):
        rows = slice(s * SEQ, (s + 1) * SEQ)
        kv = kv_ref[rows, :]
        k32 = kv[:, 0:KV_B * HD_B]
        v32 = kv[:, KV_B * HD_B:2 * KV_B * HD_B]
        kc_ref[s] = k32
        vc_ref[s] = v32
        k = k32.astype(BF16)
        vt32 = v32.T
        vts = [_with_ones_rows(vt32[n * HD_B:(n + 1) * HD_B]) for n in range(KV_B)]
        q = q_ref[rows, :].astype(BF16)
        ksl = [slice((h // G_B) * HD_B, (h // G_B + 1) * HD_B) for h in range(H_B)]
        sts = [_nt_dot(k[:, ksl[h]], q[:, h * HD_B:(h + 1) * HD_B]) for h in range(H_B)]
        ps, sinks = [], []
        for h in range(H_B):
            sink = sink_ref[0:1, h:h + 1] * LOG2_E
            m = jnp.maximum(jnp.max(sts[h], axis=0, keepdims=True), sink)
            ps.append(jnp.exp2(sts[h] - m).astype(BF16))
            sinks.append(jnp.exp2(sink - m))
        outs = []
        for h in range(H_B):
            ota = jnp.dot(vts[h // G_B], ps[h], preferred_element_type=F32)
            outs.append(ota[0:HD_B] / (ota[HD_B:HD_B + 1] + sinks[h]))
        o_ref[rows, :] = jnp.concatenate(outs, axis=0).T.astype(o_ref.dtype)


def _swa_ctx_call(p, sink, jl, mix, k_out, v_out, name):
    qcol = (3 * F_A + 2 * H_A * DV_A) // PROJ_TN
    rows = SWA_CTX_SEQS * SEQ
    cache_spec = pl.BlockSpec((SWA_CTX_SEQS, None, SEQ, KV_B * HD_B), lambda b: (b, jl, 0, 0))
    cache_shape = jax.ShapeDtypeStruct((BATCH, N_EVEN, SEQ, KV_B * HD_B), F32)
    return _aliased_call(
        _swa_ctx_kernel,
        grid=(BATCH // SWA_CTX_SEQS,),
        in_specs=[
            pl.BlockSpec((rows, H_B * HD_B), lambda b: (b, qcol)),
            pl.BlockSpec((rows, PROJ_TN), lambda b: (b, qcol + 1)),
            pl.BlockSpec((None, 1, H_B), lambda b: (jl, 0, 0)),
        ],
        args=[p, p, sink],
        out_specs=[pl.BlockSpec((rows, H_B * HD_B), lambda b: (b, 1)), cache_spec, cache_spec],
        out_shape=[jax.ShapeDtypeStruct((N_TOK, D_MIX), BF16), cache_shape, cache_shape],
        carried=[mix, k_out, v_out], sem=("parallel",), name=name)


def _swa_lat_kernel(q_ref, kv_ref, ck_ref, cv_ref, sink_ref, o_ref):
    qi = pl.program_id(1)
    tq = q_ref.shape[0]
    span = tq + 2 * WINDOW
    ws = pl.multiple_of(jnp.clip(qi * tq - WINDOW, 0, DEC_SEQ - span), WINDOW)
    kvw = kv_ref[pl.ds(ws, span), :]
    kw = kvw[:, 0:KV_B * HD_B].astype(BF16)
    vwt = kvw[:, KV_B * HD_B:2 * KV_B * HD_B].T.astype(BF16)
    kc = ck_ref[...].astype(BF16)
    vct = cv_ref[...].T.astype(BF16)
    q = q_ref[...].astype(BF16)
    t_k = ws + lax.broadcasted_iota(jnp.int32, (span, tq), 0)
    t_q = qi * tq + lax.broadcasted_iota(jnp.int32, (span, tq), 1)
    valid = jnp.abs(t_q - t_k) <= WINDOW
    outs = []

    def first(h):
        qh = q[:, h * HD_B:(h + 1) * HD_B]
        ksl = slice((h // G_B) * HD_B, (h // G_B + 1) * HD_B)
        return _nt_dot(kw[:, ksl], qh), _nt_dot(kc[:, ksl], qh)

    def second(h, scores):
        ksl = slice((h // G_B) * HD_B, (h // G_B + 1) * HD_B)
        s_w = jnp.where(valid, scores[0], -jnp.inf)
        s_c = scores[1]
        sink = sink_ref[0:1, h:h + 1] * LOG2_E
        m = jnp.maximum(jnp.maximum(jnp.max(s_w, axis=0, keepdims=True),
                                    jnp.max(s_c, axis=0, keepdims=True)), sink)
        p_w = jnp.exp2(s_w - m)
        p_c = jnp.exp2(s_c - m)
        den = (jnp.sum(p_w, axis=0, keepdims=True) + jnp.sum(p_c, axis=0, keepdims=True)
               + jnp.exp2(sink - m))
        ot = (jnp.dot(vwt[ksl, :], p_w.astype(BF16), preferred_element_type=F32)
              + jnp.dot(vct[ksl, :], p_c.astype(BF16), preferred_element_type=F32))
        outs.append(ot / den)

    _pipelined(H_B, first, second)
    o_ref[...] = jnp.concatenate(outs, axis=0).T.astype(o_ref.dtype)


def _swa_lat_call(p, ck, cv, sink, jl, mix, name):
    tq = ATT_TQ
    qcol = (3 * F_A + 2 * H_A * DV_A) // PROJ_TN
    nq = DEC_SEQ // tq
    q_blk0 = N_CTX_TOK // tq
    s_blk0 = N_CTX_TOK // DEC_SEQ
    cache_spec = pl.BlockSpec((None, None, PAST_LEN, KV_B * HD_B), lambda b, i: (b, jl, 0, 0))
    (out,) = _aliased_call(
        _swa_lat_kernel,
        grid=(DEC_BATCH, nq),
        in_specs=[
            pl.BlockSpec((tq, H_B * HD_B), lambda b, i: (q_blk0 + b * nq + i, qcol)),
            pl.BlockSpec((DEC_SEQ, PROJ_TN), lambda b, i: (s_blk0 + b, qcol + 1)),
            cache_spec, cache_spec,
            pl.BlockSpec((None, 1, H_B), lambda b, i: (jl, 0, 0)),
        ],
        args=[p, p, ck, cv, sink],
        out_specs=[pl.BlockSpec((tq, H_B * HD_B), lambda b, i: (q_blk0 + b * nq + i, 1))],
        out_shape=[jax.ShapeDtypeStruct((N_TOK, D_MIX), BF16)],
        carried=[mix], sem=("parallel", "parallel"), name=name)
    return out


def _diff_lambda(lp_ref, lam_init):
    lp = lp_ref[...]
    a = jnp.sum(lp[0:1] * lp[1:2], axis=-1, keepdims=True)
    b = jnp.sum(lp[2:3] * lp[3:4], axis=-1, keepdims=True)
    return jnp.exp(a) - jnp.exp(b) + lam_init


def _diff_scores(q, k_parts):
    return [[_nt_dot(kp[:, c * HD_C:(c + 1) * HD_C], q[:, c * HD_C:(c + 1) * HD_C])
             for kp in k_parts] for c in range(2)]


def _diff_finish(scores, vt_parts, lam, lam_init, sw_t):
    hw = 2 * HD_C
    comps = []
    for c in range(2):
        ss = scores[c]
        m = functools.reduce(jnp.maximum, [jnp.max(s, axis=0, keepdims=True) for s in ss])
        ota = functools.reduce(
            lambda a, b: a + b,
            [jnp.dot(vt, jnp.exp2(s - m).astype(BF16), preferred_element_type=F32)
             for s, vt in zip(ss, vt_parts)])
        comps.append(ota[0:hw] / ota[hw:hw + 1])
    ot = comps[0] - lam * comps[1]
    yt = ot * lax.rsqrt(jnp.mean(ot * ot, axis=0, keepdims=True) + EPS) * sw_t
    return (yt * (1.0 - lam_init)).T


def _with_ones_rows(vt):
    return jnp.concatenate([vt, jnp.ones((ONES_ROWS, vt.shape[1]), F32)], axis=0).astype(BF16)


def _diff_ctx_kernel(q_ref, k_ref, v_ref, lp_ref, sw_ref, o_ref, kc_ref, vc_ref, *, lam_init):
    lam = _diff_lambda(lp_ref, lam_init)
    hw = 2 * HD_C
    vts = {}

    def first(n):
        s, h = divmod(n, H_C)
        rows = slice(s * SEQ, (s + 1) * SEQ)
        sl = slice(h * hw, (h + 1) * hw)
        k32 = k_ref[rows, sl]
        v32 = v_ref[rows, sl]
        kc_ref[s, :, sl] = k32
        vc_ref[s, pl.ds(h, SEQ, stride=H_C), :] = v32
        vts[n] = _with_ones_rows(v32.T)
        return _diff_scores(q_ref[rows, sl].astype(BF16), [k32.astype(BF16)])

    def second(n, scores):
        s, h = divmod(n, H_C)
        y = _diff_finish(scores, [vts.pop(n)], lam, lam_init, sw_ref[...])
        o_ref[s * SEQ:(s + 1) * SEQ, h * hw:(h + 1) * hw] = y.astype(o_ref.dtype)

    _pipelined(CTX_SEQS * H_C, first, second)


def _diff_ctx_call(p, lp, sw, lam_init, jl, k_out, v_out, name):
    rows = CTX_SEQS * SEQ
    cache_spec = pl.BlockSpec((CTX_SEQS, None, SEQ, ODD_W), lambda b: (b, jl, 0, 0))
    cache_shape = jax.ShapeDtypeStruct((BATCH, N_ODD, SEQ, ODD_W), F32)
    vcache_spec = pl.BlockSpec((CTX_SEQS, None, SEQ * H_C, 2 * HD_C), lambda b: (b, jl, 0, 0))
    vcache_shape = jax.ShapeDtypeStruct((BATCH, N_ODD, SEQ * H_C, 2 * HD_C), F32)
    return _aliased_call(
        functools.partial(_diff_ctx_kernel, lam_init=lam_init),
        grid=(BATCH // CTX_SEQS,),
        in_specs=[
            pl.BlockSpec((rows, ODD_W), lambda b: (b, 0)),
            pl.BlockSpec((rows, ODD_W), lambda b: (b, 1)),
            pl.BlockSpec((rows, ODD_W), lambda b: (b, 2)),
            pl.BlockSpec((None, 4, HD_C), lambda b: (jl, 0, 0)),
            pl.BlockSpec((None, 2 * HD_C, ATT_TQ), lambda b: (jl, 0, 0)),
        ],
        args=[p, p, p, lp, sw],
        out_specs=[pl.BlockSpec((rows, ODD_W), lambda b: (b, 0)), cache_spec, vcache_spec],
        out_shape=[jax.ShapeDtypeStruct((N_TOK, D_MIX), BF16), cache_shape, vcache_shape],
        carried=[None, k_out, v_out], sem=("parallel",), name=name)


def _diff_lat_kernel(q_ref, k_ref, v_ref, ck_ref, cv_ref, lp_ref, sw_ref, o_ref, *, lam_init):
    lam = _diff_lambda(lp_ref, lam_init)
    hw = 2 * HD_C
    tq = ATT_TQ
    nq = q_ref.shape[0] // tq
    heads = []
    for g in range(q_ref.shape[1] // hw):
        sl = slice(g * hw, (g + 1) * hw)
        hd = pl.program_id(1) * (q_ref.shape[1] // hw) + g
        cv = cv_ref[pl.ds(hd, PAST_LEN, stride=H_C), :]
        heads.append(([k_ref[:, sl].astype(BF16), ck_ref[:, sl].astype(BF16)],
                      [_with_ones_rows(v_ref[:, sl].T), _with_ones_rows(cv.T)]))

    def first(n):
        g, i = divmod(n, nq)
        q = q_ref[i * tq:(i + 1) * tq, g * hw:(g + 1) * hw].astype(BF16)
        return _diff_scores(q, heads[g][0])

    def second(n, scores):
        g, i = divmod(n, nq)
        y = _diff_finish(scores, heads[g][1], lam, lam_init, sw_ref[...])
        o_ref[i * tq:(i + 1) * tq, g * hw:(g + 1) * hw] = y.astype(o_ref.dtype)

    _pipelined(len(heads) * nq, first, second)


def _diff_lat_call(p, ck, cv, lp, sw, lam_init, jl, mix, name):
    hw = 2 * HD_C
    gw = DIFF_LAT_HEADS * hw
    n_g = H_C // DIFF_LAT_HEADS
    s_blk0 = N_CTX_TOK // DEC_SEQ
    cache_spec = pl.BlockSpec((None, None, PAST_LEN, gw), lambda b, h: (b, jl, 0, h))
    (out,) = _aliased_call(
        functools.partial(_diff_lat_kernel, lam_init=lam_init),
        grid=(DEC_BATCH, n_g),
        in_specs=[
            pl.BlockSpec((DEC_SEQ, gw), lambda b, h: (s_blk0 + b, h)),
            pl.BlockSpec((DEC_SEQ, gw), lambda b, h: (s_blk0 + b, n_g + h)),
            pl.BlockSpec((DEC_SEQ, gw), lambda b, h: (s_blk0 + b, 2 * n_g + h)),
            cache_spec,
            pl.BlockSpec((None, None, PAST_LEN * H_C, hw), lambda b, h: (b, jl, 0, 0)),
            pl.BlockSpec((None, 4, HD_C), lambda b, h: (jl, 0, 0)),
            pl.BlockSpec((None, hw, ATT_TQ), lambda b, h: (jl, 0, 0)),
        ],
        args=[p, p, p, ck, cv, lp, sw],
        out_specs=[pl.BlockSpec((DEC_SEQ, gw), lambda b, h: (s_blk0 + b, h))],
        out_shape=[jax.ShapeDtypeStruct((N_TOK, D_MIX), BF16)],
        carried=[mix], sem=("parallel", "parallel"), name=name)
    return out


def _mlp_kernel(x_ref, mix_ref, wo_ref, mods_ref, nw_ref, w1_ref, w2_ref, o_ref,
                h_scr, *, group):
    i = pl.program_id(0)
    k = pl.program_id(1)
    n_ctx_tiles = N_CTX_TOK // x_ref.shape[0]
    row = {"ctx": 0, "lat": 1 + i,
           "all": jnp.where(i >= n_ctx_tiles, 1 + i - n_ctx_tiles, 0)}[group]

    def mod(a):
        return mods_ref[pl.ds(row, 1), a * D_MODEL:(a + 1) * D_MODEL]

    @pl.when(k == 0)
    def _():
        def first(r):
            return jnp.dot(mix_ref[r * MLP_RC:(r + 1) * MLP_RC, :], wo_ref[...],
                           preferred_element_type=F32)

        def second(r, y):
            rows = slice(r * MLP_RC, (r + 1) * MLP_RC)
            x1 = x_ref[rows, :] + mod(2) * y
            o_ref[rows, :] = x1
            h_scr[rows, :] = _norm_mod(x1, nw_ref[...], mod(4), mod(3)).astype(BF16)

        _pipelined(x_ref.shape[0] // MLP_RC, first, second)

    u = jnp.dot(h_scr[...], w1_ref[...].astype(BF16), preferred_element_type=F32)
    u = jnp.square(jnp.maximum(u, 0.0)).astype(BF16)
    o_ref[...] += mod(5) * jnp.dot(u, w2_ref[...].astype(BF16), preferred_element_type=F32)


def _mlp_call(x_src, x_tile0, lat, mix, wo, jl, mods, li, norm_w, w1, w2, out_prev, out_rows,
              out_tile0, name):
    tm, tk = MLP_TM, MLP_TK
    group = {False: "ctx", True: "lat", None: "all"}[lat]
    n_tiles = {"ctx": N_CTX_TOK, "lat": N_LAT_TOK, "all": N_TOK}[group] // tm
    mix_tile0 = N_CTX_TOK // tm if group == "lat" else 0
    (out,) = _aliased_call(
        functools.partial(_mlp_kernel, group=group),
        grid=(n_tiles, D_FF // tk),
        in_specs=[
            pl.BlockSpec((tm, D_MODEL), lambda i, k: (x_tile0 + i, 0)),
            pl.BlockSpec((tm, D_MIX), lambda i, k: (mix_tile0 + i, 0)),
            pl.BlockSpec((None, D_MIX, D_MODEL), lambda i, k: (jl, 0, 0)),
            pl.BlockSpec((None, MOD_ROWS, N_MOD), lambda i, k: (li, 0, 0)),
            pl.BlockSpec((None, None, 1, D_MODEL), lambda i, k: (li, 1, 0, 0)),
            pl.BlockSpec((None, D_MODEL, tk), lambda i, k: (li, 0, k)),
            pl.BlockSpec((None, tk, D_MODEL), lambda i, k: (li, k, 0)),
        ],
        args=[x_src, mix, wo, mods, norm_w, w1, w2],
        out_specs=[pl.BlockSpec((tm, D_MODEL), lambda i, k: (out_tile0 + i, 0))],
        out_shape=[jax.ShapeDtypeStruct((out_rows, D_MODEL), F32)],
        carried=[out_prev], sem=("parallel", "arbitrary"), name=name,
        scratch_shapes=[pltpu.VMEM((tm, D_MODEL), BF16)])
    return out


def _rope_tables(width):
    t = np.arange(DEC_SEQ)
    half = HEAD_GROUP // 2
    inv = ROPE_BASE ** (-np.arange(0, half, 2, dtype=np.float64) / half)
    ang = np.concatenate([(t // GRID_W)[:, None] * inv, (t % GRID_W)[:, None] * inv], axis=-1)
    cos = np.repeat(np.cos(ang), 2, axis=-1)
    sin = np.repeat(np.sin(ang), 2, axis=-1)
    sign = np.tile(np.array([-1.0, 1.0]), HEAD_GROUP // 2)
    reps = width // HEAD_GROUP
    return (jnp.asarray(np.tile(cos, (1, reps)), F32),
            jnp.asarray(np.tile(sin * sign, (1, reps)), F32))


def _block_diag_ones(n):
    g = np.arange(n) // HEAD_GROUP
    return jnp.asarray(g[:, None] == g[None, :], BF16)


def _tile_row(w, width):
    return jnp.tile(w.astype(F32), width // w.shape[0])[None, :]


def _lambda_init(li):
    return 0.8 - 0.6 * math.exp(-0.3 * li)


def kernel(x_prompt, x_sample, cache_k_swa, cache_v_swa, state_hgrn, cache_k_diff, cache_v_diff, c, c_ctx, norm_w, w_ada, b_ada, w_in_even, w_out_even, hgrn_lb_logits, hgrn_norm_w, swa_qnorm_w, swa_knorm_w, swa_sink, w_in_odd, w_out_odd, diff_qnorm_w, diff_knorm_w, diff_lambda_p, diff_subln_w, w_mlp1, w_mlp2):
    assert PROJ_TM == MLP_TM
    n_ctx_tiles = N_CTX_TOK // PROJ_TM
    x = None
    x_ctx0 = x_prompt.reshape(N_CTX_TOK, D_MODEL)
    x_lat0 = x_sample.reshape(N_LAT_TOK, D_MODEL)
    c_all = jnp.concatenate(
        [c_ctx[None, :], c, jnp.zeros((MOD_ROWS - 1 - DEC_BATCH, D_MODEL), F32)], axis=0)
    mods = _mods_call(c_all, w_ada, b_ada.reshape(DEPTH, 1, N_MOD))

    cos_t, sin_t = _rope_tables(ROPE_TW)
    bd = _block_diag_ones(256)
    tabs_f = _hgrn_tables(False)
    tabs_b = _hgrn_tables(True)
    hgrn_tabs = (tabs_f[0], tabs_f[1], tabs_b[0], tabs_b[1])
    lbl = hgrn_lb_logits.astype(F32).reshape(N_EVEN * 2, F_A)
    norm_w4 = norm_w.astype(F32).reshape(DEPTH, 2, 1, D_MODEL)

    w_out_even_b = w_out_even.astype(BF16)
    w_out_odd_b = w_out_odd.astype(BF16)

    ck_swa = cache_k_swa.reshape(DEC_BATCH, N_EVEN, PAST_LEN, KV_B * HD_B)
    cv_swa = cache_v_swa.reshape(DEC_BATCH, N_EVEN, PAST_LEN, KV_B * HD_B)
    ck_diff = cache_k_diff.reshape(DEC_BATCH, N_ODD, PAST_LEN, ODD_W)
    cv_diff = cache_v_diff.reshape(DEC_BATCH, N_ODD, PAST_LEN * H_C, 2 * HD_C)
    hgrn_nw = hgrn_norm_w.astype(F32).reshape(N_EVEN, 1, DV_A)
    sink = swa_sink.astype(F32).reshape(N_EVEN, 1, H_B)
    lam_p = diff_lambda_p.astype(F32)
    assert ATT_TQ == SEQ
    subln = jnp.broadcast_to(diff_subln_w.astype(F32)[:, :, None], (N_ODD, 2 * HD_C, ATT_TQ))

    even_kinds = ("silu_scale", "loggate0", "loggate1", "ident", "silu", "qnorm", "kv")
    odd_kinds = ("qnorm", "qnorm", "knorm", "knorm", "ident", "ident")

    k_swa = v_swa = states = k_diff = v_diff = None
    for li in range(DEPTH):
        j = li // 2
        srcs = ((x_ctx0, 0), (x_lat0, 0)) if li == 0 else ((x, 0), (x, n_ctx_tiles))
        if li % 2 == 0:
            p = None
            for lat, (src, t0) in enumerate(srcs):
                p = _proj_call(src, t0, bool(lat), p, mods, li, norm_w4, w_in_even, j, lbl,
                               _tile_row(swa_qnorm_w[j], PROJ_TN),
                               _tile_row(swa_knorm_w[j], PROJ_TN),
                               cos_t, sin_t, bd, even_kinds, f"proj_even{j}_{lat}")
            mix, states = _hgrn_call(p, hgrn_nw, hgrn_tabs, SEQ, BATCH, 0, None, j, None, states,
                                     f"hgrn_ctx{j}")
            (mix,) = _hgrn_call(p, hgrn_nw, hgrn_tabs, DEC_SEQ, DEC_BATCH, N_CTX_TOK // DEC_SEQ,
                                state_hgrn, j, mix, None, f"hgrn_lat{j}")
            mix, k_swa, v_swa = _swa_ctx_call(p, sink, j, mix, k_swa, v_swa, f"swa_ctx{j}")
            mix = _swa_lat_call(p, ck_swa, cv_swa, sink, j, mix, f"swa_lat{j}")
            wo = w_out_even_b
        else:
            p = None
            for lat, (src, t0) in enumerate(srcs):
                p = _proj_call(src, t0, bool(lat), p, mods, li, norm_w4, w_in_odd, j, lbl,
                               _tile_row(diff_qnorm_w[j], PROJ_TN),
                               _tile_row(diff_knorm_w[j], PROJ_TN),
                               cos_t, sin_t, bd, odd_kinds, f"proj_odd{j}_{lat}")
            lam_init = _lambda_init(li)
            mix, k_diff, v_diff = _diff_ctx_call(p, lam_p, subln, lam_init, j, k_diff, v_diff,
                                                 f"diff_ctx{j}")
            mix = _diff_lat_call(p, ck_diff, cv_diff, lam_p, subln, lam_init, j, mix,
                                 f"diff_lat{j}")
            wo = w_out_odd_b
        last = li == DEPTH - 1
        if li == 0 or last:
            outs = []
            x_next = None
            for lat, (src, t0) in enumerate(srcs):
                rows = (N_LAT_TOK if lat else N_CTX_TOK) if last else N_TOK
                out_t0 = 0 if last else lat * n_ctx_tiles
                x_next = _mlp_call(src, t0, bool(lat), mix, wo, j, mods, li, norm_w4, w_mlp1,
                                   w_mlp2, None if last else x_next, rows, out_t0,
                                   f"mlp{li}_{lat}")
                outs.append(x_next)
            x = x_next
        else:
            x = _mlp_call(x, 0, None, mix, wo, j, mods, li, norm_w4, w_mlp1, w_mlp2, None, N_TOK,
                          0, f"mlp{li}")

    y_prompt = outs[0].reshape(BATCH, SEQ, D_MODEL)
    y_sample = outs[1].reshape(DEC_BATCH, DEC_SEQ, D_MODEL)
    return (y_prompt, y_sample,
            k_swa.reshape(BATCH, N_EVEN, SEQ, KV_B, HD_B),
            v_swa.reshape(BATCH, N_EVEN, SEQ, KV_B, HD_B),
            states,
            k_diff.reshape(BATCH, N_ODD, SEQ, H_C, 2, HD_C),
            v_diff.reshape(BATCH, N_ODD, SEQ, H_C, 2 * HD_C))
```

```python
import functools
import math

import numpy as np
import jax
import jax.numpy as jnp
from jax import lax
from jax.experimental import pallas as pl
from jax.experimental.pallas import tpu as pltpu

F32 = jnp.float32
BF16 = jnp.bfloat16

D_MODEL = 1024
BATCH = 16
SEQ = 256
DEPTH = 4
DEC_BATCH = 4
DEC_SEQ = 1024
PAST_LEN = 512
GRID_W = 64
N_EVEN = (DEPTH + 1) // 2
N_ODD = DEPTH // 2
H_A = 4
DK_A = 128
DV_A = D_MODEL // 2 // H_A
F_A = H_A * DK_A
H_B = 8
KV_B = 2
G_B = H_B // KV_B
HD_B = D_MODEL // 2 // H_B
WINDOW = 128
H_C = 8
HD_C = D_MODEL // (2 * H_C)
D_FF = 4 * D_MODEL
ROPE_BASE = 10000.0
EPS = 1e-6
EVEN_COLS = 3 * F_A + 2 * H_A * DV_A + (H_B + 2 * KV_B) * HD_B
ODD_W = H_C * 2 * HD_C
D_MIX = D_MODEL

N_CTX_TOK = BATCH * SEQ
N_LAT_TOK = DEC_BATCH * DEC_SEQ
N_TOK = N_CTX_TOK + N_LAT_TOK
MOD_ROWS = 8
N_MOD = 6 * D_MODEL

HEAD_GROUP = 64
HGRN_CHUNK = 128
HGRN_LEVELS = 7
HGRN_SPLIT = 3
LOG2_E = math.log2(math.e)
VMEM_LIMIT = 48 * 1024 * 1024

PROJ_TM = 1024
PROJ_TN = 512
ROPE_TW = 128
PROJ_TILES_CTX = 2
PROJ_TILES_LAT = 1
PROJ_RC = 128
MLP_TM = 1024
MLP_TK = 1024
MLP_RC = 256
ADA_TN = 1536
ATT_TQ = 256
HGRN_CTX_SEQS = 2
CTX_SEQS = 2
SWA_CTX_SEQS = 4
DIFF_LAT_HEADS = 4
ONES_ROWS = 16


def _silu(x):
    return x * jax.nn.sigmoid(x)


def _nt_dot(a, b):
    return lax.dot_general(a, b, (((1,), (1,)), ((), ())), preferred_element_type=F32)


def _pipelined(n, first, second):
    cur = first(0)
    for i in range(n):
        nxt = first(i + 1) if i + 1 < n else None
        second(i, cur)
        cur = nxt


def _params(sem):
    return pltpu.CompilerParams(dimension_semantics=sem, vmem_limit_bytes=VMEM_LIMIT)


def _aliased_call(kernel, *, grid, in_specs, args, out_specs, out_shape, carried, sem, name,
                  scratch_shapes=()):
    n_in = len(args)
    extra = [buf for buf in carried if buf is not None]
    aliases = {}
    for k, buf in enumerate(carried):
        if buf is not None:
            aliases[n_in + len(aliases)] = k
    n_extra = len(extra)

    def body(*refs):
        kernel(*refs[:n_in], *refs[n_in + n_extra:])

    return pl.pallas_call(
        body,
        grid=grid,
        in_specs=list(in_specs) + [pl.BlockSpec(memory_space=pl.ANY)] * n_extra,
        out_specs=out_specs,
        out_shape=out_shape,
        input_output_aliases=aliases,
        scratch_shapes=list(scratch_shapes),
        compiler_params=_params(sem),
        name=name,
    )(*args, *extra)


def _mods_kernel(c_ref, w_ref, b_ref, o_ref):
    s = _silu(c_ref[...]).astype(BF16)
    o_ref[...] = jnp.dot(s, w_ref[...].astype(BF16), preferred_element_type=F32) + b_ref[...]


def _mods_call(c_all, w_ada, b_ada):
    return pl.pallas_call(
        _mods_kernel,
        grid=(DEPTH, N_MOD // ADA_TN),
        in_specs=[
            pl.BlockSpec((MOD_ROWS, D_MODEL), lambda l, j: (0, 0)),
            pl.BlockSpec((None, D_MODEL, ADA_TN), lambda l, j: (l, 0, j)),
            pl.BlockSpec((None, 1, ADA_TN), lambda l, j: (l, 0, j)),
        ],
        out_specs=pl.BlockSpec((None, MOD_ROWS, ADA_TN), lambda l, j: (l, 0, j)),
        out_shape=jax.ShapeDtypeStruct((DEPTH, MOD_ROWS, N_MOD), F32),
        compiler_params=_params(("parallel", "parallel")),
        name="ada_mods",
    )(c_all, w_ada, b_ada)


def _norm_mod(x, nw, sc, sh):
    ms = jnp.mean(x * x, axis=-1, keepdims=True)
    return (x * lax.rsqrt(ms + EPS) * nw) * (1.0 + sc) + sh


def _group_rms(y, w_t, bd_ref):
    yy = (y * y).astype(BF16)
    bw = bd_ref.shape[0]
    parts = [jnp.dot(yy[:, s:s + bw], bd_ref[...], preferred_element_type=F32)
             for s in range(0, y.shape[1], bw)]
    ss = parts[0] if len(parts) == 1 else jnp.concatenate(parts, axis=1)
    return y * lax.rsqrt(ss * (1.0 / HEAD_GROUP) + EPS) * w_t


def _rope(y, cos, sin):
    n = y.shape[1]
    lane = lax.broadcasted_iota(jnp.int32, y.shape, 1)
    nxt = pltpu.roll(y, n - 1, axis=1)
    prv = pltpu.roll(y, 1, axis=1)
    swapped = jnp.where((lane & 1) == 0, nxt, prv)
    return y * cos + swapped * sin


def _lower_bounds(lbl_ref, jl):
    rows = [lbl_ref[pl.ds(2 * m, 2), :] for m in range(N_EVEN)]
    mx = functools.reduce(jnp.maximum, rows)
    es = [jnp.exp(r - mx) for r in rows]
    den = functools.reduce(lambda a, b: a + b, es)
    sm = [e / den for e in es]
    cs = sm[0]
    for m in range(1, jl + 1):
        cs = cs + sm[m]
    return cs - sm[0]


def _proj_kernel(x_ref, mods_ref, nw_ref, w_ref, wkv_ref, lbl_ref, qn_ref, kn_ref, cos_ref,
                 sin_ref, bd_ref, o_ref, h_scr, w_scr, *, kinds, jl, lat):
    tiles = PROJ_TILES_LAT if lat else PROJ_TILES_CTX
    j = pl.program_id(0)
    i = pl.program_id(1)
    tm, tn = PROJ_TM, o_ref.shape[1]
    kvw = 2 * KV_B * HD_B

    @pl.when(j == 0)
    def _():
        for t in range(tiles):
            tile = tiles * i + t
            row = 1 + tile if lat else 0
            sh = mods_ref[pl.ds(row, 1), 0:D_MODEL]
            sc = mods_ref[pl.ds(row, 1), D_MODEL:2 * D_MODEL]
            h_scr[tile] = _norm_mod(x_ref[t * tm:(t + 1) * tm, :], nw_ref[...], sc,
                                    sh).astype(BF16)

    def finish(kind, y, rows):
        if kind == "silu_scale":
            return _silu(y) * (DK_A ** -0.5)
        if kind in ("loggate0", "loggate1"):
            d = int(kind[-1])
            lb = _lower_bounds(lbl_ref, jl)[d:d + 1, :]
            return jnp.log2(lb + (1.0 - lb) * jax.nn.sigmoid(y))
        if kind == "ident":
            return y
        if kind == "silu":
            return _silu(y)
        if kind in ("qnorm", "knorm"):
            w_t = qn_ref[...] if kind == "qnorm" else kn_ref[...]
            r = _group_rms(y, w_t, bd_ref)
            if lat:
                r = _rope(r, jnp.tile(cos_ref[rows, :], (1, tn // ROPE_TW)),
                          jnp.tile(sin_ref[rows, :], (1, tn // ROPE_TW)))
            return r * (HEAD_GROUP ** -0.5 * LOG2_E) if kind == "qnorm" else r
        if kind == "kv":
            kn = _group_rms(y, kn_ref[:, 0:kvw], bd_ref)
            if lat:
                kn = _rope(kn, jnp.tile(cos_ref[rows, :], (1, kvw // ROPE_TW)),
                           jnp.tile(sin_ref[rows, :], (1, kvw // ROPE_TW)))
            lane = lax.broadcasted_iota(jnp.int32, y.shape, 1)
            return jnp.where(lane < KV_B * HD_B, kn, y)
        raise ValueError(kind)

    def run(kind):
        @pl.when(i == 0)
        def _():
            if kind == "kv":
                w_scr[:, 0:kvw] = wkv_ref[...].astype(BF16)
            else:
                w_scr[...] = w_ref[...].astype(BF16)

        per_tile = tm // PROJ_RC

        def first(r):
            t, c = divmod(r, per_tile)
            w = w_scr[:, 0:kvw] if kind == "kv" else w_scr[...]
            return jnp.dot(h_scr[tiles * i + t, c * PROJ_RC:(c + 1) * PROJ_RC, :], w,
                           preferred_element_type=F32)

        def second(r, y):
            c = r % per_tile
            rows = slice(r * PROJ_RC, (r + 1) * PROJ_RC)
            seq_rows = slice(c * PROJ_RC, (c + 1) * PROJ_RC)
            if kind == "kv":
                o_ref[rows, 0:kvw] = finish(kind, y, seq_rows)
                o_ref[rows, kvw:tn] = jnp.zeros((PROJ_RC, tn - kvw), F32)
            else:
                o_ref[rows, :] = finish(kind, y, seq_rows)

        _pipelined(tiles * per_tile, first, second)

    for jj, kind in enumerate(kinds):
        pl.when(j == jj)(functools.partial(run, kind))


def _proj_call(x_src, x_tile0, lat, p_prev, mods, li, norm_w, w, jl, lbl, qn_t, kn_t, cos_t, sin_t,
               bd, kinds, name):
    tm, tn = PROJ_TM, PROJ_TN
    assert tm == DEC_SEQ
    n_ctx_tiles = N_CTX_TOK // tm
    n_tiles = (N_LAT_TOK if lat else N_CTX_TOK) // tm
    tile0 = n_ctx_tiles if lat else 0
    if not lat:
        cos_t = sin_t = jnp.zeros((8, 128), F32)
    n_main = sum(1 for k in kinds if k != "kv")
    n_cols = tn * len(kinds)
    kvw = 2 * KV_B * HD_B
    kv_blk = (n_main * tn) // kvw if "kv" in kinds else 0
    const = lambda j, i: (0, 0)
    tiles = PROJ_TILES_LAT if lat else PROJ_TILES_CTX
    n_blk = n_tiles // tiles
    blk = tiles * tm
    x_blk0, out_blk0 = x_tile0 // tiles, tile0 // tiles
    assert x_tile0 % tiles == 0 and tile0 % tiles == 0
    (out,) = _aliased_call(
        functools.partial(_proj_kernel, kinds=kinds, jl=jl, lat=lat),
        grid=(len(kinds), n_blk),
        in_specs=[
            pl.BlockSpec((blk, D_MODEL),
                         lambda j, i: (x_blk0 + jnp.where(j == 0, i, n_blk - 1), 0)),
            pl.BlockSpec((None, MOD_ROWS, N_MOD), lambda j, i: (li, 0, 0)),
            pl.BlockSpec((None, None, 1, D_MODEL), lambda j, i: (li, 0, 0, 0)),
            pl.BlockSpec((None, D_MODEL, tn), lambda j, i: (jl, 0, jnp.minimum(j, n_main - 1))),
            pl.BlockSpec((None, D_MODEL, kvw), lambda j, i: (jl, 0, kv_blk)),
            pl.BlockSpec(lbl.shape, const),
            pl.BlockSpec((1, tn), const),
            pl.BlockSpec((1, tn), const),
            pl.BlockSpec(cos_t.shape, const),
            pl.BlockSpec(sin_t.shape, const),
            pl.BlockSpec(bd.shape, const),
        ],
        args=[x_src, mods, norm_w, w, w, lbl, qn_t, kn_t, cos_t, sin_t, bd],
        out_specs=[pl.BlockSpec((blk, tn), lambda j, i: (out_blk0 + i, j))],
        out_shape=[jax.ShapeDtypeStruct((N_TOK, n_cols), F32)],
        carried=[p_prev], sem=("arbitrary", "arbitrary"), name=name,
        scratch_shapes=[pltpu.VMEM((n_tiles, tm, D_MODEL), BF16),
                        pltpu.VMEM((D_MODEL, tn), BF16)])
    return out


def _hgrn_tables(rev):
    c = HGRN_CHUNK
    t = np.arange(c)
    w = (t[None, :] <= t[:, None]) if not rev else (t[None, :] >= t[:, None])
    ws = np.concatenate([w.astype(np.float32)] * HGRN_SPLIT, axis=1)
    x = t[:, None] ^ t[None, :]
    lv = np.where(x > 0, np.floor(np.log2(np.maximum(x, 1))).astype(np.int32), HGRN_LEVELS)
    causal = (t[None, :] < t[:, None]) if not rev else (t[None, :] > t[:, None])
    lv = np.where(causal | (x == 0), lv, -1).astype(np.int32)
    return jnp.asarray(ws, BF16), jnp.asarray(lv)


def _hgrn_level_exponents(cum, rev):
    c = HGRN_CHUNK
    sub_rows = 8
    c3 = cum.reshape(c // sub_rows, sub_rows, DK_A)
    sub = lax.broadcasted_iota(jnp.int32, c3.shape, 1)
    out = []
    for l in range(1, HGRN_LEVELS):
        hb = 1 << l
        if 2 * hb <= sub_rows:
            r = None
            for b0 in range(0, sub_rows, 2 * hb):
                idx = b0 + (hb if rev else hb - 1)
                rk = c3[:, idx:idx + 1, :]
                r = rk if r is None else jnp.where(sub < b0, r, rk)
            d = c3 - r
            bit = (sub & hb) != 0
            q_role = jnp.logical_not(bit) if rev else bit
            out.append(jnp.where(q_role, d, -d).reshape(c, DK_A))
        else:
            pieces = []
            for b0 in range(0, c, 2 * hb):
                mid = b0 + hb
                ridx = mid if rev else mid - 1
                r = cum[ridx:ridx + 1, :]
                lo = cum[b0:mid]
                hi = cum[mid:b0 + 2 * hb]
                pieces += [lo - r, r - hi] if rev else [r - lo, hi - r]
            out.append(jnp.concatenate(pieces, axis=0))
    return out


def _hgrn_level_operands(l, q, k, f, z, rev, row):
    c = HGRN_CHUNK
    hb = 1 << l
    if hb >= 8:
        zero = jnp.zeros((hb, DK_A), F32)
        qparts, kparts = [], []
        for b0 in range(0, c, 2 * hb):
            lo, hi = slice(b0, b0 + hb), slice(b0 + hb, b0 + 2 * hb)
            if rev:
                qparts += [q[lo] * z[lo], zero]
                kparts += [zero, k[hi] * z[hi]]
            else:
                qparts += [zero, q[hi] * z[hi]]
                kparts += [k[lo] * z[lo], zero]
        return (jnp.concatenate(qparts, axis=0).astype(BF16),
                jnp.concatenate(kparts, axis=0).astype(BF16))
    bit = ((row >> l) & 1) == 1
    q_role = jnp.logical_not(bit) if rev else bit
    ql = jnp.where(q_role, q * (f if l == 0 else z), 0.0).astype(BF16)
    kl = jnp.where(q_role, 0.0, k if l == 0 else k * z).astype(BF16)
    return ql, kl


def _hgrn_chunks(chains):
    c = HGRN_CHUNK
    row = lax.broadcasted_iota(jnp.int32, (c, DK_A), 0)
    cums = []
    for q, g, v, st, w_ref, lv_ref, rev in chains:
        terms = []
        rem = g
        for _ in range(HGRN_SPLIT):
            term = rem.astype(BF16)
            terms.append(term)
            rem = rem - term.astype(F32)
        cums.append(jnp.dot(w_ref[...], jnp.concatenate(terms, axis=0),
                            preferred_element_type=F32))
    work = []
    for (q, g, v, st, w_ref, lv_ref, rev), cum in zip(chains, cums):
        f = jnp.exp2(g)
        k = 1.0 - f
        last = 0 if rev else c - 1
        total_e = cum[last:last + 1, :]
        qd = (q * jnp.exp2(cum)).astype(BF16)
        kd = (k * jnp.exp2(total_e - cum)).astype(BF16)
        o = _nt_dot(qd, st.astype(BF16))
        st_new = (st * jnp.exp2(total_e)
                  + jnp.dot(v.T.astype(BF16), kd, preferred_element_type=F32))
        zs = [None] + [jnp.exp2(e) for e in _hgrn_level_exponents(cum, rev)]
        work.append((k, f, zs, o, st_new))
    accs = [None] * len(chains)
    for l in reversed(range(HGRN_LEVELS)):
        for i, ((q, g, v, st, w_ref, lv_ref, rev), (k, f, zs, o, st_new)) in enumerate(
                zip(chains, work)):
            ql, kl = _hgrn_level_operands(l, q, k, f, zs[l], rev, row)
            a_l = _nt_dot(ql, kl)
            accs[i] = a_l if accs[i] is None else jnp.where(lv_ref[...] == l, a_l, accs[i])
    outs = []
    for (q, g, v, st, w_ref, lv_ref, rev), (k, f, zs, o, st_new), a in zip(chains, work, accs):
        a = jnp.where(lv_ref[...] == HGRN_LEVELS, jnp.sum(q * k, axis=-1, keepdims=True), a)
        outs.append((o + jnp.dot(a.astype(BF16), v.astype(BF16), preferred_element_type=F32),
                     st_new))
    return outs


def _hgrn_kernel(*refs, n_chunks, seqs, has_init, emit_state):
    refs = list(refs)
    q_ref, gf_ref, gb_ref, v_ref, sg_ref, nw_ref, wf_ref, wb_ref, lvf_ref, lvb_ref = refs[:10]
    pos = 10
    s0_ref = None
    if has_init:
        s0_ref = refs[pos]
        pos += 1
    o_ref = refs[pos]
    pos += 1
    so_ref = None
    if emit_state:
        so_ref = refs[pos]
        pos += 1
    of_scr, ob_scr, st_scr = refs[pos:pos + 3]

    seq_len = n_chunks * HGRN_CHUNK
    for s in range(seqs):
        for d in range(2):
            for h in range(H_A):
                if has_init:
                    st_scr[s, d, h] = s0_ref[s, d, h].T
                else:
                    st_scr[s, d, h] = jnp.zeros((DV_A, DK_A), F32)

    def body(c, carry):
        chains, dests = [], []
        for s in range(seqs):
            for h in range(H_A):
                cols = slice(h * DK_A, (h + 1) * DK_A)
                for d, (g_ref, w_ref, lv_ref, scr) in enumerate(
                        ((gf_ref, wf_ref, lvf_ref, of_scr), (gb_ref, wb_ref, lvb_ref, ob_scr))):
                    cc = c if d == 0 else n_chunks - 1 - c
                    r0 = pl.multiple_of(s * seq_len + cc * HGRN_CHUNK, HGRN_CHUNK)
                    rows = pl.ds(r0, HGRN_CHUNK)
                    chains.append((q_ref[rows, cols], g_ref[rows, cols], v_ref[rows, cols],
                                   st_scr[s, d, h], w_ref, lv_ref, d == 1))
                    dests.append((scr, rows, cols, s, d, h))
        for (o, st), (scr, rows, cols, s, d, h) in zip(_hgrn_chunks(chains), dests):
            st_scr[s, d, h] = st
            scr[rows, cols] = o
        return carry

    lax.fori_loop(0, n_chunks, body, 0)
    for h in range(H_A):
        cols = slice(h * DV_A, (h + 1) * DV_A)
        o = of_scr[:, cols] + ob_scr[:, cols]
        y = o * lax.rsqrt(jnp.mean(o * o, axis=-1, keepdims=True) + EPS) * nw_ref[...]
        o_ref[:, cols] = (y * sg_ref[:, cols]).astype(o_ref.dtype)
    if emit_state:
        for s in range(seqs):
            for d in range(2):
                for h in range(H_A):
                    so_ref[s, d, h] = st_scr[s, d, h].T


def _hgrn_call(p, nw, tabs, seq_len, n_seq, row_blk0, s0, jl, mix, state_out, name):
    wf, lvf, wb, lvb = tabs
    has_init = s0 is not None
    emit_state = s0 is None
    seqs = HGRN_CTX_SEQS if emit_state else 1
    assert row_blk0 % seqs == 0 and n_seq % seqs == 0
    blk0 = row_blk0 // seqs
    const = lambda b: (0, 0)
    blk = (seqs * seq_len, F_A)
    state_spec = pl.BlockSpec((seqs, None, 2, H_A, DK_A, DV_A), lambda b: (b, jl, 0, 0, 0, 0))
    in_specs = [pl.BlockSpec(blk, (lambda b, part=part: (blk0 + b, part))) for part in range(5)]
    in_specs += [
        pl.BlockSpec((None, 1, DV_A), lambda b: (jl, 0, 0)),
        pl.BlockSpec(wf.shape, const), pl.BlockSpec(wb.shape, const),
        pl.BlockSpec(lvf.shape, const), pl.BlockSpec(lvb.shape, const),
    ]
    args = [p, p, p, p, p, nw, wf, wb, lvf, lvb]
    if has_init:
        in_specs.append(state_spec)
        args.append(s0)
    out_shape = [jax.ShapeDtypeStruct((N_TOK, D_MIX), BF16)]
    out_specs = [pl.BlockSpec((seqs * seq_len, H_A * DV_A), lambda b: (blk0 + b, 0))]
    carried = [mix]
    if emit_state:
        out_shape.append(jax.ShapeDtypeStruct((BATCH, N_EVEN, 2, H_A, DK_A, DV_A), F32))
        out_specs.append(state_spec)
        carried.append(state_out)
    return _aliased_call(
        functools.partial(_hgrn_kernel, n_chunks=seq_len // HGRN_CHUNK, seqs=seqs,
                          has_init=has_init, emit_state=emit_state),
        grid=(n_seq // seqs,), in_specs=in_specs, args=args, out_specs=out_specs,
        out_shape=out_shape, carried=carried, sem=("parallel",), name=name,
        scratch_shapes=[pltpu.VMEM((seqs * seq_len, H_A * DV_A), F32),
                        pltpu.VMEM((seqs * seq_len, H_A * DV_A), F32),
                        pltpu.VMEM((seqs, 2, H_A, DV_A, DK_A), F32)])


def _swa_ctx_kernel(q_ref, kv_ref, sink_ref, o_ref, kc_ref, vc_ref):
    for s in range(SWA_CTX_SEQS):
        rows = slice(s * SEQ, (s + 1) * SEQ)
        kv = kv_ref[rows, :]
        k32 = kv[:, 0:KV_B * HD_B]
        v32 = kv[:, KV_B * HD_B:2 * KV_B * HD_B]
        kc_ref[s] = k32
        vc_ref[s] = v32
        k = k32.astype(BF16)
        vt32 = v32.T
        vts = [_with_ones_rows(vt32[n * HD_B:(n + 1) * HD_B]) for n in range(KV_B)]
        q = q_ref[rows, :].astype(BF16)
        ksl = [slice((h // G_B) * HD_B, (h // G_B + 1) * HD_B) for h in range(H_B)]
        sts = [_nt_dot(k[:, ksl[h]], q[:, h * HD_B:(h + 1) * HD_B]) for h in range(H_B)]
        ps, sinks = [], []
        for h in range(H_B):
            sink = sink_ref[0:1, h:h + 1] * LOG2_E
            m = jnp.maximum(jnp.max(sts[h], axis=0, keepdims=True), sink)
            ps.append(jnp.exp2(sts[h] - m).astype(BF16))
            sinks.append(jnp.exp2(sink - m))
        outs = []
        for h in range(H_B):
            ota = jnp.dot(vts[h // G_B], ps[h], preferred_element_type=F32)
            outs.append(ota[0:HD_B] / (ota[HD_B:HD_B + 1] + sinks[h]))
        o_ref[rows, :] = jnp.concatenate(outs, axis=0).T.astype(o_ref.dtype)


def _swa_ctx_call(p, sink, jl, mix, k_out, v_out, name):
    qcol = (3 * F_A + 2 * H_A * DV_A) // PROJ_TN
    rows = SWA_CTX_SEQS * SEQ
    cache_spec = pl.BlockSpec((SWA_CTX_SEQS, None, SEQ, KV_B * HD_B), lambda b: (b, jl, 0, 0))
    cache_shape = jax.ShapeDtypeStruct((BATCH, N_EVEN, SEQ, KV_B * HD_B), F32)
    return _aliased_call(
        _swa_ctx_kernel,
        grid=(BATCH // SWA_CTX_SEQS,),
        in_specs=[
            pl.BlockSpec((rows, H_B * HD_B), lambda b: (b, qcol)),
            pl.BlockSpec((rows, PROJ_TN), lambda b: (b, qcol + 1)),
            pl.BlockSpec((None, 1, H_B), lambda b: (jl, 0, 0)),
        ],
        args=[p, p, sink],
        out_specs=[pl.BlockSpec((rows, H_B * HD_B), lambda b: (b, 1)), cache_spec, cache_spec],
        out_shape=[jax.ShapeDtypeStruct((N_TOK, D_MIX), BF16), cache_shape, cache_shape],
        carried=[mix, k_out, v_out], sem=("parallel",), name=name)


def _swa_lat_kernel(q_ref, kv_ref, ck_ref, cv_ref, sink_ref, o_ref):
    qi = pl.program_id(1)
    tq = q_ref.shape[0]
    span = tq + 2 * WINDOW
    ws = pl.multiple_of(jnp.clip(qi * tq - WINDOW, 0, DEC_SEQ - span), WINDOW)
    kvw = kv_ref[pl.ds(ws, span), :]
    kw = kvw[:, 0:KV_B * HD_B].astype(BF16)
    vwt = kvw[:, KV_B * HD_B:2 * KV_B * HD_B].T.astype(BF16)
    kc = ck_ref[...].astype(BF16)
    vct = cv_ref[...].T.astype(BF16)
    q = q_ref[...].astype(BF16)
    t_k = ws + lax.broadcasted_iota(jnp.int32, (span, tq), 0)
    t_q = qi * tq + lax.broadcasted_iota(jnp.int32, (span, tq), 1)
    valid = jnp.abs(t_q - t_k) <= WINDOW
    outs = []

    def first(h):
        qh = q[:, h * HD_B:(h + 1) * HD_B]
        ksl = slice((h // G_B) * HD_B, (h // G_B + 1) * HD_B)
        return _nt_dot(kw[:, ksl], qh), _nt_dot(kc[:, ksl], qh)

    def second(h, scores):
        ksl = slice((h // G_B) * HD_B, (h // G_B + 1) * HD_B)
        s_w = jnp.where(valid, scores[0], -jnp.inf)
        s_c = scores[1]
        sink = sink_ref[0:1, h:h + 1] * LOG2_E
        m = jnp.maximum(jnp.maximum(jnp.max(s_w, axis=0, keepdims=True),
                                    jnp.max(s_c, axis=0, keepdims=True)), sink)
        p_w = jnp.exp2(s_w - m)
        p_c = jnp.exp2(s_c - m)
        den = (jnp.sum(p_w, axis=0, keepdims=True) + jnp.sum(p_c, axis=0, keepdims=True)
               + jnp.exp2(sink - m))
        ot = (jnp.dot(vwt[ksl, :], p_w.astype(BF16), preferred_element_type=F32)
              + jnp.dot(vct[ksl, :], p_c.astype(BF16), preferred_element_type=F32))
        outs.append(ot / den)

    _pipelined(H_B, first, second)
    o_ref[...] = jnp.concatenate(outs, axis=0).T.astype(o_ref.dtype)


def _swa_lat_call(p, ck, cv, sink, jl, mix, name):
    tq = ATT_TQ
    qcol = (3 * F_A + 2 * H_A * DV_A) // PROJ_TN
    nq = DEC_SEQ // tq
    q_blk0 = N_CTX_TOK // tq
    s_blk0 = N_CTX_TOK // DEC_SEQ
    cache_spec = pl.BlockSpec((None, None, PAST_LEN, KV_B * HD_B), lambda b, i: (b, jl, 0, 0))
    (out,) = _aliased_call(
        _swa_lat_kernel,
        grid=(DEC_BATCH, nq),
        in_specs=[
            pl.BlockSpec((tq, H_B * HD_B), lambda b, i: (q_blk0 + b * nq + i, qcol)),
            pl.BlockSpec((DEC_SEQ, PROJ_TN), lambda b, i: (s_blk0 + b, qcol + 1)),
            cache_spec, cache_spec,
            pl.BlockSpec((None, 1, H_B), lambda b, i: (jl, 0, 0)),
        ],
        args=[p, p, ck, cv, sink],
        out_specs=[pl.BlockSpec((tq, H_B * HD_B), lambda b, i: (q_blk0 + b * nq + i, 1))],
        out_shape=[jax.ShapeDtypeStruct((N_TOK, D_MIX), BF16)],
        carried=[mix], sem=("parallel", "parallel"), name=name)
    return out


def _diff_lambda(lp_ref, lam_init):
    lp = lp_ref[...]
    a = jnp.sum(lp[0:1] * lp[1:2], axis=-1, keepdims=True)
    b = jnp.sum(lp[2:3] * lp[3:4], axis=-1, keepdims=True)
    return jnp.exp(a) - jnp.exp(b) + lam_init


def _diff_scores(q, k_parts):
    return [[_nt_dot(kp[:, c * HD_C:(c + 1) * HD_C], q[:, c * HD_C:(c + 1) * HD_C])
             for kp in k_parts] for c in range(2)]


def _diff_finish(scores, vt_parts, lam, lam_init, sw_t):
    hw = 2 * HD_C
    comps = []
    for c in range(2):
        ss = scores[c]
        m = functools.reduce(jnp.maximum, [jnp.max(s, axis=0, keepdims=True) for s in ss])
        ota = functools.reduce(
            lambda a, b: a + b,
            [jnp.dot(vt, jnp.exp2(s - m).astype(BF16), preferred_element_type=F32)
             for s, vt in zip(ss, vt_parts)])
        comps.append(ota[0:hw] / ota[hw:hw + 1])
    ot = comps[0] - lam * comps[1]
    yt = ot * lax.rsqrt(jnp.mean(ot * ot, axis=0, keepdims=True) + EPS) * sw_t
    return (yt * (1.0 - lam_init)).T


def _with_ones_rows(vt):
    return jnp.concatenate([vt, jnp.ones((ONES_ROWS, vt.shape[1]), F32)], axis=0).astype(BF16)


def _diff_ctx_kernel(q_ref, k_ref, v_ref, lp_ref, sw_ref, o_ref, kc_ref, vc_ref, *, lam_init):
    lam = _diff_lambda(lp_ref, lam_init)
    hw = 2 * HD_C
    vts = {}

    def first(n):
        s, h = divmod(n, H_C)
        rows = slice(s * SEQ, (s + 1) * SEQ)
        sl = slice(h * hw, (h + 1) * hw)
        k32 = k_ref[rows, sl]
        v32 = v_ref[rows, sl]
        kc_ref[s, :, sl] = k32
        vc_ref[s, :, sl] = v32
        vts[n] = _with_ones_rows(v32.T)
        return _diff_scores(q_ref[rows, sl].astype(BF16), [k32.astype(BF16)])

    def second(n, scores):
        s, h = divmod(n, H_C)
        y = _diff_finish(scores, [vts.pop(n)], lam, lam_init, sw_ref[...])
        o_ref[s * SEQ:(s + 1) * SEQ, h * hw:(h + 1) * hw] = y.astype(o_ref.dtype)

    _pipelined(CTX_SEQS * H_C, first, second)


def _diff_ctx_call(p, lp, sw, lam_init, jl, k_out, v_out, name):
    rows = CTX_SEQS * SEQ
    cache_spec = pl.BlockSpec((CTX_SEQS, None, SEQ, ODD_W), lambda b: (b, jl, 0, 0))
    cache_shape = jax.ShapeDtypeStruct((BATCH, N_ODD, SEQ, ODD_W), F32)
    return _aliased_call(
        functools.partial(_diff_ctx_kernel, lam_init=lam_init),
        grid=(BATCH // CTX_SEQS,),
        in_specs=[
            pl.BlockSpec((rows, ODD_W), lambda b: (b, 0)),
            pl.BlockSpec((rows, ODD_W), lambda b: (b, 1)),
            pl.BlockSpec((rows, ODD_W), lambda b: (b, 2)),
            pl.BlockSpec((None, 4, HD_C), lambda b: (jl, 0, 0)),
            pl.BlockSpec((None, 2 * HD_C, ATT_TQ), lambda b: (jl, 0, 0)),
        ],
        args=[p, p, p, lp, sw],
        out_specs=[pl.BlockSpec((rows, ODD_W), lambda b: (b, 0)), cache_spec, cache_spec],
        out_shape=[jax.ShapeDtypeStruct((N_TOK, D_MIX), BF16), cache_shape, cache_shape],
        carried=[None, k_out, v_out], sem=("parallel",), name=name)


def _diff_lat_kernel(q_ref, k_ref, v_ref, ck_ref, cv_ref, lp_ref, sw_ref, o_ref, *, lam_init):
    lam = _diff_lambda(lp_ref, lam_init)
    hw = 2 * HD_C
    tq = ATT_TQ
    nq = q_ref.shape[0] // tq
    heads = []
    for g in range(q_ref.shape[1] // hw):
        sl = slice(g * hw, (g + 1) * hw)
        hd = pl.program_id(1) * (q_ref.shape[1] // hw) + g
        cv = cv_ref[pl.ds(hd, PAST_LEN, stride=H_C), :]
        kb = [slice(r, r + PAST_LEN) for r in range(0, k_ref.shape[0], PAST_LEN)]
        heads.append(([k_ref[r, sl].astype(BF16) for r in kb] + [ck_ref[:, sl].astype(BF16)],
                      [_with_ones_rows(v_ref[r, sl].T) for r in kb] + [_with_ones_rows(cv.T)]))

    def first(n):
        g, i = divmod(n, nq)
        q = q_ref[i * tq:(i + 1) * tq, g * hw:(g + 1) * hw].astype(BF16)
        return _diff_scores(q, heads[g][0])

    def second(n, scores):
        g, i = divmod(n, nq)
        y = _diff_finish(scores, heads[g][1], lam, lam_init, sw_ref[...])
        o_ref[i * tq:(i + 1) * tq, g * hw:(g + 1) * hw] = y.astype(o_ref.dtype)

    _pipelined(len(heads) * nq, first, second)


def _diff_lat_call(p, ck, cv, lp, sw, lam_init, jl, mix, name):
    hw = 2 * HD_C
    gw = DIFF_LAT_HEADS * hw
    n_g = H_C // DIFF_LAT_HEADS
    s_blk0 = N_CTX_TOK // DEC_SEQ
    cache_spec = pl.BlockSpec((None, None, PAST_LEN, gw), lambda b, h: (b, jl, 0, h))
    (out,) = _aliased_call(
        functools.partial(_diff_lat_kernel, lam_init=lam_init),
        grid=(DEC_BATCH, n_g),
        in_specs=[
            pl.BlockSpec((DEC_SEQ, gw), lambda b, h: (s_blk0 + b, h)),
            pl.BlockSpec((DEC_SEQ, gw), lambda b, h: (s_blk0 + b, n_g + h)),
            pl.BlockSpec((DEC_SEQ, gw), lambda b, h: (s_blk0 + b, 2 * n_g + h)),
            cache_spec,
            pl.BlockSpec((None, None, PAST_LEN * H_C, hw), lambda b, h: (b, jl, 0, 0)),
            pl.BlockSpec((None, 4, HD_C), lambda b, h: (jl, 0, 0)),
            pl.BlockSpec((None, hw, ATT_TQ), lambda b, h: (jl, 0, 0)),
        ],
        args=[p, p, p, ck, cv, lp, sw],
        out_specs=[pl.BlockSpec((DEC_SEQ, gw), lambda b, h: (s_blk0 + b, h))],
        out_shape=[jax.ShapeDtypeStruct((N_TOK, D_MIX), BF16)],
        carried=[mix], sem=("parallel", "parallel"), name=name)
    return out


def _mlp_kernel(x_ref, mix_ref, wo_ref, mods_ref, nw_ref, w1_ref, w2_ref, o_ref,
                h_scr, *, group):
    i = pl.program_id(0)
    k = pl.program_id(1)
    n_ctx_tiles = N_CTX_TOK // x_ref.shape[0]
    row = {"ctx": 0, "lat": 1 + i,
           "all": jnp.where(i >= n_ctx_tiles, 1 + i - n_ctx_tiles, 0)}[group]

    def mod(a):
        return mods_ref[pl.ds(row, 1), a * D_MODEL:(a + 1) * D_MODEL]

    @pl.when(k == 0)
    def _():
        def first(r):
            return jnp.dot(mix_ref[r * MLP_RC:(r + 1) * MLP_RC, :], wo_ref[...],
                           preferred_element_type=F32)

        def second(r, y):
            rows = slice(r * MLP_RC, (r + 1) * MLP_RC)
            x1 = x_ref[rows, :] + mod(2) * y
            o_ref[rows, :] = x1
            h_scr[rows, :] = _norm_mod(x1, nw_ref[...], mod(4), mod(3)).astype(BF16)

        _pipelined(x_ref.shape[0] // MLP_RC, first, second)

    u = jnp.dot(h_scr[...], w1_ref[...].astype(BF16), preferred_element_type=F32)
    u = jnp.square(jnp.maximum(u, 0.0)).astype(BF16)
    o_ref[...] += mod(5) * jnp.dot(u, w2_ref[...].astype(BF16), preferred_element_type=F32)


def _mlp_call(x_src, x_tile0, lat, mix, wo, jl, mods, li, norm_w, w1, w2, out_prev, out_rows,
              out_tile0, name):
    tm, tk = MLP_TM, MLP_TK
    group = {False: "ctx", True: "lat", None: "all"}[lat]
    n_tiles = {"ctx": N_CTX_TOK, "lat": N_LAT_TOK, "all": N_TOK}[group] // tm
    mix_tile0 = N_CTX_TOK // tm if group == "lat" else 0
    (out,) = _aliased_call(
        functools.partial(_mlp_kernel, group=group),
        grid=(n_tiles, D_FF // tk),
        in_specs=[
            pl.BlockSpec((tm, D_MODEL), lambda i, k: (x_tile0 + i, 0)),
            pl.BlockSpec((tm, D_MIX), lambda i, k: (mix_tile0 + i, 0)),
            pl.BlockSpec((None, D_MIX, D_MODEL), lambda i, k: (jl, 0, 0)),
            pl.BlockSpec((None, MOD_ROWS, N_MOD), lambda i, k: (li, 0, 0)),
            pl.BlockSpec((None, None, 1, D_MODEL), lambda i, k: (li, 1, 0, 0)),
            pl.BlockSpec((None, D_MODEL, tk), lambda i, k: (li, 0, k)),
            pl.BlockSpec((None, tk, D_MODEL), lambda i, k: (li, k, 0)),
        ],
        args=[x_src, mix, wo, mods, norm_w, w1, w2],
        out_specs=[pl.BlockSpec((tm, D_MODEL), lambda i, k: (out_tile0 + i, 0))],
        out_shape=[jax.ShapeDtypeStruct((out_rows, D_MODEL), F32)],
        carried=[out_prev], sem=("parallel", "arbitrary"), name=name,
        scratch_shapes=[pltpu.VMEM((tm, D_MODEL), BF16)])
    return out


def _rope_tables(width):
    t = np.arange(DEC_SEQ)
    half = HEAD_GROUP // 2
    inv = ROPE_BASE ** (-np.arange(0, half, 2, dtype=np.float64) / half)
    ang = np.concatenate([(t // GRID_W)[:, None] * inv, (t % GRID_W)[:, None] * inv], axis=-1)
    cos = np.repeat(np.cos(ang), 2, axis=-1)
    sin = np.repeat(np.sin(ang), 2, axis=-1)
    sign = np.tile(np.array([-1.0, 1.0]), HEAD_GROUP // 2)
    reps = width // HEAD_GROUP
    return (jnp.asarray(np.tile(cos, (1, reps)), F32),
            jnp.asarray(np.tile(sin * sign, (1, reps)), F32))


def _block_diag_ones(n):
    g = np.arange(n) // HEAD_GROUP
    return jnp.asarray(g[:, None] == g[None, :], BF16)


def _tile_row(w, width):
    return jnp.tile(w.astype(F32), width // w.shape[0])[None, :]


def _lambda_init(li):
    return 0.8 - 0.6 * math.exp(-0.3 * li)


def kernel(x_prompt, x_sample, cache_k_swa, cache_v_swa, state_hgrn, cache_k_diff, cache_v_diff, c, c_ctx, norm_w, w_ada, b_ada, w_in_even, w_out_even, hgrn_lb_logits, hgrn_norm_w, swa_qnorm_w, swa_knorm_w, swa_sink, w_in_odd, w_out_odd, diff_qnorm_w, diff_knorm_w, diff_lambda_p, diff_subln_w, w_mlp1, w_mlp2):
    assert PROJ_TM == MLP_TM
    n_ctx_tiles = N_CTX_TOK // PROJ_TM
    x = None
    x_ctx0 = x_prompt.reshape(N_CTX_TOK, D_MODEL)
    x_lat0 = x_sample.reshape(N_LAT_TOK, D_MODEL)
    c_all = jnp.concatenate(
        [c_ctx[None, :], c, jnp.zeros((MOD_ROWS - 1 - DEC_BATCH, D_MODEL), F32)], axis=0)
    mods = _mods_call(c_all, w_ada, b_ada.reshape(DEPTH, 1, N_MOD))

    cos_t, sin_t = _rope_tables(ROPE_TW)
    bd = _block_diag_ones(256)
    tabs_f = _hgrn_tables(False)
    tabs_b = _hgrn_tables(True)
    hgrn_tabs = (tabs_f[0], tabs_f[1], tabs_b[0], tabs_b[1])
    lbl = hgrn_lb_logits.astype(F32).reshape(N_EVEN * 2, F_A)
    norm_w4 = norm_w.astype(F32).reshape(DEPTH, 2, 1, D_MODEL)

    w_out_even_b = w_out_even.astype(BF16)
    w_out_odd_b = w_out_odd.astype(BF16)

    ck_swa = cache_k_swa.reshape(DEC_BATCH, N_EVEN, PAST_LEN, KV_B * HD_B)
    cv_swa = cache_v_swa.reshape(DEC_BATCH, N_EVEN, PAST_LEN, KV_B * HD_B)
    ck_diff = cache_k_diff.reshape(DEC_BATCH, N_ODD, PAST_LEN, ODD_W)
    cv_diff = cache_v_diff.reshape(DEC_BATCH, N_ODD, PAST_LEN * H_C, 2 * HD_C)
    hgrn_nw = hgrn_norm_w.astype(F32).reshape(N_EVEN, 1, DV_A)
    sink = swa_sink.astype(F32).reshape(N_EVEN, 1, H_B)
    lam_p = diff_lambda_p.astype(F32)
    assert ATT_TQ == SEQ
    subln = jnp.broadcast_to(diff_subln_w.astype(F32)[:, :, None], (N_ODD, 2 * HD_C, ATT_TQ))

    even_kinds = ("silu_scale", "loggate0", "loggate1", "ident", "silu", "qnorm", "kv")
    odd_kinds = ("qnorm", "qnorm", "knorm", "knorm", "ident", "ident")

    k_swa = v_swa = states = k_diff = v_diff = None
    for li in range(DEPTH):
        j = li // 2
        srcs = ((x_ctx0, 0), (x_lat0, 0)) if li == 0 else ((x, 0), (x, n_ctx_tiles))
        if li % 2 == 0:
            p = None
            for lat, (src, t0) in enumerate(srcs):
                p = _proj_call(src, t0, bool(lat), p, mods, li, norm_w4, w_in_even, j, lbl,
                               _tile_row(swa_qnorm_w[j], PROJ_TN),
                               _tile_row(swa_knorm_w[j], PROJ_TN),
                               cos_t, sin_t, bd, even_kinds, f"proj_even{j}_{lat}")
            mix, states = _hgrn_call(p, hgrn_nw, hgrn_tabs, SEQ, BATCH, 0, None, j, None, states,
                                     f"hgrn_ctx{j}")
            (mix,) = _hgrn_call(p, hgrn_nw, hgrn_tabs, DEC_SEQ, DEC_BATCH, N_CTX_TOK // DEC_SEQ,
                                state_hgrn, j, mix, None, f"hgrn_lat{j}")
            mix, k_swa, v_swa = _swa_ctx_call(p, sink, j, mix, k_swa, v_swa, f"swa_ctx{j}")
            mix = _swa_lat_call(p, ck_swa, cv_swa, sink, j, mix, f"swa_lat{j}")
            wo = w_out_even_b
        else:
            p = None
            for lat, (src, t0) in enumerate(srcs):
                p = _proj_call(src, t0, bool(lat), p, mods, li, norm_w4, w_in_odd, j, lbl,
                               _tile_row(diff_qnorm_w[j], PROJ_TN),
                               _tile_row(diff_knorm_w[j], PROJ_TN),
                               cos_t, sin_t, bd, odd_kinds, f"proj_odd{j}_{lat}")
            lam_init = _lambda_init(li)
            mix, k_diff, v_diff = _diff_ctx_call(p, lam_p, subln, lam_init, j, k_diff, v_diff,
                                                 f"diff_ctx{j}")
            mix = _diff_lat_call(p, ck_diff, cv_diff, lam_p, subln, lam_init, j, mix,
                                 f"diff_lat{j}")
            wo = w_out_odd_b
        last = li == DEPTH - 1
        if li == 0 or last:
            outs = []
            x_next = None
            for lat, (src, t0) in enumerate(srcs):
                rows = (N_LAT_TOK if lat else N_CTX_TOK) if last else N_TOK
                out_t0 = 0 if last else lat * n_ctx_tiles
                x_next = _mlp_call(src, t0, bool(lat), mix, wo, j, mods, li, norm_w4, w_mlp1,
                                   w_mlp2, None if last else x_next, rows, out_t0,
                                   f"mlp{li}_{lat}")
                outs.append(x_next)
            x = x_next
        else:
            x = _mlp_call(x, 0, None, mix, wo, j, mods, li, norm_w4, w_mlp1, w_mlp2, None, N_TOK,
                          0, f"mlp{li}")

    y_prompt = outs[0].reshape(BATCH, SEQ, D_MODEL)
    y_sample = outs[1].reshape(DEC_BATCH, DEC_SEQ, D_MODEL)
    return (y_prompt, y_sample,
            k_swa.reshape(BATCH, N_EVEN, SEQ, KV_B, HD_B),
            v_swa.reshape(BATCH, N_EVEN, SEQ, KV_B, HD_B),
            states,
            k_diff.reshape(BATCH, N_ODD, SEQ, H_C, 2, HD_C),
            v_diff.reshape(BATCH, N_ODD, SEQ, H_C, 2 * HD_C))
```
